```python
import jax, jax.numpy as jnp
from jax import lax
import numpy as np

D_MODEL = 1024
BATCH = 8
SEQ = 4096
DEPTH = 1

N_HEADS_ATTN = 8
HEAD_DIM = 64
ATTN_WIDTH = N_HEADS_ATTN * HEAD_DIM
MOBA_BLOCK = 256
MOBA_TOPK = 3
Q_CHUNK = 32
ROPE_THETA = 500000.0
ROT_DIM = HEAD_DIM // 4
CONV_WIDTH = 512
CONV_K = 3
N_BRANCH = 2
IN_COLS = 3 * ATTN_WIDTH + 3 * CONV_WIDTH + N_BRANCH * D_MODEL
N_GROUPS = 4
EXPERTS_PER_GROUP = 8
N_EXPERTS = N_GROUPS * EXPERTS_PER_GROUP
TOPK_IN_GROUP = 2
D_EXPERT = 512
MOE_BLOCK = 128
N_MOD = 6
EPS = 1e-6

kernel_name = "hybrid_moba_shortconv_hmoe_block"


def rms_norm(x, g):
    xf = x.astype(jnp.float32)
    y = xf * lax.rsqrt(jnp.mean(xf * xf, axis=-1, keepdims=True) + EPS)
    return (y * g.astype(jnp.float32)).astype(x.dtype)


def modulate(h, shift, scale):
    return h * (1 + scale[:, None, :]) + shift[:, None, :]


def rope_tables(seq_len, dtype):
    pos = jnp.arange(seq_len, dtype=jnp.float32)
    inv_freq = ROPE_THETA ** (-jnp.arange(0, ROT_DIM, 2, dtype=jnp.float32) / ROT_DIM)
    ang = pos[:, None] * inv_freq[None, :]
    return jnp.cos(ang).astype(dtype), jnp.sin(ang).astype(dtype)


def partial_rope(x, cos, sin):
    half = ROT_DIM // 2
    x1, x2, xp = x[..., :half], x[..., half:ROT_DIM], x[..., ROT_DIM:]
    return jnp.concatenate([x1 * cos - x2 * sin, x2 * cos + x1 * sin, xp], axis=-1)


def moba_attention(q, k, v):
    B, H, S, dh = q.shape
    nb = -(-S // MOBA_BLOCK)
    pad = nb * MOBA_BLOCK - S
    kb = jnp.pad(k, ((0, 0), (0, 0), (0, pad), (0, 0))).reshape(B, H, nb, MOBA_BLOCK, dh)
    vb = jnp.pad(v, ((0, 0), (0, 0), (0, pad), (0, 0))).reshape(B, H, nb, MOBA_BLOCK, dh)
    kmean = jnp.mean(kb.astype(jnp.float32), axis=3)
    topk = min(MOBA_TOPK, nb)
    scale = HEAD_DIM ** -0.5
    b_ix = jnp.arange(B)[:, None, None, None]
    h_ix = jnp.arange(H)[None, :, None, None]
    blk_ids = jnp.arange(nb)
    t_in_blk = jnp.arange(MOBA_BLOCK)

    def chunk(ci):
        q0 = ci * Q_CHUNK
        qc = lax.dynamic_slice_in_dim(q, q0, Q_CHUNK, axis=2)
        qpos = q0 + jnp.arange(Q_CHUNK)
        cur = q0 // MOBA_BLOCK
        gate = jnp.einsum('bhqd,bhnd->bhqn', qc.astype(jnp.float32), kmean)
        gate = jnp.where(blk_ids < cur, gate, -jnp.inf)
        _, sel = lax.top_k(gate, topk)
        sel_valid = jnp.arange(topk) < cur
        k_sel = kb[b_ix, h_ix, sel]
        v_sel = vb[b_ix, h_ix, sel]
        s_sel = jnp.einsum('bhqd,bhqntd->bhqnt', qc, k_sel).astype(jnp.float32) * scale
        s_sel = jnp.where(sel_valid[:, None], s_sel, -jnp.inf)
        k_own = lax.dynamic_index_in_dim(kb, cur, axis=2, keepdims=False)
        v_own = lax.dynamic_index_in_dim(vb, cur, axis=2, keepdims=False)
        s_own = jnp.einsum('bhqd,bhtd->bhqt', qc, k_own).astype(jnp.float32) * scale
        kpos = cur * MOBA_BLOCK + t_in_blk
        s_own = jnp.where(kpos[None, :] <= qpos[:, None], s_own, -jnp.inf)
        logits = jnp.concatenate([s_sel.reshape(B, H, Q_CHUNK, topk * MOBA_BLOCK), s_own], axis=-1)
        p = jax.nn.softmax(logits, axis=-1).astype(v.dtype)
        p_sel = p[..., :topk * MOBA_BLOCK].reshape(B, H, Q_CHUNK, topk, MOBA_BLOCK)
        p_own = p[..., topk * MOBA_BLOCK:]
        return (jnp.einsum('bhqnt,bhqntd->bhqd', p_sel, v_sel)
                + jnp.einsum('bhqt,bhtd->bhqd', p_own, v_own))

    out = lax.map(chunk, jnp.arange(S // Q_CHUNK))
    return out.transpose(1, 2, 0, 3, 4).reshape(B, H, S, dh)


def short_gated_conv(xb, bg, cg, conv_w, conv_b):
    u = cg * xb
    y = lax.conv_general_dilated(u, conv_w[:, None, :], window_strides=(1,), padding=[(CONV_K - 1, 0)],
                                 dimension_numbers=('NWC', 'WIO', 'NWC'), feature_group_count=CONV_WIDTH)
    return bg * (y + conv_b)


def hierarchical_moe(h, w_rg, b_rg, w_re, b_re, w1, w3, w2):
    B, S, D = h.shape
    T = B * S
    xt = h.reshape(T, D)
    g_logits = (xt @ w_rg).astype(jnp.float32) + b_rg.astype(jnp.float32)
    g_prob = jax.nn.softmax(g_logits, axis=-1)
    g_idx = jnp.argmax(g_logits, axis=-1)
    g_w = jnp.take_along_axis(g_prob, g_idx[:, None], axis=-1)
    e_logits = ((xt @ w_re).astype(jnp.float32) + b_re.astype(jnp.float32)).reshape(T, N_GROUPS, EXPERTS_PER_GROUP)
    e_logits = jnp.take_along_axis(e_logits, g_idx[:, None, None], axis=1)[:, 0]
    e_val, e_loc = lax.top_k(e_logits, TOPK_IN_GROUP)
    e_w = jax.nn.softmax(e_val, axis=-1) * g_w
    e_id = g_idx[:, None].astype(jnp.int32) * EXPERTS_PER_GROUP + e_loc.astype(jnp.int32)

    n_assign = T * TOPK_IN_GROUP
    flat_e = e_id.reshape(-1)
    flat_w = e_w.reshape(-1)
    order = jnp.argsort(flat_e)
    sorted_e = flat_e[order]
    counts = jnp.bincount(flat_e, length=N_EXPERTS)
    padded = (counts + MOE_BLOCK - 1) // MOE_BLOCK * MOE_BLOCK
    start = jnp.cumsum(counts) - counts
    pad_end = jnp.cumsum(padded)
    pad_start = pad_end - padded
    dest = pad_start[sorted_e] + (jnp.arange(n_assign) - start[sorted_e])
    n_pad = (-(-n_assign // MOE_BLOCK) + N_EXPERTS) * MOE_BLOCK
    n_blocks = n_pad // MOE_BLOCK
    tok = jnp.full((n_pad,), T, jnp.int32).at[dest].set((order // TOPK_IN_GROUP).astype(jnp.int32))
    wpad = jnp.zeros((n_pad,), jnp.float32).at[dest].set(flat_w[order])
    blk_start = jnp.arange(n_blocks) * MOE_BLOCK
    blk_expert = jnp.minimum(jnp.sum(pad_end[None, :] <= blk_start[:, None], axis=-1), N_EXPERTS - 1)
    x_ext = jnp.concatenate([xt, jnp.zeros((1, D), xt.dtype)], axis=0)
    xs = x_ext[tok].reshape(n_blocks, MOE_BLOCK, D)

    def expert_block(args):
        xb, e = args
        hid = jax.nn.silu(xb @ w1[e]) * (xb @ w3[e])
        return hid @ w2[e]

    ys = lax.map(expert_block, (xs, blk_expert)).reshape(n_pad, D)
    ys = ys * wpad[:, None].astype(ys.dtype)
    out = jnp.zeros((T + 1, D), ys.dtype).at[tok].add(ys)[:T]
    return out.reshape(B, S, D)


def setup_inputs(seed: int = 0) -> dict:
    key = jax.random.key(seed)
    ks = jax.random.split(key, 24)
    L, D = DEPTH, D_MODEL
    f32 = jnp.float32
    nrm = lambda k, shape, s: jax.random.normal(k, shape, f32) * s
    return {
        "x": nrm(ks[0], (BATCH, SEQ, D), 1.0),
        "c": nrm(ks[1], (BATCH, D), 1.0),
        "w_ada": nrm(ks[2], (L, D, N_MOD * D), 0.5 * D ** -0.5),
        "b_ada": nrm(ks[3], (L, N_MOD * D), 0.02),
        "g_norm1": 1.0 + nrm(ks[4], (L, D), 0.02),
        "g_norm2": 1.0 + nrm(ks[5], (L, D), 0.02),
        "w_in": nrm(ks[6], (L, D, IN_COLS), D ** -0.5),
        "g_q": 1.0 + nrm(ks[7], (L, HEAD_DIM), 0.02),
        "g_k": 1.0 + nrm(ks[8], (L, HEAD_DIM), 0.02),
        "conv_w": nrm(ks[9], (L, CONV_K, CONV_WIDTH), CONV_K ** -0.5),
        "conv_b": nrm(ks[10], (L, CONV_WIDTH), 0.02),
        "w_pa": nrm(ks[11], (L, ATTN_WIDTH, D), ATTN_WIDTH ** -0.5),
        "w_pb": nrm(ks[12], (L, CONV_WIDTH, D), CONV_WIDTH ** -0.5),
        "w_o": nrm(ks[13], (L, D, D), D ** -0.5),
        "w_rg": nrm(ks[14], (L, D, N_GROUPS), D ** -0.5),
        "b_rg": nrm(ks[15], (L, N_GROUPS), 0.01),
        "w_re": nrm(ks[16], (L, D, N_EXPERTS), D ** -0.5),
        "b_re": nrm(ks[17], (L, N_EXPERTS), 0.01),
        "w1": nrm(ks[18], (L, N_EXPERTS, D, D_EXPERT), D ** -0.5),
        "w3": nrm(ks[19], (L, N_EXPERTS, D, D_EXPERT), D ** -0.5),
        "w2": nrm(ks[20], (L, N_EXPERTS, D_EXPERT, D), D_EXPERT ** -0.5),
    }


def reference(x, c, w_ada, b_ada, g_norm1, g_norm2, w_in, g_q, g_k, conv_w, conv_b,
              w_pa, w_pb, w_o, w_rg, b_rg, w_re, b_re, w1, w3, w2):
    B, S, D = x.shape
    cos, sin = rope_tables(S, x.dtype)
    splits = [ATTN_WIDTH, 2 * ATTN_WIDTH, 3 * ATTN_WIDTH,
              3 * ATTN_WIDTH + CONV_WIDTH, 3 * ATTN_WIDTH + 2 * CONV_WIDTH, 3 * ATTN_WIDTH + 3 * CONV_WIDTH]
    for l in range(DEPTH):
        mod = jax.nn.silu(c) @ w_ada[l] + b_ada[l]
        sh1, sc1, ga1, sh2, sc2, ga2 = jnp.split(mod, N_MOD, axis=-1)

        h = modulate(rms_norm(x, g_norm1[l]), sh1, sc1)
        z = h @ w_in[l]
        q, k, v, xb, bg, cg, gates = jnp.split(z, splits, axis=-1)
        to_heads = lambda t: t.reshape(B, S, N_HEADS_ATTN, HEAD_DIM).transpose(0, 2, 1, 3)
        q = partial_rope(rms_norm(to_heads(q), g_q[l]), cos, sin)
        k = partial_rope(rms_norm(to_heads(k), g_k[l]), cos, sin)
        v = to_heads(v)
        y_a = moba_attention(q, k, v).transpose(0, 2, 1, 3).reshape(B, S, ATTN_WIDTH)
        y_b = short_gated_conv(xb, bg, cg, conv_w[l], conv_b[l])
        gate_a, gate_b = jnp.split(jax.nn.sigmoid(gates), N_BRANCH, axis=-1)
        merged = gate_a * (y_a @ w_pa[l]) + gate_b * (y_b @ w_pb[l])
        x = x + ga1[:, None, :] * (merged @ w_o[l])

        h2 = modulate(rms_norm(x, g_norm2[l]), sh2, sc2)
        x = x + ga2[:, None, :] * hierarchical_moe(h2, w_rg[l], b_rg[l], w_re[l], b_re[l], w1[l], w3[l], w2[l])
    return x
```

```python
import functools

import jax
import jax.numpy as jnp
from jax import lax
from jax.experimental import pallas as pl
from jax.experimental.pallas import tpu as pltpu

F32 = jnp.float32
BF16 = jnp.bfloat16
HIGHEST = lax.Precision.HIGHEST

N_HEADS = 8
HEAD_DIM = 64
ATTN_WIDTH = N_HEADS * HEAD_DIM
CONV_WIDTH = 512
CONV_K = 3
MOBA_BLOCK = 256
MOBA_TOPK = 3
ROPE_THETA = 500000.0
ROT_DIM = HEAD_DIM // 4
N_GROUPS = 4
EXPERTS_PER_GROUP = 8
N_EXPERTS = N_GROUPS * EXPERTS_PER_GROUP
TOPK_IN_GROUP = 2
N_MOD = 6
EPS = 1e-6

LANES = 128
NEG = -1e30
ROW_TILE = 512
EXPERT_ROWS = 256
VMEM_LIMIT = 56 * 1024 * 1024


def _cparams(*sem):
    return pltpu.CompilerParams(dimension_semantics=sem, vmem_limit_bytes=VMEM_LIMIT)


def _ada_kernel(c_ref, w_ref, b_ref, o_ref):
    c = c_ref[...]
    a = c * jax.nn.sigmoid(c)
    o_ref[...] = jnp.dot(a, w_ref[...], preferred_element_type=F32, precision=HIGHEST) + b_ref[...]


def _ada(c, w_ada, b_ada):
    B, D = c.shape
    N = w_ada.shape[1]
    tn = 1536
    return pl.pallas_call(
        _ada_kernel,
        grid=(N // tn,),
        in_specs=[pl.BlockSpec((B, D), lambda j: (0, 0)),
                  pl.BlockSpec((D, tn), lambda j: (0, j)),
                  pl.BlockSpec((1, tn), lambda j: (0, j))],
        out_specs=pl.BlockSpec((B, tn), lambda j: (0, j)),
        out_shape=jax.ShapeDtypeStruct((B, N), F32),
        compiler_params=_cparams("arbitrary"),
        name="ada",
    )(c, w_ada, b_ada.reshape(1, N))


def _inproj_kernel(x_ref, g_ref, sc_ref, sh_ref, w_ref, z_ref, *, n_chunk):
    x = x_ref[...]
    ms = jnp.mean(x * x, axis=-1, keepdims=True)
    y = x * lax.rsqrt(ms + EPS) * g_ref[...]
    h = (y * (1.0 + sc_ref[0]) + sh_ref[0]).astype(BF16)
    for n in range(0, z_ref.shape[1], n_chunk):
        z_ref[:, n:n + n_chunk] = jnp.dot(h, w_ref[:, n:n + n_chunk],
                                          preferred_element_type=F32).astype(BF16)


def _inproj(x2, g1, sc1, sh1, w_in_bf, tiles_per_seq):
    T, D = x2.shape
    N = w_in_bf.shape[1]
    tm = ROW_TILE
    bmap = lambda i: (i // tiles_per_seq, 0, 0)
    return pl.pallas_call(
        functools.partial(_inproj_kernel, n_chunk=512),
        grid=(T // tm,),
        in_specs=[pl.BlockSpec((tm, D), lambda i: (i, 0)),
                  pl.BlockSpec((1, D), lambda i: (0, 0)),
                  pl.BlockSpec((1, 1, D), bmap),
                  pl.BlockSpec((1, 1, D), bmap),
                  pl.BlockSpec((D, N), lambda i: (0, 0))],
        out_specs=pl.BlockSpec((tm, N), lambda i: (i, 0)),
        out_shape=jax.ShapeDtypeStruct((T, N), BF16),
        compiler_params=_cparams("arbitrary"),
        name="inproj",
    )(x2, g1, sc1, sh1, w_in_bf)


def _attn_kernel(q_ref, k_ref, v_ref, cos_ref, sin_ref, gq_ref, gk_ref, o_ref, kaug_ref, kmp_ref):
    S = q_ref.shape[1]
    nb = S // MOBA_BLOCK
    blk = MOBA_BLOCK
    lane = lax.broadcasted_iota(jnp.int32, (blk, LANES), 1)
    head0 = lane < HEAD_DIM
    lane_in_head = lane & (HEAD_DIM - 1)
    rot_lo = lane_in_head < (ROT_DIM // 2)

    def norm_rope(xb, g, r0):
        x = xb.astype(F32)
        sq = x * x
        s0 = jnp.sum(jnp.where(head0, sq, 0.0), axis=-1, keepdims=True)
        s1 = jnp.sum(jnp.where(head0, 0.0, sq), axis=-1, keepdims=True)
        inv = jnp.where(head0, lax.rsqrt(s0 * (1.0 / HEAD_DIM) + EPS), lax.rsqrt(s1 * (1.0 / HEAD_DIM) + EPS))
        y = x * inv * g
        half = ROT_DIM // 2
        rot = jnp.where(rot_lo, pltpu.roll(y, LANES - half, 1), pltpu.roll(y, half, 1))
        return y * cos_ref[pl.ds(r0, blk), :] + rot * sin_ref[pl.ds(r0, blk), :]

    kmp_ref[...] = jnp.zeros_like(kmp_ref)

    def kbody(j, carry):
        r0 = pl.multiple_of(j * blk, blk)
        kr = norm_rope(k_ref[0, pl.ds(r0, blk), :], gk_ref[...], r0)
        kmp_ref[pl.ds(HEAD_DIM + j, 1), :] = jnp.sum(kr, axis=0, keepdims=True) * (1.0 / blk)
        onehot = jnp.where(lane - HEAD_DIM == j, 1.0, 0.0)
        kaug_ref[0, pl.ds(r0, blk), :] = jnp.where(head0, kr, onehot).astype(BF16)
        kaug_ref[1, pl.ds(r0, blk), :] = jnp.where(head0, pltpu.roll(kr, HEAD_DIM, 1), onehot).astype(BF16)
        return carry

    lax.fori_loop(0, nb, kbody, 0)

    row = lax.broadcasted_iota(jnp.int32, (blk, blk), 0)
    col = lax.broadcasted_iota(jnp.int32, (blk, blk), 1)
    causal = col <= row
    contract_lanes = (((1,), (1,)), ((), ()))

    def qbody(qi, carry):
        r0 = pl.multiple_of(qi * blk, blk)
        qr = norm_rope(q_ref[0, pl.ds(r0, blk), :], gq_ref[...], r0)
        outs = []
        for h in range(2):
            qm = jnp.where(head0, qr, 0.0) if h == 0 else jnp.where(head0, 0.0, qr)
            gate = lax.dot_general(qm, kmp_ref[...], contract_lanes, precision=HIGHEST,
                                   preferred_element_type=F32)
            g = jnp.where((lane >= HEAD_DIM) & (lane < HEAD_DIM + qi), gate, -jnp.inf)
            keep = lane == HEAD_DIM + qi
            for r in range(MOBA_TOPK):
                m = jnp.max(g, axis=-1, keepdims=True)
                idx = jnp.min(jnp.where(g == m, lane, 1 << 20), axis=-1, keepdims=True)
                pick = (lane == idx) & (qi > r)
                keep = keep | pick
                g = jnp.where(pick, -jnp.inf, g)
            bias = jnp.where(keep, 0.0, NEG)
            qh = qr if h == 0 else pltpu.roll(qr, HEAD_DIM, 1)
            qa = jnp.where(head0, qh * (HEAD_DIM ** -0.5), bias).astype(BF16)

            s = lax.dot_general(qa, kaug_ref[h, pl.ds(r0, blk), :], contract_lanes, preferred_element_type=F32)
            s = jnp.where(causal, s, NEG)
            m0 = jnp.max(s, axis=-1, keepdims=True)
            p = jnp.exp(s - m0)
            l0 = jnp.sum(p, axis=-1, keepdims=True)
            acc0 = jnp.dot(p.astype(BF16), v_ref[0, pl.ds(r0, blk), :], preferred_element_type=F32)

            def jbody(j, c):
                m_i, l_i, acc = c
                c0 = pl.multiple_of(j * blk, blk)
                s = lax.dot_general(qa, kaug_ref[h, pl.ds(c0, blk), :], contract_lanes,
                                    preferred_element_type=F32)
                m_new = jnp.maximum(m_i, jnp.max(s, axis=-1, keepdims=True))
                alpha = jnp.exp(m_i - m_new)
                p = jnp.exp(s - m_new)
                l_new = alpha * l_i + jnp.sum(p, axis=-1, keepdims=True)
                acc_new = alpha * acc + jnp.dot(p.astype(BF16), v_ref[0, pl.ds(c0, blk), :],
                                                preferred_element_type=F32)
                return m_new, l_new, acc_new

            _, l_f, acc_f = lax.fori_loop(0, qi, jbody, (m0, l0, acc0))
            outs.append(acc_f / l_f)
        o_ref[0, pl.ds(r0, blk), :] = jnp.where(head0, outs[0], outs[1]).astype(BF16)
        return carry

    lax.fori_loop(0, nb, qbody, 0)


def _attention(z3, cosf, sinf, gq2, gk2):
    B, S, _ = z3.shape
    n_pair = N_HEADS // 2
    kq = ATTN_WIDTH // LANES
    return pl.pallas_call(
        _attn_kernel,
        grid=(B, n_pair),
        in_specs=[pl.BlockSpec((1, S, LANES), lambda b, p: (b, 0, p)),
                  pl.BlockSpec((1, S, LANES), lambda b, p: (b, 0, kq + p)),
                  pl.BlockSpec((1, S, LANES), lambda b, p: (b, 0, 2 * kq + p)),
                  pl.BlockSpec((S, LANES), lambda b, p: (0, 0)),
                  pl.BlockSpec((S, LANES), lambda b, p: (0, 0)),
                  pl.BlockSpec((1, LANES), lambda b, p: (0, 0)),
                  pl.BlockSpec((1, LANES), lambda b, p: (0, 0))],
        out_specs=pl.BlockSpec((1, S, LANES), lambda b, p: (b, 0, p)),
        out_shape=jax.ShapeDtypeStruct((B, S, ATTN_WIDTH), BF16),
        scratch_shapes=[pltpu.VMEM((2, S, LANES), BF16),
                        pltpu.VMEM((LANES, LANES), F32)],
        compiler_params=_cparams("arbitrary", "arbitrary"),
        name="attn",
    )(z3, z3, z3, cosf, sinf, gq2, gk2)


def _post_kernel(x_ref, ya_ref, xb_ref, bg_ref, cg_ref, gta_ref, gtb_ref, ga1_ref, sc2_ref, sh2_ref,
                 cw_ref, cb_ref, wpa_ref, wpb_ref, wo_ref, g2_ref, wr_ref, br_ref, tri_ref,
                 x1_ref, h2_ref, ri_ref, rf_ref, cnt_ref, ubuf_ref, run_ref, *, tiles_per_seq):
    i = pl.program_id(0)
    tm = x_ref.shape[0]
    halo = 8

    @pl.when(i == 0)
    def _():
        run_ref[...] = jnp.zeros_like(run_ref)

    @pl.when(i % tiles_per_seq == 0)
    def _():
        ubuf_ref[0:halo, :] = jnp.zeros((halo, CONV_WIDTH), F32)

    u = cg_ref[...].astype(F32) * xb_ref[...].astype(F32)
    ubuf_ref[halo:halo + tm, :] = u
    cw = cw_ref[...]
    conv = (cw[0:1, :] * ubuf_ref[halo - 2:halo - 2 + tm, :]
            + cw[1:2, :] * ubuf_ref[halo - 1:halo - 1 + tm, :]
            + cw[2:3, :] * u)
    ubuf_ref[0:halo, :] = ubuf_ref[tm:tm + halo, :]
    y_b = (bg_ref[...].astype(F32) * (conv + cb_ref[...])).astype(BF16)

    pa = jnp.dot(ya_ref[...], wpa_ref[...], preferred_element_type=F32)
    pb = jnp.dot(y_b, wpb_ref[...], preferred_element_type=F32)
    merged = (jax.nn.sigmoid(gta_ref[...].astype(F32)) * pa
              + jax.nn.sigmoid(gtb_ref[...].astype(F32)) * pb).astype(BF16)
    x1 = x_ref[...] + ga1_ref[0] * jnp.dot(merged, wo_ref[...], preferred_element_type=F32)
    x1_ref[...] = x1

    ms = jnp.mean(x1 * x1, axis=-1, keepdims=True)
    h2 = x1 * lax.rsqrt(ms + EPS) * g2_ref[...]
    h2 = h2 * (1.0 + sc2_ref[0]) + sh2_ref[0]
    h2_ref[...] = h2.astype(BF16)

    logit = jnp.dot(h2, wr_ref[...], preferred_element_type=F32, precision=HIGHEST) + br_ref[...]
    lane = lax.broadcasted_iota(jnp.int32, (tm, LANES), 1)
    big = 1 << 20
    gmask = lane < N_GROUPS
    gl = jnp.where(gmask, logit, -jnp.inf)
    gmax = jnp.max(gl, axis=-1, keepdims=True)
    g_idx = jnp.min(jnp.where(gl == gmax, lane, big), axis=-1, keepdims=True)
    g_w = 1.0 / jnp.sum(jnp.exp(gl - gmax), axis=-1, keepdims=True)
    e_lo = N_GROUPS + EXPERTS_PER_GROUP * g_idx
    emask = (lane >= e_lo) & (lane < e_lo + EXPERTS_PER_GROUP)
    el = jnp.where(emask, logit, -jnp.inf)
    v0 = jnp.max(el, axis=-1, keepdims=True)
    i0 = jnp.min(jnp.where(el == v0, lane, big), axis=-1, keepdims=True)
    el = jnp.where(lane == i0, -jnp.inf, el)
    v1 = jnp.max(el, axis=-1, keepdims=True)
    i1 = jnp.min(jnp.where(el == v1, lane, big), axis=-1, keepdims=True)
    t = jnp.exp(v1 - v0)
    w0 = g_w / (1.0 + t)
    w1 = g_w * t / (1.0 + t)
    e0 = i0 - N_GROUPS
    e1 = i1 - N_GROUPS

    oh0 = lane == e0
    oh1 = lane == e1
    oh = jnp.where(oh0 | oh1, 1.0, 0.0)
    before = jnp.dot(tri_ref[...], oh.astype(BF16), preferred_element_type=F32) + run_ref[0:1, :]
    r0 = jnp.sum(jnp.where(oh0, before, 0.0), axis=-1, keepdims=True).astype(jnp.int32)
    r1 = jnp.sum(jnp.where(oh1, before, 0.0), axis=-1, keepdims=True).astype(jnp.int32)
    run_new = run_ref[0:1, :] + jnp.sum(oh, axis=0, keepdims=True)
    run_ref[...] = jnp.broadcast_to(run_new, run_ref.shape)
    cnt_ref[...] = jnp.broadcast_to(run_new, cnt_ref.shape)

    ri_ref[...] = jnp.where(lane == 0, e0, jnp.where(lane == 1, e1, jnp.where(lane == 2, r0, jnp.where(lane == 3, r1, 0))))
    rf_ref[...] = jnp.where(lane == 0, w0, jnp.where(lane == 1, w1, 0.0))


def _post(x2, ya2, z2, ga1, sc2, sh2, conv_w, conv_b, wpa, wpb, wo, g2, wr, br, tri, tiles_per_seq):
    T, D = x2.shape
    tm = ROW_TILE
    cw = CONV_WIDTH
    xcol = 3 * ATTN_WIDTH // cw
    gcol = (3 * ATTN_WIDTH + 3 * cw) // D
    bmap = lambda i: (i // tiles_per_seq, 0, 0)
    const = lambda i: (0, 0)
    return pl.pallas_call(
        functools.partial(_post_kernel, tiles_per_seq=tiles_per_seq),
        grid=(T // tm,),
        in_specs=[pl.BlockSpec((tm, D), lambda i: (i, 0)),
                  pl.BlockSpec((tm, ATTN_WIDTH), lambda i: (i, 0)),
                  pl.BlockSpec((tm, cw), lambda i: (i, xcol)),
                  pl.BlockSpec((tm, cw), lambda i: (i, xcol + 1)),
                  pl.BlockSpec((tm, cw), lambda i: (i, xcol + 2)),
                  pl.BlockSpec((tm, D), lambda i: (i, gcol)),
                  pl.BlockSpec((tm, D), lambda i: (i, gcol + 1)),
                  pl.BlockSpec((1, 1, D), bmap),
                  pl.BlockSpec((1, 1, D), bmap),
                  pl.BlockSpec((1, 1, D), bmap),
                  pl.BlockSpec((CONV_K, cw), const),
                  pl.BlockSpec((1, cw), const),
                  pl.BlockSpec((ATTN_WIDTH, D), const),
                  pl.BlockSpec((cw, D), const),
                  pl.BlockSpec((D, D), const),
                  pl.BlockSpec((1, D), const),
                  pl.BlockSpec((D, LANES), const),
                  pl.BlockSpec((1, LANES), const),
                  pl.BlockSpec((tm, tm), const)],
        out_specs=[pl.BlockSpec((tm, D), lambda i: (i, 0)),
                   pl.BlockSpec((tm, D), lambda i: (i, 0)),
                   pl.BlockSpec((tm, LANES), lambda i: (i, 0)),
                   pl.BlockSpec((tm, LANES), lambda i: (i, 0)),
                   pl.BlockSpec((8, LANES), const)],
        out_shape=[jax.ShapeDtypeStruct((T, D), F32),
                   jax.ShapeDtypeStruct((T, D), BF16),
                   jax.ShapeDtypeStruct((T, LANES), jnp.int32),
                   jax.ShapeDtypeStruct((T, LANES), F32),
                   jax.ShapeDtypeStruct((8, LANES), F32)],
        scratch_shapes=[pltpu.VMEM((tm + 16, cw), F32),
                        pltpu.VMEM((8, LANES), F32)],
        compiler_params=_cparams("arbitrary"),
        name="post",
    )(x2, ya2, z2, z2, z2, z2, z2, ga1, sc2, sh2, conv_w, conv_b, wpa, wpb, wo, g2, wr, br, tri)


def _expert_kernel(be_ref, xs_ref, w1_ref, w3_ref, w2_ref, ys_ref):
    x = xs_ref[...]
    a = jnp.dot(x, w1_ref[0], preferred_element_type=F32)
    b = jnp.dot(x, w3_ref[0], preferred_element_type=F32)
    hid = (a * jax.nn.sigmoid(a) * b).astype(BF16)
    ys_ref[...] = jnp.dot(hid, w2_ref[0], preferred_element_type=F32)


def _experts(blk_expert, xs, w1b, w3b, w2b):
    n_pad, D = xs.shape
    F = w1b.shape[2]
    rb = EXPERT_ROWS
    grid_spec = pltpu.PrefetchScalarGridSpec(
        num_scalar_prefetch=1,
        grid=(n_pad // rb,),
        in_specs=[pl.BlockSpec((rb, D), lambda i, be: (i, 0)),
                  pl.BlockSpec((1, D, F), lambda i, be: (be[i], 0, 0)),
                  pl.BlockSpec((1, D, F), lambda i, be: (be[i], 0, 0)),
                  pl.BlockSpec((1, F, D), lambda i, be: (be[i], 0, 0))],
        out_specs=pl.BlockSpec((rb, D), lambda i, be: (i, 0)),
    )
    return pl.pallas_call(
        _expert_kernel,
        grid_spec=grid_spec,
        out_shape=jax.ShapeDtypeStruct((n_pad, D), F32),
        compiler_params=_cparams("arbitrary"),
        name="experts",
    )(blk_expert, xs, w1b, w3b, w2b)


def _combine_kernel(x1_ref, y0_ref, y1_ref, rf_ref, ga2_ref, o_ref):
    rf = rf_ref[...]
    w0 = rf[:, 0:1]
    w1 = rf[:, 1:2]
    o_ref[...] = x1_ref[...] + ga2_ref[0] * (w0 * y0_ref[...] + w1 * y1_ref[...])


def _combine(x1, y0, y1, rf, ga2, tiles_per_seq):
    T, D = x1.shape
    tm = ROW_TILE
    row = lambda i: (i, 0)
    return pl.pallas_call(
        _combine_kernel,
        grid=(T // tm,),
        in_specs=[pl.BlockSpec((tm, D), row), pl.BlockSpec((tm, D), row), pl.BlockSpec((tm, D), row),
                  pl.BlockSpec((tm, LANES), row),
                  pl.BlockSpec((1, 1, D), lambda i: (i // tiles_per_seq, 0, 0))],
        out_specs=pl.BlockSpec((tm, D), row),
        out_shape=jax.ShapeDtypeStruct((T, D), F32),
        compiler_params=_cparams("arbitrary"),
        name="combine",
    )(x1, y0, y1, rf, ga2)


def _rope_tables(S):
    pos = jnp.arange(S, dtype=F32)
    inv_freq = ROPE_THETA ** (-jnp.arange(0, ROT_DIM, 2, dtype=F32) / ROT_DIM)
    ang = pos[:, None] * inv_freq[None, :]
    cos, sin = jnp.cos(ang), jnp.sin(ang)
    half = ROT_DIM // 2
    ones = jnp.ones((S, HEAD_DIM - ROT_DIM), F32)
    cos_h = jnp.concatenate([cos, cos, ones], axis=1)
    sin_h = jnp.concatenate([-sin, sin, 0.0 * ones], axis=1)
    return jnp.tile(cos_h, (1, LANES // HEAD_DIM)), jnp.tile(sin_h, (1, LANES // HEAD_DIM))


def kernel(x, c, w_ada, b_ada, g_norm1, g_norm2, w_in, g_q, g_k, conv_w, conv_b,
           w_pa, w_pb, w_o, w_rg, b_rg, w_re, b_re, w1, w3, w2):
    B, S, D = x.shape
    T = B * S
    assert S % ROW_TILE == 0 and S % MOBA_BLOCK == 0 and S // MOBA_BLOCK <= LANES - HEAD_DIM
    tiles_per_seq = S // ROW_TILE
    l = 0

    mod = _ada(c, w_ada[l], b_ada[l])
    sh1, sc1, ga1, sh2, sc2, ga2 = [m.reshape(B, 1, D) for m in jnp.split(mod, N_MOD, axis=-1)]

    x2 = x.reshape(T, D)
    z2 = _inproj(x2, g_norm1[l].reshape(1, D), sc1, sh1, w_in[l].astype(BF16), tiles_per_seq)

    cosf, sinf = _rope_tables(S)
    rep = LANES // HEAD_DIM
    ya = _attention(z2.reshape(B, S, -1), cosf, sinf,
                    jnp.tile(g_q[l], rep).reshape(1, LANES), jnp.tile(g_k[l], rep).reshape(1, LANES))

    wr = jnp.zeros((D, LANES), F32).at[:, :N_GROUPS].set(w_rg[l]).at[:, N_GROUPS:N_GROUPS + N_EXPERTS].set(w_re[l])
    br = jnp.zeros((1, LANES), F32).at[0, :N_GROUPS].set(b_rg[l]).at[0, N_GROUPS:N_GROUPS + N_EXPERTS].set(b_re[l])
    tri = (lax.broadcasted_iota(jnp.int32, (ROW_TILE, ROW_TILE), 1)
           < lax.broadcasted_iota(jnp.int32, (ROW_TILE, ROW_TILE), 0)).astype(BF16)
    x1, h2, ri, rf, cnt = _post(x2, ya.reshape(T, ATTN_WIDTH), z2, ga1, sc2, sh2,
                                conv_w[l], conv_b[l].reshape(1, CONV_WIDTH),
                                w_pa[l].astype(BF16), w_pb[l].astype(BF16), w_o[l].astype(BF16),
                                g_norm2[l].reshape(1, D), wr, br, tri, tiles_per_seq)

    rb = EXPERT_ROWS
    counts = cnt[0, :N_EXPERTS].astype(jnp.int32)
    padded = (counts + rb - 1) // rb * rb
    pad_end = jnp.cumsum(padded)
    pad_start = pad_end - padded
    e0, e1, r0, r1 = ri[:, 0], ri[:, 1], ri[:, 2], ri[:, 3]
    dest0 = pad_start[e0] + r0
    dest1 = pad_start[e1] + r1
    n_blocks = -(-T * TOPK_IN_GROUP // rb) + N_EXPERTS
    n_pad = n_blocks * rb
    blk_start = jnp.arange(n_blocks, dtype=jnp.int32) * rb
    blk_expert = jnp.minimum(jnp.sum(pad_end[None, :] <= blk_start[:, None], axis=-1), N_EXPERTS - 1).astype(jnp.int32)

    tok_ids = jnp.arange(T, dtype=jnp.int32)
    tok = jnp.full((n_pad,), T, jnp.int32).at[dest0].set(tok_ids).at[dest1].set(tok_ids)
    h2_ext = jnp.concatenate([h2, jnp.zeros((1, D), BF16)], axis=0)
    xs = h2_ext[tok]
    ys = _experts(blk_expert, xs, w1[l].astype(BF16), w3[l].astype(BF16), w2[l].astype(BF16))
    out = _combine(x1, ys[dest0], ys[dest1], rf, ga2, tiles_per_seq)
    return out.reshape(B, S, D)
```

```python
import functools

import jax
import jax.numpy as jnp
from jax import lax
from jax.experimental import pallas as pl
from jax.experimental.pallas import tpu as pltpu

F32 = jnp.float32
BF16 = jnp.bfloat16
HIGHEST = lax.Precision.HIGHEST

N_HEADS = 8
HEAD_DIM = 64
ATTN_WIDTH = N_HEADS * HEAD_DIM
CONV_WIDTH = 512
CONV_K = 3
MOBA_BLOCK = 256
MOBA_TOPK = 3
ROPE_THETA = 500000.0
ROT_DIM = HEAD_DIM // 4
N_GROUPS = 4
EXPERTS_PER_GROUP = 8
N_EXPERTS = N_GROUPS * EXPERTS_PER_GROUP
TOPK_IN_GROUP = 2
N_MOD = 6
EPS = 1e-6

LANES = 128
NEG = -1e30
ROW_TILE = 512
EXPERT_ROWS = 256
VMEM_LIMIT = 56 * 1024 * 1024


def _cparams(*sem):
    return pltpu.CompilerParams(dimension_semantics=sem, vmem_limit_bytes=VMEM_LIMIT)


def _ada_kernel(c_ref, w_ref, b_ref, o_ref):
    c = c_ref[...]
    a = c * jax.nn.sigmoid(c)
    o_ref[...] = jnp.dot(a, w_ref[...], preferred_element_type=F32, precision=HIGHEST) + b_ref[...]


def _ada(c, w_ada, b_ada):
    B, D = c.shape
    N = w_ada.shape[1]
    tn = 1536
    return pl.pallas_call(
        _ada_kernel,
        grid=(N // tn,),
        in_specs=[pl.BlockSpec((B, D), lambda j: (0, 0)),
                  pl.BlockSpec((D, tn), lambda j: (0, j)),
                  pl.BlockSpec((1, tn), lambda j: (0, j))],
        out_specs=pl.BlockSpec((B, tn), lambda j: (0, j)),
        out_shape=jax.ShapeDtypeStruct((B, N), F32),
        compiler_params=_cparams("arbitrary"),
        name="ada",
    )(c, w_ada, b_ada.reshape(1, N))


def _inproj_kernel(x_ref, g_ref, sc_ref, sh_ref, w_ref, z_ref, *, n_chunk):
    x = x_ref[...]
    ms = jnp.mean(x * x, axis=-1, keepdims=True)
    y = x * lax.rsqrt(ms + EPS) * g_ref[...]
    h = (y * (1.0 + sc_ref[0]) + sh_ref[0]).astype(BF16)
    for n in range(0, z_ref.shape[1], n_chunk):
        z_ref[:, n:n + n_chunk] = jnp.dot(h, w_ref[:, n:n + n_chunk],
                                          preferred_element_type=F32).astype(BF16)


def _inproj(x2, g1, sc1, sh1, w_in_bf, tiles_per_seq):
    T, D = x2.shape
    N = w_in_bf.shape[1]
    tm = ROW_TILE
    bmap = lambda i: (i // tiles_per_seq, 0, 0)
    return pl.pallas_call(
        functools.partial(_inproj_kernel, n_chunk=512),
        grid=(T // tm,),
        in_specs=[pl.BlockSpec((tm, D), lambda i: (i, 0)),
                  pl.BlockSpec((1, D), lambda i: (0, 0)),
                  pl.BlockSpec((1, 1, D), bmap),
                  pl.BlockSpec((1, 1, D), bmap),
                  pl.BlockSpec((D, N), lambda i: (0, 0))],
        out_specs=pl.BlockSpec((tm, N), lambda i: (i, 0)),
        out_shape=jax.ShapeDtypeStruct((T, N), BF16),
        compiler_params=_cparams("arbitrary"),
        name="inproj",
    )(x2, g1, sc1, sh1, w_in_bf)


def _fold_rows(x, op):
    parts = [x[r:r + 8] for r in range(0, x.shape[0], 8)]
    while len(parts) > 1:
        parts = [op(parts[i], parts[i + 1]) for i in range(0, len(parts) - 1, 2)] + (
            [parts[-1]] if len(parts) % 2 else [])
    return parts[0]


def _attn_kernel(q_ref, k_ref, v_ref, cos_ref, sin_ref, gq_ref, gk_ref, o_ref,
                 kaug_ref, vt_ref, kmp_ref, s_ref):
    S = q_ref.shape[1]
    blk = MOBA_BLOCK
    nb = S // blk
    nbp = kmp_ref.shape[0]
    lane = lax.broadcasted_iota(jnp.int32, (blk, LANES), 1)
    head0 = lane < HEAD_DIM
    rot_lo = (lane & (HEAD_DIM - 1)) < (ROT_DIM // 2)

    def norm_rope(xb, g, r0):
        x = xb.astype(F32)
        sq = x * x
        s0 = jnp.sum(jnp.where(head0, sq, 0.0), axis=-1, keepdims=True)
        s1 = jnp.sum(jnp.where(head0, 0.0, sq), axis=-1, keepdims=True)
        inv = jnp.where(head0, lax.rsqrt(s0 * (1.0 / HEAD_DIM) + EPS), lax.rsqrt(s1 * (1.0 / HEAD_DIM) + EPS))
        y = x * inv * g
        half = ROT_DIM // 2
        rot = jnp.where(rot_lo, pltpu.roll(y, LANES - half, 1), pltpu.roll(y, half, 1))
        return y * cos_ref[pl.ds(r0, blk), :] + rot * sin_ref[pl.ds(r0, blk), :]

    kmp_ref[...] = jnp.zeros_like(kmp_ref)
    ones_row = jnp.where(lax.broadcasted_iota(jnp.int32, (16, blk), 0) == 0, 1.0, 0.0).astype(BF16)

    def kbody(j, carry):
        r0 = pl.multiple_of(j * blk, blk)
        kr = norm_rope(k_ref[0, pl.ds(r0, blk), :], gk_ref[...], r0)
        kmp_ref[pl.ds(j, 1), :] = jnp.sum(kr, axis=0, keepdims=True) * (1.0 / blk)
        onehot = jnp.where(lane - HEAD_DIM == j, 1.0, 0.0)
        kaug_ref[0, j] = jnp.where(head0, kr, onehot).astype(BF16)
        kaug_ref[1, j] = jnp.where(head0, pltpu.roll(kr, HEAD_DIM, 1), onehot).astype(BF16)
        vT = v_ref[0, pl.ds(r0, blk), :].astype(F32).T
        for h in range(2):
            vt_ref[h, j, 0:HEAD_DIM, :] = vT[h * HEAD_DIM:(h + 1) * HEAD_DIM].astype(BF16)
            vt_ref[h, j, HEAD_DIM:HEAD_DIM + 16, :] = ones_row
        return carry

    lax.fori_loop(0, nb, kbody, 0)

    key_i = lax.broadcasted_iota(jnp.int32, (blk, blk), 0)
    qry_i = lax.broadcasted_iota(jnp.int32, (blk, blk), 1)
    causal = key_i <= qry_i
    rowf = lax.broadcasted_iota(jnp.int32, (nbp, blk), 0).astype(F32)
    row128 = lax.broadcasted_iota(jnp.int32, (LANES, blk), 0)
    q_scale = (HEAD_DIM ** -0.5) * 1.4426950408889634

    def qbody(qi, carry):
        r0 = pl.multiple_of(qi * blk, blk)
        qT = norm_rope(q_ref[0, pl.ds(r0, blk), :], gq_ref[...], r0).T
        qif = qi.astype(F32)
        qa_own, qa_past = [], []
        for h in range(2):
            qmT = jnp.where(row128 < HEAD_DIM, qT, 0.0) if h == 0 else jnp.where(row128 < HEAD_DIM, 0.0, qT)
            gate = jnp.dot(kmp_ref[...], qmT, precision=HIGHEST, preferred_element_type=F32)
            g = jnp.where(rowf < qif, gate, -jnp.inf)
            keep = jnp.zeros((nbp, blk), jnp.bool_)
            for r in range(MOBA_TOPK):
                m = jnp.max(g, axis=0, keepdims=True)
                idx = jnp.min(jnp.where(g == m, rowf, 1e9), axis=0, keepdims=True)
                pick = (rowf == idx) & (qi > r)
                keep = keep | pick
                g = jnp.where(pick, -jnp.inf, g)
            bias = jnp.where(keep, 0.0, NEG)
            qs = qT[h * HEAD_DIM:(h + 1) * HEAD_DIM] * q_scale
            pad = jnp.zeros((LANES - HEAD_DIM - nbp, blk), F32)
            qa_past.append(jnp.concatenate([qs, bias, pad], axis=0).astype(BF16))
            qa_own.append(jnp.concatenate([qs, jnp.zeros((LANES - HEAD_DIM, blk), F32)], axis=0).astype(BF16))

        mx0 = []
        for h in range(2):
            sT = jnp.dot(kaug_ref[h, qi], qa_own[h], preferred_element_type=F32)
            sT = jnp.where(causal, sT, NEG)
            s_ref[h, nb] = sT
            mx0.append(_fold_rows(sT, jnp.maximum))
        n_pair = (qi + 1) // 2

        def p1(p, mx):
            mx = list(mx)
            for u in range(2):
                j = 2 * p + u
                for h in range(2):
                    sT = jnp.dot(kaug_ref[h, j], qa_past[h], preferred_element_type=F32)
                    s_ref[h, j] = sT
                    mx[h] = jnp.maximum(mx[h], _fold_rows(sT, jnp.maximum))
            return tuple(mx)

        mx = lax.fori_loop(0, n_pair, p1, tuple(mx0))
        mcol = [jnp.max(mx[h], axis=0, keepdims=True) for h in range(2)]

        acc0 = []
        for h in range(2):
            pT = jnp.exp2(s_ref[h, nb] - mcol[h]).astype(BF16)
            acc0.append(jnp.dot(vt_ref[h, qi], pT, preferred_element_type=F32))

        def p2(p, acc):
            acc = list(acc)
            for u in range(2):
                j = 2 * p + u
                for h in range(2):
                    pT = jnp.exp2(s_ref[h, j] - mcol[h]).astype(BF16)
                    acc[h] = acc[h] + jnp.dot(vt_ref[h, j], pT, preferred_element_type=F32)
            return tuple(acc)

        acc = lax.fori_loop(0, n_pair, p2, tuple(acc0))
        outT = jnp.concatenate([acc[h][0:HEAD_DIM] / acc[h][HEAD_DIM:HEAD_DIM + 1] for h in range(2)], axis=0)
        o_ref[0, pl.ds(r0, blk), :] = outT.T.astype(BF16)
        return carry

    lax.fori_loop(0, nb, qbody, 0)


def _attention(z3, cosf, sinf, gq2, gk2):
    B, S, _ = z3.shape
    n_pair = N_HEADS // 2
    kq = ATTN_WIDTH // LANES
    nb = S // MOBA_BLOCK
    nbp = -(-nb // 16) * 16
    return pl.pallas_call(
        _attn_kernel,
        grid=(B, n_pair),
        in_specs=[pl.BlockSpec((1, S, LANES), lambda b, p: (b, 0, p)),
                  pl.BlockSpec((1, S, LANES), lambda b, p: (b, 0, kq + p)),
                  pl.BlockSpec((1, S, LANES), lambda b, p: (b, 0, 2 * kq + p)),
                  pl.BlockSpec((S, LANES), lambda b, p: (0, 0)),
                  pl.BlockSpec((S, LANES), lambda b, p: (0, 0)),
                  pl.BlockSpec((1, LANES), lambda b, p: (0, 0)),
                  pl.BlockSpec((1, LANES), lambda b, p: (0, 0))],
        out_specs=pl.BlockSpec((1, S, LANES), lambda b, p: (b, 0, p)),
        out_shape=jax.ShapeDtypeStruct((B, S, ATTN_WIDTH), BF16),
        scratch_shapes=[pltpu.VMEM((2, nb, MOBA_BLOCK, LANES), BF16),
                        pltpu.VMEM((2, nb, HEAD_DIM + 16, MOBA_BLOCK), BF16),
                        pltpu.VMEM((nbp, LANES), F32),
                        pltpu.VMEM((2, nb + 1, MOBA_BLOCK, MOBA_BLOCK), F32)],
        compiler_params=_cparams("arbitrary", "arbitrary"),
        name="attn",
    )(z3, z3, z3, cosf, sinf, gq2, gk2)


def _post_kernel(x_ref, ya_ref, xb_ref, bg_ref, cg_ref, gta_ref, gtb_ref, ga1_ref, sc2_ref, sh2_ref,
                 cw_ref, cb_ref, wpa_ref, wpb_ref, wo_ref, g2_ref, wr_ref, br_ref, tri_ref,
                 x1_ref, h2_ref, ri_ref, rf_ref, cnt_ref, ubuf_ref, run_ref, *, tiles_per_seq):
    i = pl.program_id(0)
    tm = x_ref.shape[0]
    halo = 8

    @pl.when(i == 0)
    def _():
        run_ref[...] = jnp.zeros_like(run_ref)

    @pl.when(i % tiles_per_seq == 0)
    def _():
        ubuf_ref[0:halo, :] = jnp.zeros((halo, CONV_WIDTH), F32)

    u = cg_ref[...].astype(F32) * xb_ref[...].astype(F32)
    ubuf_ref[halo:halo + tm, :] = u
    cw = cw_ref[...]
    conv = (cw[0:1, :] * ubuf_ref[halo - 2:halo - 2 + tm, :]
            + cw[1:2, :] * ubuf_ref[halo - 1:halo - 1 + tm, :]
            + cw[2:3, :] * u)
    ubuf_ref[0:halo, :] = ubuf_ref[tm:tm + halo, :]
    y_b = (bg_ref[...].astype(F32) * (conv + cb_ref[...])).astype(BF16)

    pa = jnp.dot(ya_ref[...], wpa_ref[...], preferred_element_type=F32)
    pb = jnp.dot(y_b, wpb_ref[...], preferred_element_type=F32)
    merged = (jax.nn.sigmoid(gta_ref[...].astype(F32)) * pa
              + jax.nn.sigmoid(gtb_ref[...].astype(F32)) * pb).astype(BF16)
    x1 = x_ref[...] + ga1_ref[0] * jnp.dot(merged, wo_ref[...], preferred_element_type=F32)
    x1_ref[...] = x1

    ms = jnp.mean(x1 * x1, axis=-1, keepdims=True)
    h2 = x1 * lax.rsqrt(ms + EPS) * g2_ref[...]
    h2 = h2 * (1.0 + sc2_ref[0]) + sh2_ref[0]
    h2_ref[...] = h2.astype(BF16)

    logit = jnp.dot(h2, wr_ref[...], preferred_element_type=F32, precision=HIGHEST) + br_ref[...]
    lane = lax.broadcasted_iota(jnp.int32, (tm, LANES), 1)
    big = 1 << 20
    gmask = lane < N_GROUPS
    gl = jnp.where(gmask, logit, -jnp.inf)
    gmax = jnp.max(gl, axis=-1, keepdims=True)
    g_idx = jnp.min(jnp.where(gl == gmax, lane, big), axis=-1, keepdims=True)
    g_w = 1.0 / jnp.sum(jnp.exp(gl - gmax), axis=-1, keepdims=True)
    e_lo = N_GROUPS + EXPERTS_PER_GROUP * g_idx
    emask = (lane >= e_lo) & (lane < e_lo + EXPERTS_PER_GROUP)
    el = jnp.where(emask, logit, -jnp.inf)
    v0 = jnp.max(el, axis=-1, keepdims=True)
    i0 = jnp.min(jnp.where(el == v0, lane, big), axis=-1, keepdims=True)
    el = jnp.where(lane == i0, -jnp.inf, el)
    v1 = jnp.max(el, axis=-1, keepdims=True)
    i1 = jnp.min(jnp.where(el == v1, lane, big), axis=-1, keepdims=True)
    t = jnp.exp(v1 - v0)
    w0 = g_w / (1.0 + t)
    w1 = g_w * t / (1.0 + t)
    e0 = i0 - N_GROUPS
    e1 = i1 - N_GROUPS

    oh0 = lane == e0
    oh1 = lane == e1
    oh = jnp.where(oh0 | oh1, 1.0, 0.0)
    before = jnp.dot(tri_ref[...], oh.astype(BF16), preferred_element_type=F32) + run_ref[0:1, :]
    r0 = jnp.sum(jnp.where(oh0, before, 0.0), axis=-1, keepdims=True).astype(jnp.int32)
    r1 = jnp.sum(jnp.where(oh1, before, 0.0), axis=-1, keepdims=True).astype(jnp.int32)
    run_new = run_ref[0:1, :] + jnp.sum(oh, axis=0, keepdims=True)
    run_ref[...] = jnp.broadcast_to(run_new, run_ref.shape)
    cnt_ref[...] = jnp.broadcast_to(run_new, cnt_ref.shape)

    ri_ref[...] = jnp.where(lane == 0, e0, jnp.where(lane == 1, e1, jnp.where(lane == 2, r0, jnp.where(lane == 3, r1, 0))))
    rf_ref[...] = jnp.where(lane == 0, w0, jnp.where(lane == 1, w1, 0.0))


def _post(x2, ya2, z2, ga1, sc2, sh2, conv_w, conv_b, wpa, wpb, wo, g2, wr, br, tri, tiles_per_seq):
    T, D = x2.shape
    tm = ROW_TILE
    cw = CONV_WIDTH
    xcol = 3 * ATTN_WIDTH // cw
    gcol = (3 * ATTN_WIDTH + 3 * cw) // D
    bmap = lambda i: (i // tiles_per_seq, 0, 0)
    const = lambda i: (0, 0)
    return pl.pallas_call(
        functools.partial(_post_kernel, tiles_per_seq=tiles_per_seq),
        grid=(T // tm,),
        in_specs=[pl.BlockSpec((tm, D), lambda i: (i, 0)),
                  pl.BlockSpec((tm, ATTN_WIDTH), lambda i: (i, 0)),
                  pl.BlockSpec((tm, cw), lambda i: (i, xcol)),
                  pl.BlockSpec((tm, cw), lambda i: (i, xcol + 1)),
                  pl.BlockSpec((tm, cw), lambda i: (i, xcol + 2)),
                  pl.BlockSpec((tm, D), lambda i: (i, gcol)),
                  pl.BlockSpec((tm, D), lambda i: (i, gcol + 1)),
                  pl.BlockSpec((1, 1, D), bmap),
                  pl.BlockSpec((1, 1, D), bmap),
                  pl.BlockSpec((1, 1, D), bmap),
                  pl.BlockSpec((CONV_K, cw), const),
                  pl.BlockSpec((1, cw), const),
                  pl.BlockSpec((ATTN_WIDTH, D), const),
                  pl.BlockSpec((cw, D), const),
                  pl.BlockSpec((D, D), const),
                  pl.BlockSpec((1, D), const),
                  pl.BlockSpec((D, LANES), const),
                  pl.BlockSpec((1, LANES), const),
                  pl.BlockSpec((tm, tm), const)],
        out_specs=[pl.BlockSpec((tm, D), lambda i: (i, 0)),
                   pl.BlockSpec((tm, D), lambda i: (i, 0)),
                   pl.BlockSpec((tm, LANES), lambda i: (i, 0)),
                   pl.BlockSpec((tm, LANES), lambda i: (i, 0)),
                   pl.BlockSpec((8, LANES), const)],
        out_shape=[jax.ShapeDtypeStruct((T, D), F32),
                   jax.ShapeDtypeStruct((T, D), BF16),
                   jax.ShapeDtypeStruct((T, LANES), jnp.int32),
                   jax.ShapeDtypeStruct((T, LANES), F32),
                   jax.ShapeDtypeStruct((8, LANES), F32)],
        scratch_shapes=[pltpu.VMEM((tm + 16, cw), F32),
                        pltpu.VMEM((8, LANES), F32)],
        compiler_params=_cparams("arbitrary"),
        name="post",
    )(x2, ya2, z2, z2, z2, z2, z2, ga1, sc2, sh2, conv_w, conv_b, wpa, wpb, wo, g2, wr, br, tri)


def _expert_kernel(be_ref, xs_ref, w1_ref, w3_ref, w2_ref, ys_ref):
    x = xs_ref[...]
    a = jnp.dot(x, w1_ref[0], preferred_element_type=F32)
    b = jnp.dot(x, w3_ref[0], preferred_element_type=F32)
    hid = (a * jax.nn.sigmoid(a) * b).astype(BF16)
    ys_ref[...] = jnp.dot(hid, w2_ref[0], preferred_element_type=F32)


def _experts(blk_expert, xs, w1b, w3b, w2b):
    n_pad, D = xs.shape
    F = w1b.shape[2]
    rb = EXPERT_ROWS
    grid_spec = pltpu.PrefetchScalarGridSpec(
        num_scalar_prefetch=1,
        grid=(n_pad // rb,),
        in_specs=[pl.BlockSpec((rb, D), lambda i, be: (i, 0)),
                  pl.BlockSpec((1, D, F), lambda i, be: (be[i], 0, 0)),
                  pl.BlockSpec((1, D, F), lambda i, be: (be[i], 0, 0)),
                  pl.BlockSpec((1, F, D), lambda i, be: (be[i], 0, 0))],
        out_specs=pl.BlockSpec((rb, D), lambda i, be: (i, 0)),
    )
    return pl.pallas_call(
        _expert_kernel,
        grid_spec=grid_spec,
        out_shape=jax.ShapeDtypeStruct((n_pad, D), F32),
        compiler_params=_cparams("arbitrary"),
        name="experts",
    )(blk_expert, xs, w1b, w3b, w2b)


def _combine_kernel(x1_ref, y0_ref, y1_ref, rf_ref, ga2_ref, o_ref):
    rf = rf_ref[...]
    w0 = rf[:, 0:1]
    w1 = rf[:, 1:2]
    o_ref[...] = x1_ref[...] + ga2_ref[0] * (w0 * y0_ref[...] + w1 * y1_ref[...])


def _combine(x1, y0, y1, rf, ga2, tiles_per_seq):
    T, D = x1.shape
    tm = ROW_TILE
    row = lambda i: (i, 0)
    return pl.pallas_call(
        _combine_kernel,
        grid=(T // tm,),
        in_specs=[pl.BlockSpec((tm, D), row), pl.BlockSpec((tm, D), row), pl.BlockSpec((tm, D), row),
                  pl.BlockSpec((tm, LANES), row),
                  pl.BlockSpec((1, 1, D), lambda i: (i // tiles_per_seq, 0, 0))],
        out_specs=pl.BlockSpec((tm, D), row),
        out_shape=jax.ShapeDtypeStruct((T, D), F32),
        compiler_params=_cparams("arbitrary"),
        name="combine",
    )(x1, y0, y1, rf, ga2)


def _rope_tables(S):
    pos = jnp.arange(S, dtype=F32)
    inv_freq = ROPE_THETA ** (-jnp.arange(0, ROT_DIM, 2, dtype=F32) / ROT_DIM)
    ang = pos[:, None] * inv_freq[None, :]
    cos, sin = jnp.cos(ang), jnp.sin(ang)
    half = ROT_DIM // 2
    ones = jnp.ones((S, HEAD_DIM - ROT_DIM), F32)
    cos_h = jnp.concatenate([cos, cos, ones], axis=1)
    sin_h = jnp.concatenate([-sin, sin, 0.0 * ones], axis=1)
    return jnp.tile(cos_h, (1, LANES // HEAD_DIM)), jnp.tile(sin_h, (1, LANES // HEAD_DIM))


def kernel(x, c, w_ada, b_ada, g_norm1, g_norm2, w_in, g_q, g_k, conv_w, conv_b,
           w_pa, w_pb, w_o, w_rg, b_rg, w_re, b_re, w1, w3, w2):
    B, S, D = x.shape
    T = B * S
    assert S % ROW_TILE == 0 and S % MOBA_BLOCK == 0 and S // MOBA_BLOCK <= LANES - HEAD_DIM
    tiles_per_seq = S // ROW_TILE
    l = 0

    mod = _ada(c, w_ada[l], b_ada[l])
    sh1, sc1, ga1, sh2, sc2, ga2 = [m.reshape(B, 1, D) for m in jnp.split(mod, N_MOD, axis=-1)]

    x2 = x.reshape(T, D)
    z2 = _inproj(x2, g_norm1[l].reshape(1, D), sc1, sh1, w_in[l].astype(BF16), tiles_per_seq)

    cosf, sinf = _rope_tables(S)
    rep = LANES // HEAD_DIM
    ya = _attention(z2.reshape(B, S, -1), cosf, sinf,
                    jnp.tile(g_q[l], rep).reshape(1, LANES), jnp.tile(g_k[l], rep).reshape(1, LANES))

    wr = jnp.zeros((D, LANES), F32).at[:, :N_GROUPS].set(w_rg[l]).at[:, N_GROUPS:N_GROUPS + N_EXPERTS].set(w_re[l])
    br = jnp.zeros((1, LANES), F32).at[0, :N_GROUPS].set(b_rg[l]).at[0, N_GROUPS:N_GROUPS + N_EXPERTS].set(b_re[l])
    tri = (lax.broadcasted_iota(jnp.int32, (ROW_TILE, ROW_TILE), 1)
           < lax.broadcasted_iota(jnp.int32, (ROW_TILE, ROW_TILE), 0)).astype(BF16)
    x1, h2, ri, rf, cnt = _post(x2, ya.reshape(T, ATTN_WIDTH), z2, ga1, sc2, sh2,
                                conv_w[l], conv_b[l].reshape(1, CONV_WIDTH),
                                w_pa[l].astype(BF16), w_pb[l].astype(BF16), w_o[l].astype(BF16),
                                g_norm2[l].reshape(1, D), wr, br, tri, tiles_per_seq)

    rb = EXPERT_ROWS
    counts = cnt[0, :N_EXPERTS].astype(jnp.int32)
    padded = (counts + rb - 1) // rb * rb
    pad_end = jnp.cumsum(padded)
    pad_start = pad_end - padded
    e0, e1, r0, r1 = ri[:, 0], ri[:, 1], ri[:, 2], ri[:, 3]
    dest0 = pad_start[e0] + r0
    dest1 = pad_start[e1] + r1
    n_blocks = -(-T * TOPK_IN_GROUP // rb) + N_EXPERTS
    n_pad = n_blocks * rb
    blk_start = jnp.arange(n_blocks, dtype=jnp.int32) * rb
    blk_expert = jnp.minimum(jnp.sum(pad_end[None, :] <= blk_start[:, None], axis=-1), N_EXPERTS - 1).astype(jnp.int32)

    tok_ids = jnp.arange(T, dtype=jnp.int32)
    tok = jnp.full((n_pad,), T, jnp.int32).at[dest0].set(tok_ids).at[dest1].set(tok_ids)
    h2_ext = jnp.concatenate([h2, jnp.zeros((1, D), BF16)], axis=0)
    xs = h2_ext[tok]
    ys = _experts(blk_expert, xs, w1[l].astype(BF16), w3[l].astype(BF16), w2[l].astype(BF16))
    out = _combine(x1, ys[dest0], ys[dest1], rf, ga2, tiles_per_seq)
    return out.reshape(B, S, D)
```

```python
import functools

import jax
import jax.numpy as jnp
from jax import lax
from jax.experimental import pallas as pl
from jax.experimental.pallas import tpu as pltpu

F32 = jnp.float32
BF16 = jnp.bfloat16
HIGHEST = lax.Precision.HIGHEST

N_HEADS = 8
HEAD_DIM = 64
ATTN_WIDTH = N_HEADS * HEAD_DIM
CONV_WIDTH = 512
CONV_K = 3
MOBA_BLOCK = 256
MOBA_TOPK = 3
ROPE_THETA = 500000.0
ROT_DIM = HEAD_DIM // 4
N_GROUPS = 4
EXPERTS_PER_GROUP = 8
N_EXPERTS = N_GROUPS * EXPERTS_PER_GROUP
TOPK_IN_GROUP = 2
N_MOD = 6
EPS = 1e-6

LANES = 128
NEG = -1e30
ROW_TILE = 512
EXPERT_ROWS = 256
VMEM_LIMIT = 56 * 1024 * 1024


def _cparams(*sem):
    return pltpu.CompilerParams(dimension_semantics=sem, vmem_limit_bytes=VMEM_LIMIT)


def _ada_kernel(c_ref, w_ref, b_ref, o_ref):
    c = c_ref[...]
    a = c * jax.nn.sigmoid(c)
    o_ref[...] = jnp.dot(a, w_ref[...], preferred_element_type=F32, precision=HIGHEST) + b_ref[...]


def _ada(c, w_ada, b_ada):
    B, D = c.shape
    N = w_ada.shape[1]
    tn = 1536
    return pl.pallas_call(
        _ada_kernel,
        grid=(N // tn,),
        in_specs=[pl.BlockSpec((B, D), lambda j: (0, 0)),
                  pl.BlockSpec((D, tn), lambda j: (0, j)),
                  pl.BlockSpec((1, tn), lambda j: (0, j))],
        out_specs=pl.BlockSpec((B, tn), lambda j: (0, j)),
        out_shape=jax.ShapeDtypeStruct((B, N), F32),
        compiler_params=_cparams("arbitrary"),
        name="ada",
    )(c, w_ada, b_ada.reshape(1, N))


def _inproj_kernel(x_ref, g_ref, sc_ref, sh_ref, w_ref, z_ref, *, n_chunk):
    x = x_ref[...]
    ms = jnp.mean(x * x, axis=-1, keepdims=True)
    y = x * lax.rsqrt(ms + EPS) * g_ref[...]
    h = (y * (1.0 + sc_ref[0]) + sh_ref[0]).astype(BF16)
    for n in range(0, z_ref.shape[1], n_chunk):
        z_ref[:, n:n + n_chunk] = jnp.dot(h, w_ref[:, n:n + n_chunk],
                                          preferred_element_type=F32).astype(BF16)


def _inproj(x2, g1, sc1, sh1, w_in_bf, tiles_per_seq):
    T, D = x2.shape
    N = w_in_bf.shape[1]
    tm = ROW_TILE
    bmap = lambda i: (i // tiles_per_seq, 0, 0)
    return pl.pallas_call(
        functools.partial(_inproj_kernel, n_chunk=512),
        grid=(T // tm,),
        in_specs=[pl.BlockSpec((tm, D), lambda i: (i, 0)),
                  pl.BlockSpec((1, D), lambda i: (0, 0)),
                  pl.BlockSpec((1, 1, D), bmap),
                  pl.BlockSpec((1, 1, D), bmap),
                  pl.BlockSpec((D, N), lambda i: (0, 0))],
        out_specs=pl.BlockSpec((tm, N), lambda i: (i, 0)),
        out_shape=jax.ShapeDtypeStruct((T, N), BF16),
        compiler_params=_cparams("arbitrary"),
        name="inproj",
    )(x2, g1, sc1, sh1, w_in_bf)


def _fold_rows(x, op):
    parts = [x[r:r + 8] for r in range(0, x.shape[0], 8)]
    while len(parts) > 1:
        parts = [op(parts[i], parts[i + 1]) for i in range(0, len(parts) - 1, 2)] + (
            [parts[-1]] if len(parts) % 2 else [])
    return parts[0]


def _attn_kernel(q_ref, k_ref, v_ref, cos_ref, sin_ref, gq_ref, gk_ref, o_ref,
                 kaug_ref, vt_ref, kmp_ref, s_ref):
    S = q_ref.shape[1]
    blk = MOBA_BLOCK
    nb = S // blk
    nbp = kmp_ref.shape[0]
    lane = lax.broadcasted_iota(jnp.int32, (blk, LANES), 1)
    head0 = lane < HEAD_DIM
    rot_lo = (lane & (HEAD_DIM - 1)) < (ROT_DIM // 2)

    def norm_rope(xb, g, r0):
        x = xb.astype(F32)
        sq = x * x
        s0 = jnp.sum(jnp.where(head0, sq, 0.0), axis=-1, keepdims=True)
        s1 = jnp.sum(jnp.where(head0, 0.0, sq), axis=-1, keepdims=True)
        inv = jnp.where(head0, lax.rsqrt(s0 * (1.0 / HEAD_DIM) + EPS), lax.rsqrt(s1 * (1.0 / HEAD_DIM) + EPS))
        y = x * inv * g
        half = ROT_DIM // 2
        rot = jnp.where(rot_lo, pltpu.roll(y, LANES - half, 1), pltpu.roll(y, half, 1))
        return y * cos_ref[pl.ds(r0, blk), :] + rot * sin_ref[pl.ds(r0, blk), :]

    kmp_ref[...] = jnp.zeros_like(kmp_ref)
    ones_row = jnp.where(lax.broadcasted_iota(jnp.int32, (16, blk), 0) == 0, 1.0, 0.0).astype(BF16)

    def kbody(j, carry):
        r0 = pl.multiple_of(j * blk, blk)
        kr = norm_rope(k_ref[0, pl.ds(r0, blk), :], gk_ref[...], r0)
        kmp_ref[pl.ds(j, 1), :] = jnp.sum(kr, axis=0, keepdims=True) * (1.0 / blk)
        onehot = jnp.where(lane - HEAD_DIM == j, 1.0, 0.0)
        kaug_ref[0, j] = jnp.where(head0, kr, onehot).astype(BF16)
        kaug_ref[1, j] = jnp.where(head0, pltpu.roll(kr, HEAD_DIM, 1), onehot).astype(BF16)
        vT = v_ref[0, pl.ds(r0, blk), :].astype(F32).T
        for h in range(2):
            vt_ref[h, j, 0:HEAD_DIM, :] = vT[h * HEAD_DIM:(h + 1) * HEAD_DIM].astype(BF16)
            vt_ref[h, j, HEAD_DIM:HEAD_DIM + 16, :] = ones_row
        return carry

    lax.fori_loop(0, nb, kbody, 0)

    key_i = lax.broadcasted_iota(jnp.int32, (blk, blk), 0)
    qry_i = lax.broadcasted_iota(jnp.int32, (blk, blk), 1)
    causal = key_i <= qry_i
    rowf = lax.broadcasted_iota(jnp.int32, (nbp, blk), 0).astype(F32)
    row128 = lax.broadcasted_iota(jnp.int32, (LANES, blk), 0)
    q_scale = (HEAD_DIM ** -0.5) * 1.4426950408889634

    def qbody(qi, carry):
        r0 = pl.multiple_of(qi * blk, blk)
        qT = norm_rope(q_ref[0, pl.ds(r0, blk), :], gq_ref[...], r0).T
        qif = lax.convert_element_type(qi, F32)
        qa_own, qa_past = [], []
        for h in range(2):
            qmT = jnp.where(row128 < HEAD_DIM, qT, 0.0) if h == 0 else jnp.where(row128 < HEAD_DIM, 0.0, qT)
            gate = jnp.dot(kmp_ref[...], qmT, precision=HIGHEST, preferred_element_type=F32)
            g = jnp.where(rowf < qif, gate, -jnp.inf)
            keep = jnp.zeros((nbp, blk), jnp.bool_)
            for r in range(MOBA_TOPK):
                m = jnp.max(g, axis=0, keepdims=True)
                idx = jnp.min(jnp.where(g == m, rowf, 1e9), axis=0, keepdims=True)
                pick = (rowf == idx) & (qi > r)
                keep = keep | pick
                g = jnp.where(pick, -jnp.inf, g)
            bias = jnp.where(keep, 0.0, NEG)
            qs = qT[h * HEAD_DIM:(h + 1) * HEAD_DIM] * q_scale
            pad = jnp.zeros((LANES - HEAD_DIM - nbp, blk), F32)
            qa_past.append(jnp.concatenate([qs, bias, pad], axis=0).astype(BF16))
            qa_own.append(jnp.concatenate([qs, jnp.zeros((LANES - HEAD_DIM, blk), F32)], axis=0).astype(BF16))

        mx0 = []
        for h in range(2):
            sT = jnp.dot(kaug_ref[h, qi], qa_own[h], preferred_element_type=F32)
            sT = jnp.where(causal, sT, NEG)
            s_ref[h, nb] = sT
            mx0.append(_fold_rows(sT, jnp.maximum))
        n_pair = (qi + 1) // 2

        def p1(p, mx):
            mx = list(mx)
            for u in range(2):
                j = 2 * p + u
                for h in range(2):
                    sT = jnp.dot(kaug_ref[h, j], qa_past[h], preferred_element_type=F32)
                    s_ref[h, j] = sT
                    mx[h] = jnp.maximum(mx[h], _fold_rows(sT, jnp.maximum))
            return tuple(mx)

        mx = lax.fori_loop(0, n_pair, p1, tuple(mx0))
        mcol = [jnp.max(mx[h], axis=0, keepdims=True) for h in range(2)]

        acc0 = []
        for h in range(2):
            pT = jnp.exp2(s_ref[h, nb] - mcol[h]).astype(BF16)
            acc0.append(jnp.dot(vt_ref[h, qi], pT, preferred_element_type=F32))

        def p2(p, acc):
            acc = list(acc)
            for u in range(2):
                j = 2 * p + u
                for h in range(2):
                    pT = jnp.exp2(s_ref[h, j] - mcol[h]).astype(BF16)
                    acc[h] = acc[h] + jnp.dot(vt_ref[h, j], pT, preferred_element_type=F32)
            return tuple(acc)

        acc = lax.fori_loop(0, n_pair, p2, tuple(acc0))
        outT = jnp.concatenate([acc[h][0:HEAD_DIM] / acc[h][HEAD_DIM:HEAD_DIM + 1] for h in range(2)], axis=0)
        o_ref[0, pl.ds(r0, blk), :] = outT.T.astype(BF16)
        return carry

    lax.fori_loop(0, nb, qbody, 0)


def _attention(z3, cosf, sinf, gq2, gk2):
    B, S, _ = z3.shape
    n_pair = N_HEADS // 2
    kq = ATTN_WIDTH // LANES
    nb = S // MOBA_BLOCK
    nbp = -(-nb // 16) * 16
    return pl.pallas_call(
        _attn_kernel,
        grid=(B, n_pair),
        in_specs=[pl.BlockSpec((1, S, LANES), lambda b, p: (b, 0, p)),
                  pl.BlockSpec((1, S, LANES), lambda b, p: (b, 0, kq + p)),
                  pl.BlockSpec((1, S, LANES), lambda b, p: (b, 0, 2 * kq + p)),
                  pl.BlockSpec((S, LANES), lambda b, p: (0, 0)),
                  pl.BlockSpec((S, LANES), lambda b, p: (0, 0)),
                  pl.BlockSpec((1, LANES), lambda b, p: (0, 0)),
                  pl.BlockSpec((1, LANES), lambda b, p: (0, 0))],
        out_specs=pl.BlockSpec((1, S, LANES), lambda b, p: (b, 0, p)),
        out_shape=jax.ShapeDtypeStruct((B, S, ATTN_WIDTH), BF16),
        scratch_shapes=[pltpu.VMEM((2, nb, MOBA_BLOCK, LANES), BF16),
                        pltpu.VMEM((2, nb, HEAD_DIM + 16, MOBA_BLOCK), BF16),
                        pltpu.VMEM((nbp, LANES), F32),
                        pltpu.VMEM((2, nb + 1, MOBA_BLOCK, MOBA_BLOCK), F32)],
        compiler_params=_cparams("arbitrary", "arbitrary"),
        name="attn",
    )(z3, z3, z3, cosf, sinf, gq2, gk2)


def _post_kernel(x_ref, ya_ref, xb_ref, bg_ref, cg_ref, gta_ref, gtb_ref, ga1_ref, sc2_ref, sh2_ref,
                 cw_ref, cb_ref, wpa_ref, wpb_ref, wo_ref, g2_ref, wr_ref, br_ref, tri_ref,
                 x1_ref, h2_ref, ri_ref, rf_ref, cnt_ref, ubuf_ref, run_ref, *, tiles_per_seq):
    i = pl.program_id(0)
    tm = x_ref.shape[0]
    halo = 8

    @pl.when(i == 0)
    def _():
        run_ref[...] = jnp.zeros_like(run_ref)

    @pl.when(i % tiles_per_seq == 0)
    def _():
        ubuf_ref[0:halo, :] = jnp.zeros((halo, CONV_WIDTH), F32)

    u = cg_ref[...].astype(F32) * xb_ref[...].astype(F32)
    ubuf_ref[halo:halo + tm, :] = u
    cw = cw_ref[...]
    conv = (cw[0:1, :] * ubuf_ref[halo - 2:halo - 2 + tm, :]
            + cw[1:2, :] * ubuf_ref[halo - 1:halo - 1 + tm, :]
            + cw[2:3, :] * u)
    ubuf_ref[0:halo, :] = ubuf_ref[tm:tm + halo, :]
    y_b = (bg_ref[...].astype(F32) * (conv + cb_ref[...])).astype(BF16)

    pa = jnp.dot(ya_ref[...], wpa_ref[...], preferred_element_type=F32)
    pb = jnp.dot(y_b, wpb_ref[...], preferred_element_type=F32)
    merged = (jax.nn.sigmoid(gta_ref[...].astype(F32)) * pa
              + jax.nn.sigmoid(gtb_ref[...].astype(F32)) * pb).astype(BF16)
    x1 = x_ref[...] + ga1_ref[0] * jnp.dot(merged, wo_ref[...], preferred_element_type=F32)
    x1_ref[...] = x1

    ms = jnp.mean(x1 * x1, axis=-1, keepdims=True)
    h2 = x1 * lax.rsqrt(ms + EPS) * g2_ref[...]
    h2 = h2 * (1.0 + sc2_ref[0]) + sh2_ref[0]
    bits = pltpu.bitcast(h2.astype(BF16).astype(F32), jnp.uint32)
    half_d = h2.shape[1] // 2
    h2_ref[...] = (bits[:, :half_d] >> 16) | bits[:, half_d:]

    logit = jnp.dot(h2, wr_ref[...], preferred_element_type=F32, precision=HIGHEST) + br_ref[...]
    lane = lax.broadcasted_iota(jnp.int32, (tm, LANES), 1)
    big = 1 << 20
    gmask = lane < N_GROUPS
    gl = jnp.where(gmask, logit, -jnp.inf)
    gmax = jnp.max(gl, axis=-1, keepdims=True)
    g_idx = jnp.min(jnp.where(gl == gmax, lane, big), axis=-1, keepdims=True)
    g_w = 1.0 / jnp.sum(jnp.exp(gl - gmax), axis=-1, keepdims=True)
    e_lo = N_GROUPS + EXPERTS_PER_GROUP * g_idx
    emask = (lane >= e_lo) & (lane < e_lo + EXPERTS_PER_GROUP)
    el = jnp.where(emask, logit, -jnp.inf)
    v0 = jnp.max(el, axis=-1, keepdims=True)
    i0 = jnp.min(jnp.where(el == v0, lane, big), axis=-1, keepdims=True)
    el = jnp.where(lane == i0, -jnp.inf, el)
    v1 = jnp.max(el, axis=-1, keepdims=True)
    i1 = jnp.min(jnp.where(el == v1, lane, big), axis=-1, keepdims=True)
    t = jnp.exp(v1 - v0)
    w0 = g_w / (1.0 + t)
    w1 = g_w * t / (1.0 + t)
    e0 = i0 - N_GROUPS
    e1 = i1 - N_GROUPS

    oh0 = lane == e0
    oh1 = lane == e1
    oh = jnp.where(oh0 | oh1, 1.0, 0.0)
    before = jnp.dot(tri_ref[...], oh.astype(BF16), preferred_element_type=F32) + run_ref[0:1, :]
    r0 = jnp.sum(jnp.where(oh0, before, 0.0), axis=-1, keepdims=True).astype(jnp.int32)
    r1 = jnp.sum(jnp.where(oh1, before, 0.0), axis=-1, keepdims=True).astype(jnp.int32)
    run_new = run_ref[0:1, :] + jnp.sum(oh, axis=0, keepdims=True)
    run_ref[...] = jnp.broadcast_to(run_new, run_ref.shape)
    cnt_ref[...] = jnp.broadcast_to(run_new, cnt_ref.shape)

    ri_ref[...] = jnp.where(lane == 0, e0, jnp.where(lane == 1, e1, jnp.where(lane == 2, r0, jnp.where(lane == 3, r1, 0))))
    rf_ref[...] = jnp.where(lane == 0, w0, jnp.where(lane == 1, w1, 0.0))


def _post(x2, ya2, z2, ga1, sc2, sh2, conv_w, conv_b, wpa, wpb, wo, g2, wr, br, tri, tiles_per_seq):
    T, D = x2.shape
    tm = ROW_TILE
    cw = CONV_WIDTH
    xcol = 3 * ATTN_WIDTH // cw
    gcol = (3 * ATTN_WIDTH + 3 * cw) // D
    bmap = lambda i: (i // tiles_per_seq, 0, 0)
    const = lambda i: (0, 0)
    return pl.pallas_call(
        functools.partial(_post_kernel, tiles_per_seq=tiles_per_seq),
        grid=(T // tm,),
        in_specs=[pl.BlockSpec((tm, D), lambda i: (i, 0)),
                  pl.BlockSpec((tm, ATTN_WIDTH), lambda i: (i, 0)),
                  pl.BlockSpec((tm, cw), lambda i: (i, xcol)),
                  pl.BlockSpec((tm, cw), lambda i: (i, xcol + 1)),
                  pl.BlockSpec((tm, cw), lambda i: (i, xcol + 2)),
                  pl.BlockSpec((tm, D), lambda i: (i, gcol)),
                  pl.BlockSpec((tm, D), lambda i: (i, gcol + 1)),
                  pl.BlockSpec((1, 1, D), bmap),
                  pl.BlockSpec((1, 1, D), bmap),
                  pl.BlockSpec((1, 1, D), bmap),
                  pl.BlockSpec((CONV_K, cw), const),
                  pl.BlockSpec((1, cw), const),
                  pl.BlockSpec((ATTN_WIDTH, D), const),
                  pl.BlockSpec((cw, D), const),
                  pl.BlockSpec((D, D), const),
                  pl.BlockSpec((1, D), const),
                  pl.BlockSpec((D, LANES), const),
                  pl.BlockSpec((1, LANES), const),
                  pl.BlockSpec((tm, tm), const)],
        out_specs=[pl.BlockSpec((tm, D), lambda i: (i, 0)),
                   pl.BlockSpec((tm, D // 2), lambda i: (i, 0)),
                   pl.BlockSpec((tm, LANES), lambda i: (i, 0)),
                   pl.BlockSpec((tm, LANES), lambda i: (i, 0)),
                   pl.BlockSpec((8, LANES), const)],
        out_shape=[jax.ShapeDtypeStruct((T, D), F32),
                   jax.ShapeDtypeStruct((T, D // 2), jnp.uint32),
                   jax.ShapeDtypeStruct((T, LANES), jnp.int32),
                   jax.ShapeDtypeStruct((T, LANES), F32),
                   jax.ShapeDtypeStruct((8, LANES), F32)],
        scratch_shapes=[pltpu.VMEM((tm + 16, cw), F32),
                        pltpu.VMEM((8, LANES), F32)],
        compiler_params=_cparams("arbitrary"),
        name="post",
    )(x2, ya2, z2, z2, z2, z2, z2, ga1, sc2, sh2, conv_w, conv_b, wpa, wpb, wo, g2, wr, br, tri)


def _dispatch_kernel(d0_ref, d1_ref, h_ref, xs_in_ref, xs_ref, sem):
    del xs_in_ref
    tm = h_ref.shape[0]
    base = pl.program_id(0) * tm

    def body(r, carry):
        src = h_ref.at[pl.ds(r, 1), :]
        pltpu.make_async_copy(src, xs_ref.at[pl.ds(d0_ref[base + r], 1), :], sem).start()
        pltpu.make_async_copy(src, xs_ref.at[pl.ds(d1_ref[base + r], 1), :], sem).start()
        return carry

    lax.fori_loop(0, tm, body, 0)
    for _ in range(TOPK_IN_GROUP):
        pltpu.make_async_copy(h_ref, xs_ref.at[pl.ds(0, tm), :], sem).wait()


def _dispatch(dest0, dest1, h2p, n_pad):
    T, W = h2p.shape
    tm = ROW_TILE
    grid_spec = pltpu.PrefetchScalarGridSpec(
        num_scalar_prefetch=2,
        grid=(T // tm,),
        in_specs=[pl.BlockSpec((tm, W), lambda i, d0, d1: (i, 0)),
                  pl.BlockSpec(memory_space=pl.ANY)],
        out_specs=pl.BlockSpec(memory_space=pl.ANY),
        scratch_shapes=[pltpu.SemaphoreType.DMA(())],
    )
    return pl.pallas_call(
        _dispatch_kernel,
        grid_spec=grid_spec,
        out_shape=jax.ShapeDtypeStruct((n_pad, W), h2p.dtype),
        input_output_aliases={3: 0},
        compiler_params=_cparams("arbitrary"),
        name="dispatch",
    )(dest0, dest1, h2p, jnp.zeros((n_pad, W), h2p.dtype))


def _expert_kernel(be_ref, xs_ref, w1_ref, w3_ref, w2_ref, ys_ref):
    xp = xs_ref[...]
    half_d = xp.shape[1]
    x_lo = pltpu.bitcast(xp << 16, F32).astype(BF16)
    x_hi = pltpu.bitcast(xp & jnp.uint32(0xFFFF0000), F32).astype(BF16)
    a = (jnp.dot(x_lo, w1_ref[0, :half_d], preferred_element_type=F32)
         + jnp.dot(x_hi, w1_ref[0, half_d:], preferred_element_type=F32))
    b = (jnp.dot(x_lo, w3_ref[0, :half_d], preferred_element_type=F32)
         + jnp.dot(x_hi, w3_ref[0, half_d:], preferred_element_type=F32))
    hid = (a * jax.nn.sigmoid(a) * b).astype(BF16)
    ys_ref[...] = jnp.dot(hid, w2_ref[0], preferred_element_type=F32)


def _experts(blk_expert, xs, w1b, w3b, w2b):
    n_pad, W = xs.shape
    _, D, F = w1b.shape
    rb = EXPERT_ROWS
    grid_spec = pltpu.PrefetchScalarGridSpec(
        num_scalar_prefetch=1,
        grid=(n_pad // rb,),
        in_specs=[pl.BlockSpec((rb, W), lambda i, be: (i, 0)),
                  pl.BlockSpec((1, D, F), lambda i, be: (be[i], 0, 0)),
                  pl.BlockSpec((1, D, F), lambda i, be: (be[i], 0, 0)),
                  pl.BlockSpec((1, F, D), lambda i, be: (be[i], 0, 0))],
        out_specs=pl.BlockSpec((rb, D), lambda i, be: (i, 0)),
    )
    return pl.pallas_call(
        _expert_kernel,
        grid_spec=grid_spec,
        out_shape=jax.ShapeDtypeStruct((n_pad, D), F32),
        compiler_params=_cparams("arbitrary"),
        name="experts",
    )(blk_expert, xs, w1b, w3b, w2b)


def _combine_kernel(d0_ref, d1_ref, x1_ref, rf_ref, ga2_ref, ys_ref, o_ref, buf_ref, sem):
    i = pl.program_id(0)
    n = pl.num_programs(0)
    tm = x1_ref.shape[0]

    def gather(step, slot):
        base = step * tm

        def body(r, carry):
            pltpu.make_async_copy(ys_ref.at[pl.ds(d0_ref[base + r], 1), :],
                                  buf_ref.at[slot, 0, pl.ds(r, 1), :], sem.at[slot]).start()
            pltpu.make_async_copy(ys_ref.at[pl.ds(d1_ref[base + r], 1), :],
                                  buf_ref.at[slot, 1, pl.ds(r, 1), :], sem.at[slot]).start()
            return carry

        lax.fori_loop(0, tm, body, 0)

    @pl.when(i == 0)
    def _():
        gather(0, 0)

    @pl.when(i + 1 < n)
    def _():
        gather(i + 1, (i + 1) % 2)

    slot = i % 2
    for k in range(TOPK_IN_GROUP):
        pltpu.make_async_copy(ys_ref.at[pl.ds(0, tm), :], buf_ref.at[slot, k], sem.at[slot]).wait()
    rf = rf_ref[...]
    y = rf[:, 0:1] * buf_ref[slot, 0] + rf[:, 1:2] * buf_ref[slot, 1]
    o_ref[...] = x1_ref[...] + ga2_ref[0] * y


def _combine(dest0, dest1, x1, rf, ga2, ys, tiles_per_seq):
    T, D = x1.shape
    tm = ROW_TILE
    row = lambda i, d0, d1: (i, 0)
    grid_spec = pltpu.PrefetchScalarGridSpec(
        num_scalar_prefetch=2,
        grid=(T // tm,),
        in_specs=[pl.BlockSpec((tm, D), row),
                  pl.BlockSpec((tm, LANES), row),
                  pl.BlockSpec((1, 1, D), lambda i, d0, d1: (i // tiles_per_seq, 0, 0)),
                  pl.BlockSpec(memory_space=pl.ANY)],
        out_specs=pl.BlockSpec((tm, D), row),
        scratch_shapes=[pltpu.VMEM((2, TOPK_IN_GROUP, tm, D), F32),
                        pltpu.SemaphoreType.DMA((2,))],
    )
    return pl.pallas_call(
        _combine_kernel,
        grid_spec=grid_spec,
        out_shape=jax.ShapeDtypeStruct((T, D), F32),
        compiler_params=_cparams("arbitrary"),
        name="combine",
    )(dest0, dest1, x1, rf, ga2, ys)


def _rope_tables(S):
    pos = jnp.arange(S, dtype=F32)
    inv_freq = ROPE_THETA ** (-jnp.arange(0, ROT_DIM, 2, dtype=F32) / ROT_DIM)
    ang = pos[:, None] * inv_freq[None, :]
    cos, sin = jnp.cos(ang), jnp.sin(ang)
    half = ROT_DIM // 2
    ones = jnp.ones((S, HEAD_DIM - ROT_DIM), F32)
    cos_h = jnp.concatenate([cos, cos, ones], axis=1)
    sin_h = jnp.concatenate([-sin, sin, 0.0 * ones], axis=1)
    return jnp.tile(cos_h, (1, LANES // HEAD_DIM)), jnp.tile(sin_h, (1, LANES // HEAD_DIM))


def kernel(x, c, w_ada, b_ada, g_norm1, g_norm2, w_in, g_q, g_k, conv_w, conv_b,
           w_pa, w_pb, w_o, w_rg, b_rg, w_re, b_re, w1, w3, w2):
    B, S, D = x.shape
    T = B * S
    assert S % ROW_TILE == 0 and S % MOBA_BLOCK == 0 and S // MOBA_BLOCK <= LANES - HEAD_DIM
    tiles_per_seq = S // ROW_TILE
    l = 0

    mod = _ada(c, w_ada[l], b_ada[l])
    sh1, sc1, ga1, sh2, sc2, ga2 = [m.reshape(B, 1, D) for m in jnp.split(mod, N_MOD, axis=-1)]

    x2 = x.reshape(T, D)
    z2 = _inproj(x2, g_norm1[l].reshape(1, D), sc1, sh1, w_in[l].astype(BF16), tiles_per_seq)

    cosf, sinf = _rope_tables(S)
    rep = LANES // HEAD_DIM
    ya = _attention(z2.reshape(B, S, -1), cosf, sinf,
                    jnp.tile(g_q[l], rep).reshape(1, LANES), jnp.tile(g_k[l], rep).reshape(1, LANES))

    wr = jnp.zeros((D, LANES), F32).at[:, :N_GROUPS].set(w_rg[l]).at[:, N_GROUPS:N_GROUPS + N_EXPERTS].set(w_re[l])
    br = jnp.zeros((1, LANES), F32).at[0, :N_GROUPS].set(b_rg[l]).at[0, N_GROUPS:N_GROUPS + N_EXPERTS].set(b_re[l])
    tri = (lax.broadcasted_iota(jnp.int32, (ROW_TILE, ROW_TILE), 1)
           < lax.broadcasted_iota(jnp.int32, (ROW_TILE, ROW_TILE), 0)).astype(BF16)
    x1, h2, ri, rf, cnt = _post(x2, ya.reshape(T, ATTN_WIDTH), z2, ga1, sc2, sh2,
                                conv_w[l], conv_b[l].reshape(1, CONV_WIDTH),
                                w_pa[l].astype(BF16), w_pb[l].astype(BF16), w_o[l].astype(BF16),
                                g_norm2[l].reshape(1, D), wr, br, tri, tiles_per_seq)

    rb = EXPERT_ROWS
    counts = cnt[0, :N_EXPERTS].astype(jnp.int32)
    padded = (counts + rb - 1) // rb * rb
    pad_end = jnp.cumsum(padded)
    pad_start = pad_end - padded
    e0, e1, r0, r1 = ri[:, 0], ri[:, 1], ri[:, 2], ri[:, 3]
    dest0 = pad_start[e0] + r0
    dest1 = pad_start[e1] + r1
    n_blocks = -(-T * TOPK_IN_GROUP // rb) + N_EXPERTS
    n_pad = n_blocks * rb
    blk_start = jnp.arange(n_blocks, dtype=jnp.int32) * rb
    blk_expert = jnp.minimum(jnp.sum(pad_end[None, :] <= blk_start[:, None], axis=-1), N_EXPERTS - 1).astype(jnp.int32)

    xs = _dispatch(dest0, dest1, h2, n_pad)
    ys = _experts(blk_expert, xs, w1[l].astype(BF16), w3[l].astype(BF16), w2[l].astype(BF16))
    out = _combine(dest0, dest1, x1, rf, ga2, ys, tiles_per_seq)
    return out.reshape(B, S, D)
```

```python
import functools

import jax
import jax.numpy as jnp
from jax import lax
from jax.experimental import pallas as pl
from jax.experimental.pallas import tpu as pltpu

F32 = jnp.float32
BF16 = jnp.bfloat16
HIGHEST = lax.Precision.HIGHEST

N_HEADS = 8
HEAD_DIM = 64
ATTN_WIDTH = N_HEADS * HEAD_DIM
CONV_WIDTH = 512
CONV_K = 3
MOBA_BLOCK = 256
MOBA_TOPK = 3
ROPE_THETA = 500000.0
ROT_DIM = HEAD_DIM // 4
N_GROUPS = 4
EXPERTS_PER_GROUP = 8
N_EXPERTS = N_GROUPS * EXPERTS_PER_GROUP
TOPK_IN_GROUP = 2
N_MOD = 6
EPS = 1e-6

LANES = 128
NEG = -1e30
ROW_TILE = 512
QUERY_TILE = 512
DMA_UNROLL = 8
EXPERT_ROWS = 256
VMEM_LIMIT = 56 * 1024 * 1024


def _cparams(*sem):
    return pltpu.CompilerParams(dimension_semantics=sem, vmem_limit_bytes=VMEM_LIMIT)


def _ada_kernel(c_ref, w_ref, b_ref, o_ref):
    c = c_ref[...]
    a = c * jax.nn.sigmoid(c)
    o_ref[...] = jnp.dot(a, w_ref[...], preferred_element_type=F32, precision=HIGHEST) + b_ref[...]


def _ada(c, w_ada, b_ada):
    B, D = c.shape
    N = w_ada.shape[1]
    tn = 1536
    return pl.pallas_call(
        _ada_kernel,
        grid=(N // tn,),
        in_specs=[pl.BlockSpec((B, D), lambda j: (0, 0)),
                  pl.BlockSpec((D, tn), lambda j: (0, j)),
                  pl.BlockSpec((1, tn), lambda j: (0, j))],
        out_specs=pl.BlockSpec((B, tn), lambda j: (0, j)),
        out_shape=jax.ShapeDtypeStruct((B, N), F32),
        compiler_params=_cparams("arbitrary"),
        name="ada",
    )(c, w_ada, b_ada.reshape(1, N))


def _inproj_kernel(x_ref, g_ref, sc_ref, sh_ref, w_ref, z_ref, *, n_chunk):
    x = x_ref[...]
    ms = jnp.mean(x * x, axis=-1, keepdims=True)
    y = x * lax.rsqrt(ms + EPS) * g_ref[...]
    h = (y * (1.0 + sc_ref[0]) + sh_ref[0]).astype(BF16)
    for n in range(0, z_ref.shape[1], n_chunk):
        z_ref[:, n:n + n_chunk] = jnp.dot(h, w_ref[:, n:n + n_chunk],
                                          preferred_element_type=F32).astype(BF16)


def _inproj(x2, g1, sc1, sh1, w_in_bf, tiles_per_seq):
    T, D = x2.shape
    N = w_in_bf.shape[1]
    tm = ROW_TILE
    bmap = lambda i: (i // tiles_per_seq, 0, 0)
    return pl.pallas_call(
        functools.partial(_inproj_kernel, n_chunk=512),
        grid=(T // tm,),
        in_specs=[pl.BlockSpec((tm, D), lambda i: (i, 0)),
                  pl.BlockSpec((1, D), lambda i: (0, 0)),
                  pl.BlockSpec((1, 1, D), bmap),
                  pl.BlockSpec((1, 1, D), bmap),
                  pl.BlockSpec((D, N), lambda i: (0, 0))],
        out_specs=pl.BlockSpec((tm, N), lambda i: (i, 0)),
        out_shape=jax.ShapeDtypeStruct((T, N), BF16),
        compiler_params=_cparams("arbitrary"),
        name="inproj",
    )(x2, g1, sc1, sh1, w_in_bf)


def _fold_rows(x, op):
    parts = [x[r:r + 8] for r in range(0, x.shape[0], 8)]
    while len(parts) > 1:
        parts = [op(parts[i], parts[i + 1]) for i in range(0, len(parts) - 1, 2)] + (
            [parts[-1]] if len(parts) % 2 else [])
    return parts[0]


def _attn_kernel(q_ref, k_ref, v_ref, cos_ref, sin_ref, gq_ref, gk_ref, o_ref,
                 kaug_ref, vt_ref, kmp_ref, s_ref):
    S = q_ref.shape[1]
    blk = MOBA_BLOCK
    qt = QUERY_TILE
    sub = qt // blk
    nb = S // blk
    nbp = kmp_ref.shape[0]

    def norm_rope(xb, g, r0):
        rows = xb.shape[0]
        lane = lax.broadcasted_iota(jnp.int32, (rows, LANES), 1)
        head0 = lane < HEAD_DIM
        x = xb.astype(F32)
        sq = x * x
        s0 = jnp.sum(jnp.where(head0, sq, 0.0), axis=-1, keepdims=True)
        s1 = jnp.sum(jnp.where(head0, 0.0, sq), axis=-1, keepdims=True)
        inv = jnp.where(head0, lax.rsqrt(s0 * (1.0 / HEAD_DIM) + EPS), lax.rsqrt(s1 * (1.0 / HEAD_DIM) + EPS))
        y = x * inv * g
        half = ROT_DIM // 2
        rot_lo = (lane & (HEAD_DIM - 1)) < half
        rot = jnp.where(rot_lo, pltpu.roll(y, LANES - half, 1), pltpu.roll(y, half, 1))
        return y * cos_ref[pl.ds(r0, rows), :] + rot * sin_ref[pl.ds(r0, rows), :]

    kmp_ref[...] = jnp.zeros_like(kmp_ref)
    ones_row = jnp.where(lax.broadcasted_iota(jnp.int32, (16, blk), 0) == 0, 1.0, 0.0).astype(BF16)
    lane_k = lax.broadcasted_iota(jnp.int32, (blk, LANES), 1)
    head0_k = lane_k < HEAD_DIM

    def kbody(j, carry):
        r0 = pl.multiple_of(j * blk, blk)
        kr = norm_rope(k_ref[0, pl.ds(r0, blk), :], gk_ref[...], r0)
        kmp_ref[pl.ds(j, 1), :] = jnp.sum(kr, axis=0, keepdims=True) * (1.0 / blk)
        onehot = jnp.where(lane_k - HEAD_DIM == j, 1.0, 0.0)
        kaug_ref[0, j] = jnp.where(head0_k, kr, onehot).astype(BF16)
        kaug_ref[1, j] = jnp.where(head0_k, pltpu.roll(kr, HEAD_DIM, 1), onehot).astype(BF16)
        vT = v_ref[0, pl.ds(r0, blk), :].astype(F32).T
        for h in range(2):
            vt_ref[h, j, 0:HEAD_DIM, :] = vT[h * HEAD_DIM:(h + 1) * HEAD_DIM].astype(BF16)
            vt_ref[h, j, HEAD_DIM:HEAD_DIM + 16, :] = ones_row
        return carry

    lax.fori_loop(0, nb, kbody, 0)

    key_i = lax.broadcasted_iota(jnp.int32, (blk, qt), 0)
    qry_i = lax.broadcasted_iota(jnp.int32, (blk, qt), 1)
    causal = [(qry_i < u * blk) | (qry_i >= (u + 1) * blk) | (key_i <= qry_i - u * blk) for u in range(sub)]
    rowf = lax.broadcasted_iota(jnp.int32, (nbp, qt), 0).astype(F32)
    subf = (lax.broadcasted_iota(jnp.int32, (nbp, qt), 1) // blk).astype(F32)
    row128 = lax.broadcasted_iota(jnp.int32, (LANES, qt), 0)
    q_scale = (HEAD_DIM ** -0.5) * 1.4426950408889634

    def qbody(qi, carry):
        r0 = pl.multiple_of(qi * qt, qt)
        j0 = qi * sub
        qT = norm_rope(q_ref[0, pl.ds(r0, qt), :], gq_ref[...], r0).T
        cur = lax.convert_element_type(j0, F32) + subf
        qa = []
        for h in range(2):
            qmT = jnp.where(row128 < HEAD_DIM, qT, 0.0) if h == 0 else jnp.where(row128 < HEAD_DIM, 0.0, qT)
            gate = jnp.dot(kmp_ref[...], qmT, precision=HIGHEST, preferred_element_type=F32)
            g = jnp.where(rowf < cur, gate, -jnp.inf)
            keep = rowf == cur
            for r in range(MOBA_TOPK):
                m = jnp.max(g, axis=0, keepdims=True)
                idx = jnp.min(jnp.where(g == m, rowf, 1e9), axis=0, keepdims=True)
                pick = (rowf == idx) & (cur > r)
                keep = keep | pick
                g = jnp.where(pick, -jnp.inf, g)
            bias = jnp.where(keep, 0.0, NEG)
            qs = qT[h * HEAD_DIM:(h + 1) * HEAD_DIM] * q_scale
            pad = jnp.zeros((LANES - HEAD_DIM - nbp, qt), F32)
            qa.append(jnp.concatenate([qs, bias, pad], axis=0).astype(BF16))

        mx = [None, None]
        for u in range(sub):
            for h in range(2):
                sT = jnp.dot(kaug_ref[h, j0 + u], qa[h], preferred_element_type=F32)
                sT = jnp.where(causal[u], sT, NEG)
                s_ref[h, j0 + u] = sT
                f = _fold_rows(sT, jnp.maximum)
                mx[h] = f if mx[h] is None else jnp.maximum(mx[h], f)
        n_pair = j0 // 2

        def p1(p, mx):
            mx = list(mx)
            for u in range(2):
                j = 2 * p + u
                for h in range(2):
                    sT = jnp.dot(kaug_ref[h, j], qa[h], preferred_element_type=F32)
                    s_ref[h, j] = sT
                    mx[h] = jnp.maximum(mx[h], _fold_rows(sT, jnp.maximum))
            return tuple(mx)

        mx = lax.fori_loop(0, n_pair, p1, tuple(mx))
        mcol = [jnp.max(mx[h], axis=0, keepdims=True) for h in range(2)]

        acc = [None, None]
        for u in range(sub):
            for h in range(2):
                pT = jnp.exp2(s_ref[h, j0 + u] - mcol[h]).astype(BF16)
                d = jnp.dot(vt_ref[h, j0 + u], pT, preferred_element_type=F32)
                acc[h] = d if acc[h] is None else acc[h] + d

        def p2(p, acc):
            acc = list(acc)
            for u in range(2):
                j = 2 * p + u
                for h in range(2):
                    pT = jnp.exp2(s_ref[h, j] - mcol[h]).astype(BF16)
                    acc[h] = acc[h] + jnp.dot(vt_ref[h, j], pT, preferred_element_type=F32)
            return tuple(acc)

        acc = lax.fori_loop(0, n_pair, p2, tuple(acc))
        outT = jnp.concatenate([acc[h][0:HEAD_DIM] / acc[h][HEAD_DIM:HEAD_DIM + 1] for h in range(2)], axis=0)
        o_ref[0, pl.ds(r0, qt), :] = outT.T.astype(BF16)
        return carry

    lax.fori_loop(0, S // qt, qbody, 0)


def _attention(z3, cosf, sinf, gq2, gk2):
    B, S, _ = z3.shape
    n_pair = N_HEADS // 2
    kq = ATTN_WIDTH // LANES
    nb = S // MOBA_BLOCK
    nbp = -(-nb // 16) * 16
    return pl.pallas_call(
        _attn_kernel,
        grid=(B, n_pair),
        in_specs=[pl.BlockSpec((1, S, LANES), lambda b, p: (b, 0, p)),
                  pl.BlockSpec((1, S, LANES), lambda b, p: (b, 0, kq + p)),
                  pl.BlockSpec((1, S, LANES), lambda b, p: (b, 0, 2 * kq + p)),
                  pl.BlockSpec((S, LANES), lambda b, p: (0, 0)),
                  pl.BlockSpec((S, LANES), lambda b, p: (0, 0)),
                  pl.BlockSpec((1, LANES), lambda b, p: (0, 0)),
                  pl.BlockSpec((1, LANES), lambda b, p: (0, 0))],
        out_specs=pl.BlockSpec((1, S, LANES), lambda b, p: (b, 0, p)),
        out_shape=jax.ShapeDtypeStruct((B, S, ATTN_WIDTH), BF16),
        scratch_shapes=[pltpu.VMEM((2, nb, MOBA_BLOCK, LANES), BF16),
                        pltpu.VMEM((2, nb, HEAD_DIM + 16, MOBA_BLOCK), BF16),
                        pltpu.VMEM((nbp, LANES), F32),
                        pltpu.VMEM((2, nb, MOBA_BLOCK, QUERY_TILE), F32)],
        compiler_params=_cparams("arbitrary", "arbitrary"),
        name="attn",
    )(z3, z3, z3, cosf, sinf, gq2, gk2)


def _post_kernel(x_ref, ya_ref, xb_ref, bg_ref, cg_ref, gta_ref, gtb_ref, ga1_ref, sc2_ref, sh2_ref,
                 cw_ref, cb_ref, wpa_ref, wpb_ref, wo_ref, g2_ref, wr_ref, br_ref, tri_ref,
                 x1_ref, h2_ref, ri_ref, rf_ref, cnt_ref, ubuf_ref, run_ref, *, tiles_per_seq):
    i = pl.program_id(0)
    tm = x_ref.shape[0]
    halo = 8

    @pl.when(i == 0)
    def _():
        run_ref[...] = jnp.zeros_like(run_ref)

    @pl.when(i % tiles_per_seq == 0)
    def _():
        ubuf_ref[0:halo, :] = jnp.zeros((halo, CONV_WIDTH), F32)

    u = cg_ref[...].astype(F32) * xb_ref[...].astype(F32)
    ubuf_ref[halo:halo + tm, :] = u
    cw = cw_ref[...]
    conv = (cw[0:1, :] * ubuf_ref[halo - 2:halo - 2 + tm, :]
            + cw[1:2, :] * ubuf_ref[halo - 1:halo - 1 + tm, :]
            + cw[2:3, :] * u)
    ubuf_ref[0:halo, :] = ubuf_ref[tm:tm + halo, :]
    y_b = (bg_ref[...].astype(F32) * (conv + cb_ref[...])).astype(BF16)

    pa = jnp.dot(ya_ref[...], wpa_ref[...], preferred_element_type=F32)
    pb = jnp.dot(y_b, wpb_ref[...], preferred_element_type=F32)
    merged = (jax.nn.sigmoid(gta_ref[...].astype(F32)) * pa
              + jax.nn.sigmoid(gtb_ref[...].astype(F32)) * pb).astype(BF16)
    x1 = x_ref[...] + ga1_ref[0] * jnp.dot(merged, wo_ref[...], preferred_element_type=F32)
    x1_ref[...] = x1

    ms = jnp.mean(x1 * x1, axis=-1, keepdims=True)
    h2 = x1 * lax.rsqrt(ms + EPS) * g2_ref[...]
    h2 = h2 * (1.0 + sc2_ref[0]) + sh2_ref[0]
    bits = pltpu.bitcast(h2.astype(BF16).astype(F32), jnp.uint32)
    half_d = h2.shape[1] // 2
    h2_ref[...] = (bits[:, :half_d] >> 16) | bits[:, half_d:]

    logit = jnp.dot(h2, wr_ref[...], preferred_element_type=F32, precision=HIGHEST) + br_ref[...]
    lane = lax.broadcasted_iota(jnp.int32, (tm, LANES), 1)
    big = 1 << 20
    gmask = lane < N_GROUPS
    gl = jnp.where(gmask, logit, -jnp.inf)
    gmax = jnp.max(gl, axis=-1, keepdims=True)
    g_idx = jnp.min(jnp.where(gl == gmax, lane, big), axis=-1, keepdims=True)
    g_w = 1.0 / jnp.sum(jnp.exp(gl - gmax), axis=-1, keepdims=True)
    e_lo = N_GROUPS + EXPERTS_PER_GROUP * g_idx
    emask = (lane >= e_lo) & (lane < e_lo + EXPERTS_PER_GROUP)
    el = jnp.where(emask, logit, -jnp.inf)
    v0 = jnp.max(el, axis=-1, keepdims=True)
    i0 = jnp.min(jnp.where(el == v0, lane, big), axis=-1, keepdims=True)
    el = jnp.where(lane == i0, -jnp.inf, el)
    v1 = jnp.max(el, axis=-1, keepdims=True)
    i1 = jnp.min(jnp.where(el == v1, lane, big), axis=-1, keepdims=True)
    t = jnp.exp(v1 - v0)
    w0 = g_w / (1.0 + t)
    w1 = g_w * t / (1.0 + t)
    e0 = i0 - N_GROUPS
    e1 = i1 - N_GROUPS

    oh0 = lane == e0
    oh1 = lane == e1
    oh = jnp.where(oh0 | oh1, 1.0, 0.0)
    before = jnp.dot(tri_ref[...], oh.astype(BF16), preferred_element_type=F32) + run_ref[0:1, :]
    r0 = jnp.sum(jnp.where(oh0, before, 0.0), axis=-1, keepdims=True).astype(jnp.int32)
    r1 = jnp.sum(jnp.where(oh1, before, 0.0), axis=-1, keepdims=True).astype(jnp.int32)
    run_new = run_ref[0:1, :] + jnp.sum(oh, axis=0, keepdims=True)
    run_ref[...] = jnp.broadcast_to(run_new, run_ref.shape)
    cnt_ref[...] = jnp.broadcast_to(run_new, cnt_ref.shape)

    ri = jnp.where(lane == 0, e0, jnp.where(lane == 1, e1, jnp.where(lane == 2, r0, jnp.where(lane == 3, r1, 0))))
    ri_ref[...] = ri.T[0:8]
    rf_ref[...] = jnp.where(lane == 0, w0, jnp.where(lane == 1, w1, 0.0))


def _post(x2, ya2, z2, ga1, sc2, sh2, conv_w, conv_b, wpa, wpb, wo, g2, wr, br, tri, tiles_per_seq):
    T, D = x2.shape
    tm = ROW_TILE
    cw = CONV_WIDTH
    xcol = 3 * ATTN_WIDTH // cw
    gcol = (3 * ATTN_WIDTH + 3 * cw) // D
    bmap = lambda i: (i // tiles_per_seq, 0, 0)
    const = lambda i: (0, 0)
    return pl.pallas_call(
        functools.partial(_post_kernel, tiles_per_seq=tiles_per_seq),
        grid=(T // tm,),
        in_specs=[pl.BlockSpec((tm, D), lambda i: (i, 0)),
                  pl.BlockSpec((tm, ATTN_WIDTH), lambda i: (i, 0)),
                  pl.BlockSpec((tm, cw), lambda i: (i, xcol)),
                  pl.BlockSpec((tm, cw), lambda i: (i, xcol + 1)),
                  pl.BlockSpec((tm, cw), lambda i: (i, xcol + 2)),
                  pl.BlockSpec((tm, D), lambda i: (i, gcol)),
                  pl.BlockSpec((tm, D), lambda i: (i, gcol + 1)),
                  pl.BlockSpec((1, 1, D), bmap),
                  pl.BlockSpec((1, 1, D), bmap),
                  pl.BlockSpec((1, 1, D), bmap),
                  pl.BlockSpec((CONV_K, cw), const),
                  pl.BlockSpec((1, cw), const),
                  pl.BlockSpec((ATTN_WIDTH, D), const),
                  pl.BlockSpec((cw, D), const),
                  pl.BlockSpec((D, D), const),
                  pl.BlockSpec((1, D), const),
                  pl.BlockSpec((D, LANES), const),
                  pl.BlockSpec((1, LANES), const),
                  pl.BlockSpec((tm, tm), const)],
        out_specs=[pl.BlockSpec((tm, D), lambda i: (i, 0)),
                   pl.BlockSpec((tm, D // 2), lambda i: (i, 0)),
                   pl.BlockSpec((8, tm), lambda i: (0, i)),
                   pl.BlockSpec((tm, LANES), lambda i: (i, 0)),
                   pl.BlockSpec((8, LANES), const)],
        out_shape=[jax.ShapeDtypeStruct((T, D), F32),
                   jax.ShapeDtypeStruct((T, D // 2), jnp.uint32),
                   jax.ShapeDtypeStruct((8, T), jnp.int32),
                   jax.ShapeDtypeStruct((T, LANES), F32),
                   jax.ShapeDtypeStruct((8, LANES), F32)],
        scratch_shapes=[pltpu.VMEM((tm + 16, cw), F32),
                        pltpu.VMEM((8, LANES), F32)],
        compiler_params=_cparams("arbitrary"),
        name="post",
    )(x2, ya2, z2, z2, z2, z2, z2, ga1, sc2, sh2, conv_w, conv_b, wpa, wpb, wo, g2, wr, br, tri)


def _slots_kernel(ps_ref, ri_ref, d_ref):
    e = ri_ref[0:TOPK_IN_GROUP, :]
    start = jnp.zeros(e.shape, jnp.int32)
    for k in range(N_EXPERTS):
        start = jnp.where(e == k, ps_ref[k], start)
    d_ref[...] = start + ri_ref[TOPK_IN_GROUP:2 * TOPK_IN_GROUP, :]


def _slots(pad_start, riT):
    T = riT.shape[1]
    grid_spec = pltpu.PrefetchScalarGridSpec(
        num_scalar_prefetch=1,
        grid=(1,),
        in_specs=[pl.BlockSpec(riT.shape, lambda i, ps: (0, 0))],
        out_specs=pl.BlockSpec((TOPK_IN_GROUP, T), lambda i, ps: (0, 0)),
    )
    return pl.pallas_call(
        _slots_kernel,
        grid_spec=grid_spec,
        out_shape=jax.ShapeDtypeStruct((TOPK_IN_GROUP, T), jnp.int32),
        compiler_params=_cparams("arbitrary"),
        name="slots",
    )(pad_start, riT)


def _dispatch_kernel(d0_ref, d1_ref, h_ref, xs_in_ref, xs_ref, sem):
    del xs_in_ref
    tm = h_ref.shape[0]
    base = pl.program_id(0) * tm

    def body(g, carry):
        for u in range(DMA_UNROLL):
            r = g * DMA_UNROLL + u
            src = h_ref.at[pl.ds(r, 1), :]
            pltpu.make_async_copy(src, xs_ref.at[pl.ds(d0_ref[base + r], 1), :], sem).start()
            pltpu.make_async_copy(src, xs_ref.at[pl.ds(d1_ref[base + r], 1), :], sem).start()
        return carry

    lax.fori_loop(0, tm // DMA_UNROLL, body, 0)
    for _ in range(TOPK_IN_GROUP):
        pltpu.make_async_copy(h_ref, xs_ref.at[pl.ds(0, tm), :], sem).wait()


def _dispatch(dest0, dest1, h2p, n_pad):
    T, W = h2p.shape
    tm = ROW_TILE
    grid_spec = pltpu.PrefetchScalarGridSpec(
        num_scalar_prefetch=2,
        grid=(T // tm,),
        in_specs=[pl.BlockSpec((tm, W), lambda i, d0, d1: (i, 0)),
                  pl.BlockSpec(memory_space=pl.ANY)],
        out_specs=pl.BlockSpec(memory_space=pl.ANY),
        scratch_shapes=[pltpu.SemaphoreType.DMA(())],
    )
    return pl.pallas_call(
        _dispatch_kernel,
        grid_spec=grid_spec,
        out_shape=jax.ShapeDtypeStruct((n_pad, W), h2p.dtype),
        input_output_aliases={3: 0},
        compiler_params=_cparams("arbitrary"),
        name="dispatch",
    )(dest0, dest1, h2p, jnp.zeros((n_pad, W), h2p.dtype))


def _expert_kernel(be_ref, xs_ref, w1_ref, w3_ref, w2_ref, ys_ref):
    xp = xs_ref[...]
    half_d = xp.shape[1]
    x_lo = pltpu.bitcast(xp << 16, F32).astype(BF16)
    x_hi = pltpu.bitcast(xp & jnp.uint32(0xFFFF0000), F32).astype(BF16)
    a = (jnp.dot(x_lo, w1_ref[0, :half_d], preferred_element_type=F32)
         + jnp.dot(x_hi, w1_ref[0, half_d:], preferred_element_type=F32))
    b = (jnp.dot(x_lo, w3_ref[0, :half_d], preferred_element_type=F32)
         + jnp.dot(x_hi, w3_ref[0, half_d:], preferred_element_type=F32))
    hid = (a * jax.nn.sigmoid(a) * b).astype(BF16)
    ys_ref[...] = jnp.dot(hid, w2_ref[0], preferred_element_type=F32)


def _experts(blk_expert, xs, w1b, w3b, w2b):
    n_pad, W = xs.shape
    _, D, F = w1b.shape
    rb = EXPERT_ROWS
    grid_spec = pltpu.PrefetchScalarGridSpec(
        num_scalar_prefetch=1,
        grid=(n_pad // rb,),
        in_specs=[pl.BlockSpec((rb, W), lambda i, be: (i, 0)),
                  pl.BlockSpec((1, D, F), lambda i, be: (be[i], 0, 0)),
                  pl.BlockSpec((1, D, F), lambda i, be: (be[i], 0, 0)),
                  pl.BlockSpec((1, F, D), lambda i, be: (be[i], 0, 0))],
        out_specs=pl.BlockSpec((rb, D), lambda i, be: (i, 0)),
    )
    return pl.pallas_call(
        _expert_kernel,
        grid_spec=grid_spec,
        out_shape=jax.ShapeDtypeStruct((n_pad, D), F32),
        compiler_params=_cparams("arbitrary"),
        name="experts",
    )(blk_expert, xs, w1b, w3b, w2b)


def _combine_kernel(d0_ref, d1_ref, x1_ref, rf_ref, ga2_ref, ys_ref, o_ref, buf_ref, sem):
    i = pl.program_id(0)
    n = pl.num_programs(0)
    tm = x1_ref.shape[0]

    def gather(step, slot):
        base = step * tm

        def body(g, carry):
            for u in range(DMA_UNROLL):
                r = g * DMA_UNROLL + u
                pltpu.make_async_copy(ys_ref.at[pl.ds(d0_ref[base + r], 1), :],
                                      buf_ref.at[slot, 0, pl.ds(r, 1), :], sem.at[slot]).start()
                pltpu.make_async_copy(ys_ref.at[pl.ds(d1_ref[base + r], 1), :],
                                      buf_ref.at[slot, 1, pl.ds(r, 1), :], sem.at[slot]).start()
            return carry

        lax.fori_loop(0, tm // DMA_UNROLL, body, 0)

    @pl.when(i == 0)
    def _():
        gather(0, 0)

    @pl.when(i + 1 < n)
    def _():
        gather(i + 1, (i + 1) % 2)

    slot = i % 2
    for k in range(TOPK_IN_GROUP):
        pltpu.make_async_copy(ys_ref.at[pl.ds(0, tm), :], buf_ref.at[slot, k], sem.at[slot]).wait()
    rf = rf_ref[...]
    y = rf[:, 0:1] * buf_ref[slot, 0] + rf[:, 1:2] * buf_ref[slot, 1]
    o_ref[...] = x1_ref[...] + ga2_ref[0] * y


def _combine(dest0, dest1, x1, rf, ga2, ys, tiles_per_seq):
    T, D = x1.shape
    tm = ROW_TILE
    row = lambda i, d0, d1: (i, 0)
    grid_spec = pltpu.PrefetchScalarGridSpec(
        num_scalar_prefetch=2,
        grid=(T // tm,),
        in_specs=[pl.BlockSpec((tm, D), row),
                  pl.BlockSpec((tm, LANES), row),
                  pl.BlockSpec((1, 1, D), lambda i, d0, d1: (i // tiles_per_seq, 0, 0)),
                  pl.BlockSpec(memory_space=pl.ANY)],
        out_specs=pl.BlockSpec((tm, D), row),
        scratch_shapes=[pltpu.VMEM((2, TOPK_IN_GROUP, tm, D), F32),
                        pltpu.SemaphoreType.DMA((2,))],
    )
    return pl.pallas_call(
        _combine_kernel,
        grid_spec=grid_spec,
        out_shape=jax.ShapeDtypeStruct((T, D), F32),
        compiler_params=_cparams("arbitrary"),
        name="combine",
    )(dest0, dest1, x1, rf, ga2, ys)


def _rope_tables(S):
    pos = jnp.arange(S, dtype=F32)
    inv_freq = ROPE_THETA ** (-jnp.arange(0, ROT_DIM, 2, dtype=F32) / ROT_DIM)
    ang = pos[:, None] * inv_freq[None, :]
    cos, sin = jnp.cos(ang), jnp.sin(ang)
    half = ROT_DIM // 2
    ones = jnp.ones((S, HEAD_DIM - ROT_DIM), F32)
    cos_h = jnp.concatenate([cos, cos, ones], axis=1)
    sin_h = jnp.concatenate([-sin, sin, 0.0 * ones], axis=1)
    return jnp.tile(cos_h, (1, LANES // HEAD_DIM)), jnp.tile(sin_h, (1, LANES // HEAD_DIM))


def kernel(x, c, w_ada, b_ada, g_norm1, g_norm2, w_in, g_q, g_k, conv_w, conv_b,
           w_pa, w_pb, w_o, w_rg, b_rg, w_re, b_re, w1, w3, w2):
    B, S, D = x.shape
    T = B * S
    assert S % ROW_TILE == 0 and S % QUERY_TILE == 0 and QUERY_TILE % (2 * MOBA_BLOCK) == 0
    assert S // MOBA_BLOCK <= LANES - HEAD_DIM
    tiles_per_seq = S // ROW_TILE
    l = 0

    mod = _ada(c, w_ada[l], b_ada[l])
    sh1, sc1, ga1, sh2, sc2, ga2 = [m.reshape(B, 1, D) for m in jnp.split(mod, N_MOD, axis=-1)]

    x2 = x.reshape(T, D)
    z2 = _inproj(x2, g_norm1[l].reshape(1, D), sc1, sh1, w_in[l].astype(BF16), tiles_per_seq)

    cosf, sinf = _rope_tables(S)
    rep = LANES // HEAD_DIM
    ya = _attention(z2.reshape(B, S, -1), cosf, sinf,
                    jnp.tile(g_q[l], rep).reshape(1, LANES), jnp.tile(g_k[l], rep).reshape(1, LANES))

    wr = jnp.zeros((D, LANES), F32).at[:, :N_GROUPS].set(w_rg[l]).at[:, N_GROUPS:N_GROUPS + N_EXPERTS].set(w_re[l])
    br = jnp.zeros((1, LANES), F32).at[0, :N_GROUPS].set(b_rg[l]).at[0, N_GROUPS:N_GROUPS + N_EXPERTS].set(b_re[l])
    tri = (lax.broadcasted_iota(jnp.int32, (ROW_TILE, ROW_TILE), 1)
           < lax.broadcasted_iota(jnp.int32, (ROW_TILE, ROW_TILE), 0)).astype(BF16)
    x1, h2, ri, rf, cnt = _post(x2, ya.reshape(T, ATTN_WIDTH), z2, ga1, sc2, sh2,
                                conv_w[l], conv_b[l].reshape(1, CONV_WIDTH),
                                w_pa[l].astype(BF16), w_pb[l].astype(BF16), w_o[l].astype(BF16),
                                g_norm2[l].reshape(1, D), wr, br, tri, tiles_per_seq)

    rb = EXPERT_ROWS
    counts = cnt[0, :N_EXPERTS].astype(jnp.int32)
    padded = (counts + rb - 1) // rb * rb
    pad_end = jnp.cumsum(padded)
    pad_start = pad_end - padded
    dest = _slots(pad_start.astype(jnp.int32), ri)
    dest0, dest1 = dest[0], dest[1]
    n_blocks = -(-T * TOPK_IN_GROUP // rb) + N_EXPERTS
    n_pad = n_blocks * rb
    blk_start = jnp.arange(n_blocks, dtype=jnp.int32) * rb
    blk_expert = jnp.minimum(jnp.sum(pad_end[None, :] <= blk_start[:, None], axis=-1), N_EXPERTS - 1).astype(jnp.int32)

    xs = _dispatch(dest0, dest1, h2, n_pad)
    ys = _experts(blk_expert, xs, w1[l].astype(BF16), w3[l].astype(BF16), w2[l].astype(BF16))
    out = _combine(dest0, dest1, x1, rf, ga2, ys, tiles_per_seq)
    return out.reshape(B, S, D)
```

```python
import functools

import jax
import jax.numpy as jnp
from jax import lax
from jax.experimental import pallas as pl
from jax.experimental.pallas import tpu as pltpu

F32 = jnp.float32
BF16 = jnp.bfloat16
HIGHEST = lax.Precision.HIGHEST

N_HEADS = 8
HEAD_DIM = 64
ATTN_WIDTH = N_HEADS * HEAD_DIM
CONV_WIDTH = 512
CONV_K = 3
MOBA_BLOCK = 256
MOBA_TOPK = 3
ROPE_THETA = 500000.0
ROT_DIM = HEAD_DIM // 4
N_GROUPS = 4
EXPERTS_PER_GROUP = 8
N_EXPERTS = N_GROUPS * EXPERTS_PER_GROUP
TOPK_IN_GROUP = 2
N_MOD = 6
EPS = 1e-6

LANES = 128
NEG = -1e30
ROW_TILE = 512
POST_CHUNK = 256
QUERY_TILE = 512
DMA_UNROLL = 8
EXPERT_ROWS = 256
VMEM_LIMIT = 56 * 1024 * 1024


def _cparams(*sem):
    return pltpu.CompilerParams(dimension_semantics=sem, vmem_limit_bytes=VMEM_LIMIT)


def _ada_kernel(c_ref, w_ref, b_ref, o_ref):
    c = c_ref[...]
    a = c * jax.nn.sigmoid(c)
    o_ref[...] = jnp.dot(a, w_ref[...], preferred_element_type=F32, precision=HIGHEST) + b_ref[...]


def _ada(c, w_ada, b_ada):
    B, D = c.shape
    N = w_ada.shape[1]
    tn = 1536
    return pl.pallas_call(
        _ada_kernel,
        grid=(N // tn,),
        in_specs=[pl.BlockSpec((B, D), lambda j: (0, 0)),
                  pl.BlockSpec((D, tn), lambda j: (0, j)),
                  pl.BlockSpec((1, tn), lambda j: (0, j))],
        out_specs=pl.BlockSpec((B, tn), lambda j: (0, j)),
        out_shape=jax.ShapeDtypeStruct((B, N), F32),
        compiler_params=_cparams("arbitrary"),
        name="ada",
    )(c, w_ada, b_ada.reshape(1, N))


def _inproj_kernel(x_ref, g_ref, sc_ref, sh_ref, w_ref, z_ref, *, n_chunk):
    x = x_ref[...]
    ms = jnp.mean(x * x, axis=-1, keepdims=True)
    y = x * lax.rsqrt(ms + EPS) * g_ref[...]
    h = (y * (1.0 + sc_ref[0]) + sh_ref[0]).astype(BF16)
    for n in range(0, z_ref.shape[1], n_chunk):
        z_ref[:, n:n + n_chunk] = jnp.dot(h, w_ref[:, n:n + n_chunk],
                                          preferred_element_type=F32).astype(BF16)


def _inproj(x2, g1, sc1, sh1, w_in_bf, tiles_per_seq):
    T, D = x2.shape
    N = w_in_bf.shape[1]
    tm = ROW_TILE
    bmap = lambda i: (i // tiles_per_seq, 0, 0)
    return pl.pallas_call(
        functools.partial(_inproj_kernel, n_chunk=512),
        grid=(T // tm,),
        in_specs=[pl.BlockSpec((tm, D), lambda i: (i, 0)),
                  pl.BlockSpec((1, D), lambda i: (0, 0)),
                  pl.BlockSpec((1, 1, D), bmap),
                  pl.BlockSpec((1, 1, D), bmap),
                  pl.BlockSpec((D, N), lambda i: (0, 0))],
        out_specs=pl.BlockSpec((tm, N), lambda i: (i, 0)),
        out_shape=jax.ShapeDtypeStruct((T, N), BF16),
        compiler_params=_cparams("arbitrary"),
        name="inproj",
    )(x2, g1, sc1, sh1, w_in_bf)


def _fold_rows(x, op):
    parts = [x[r:r + 8] for r in range(0, x.shape[0], 8)]
    while len(parts) > 1:
        parts = [op(parts[i], parts[i + 1]) for i in range(0, len(parts) - 1, 2)] + (
            [parts[-1]] if len(parts) % 2 else [])
    return parts[0]


def _attn_kernel(q_ref, k_ref, v_ref, cos_ref, sin_ref, gq_ref, gk_ref, o_ref,
                 kaug_ref, vt_ref, kmp_ref, s_ref):
    S = q_ref.shape[1]
    blk = MOBA_BLOCK
    qt = QUERY_TILE
    sub = qt // blk
    nb = S // blk
    nbp = kmp_ref.shape[0]

    def norm_rope(xb, g, r0):
        rows = xb.shape[0]
        lane = lax.broadcasted_iota(jnp.int32, (rows, LANES), 1)
        head0 = lane < HEAD_DIM
        x = xb.astype(F32)
        sq = x * x
        s0 = jnp.sum(jnp.where(head0, sq, 0.0), axis=-1, keepdims=True)
        s1 = jnp.sum(jnp.where(head0, 0.0, sq), axis=-1, keepdims=True)
        inv = jnp.where(head0, lax.rsqrt(s0 * (1.0 / HEAD_DIM) + EPS), lax.rsqrt(s1 * (1.0 / HEAD_DIM) + EPS))
        y = x * inv * g
        half = ROT_DIM // 2
        rot_lo = (lane & (HEAD_DIM - 1)) < half
        rot = jnp.where(rot_lo, pltpu.roll(y, LANES - half, 1), pltpu.roll(y, half, 1))
        return y * cos_ref[pl.ds(r0, rows), :] + rot * sin_ref[pl.ds(r0, rows), :]

    kmp_ref[...] = jnp.zeros_like(kmp_ref)
    ones_row = jnp.where(lax.broadcasted_iota(jnp.int32, (16, blk), 0) == 0, 1.0, 0.0).astype(BF16)
    lane_k = lax.broadcasted_iota(jnp.int32, (blk, LANES), 1)
    head0_k = lane_k < HEAD_DIM

    def kbody(j, carry):
        r0 = pl.multiple_of(j * blk, blk)
        kr = norm_rope(k_ref[0, pl.ds(r0, blk), :], gk_ref[...], r0)
        kmp_ref[pl.ds(j, 1), :] = jnp.sum(kr, axis=0, keepdims=True) * (1.0 / blk)
        onehot = jnp.where(lane_k - HEAD_DIM == j, 1.0, 0.0)
        kaug_ref[0, j] = jnp.where(head0_k, kr, onehot).astype(BF16)
        kaug_ref[1, j] = jnp.where(head0_k, pltpu.roll(kr, HEAD_DIM, 1), onehot).astype(BF16)
        vT = v_ref[0, pl.ds(r0, blk), :].astype(F32).T
        for h in range(2):
            vt_ref[h, j, 0:HEAD_DIM, :] = vT[h * HEAD_DIM:(h + 1) * HEAD_DIM].astype(BF16)
            vt_ref[h, j, HEAD_DIM:HEAD_DIM + 16, :] = ones_row
        return carry

    lax.fori_loop(0, nb, kbody, 0)

    key_i = lax.broadcasted_iota(jnp.int32, (blk, qt), 0)
    qry_i = lax.broadcasted_iota(jnp.int32, (blk, qt), 1)
    causal = [(qry_i < u * blk) | (qry_i >= (u + 1) * blk) | (key_i <= qry_i - u * blk) for u in range(sub)]
    rowf = lax.broadcasted_iota(jnp.int32, (nbp, qt), 0).astype(F32)
    subf = (lax.broadcasted_iota(jnp.int32, (nbp, qt), 1) // blk).astype(F32)
    row128 = lax.broadcasted_iota(jnp.int32, (LANES, qt), 0)
    q_scale = (HEAD_DIM ** -0.5) * 1.4426950408889634

    def qbody(qi, carry):
        r0 = pl.multiple_of(qi * qt, qt)
        j0 = qi * sub
        qT = norm_rope(q_ref[0, pl.ds(r0, qt), :], gq_ref[...], r0).T
        cur = lax.convert_element_type(j0, F32) + subf
        qa = []
        for h in range(2):
            qmT = jnp.where(row128 < HEAD_DIM, qT, 0.0) if h == 0 else jnp.where(row128 < HEAD_DIM, 0.0, qT)
            gate = jnp.dot(kmp_ref[...], qmT, precision=HIGHEST, preferred_element_type=F32)
            g = jnp.where(rowf < cur, gate, -jnp.inf)
            keep = rowf == cur
            for r in range(MOBA_TOPK):
                m = jnp.max(g, axis=0, keepdims=True)
                idx = jnp.min(jnp.where(g == m, rowf, 1e9), axis=0, keepdims=True)
                pick = (rowf == idx) & (cur > r)
                keep = keep | pick
                g = jnp.where(pick, -jnp.inf, g)
            bias = jnp.where(keep, 0.0, NEG)
            qs = qT[h * HEAD_DIM:(h + 1) * HEAD_DIM] * q_scale
            pad = jnp.zeros((LANES - HEAD_DIM - nbp, qt), F32)
            qa.append(jnp.concatenate([qs, bias, pad], axis=0).astype(BF16))

        mx = [None, None]
        for u in range(sub):
            for h in range(2):
                sT = jnp.dot(kaug_ref[h, j0 + u], qa[h], preferred_element_type=F32)
                sT = jnp.where(causal[u], sT, NEG)
                s_ref[h, j0 + u] = sT
                f = _fold_rows(sT, jnp.maximum)
                mx[h] = f if mx[h] is None else jnp.maximum(mx[h], f)
        n_pair = j0 // 2

        def p1(p, mx):
            mx = list(mx)
            for u in range(2):
                j = 2 * p + u
                for h in range(2):
                    sT = jnp.dot(kaug_ref[h, j], qa[h], preferred_element_type=F32)
                    s_ref[h, j] = sT
                    mx[h] = jnp.maximum(mx[h], _fold_rows(sT, jnp.maximum))
            return tuple(mx)

        mx = lax.fori_loop(0, n_pair, p1, tuple(mx))
        mcol = [jnp.max(mx[h], axis=0, keepdims=True) for h in range(2)]

        acc = [None, None]
        for u in range(sub):
            for h in range(2):
                pT = jnp.exp2(s_ref[h, j0 + u] - mcol[h]).astype(BF16)
                d = jnp.dot(vt_ref[h, j0 + u], pT, preferred_element_type=F32)
                acc[h] = d if acc[h] is None else acc[h] + d

        def p2(p, acc):
            acc = list(acc)
            for u in range(2):
                j = 2 * p + u
                for h in range(2):
                    pT = jnp.exp2(s_ref[h, j] - mcol[h]).astype(BF16)
                    acc[h] = acc[h] + jnp.dot(vt_ref[h, j], pT, preferred_element_type=F32)
            return tuple(acc)

        acc = lax.fori_loop(0, n_pair, p2, tuple(acc))
        outT = jnp.concatenate([acc[h][0:HEAD_DIM] / acc[h][HEAD_DIM:HEAD_DIM + 1] for h in range(2)], axis=0)
        o_ref[0, pl.ds(r0, qt), :] = outT.T.astype(BF16)
        return carry

    lax.fori_loop(0, S // qt, qbody, 0)


def _attention(z3, cosf, sinf, gq2, gk2):
    B, S, _ = z3.shape
    n_pair = N_HEADS // 2
    kq = ATTN_WIDTH // LANES
    nb = S // MOBA_BLOCK
    nbp = -(-nb // 16) * 16
    return pl.pallas_call(
        _attn_kernel,
        grid=(B, n_pair),
        in_specs=[pl.BlockSpec((1, S, LANES), lambda b, p: (b, 0, p)),
                  pl.BlockSpec((1, S, LANES), lambda b, p: (b, 0, kq + p)),
                  pl.BlockSpec((1, S, LANES), lambda b, p: (b, 0, 2 * kq + p)),
                  pl.BlockSpec((S, LANES), lambda b, p: (0, 0)),
                  pl.BlockSpec((S, LANES), lambda b, p: (0, 0)),
                  pl.BlockSpec((1, LANES), lambda b, p: (0, 0)),
                  pl.BlockSpec((1, LANES), lambda b, p: (0, 0))],
        out_specs=pl.BlockSpec((1, S, LANES), lambda b, p: (b, 0, p)),
        out_shape=jax.ShapeDtypeStruct((B, S, ATTN_WIDTH), BF16),
        scratch_shapes=[pltpu.VMEM((2, nb, MOBA_BLOCK, LANES), BF16),
                        pltpu.VMEM((2, nb, HEAD_DIM + 16, MOBA_BLOCK), BF16),
                        pltpu.VMEM((nbp, LANES), F32),
                        pltpu.VMEM((2, nb, MOBA_BLOCK, QUERY_TILE), F32)],
        compiler_params=_cparams("arbitrary", "arbitrary"),
        name="attn",
    )(z3, z3, z3, cosf, sinf, gq2, gk2)


def _post_kernel(x_ref, ya_ref, xb_ref, bg_ref, cg_ref, gta_ref, gtb_ref, ga1_ref, sc2_ref, sh2_ref,
                 cw_ref, cb_ref, wpa_ref, wpb_ref, wo_ref, g2_ref, wr_ref, br_ref, tri_ref,
                 x1_ref, h2_ref, ri_ref, rf_ref, cnt_ref, ubuf_ref, run_ref, *, tiles_per_seq):
    i = pl.program_id(0)
    tm = x_ref.shape[0]
    rc = POST_CHUNK
    halo = 8

    @pl.when(i == 0)
    def _():
        run_ref[...] = jnp.zeros_like(run_ref)

    @pl.when(i % tiles_per_seq == 0)
    def _():
        ubuf_ref[0:halo, :] = jnp.zeros((halo, CONV_WIDTH), F32)

    ubuf_ref[halo:halo + tm, :] = cg_ref[...].astype(F32) * xb_ref[...].astype(F32)
    cw = cw_ref[...]
    lanef = lax.broadcasted_iota(jnp.int32, (rc, LANES), 1).astype(F32)
    half_d = x_ref.shape[1] // 2
    run = run_ref[0:1, :]

    for c in range(tm // rc):
        rows = pl.ds(c * rc, rc)
        conv = (cw[0:1, :] * ubuf_ref[pl.ds(halo - 2 + c * rc, rc), :]
                + cw[1:2, :] * ubuf_ref[pl.ds(halo - 1 + c * rc, rc), :]
                + cw[2:3, :] * ubuf_ref[pl.ds(halo + c * rc, rc), :])
        y_b = (bg_ref[rows, :].astype(F32) * (conv + cb_ref[...])).astype(BF16)
        pa = jnp.dot(ya_ref[rows, :], wpa_ref[...], preferred_element_type=F32)
        pb = jnp.dot(y_b, wpb_ref[...], preferred_element_type=F32)
        merged = (jax.nn.sigmoid(gta_ref[rows, :].astype(F32)) * pa
                  + jax.nn.sigmoid(gtb_ref[rows, :].astype(F32)) * pb).astype(BF16)
        x1 = x_ref[rows, :] + ga1_ref[0] * jnp.dot(merged, wo_ref[...], preferred_element_type=F32)
        x1_ref[rows, :] = x1

        ms = jnp.mean(x1 * x1, axis=-1, keepdims=True)
        h2 = x1 * lax.rsqrt(ms + EPS) * g2_ref[...]
        h2 = h2 * (1.0 + sc2_ref[0]) + sh2_ref[0]
        h_hi = h2.astype(BF16)
        h_hi32 = h_hi.astype(F32)
        bits = pltpu.bitcast(h_hi32, jnp.uint32)
        h2_ref[rows, :] = (bits[:, :half_d] >> 16) | bits[:, half_d:]

        h_lo = (h2 - h_hi32).astype(BF16)
        r = jnp.dot(h_hi, wr_ref[...], preferred_element_type=F32)
        logit = (r[:, :LANES] + r[:, LANES:]
                 + jnp.dot(h_lo, wr_ref[:, :LANES], preferred_element_type=F32) + br_ref[...])
        gl = jnp.where(lanef < N_GROUPS, logit, -jnp.inf)
        gmax = jnp.max(gl, axis=-1, keepdims=True)
        g_idx = jnp.min(jnp.where(gl == gmax, lanef, 1e9), axis=-1, keepdims=True)
        g_w = 1.0 / jnp.sum(jnp.exp(gl - gmax), axis=-1, keepdims=True)
        e_lo = N_GROUPS + EXPERTS_PER_GROUP * g_idx
        el = jnp.where((lanef >= e_lo) & (lanef < e_lo + EXPERTS_PER_GROUP), logit, -jnp.inf)
        v0 = jnp.max(el, axis=-1, keepdims=True)
        i0 = jnp.min(jnp.where(el == v0, lanef, 1e9), axis=-1, keepdims=True)
        el = jnp.where(lanef == i0, -jnp.inf, el)
        v1 = jnp.max(el, axis=-1, keepdims=True)
        i1 = jnp.min(jnp.where(el == v1, lanef, 1e9), axis=-1, keepdims=True)
        t = jnp.exp(v1 - v0)
        w0 = g_w / (1.0 + t)
        w1 = g_w * t / (1.0 + t)
        e0 = i0 - N_GROUPS
        e1 = i1 - N_GROUPS

        oh0 = lanef == e0
        oh1 = lanef == e1
        oh = jnp.where(oh0 | oh1, 1.0, 0.0)
        before = jnp.dot(tri_ref[...], oh.astype(BF16), preferred_element_type=F32) + run
        r0 = jnp.sum(jnp.where(oh0, before, 0.0), axis=-1, keepdims=True)
        r1 = jnp.sum(jnp.where(oh1, before, 0.0), axis=-1, keepdims=True)
        run = run + jnp.sum(oh, axis=0, keepdims=True)

        ri = jnp.where(lanef == 0, e0, jnp.where(lanef == 1, e1, jnp.where(lanef == 2, r0, jnp.where(lanef == 3, r1, 0.0))))
        ri_ref[:, rows] = ri.astype(jnp.int32).T[0:8]
        rf_ref[rows, :] = jnp.where(lanef == 0, w0, jnp.where(lanef == 1, w1, 0.0))

    ubuf_ref[0:halo, :] = ubuf_ref[tm:tm + halo, :]
    run_ref[...] = jnp.broadcast_to(run, run_ref.shape)
    cnt_ref[...] = jnp.broadcast_to(run, cnt_ref.shape)


def _post(x2, ya2, z2, ga1, sc2, sh2, conv_w, conv_b, wpa, wpb, wo, g2, wr, br, tri, tiles_per_seq):
    T, D = x2.shape
    tm = ROW_TILE
    cw = CONV_WIDTH
    xcol = 3 * ATTN_WIDTH // cw
    gcol = (3 * ATTN_WIDTH + 3 * cw) // D
    bmap = lambda i: (i // tiles_per_seq, 0, 0)
    const = lambda i: (0, 0)
    return pl.pallas_call(
        functools.partial(_post_kernel, tiles_per_seq=tiles_per_seq),
        grid=(T // tm,),
        in_specs=[pl.BlockSpec((tm, D), lambda i: (i, 0)),
                  pl.BlockSpec((tm, ATTN_WIDTH), lambda i: (i, 0)),
                  pl.BlockSpec((tm, cw), lambda i: (i, xcol)),
                  pl.BlockSpec((tm, cw), lambda i: (i, xcol + 1)),
                  pl.BlockSpec((tm, cw), lambda i: (i, xcol + 2)),
                  pl.BlockSpec((tm, D), lambda i: (i, gcol)),
                  pl.BlockSpec((tm, D), lambda i: (i, gcol + 1)),
                  pl.BlockSpec((1, 1, D), bmap),
                  pl.BlockSpec((1, 1, D), bmap),
                  pl.BlockSpec((1, 1, D), bmap),
                  pl.BlockSpec((CONV_K, cw), const),
                  pl.BlockSpec((1, cw), const),
                  pl.BlockSpec((ATTN_WIDTH, D), const),
                  pl.BlockSpec((cw, D), const),
                  pl.BlockSpec((D, D), const),
                  pl.BlockSpec((1, D), const),
                  pl.BlockSpec((D, 2 * LANES), const),
                  pl.BlockSpec((1, LANES), const),
                  pl.BlockSpec((POST_CHUNK, POST_CHUNK), const)],
        out_specs=[pl.BlockSpec((tm, D), lambda i: (i, 0)),
                   pl.BlockSpec((tm, D // 2), lambda i: (i, 0)),
                   pl.BlockSpec((8, tm), lambda i: (0, i)),
                   pl.BlockSpec((tm, LANES), lambda i: (i, 0)),
                   pl.BlockSpec((8, LANES), const)],
        out_shape=[jax.ShapeDtypeStruct((T, D), F32),
                   jax.ShapeDtypeStruct((T, D // 2), jnp.uint32),
                   jax.ShapeDtypeStruct((8, T), jnp.int32),
                   jax.ShapeDtypeStruct((T, LANES), F32),
                   jax.ShapeDtypeStruct((8, LANES), F32)],
        scratch_shapes=[pltpu.VMEM((tm + 16, cw), F32),
                        pltpu.VMEM((8, LANES), F32)],
        compiler_params=_cparams("arbitrary"),
        name="post",
    )(x2, ya2, z2, z2, z2, z2, z2, ga1, sc2, sh2, conv_w, conv_b, wpa, wpb, wo, g2, wr, br, tri)


def _slots_kernel(ps_ref, ri_ref, d_ref):
    e = ri_ref[0:TOPK_IN_GROUP, :]
    start = jnp.zeros(e.shape, jnp.int32)
    for k in range(N_EXPERTS):
        start = jnp.where(e == k, ps_ref[k], start)
    d_ref[...] = start + ri_ref[TOPK_IN_GROUP:2 * TOPK_IN_GROUP, :]


def _slots(pad_start, riT):
    T = riT.shape[1]
    grid_spec = pltpu.PrefetchScalarGridSpec(
        num_scalar_prefetch=1,
        grid=(1,),
        in_specs=[pl.BlockSpec(riT.shape, lambda i, ps: (0, 0))],
        out_specs=pl.BlockSpec((TOPK_IN_GROUP, T), lambda i, ps: (0, 0)),
    )
    return pl.pallas_call(
        _slots_kernel,
        grid_spec=grid_spec,
        out_shape=jax.ShapeDtypeStruct((TOPK_IN_GROUP, T), jnp.int32),
        compiler_params=_cparams("arbitrary"),
        name="slots",
    )(pad_start, riT)


def _dispatch_kernel(d0_ref, d1_ref, h_ref, xs_in_ref, xs_ref, sem):
    del xs_in_ref
    tm = h_ref.shape[0]
    base = pl.program_id(0) * tm

    def body(g, carry):
        for u in range(DMA_UNROLL):
            r = g * DMA_UNROLL + u
            src = h_ref.at[pl.ds(r, 1), :]
            pltpu.make_async_copy(src, xs_ref.at[pl.ds(d0_ref[base + r], 1), :], sem).start()
            pltpu.make_async_copy(src, xs_ref.at[pl.ds(d1_ref[base + r], 1), :], sem).start()
        return carry

    lax.fori_loop(0, tm // DMA_UNROLL, body, 0)
    for _ in range(TOPK_IN_GROUP):
        pltpu.make_async_copy(h_ref, xs_ref.at[pl.ds(0, tm), :], sem).wait()


def _dispatch(dest0, dest1, h2p, n_pad):
    T, W = h2p.shape
    tm = ROW_TILE
    grid_spec = pltpu.PrefetchScalarGridSpec(
        num_scalar_prefetch=2,
        grid=(T // tm,),
        in_specs=[pl.BlockSpec((tm, W), lambda i, d0, d1: (i, 0)),
                  pl.BlockSpec(memory_space=pl.ANY)],
        out_specs=pl.BlockSpec(memory_space=pl.ANY),
        scratch_shapes=[pltpu.SemaphoreType.DMA(())],
    )
    return pl.pallas_call(
        _dispatch_kernel,
        grid_spec=grid_spec,
        out_shape=jax.ShapeDtypeStruct((n_pad, W), h2p.dtype),
        input_output_aliases={3: 0},
        compiler_params=_cparams("arbitrary"),
        name="dispatch",
    )(dest0, dest1, h2p, jnp.zeros((n_pad, W), h2p.dtype))


def _expert_kernel(be_ref, xs_ref, w1_ref, w3_ref, w2_ref, ys_ref):
    xp = xs_ref[...]
    half_d = xp.shape[1]
    x_lo = pltpu.bitcast(xp << 16, F32).astype(BF16)
    x_hi = pltpu.bitcast(xp & jnp.uint32(0xFFFF0000), F32).astype(BF16)
    a = (jnp.dot(x_lo, w1_ref[0, :half_d], preferred_element_type=F32)
         + jnp.dot(x_hi, w1_ref[0, half_d:], preferred_element_type=F32))
    b = (jnp.dot(x_lo, w3_ref[0, :half_d], preferred_element_type=F32)
         + jnp.dot(x_hi, w3_ref[0, half_d:], preferred_element_type=F32))
    hid = (a * jax.nn.sigmoid(a) * b).astype(BF16)
    y = jnp.dot(hid, w2_ref[0], preferred_element_type=F32)
    bits = pltpu.bitcast(y.astype(BF16).astype(F32), jnp.uint32)
    ys_ref[...] = (bits[:, :half_d] >> 16) | bits[:, half_d:]


def _experts(blk_expert, xs, w1b, w3b, w2b):
    n_pad, W = xs.shape
    _, D, F = w1b.shape
    rb = EXPERT_ROWS
    grid_spec = pltpu.PrefetchScalarGridSpec(
        num_scalar_prefetch=1,
        grid=(n_pad // rb,),
        in_specs=[pl.BlockSpec((rb, W), lambda i, be: (i, 0)),
                  pl.BlockSpec((1, D, F), lambda i, be: (be[i], 0, 0)),
                  pl.BlockSpec((1, D, F), lambda i, be: (be[i], 0, 0)),
                  pl.BlockSpec((1, F, D), lambda i, be: (be[i], 0, 0))],
        out_specs=pl.BlockSpec((rb, W), lambda i, be: (i, 0)),
    )
    return pl.pallas_call(
        _expert_kernel,
        grid_spec=grid_spec,
        out_shape=jax.ShapeDtypeStruct((n_pad, W), jnp.uint32),
        compiler_params=_cparams("arbitrary"),
        name="experts",
    )(blk_expert, xs, w1b, w3b, w2b)


def _combine_kernel(d0_ref, d1_ref, x1_ref, rf_ref, ga2_ref, ys_ref, o_ref, buf_ref, sem):
    i = pl.program_id(0)
    n = pl.num_programs(0)
    tm = x1_ref.shape[0]

    def gather(step, slot):
        base = step * tm

        def body(g, carry):
            for u in range(DMA_UNROLL):
                r = g * DMA_UNROLL + u
                pltpu.make_async_copy(ys_ref.at[pl.ds(d0_ref[base + r], 1), :],
                                      buf_ref.at[slot, 0, pl.ds(r, 1), :], sem.at[slot]).start()
                pltpu.make_async_copy(ys_ref.at[pl.ds(d1_ref[base + r], 1), :],
                                      buf_ref.at[slot, 1, pl.ds(r, 1), :], sem.at[slot]).start()
            return carry

        lax.fori_loop(0, tm // DMA_UNROLL, body, 0)

    @pl.when(i == 0)
    def _():
        gather(0, 0)

    @pl.when(i + 1 < n)
    def _():
        gather(i + 1, (i + 1) % 2)

    slot = i % 2
    for k in range(TOPK_IN_GROUP):
        pltpu.make_async_copy(ys_ref.at[pl.ds(0, tm), :], buf_ref.at[slot, k], sem.at[slot]).wait()
    rf = rf_ref[...]
    w = [rf[:, k:k + 1] for k in range(TOPK_IN_GROUP)]
    yp = [buf_ref[slot, k] for k in range(TOPK_IN_GROUP)]
    half_d = yp[0].shape[1]
    y_lo = sum(w[k] * pltpu.bitcast(yp[k] << 16, F32) for k in range(TOPK_IN_GROUP))
    y_hi = sum(w[k] * pltpu.bitcast(yp[k] & jnp.uint32(0xFFFF0000), F32) for k in range(TOPK_IN_GROUP))
    ga2 = ga2_ref[0]
    o_ref[:, :half_d] = x1_ref[:, :half_d] + ga2[:, :half_d] * y_lo
    o_ref[:, half_d:] = x1_ref[:, half_d:] + ga2[:, half_d:] * y_hi


def _combine(dest0, dest1, x1, rf, ga2, ys, tiles_per_seq):
    T, D = x1.shape
    tm = ROW_TILE
    row = lambda i, d0, d1: (i, 0)
    grid_spec = pltpu.PrefetchScalarGridSpec(
        num_scalar_prefetch=2,
        grid=(T // tm,),
        in_specs=[pl.BlockSpec((tm, D), row),
                  pl.BlockSpec((tm, LANES), row),
                  pl.BlockSpec((1, 1, D), lambda i, d0, d1: (i // tiles_per_seq, 0, 0)),
                  pl.BlockSpec(memory_space=pl.ANY)],
        out_specs=pl.BlockSpec((tm, D), row),
        scratch_shapes=[pltpu.VMEM((2, TOPK_IN_GROUP, tm, ys.shape[1]), ys.dtype),
                        pltpu.SemaphoreType.DMA((2,))],
    )
    return pl.pallas_call(
        _combine_kernel,
        grid_spec=grid_spec,
        out_shape=jax.ShapeDtypeStruct((T, D), F32),
        compiler_params=_cparams("arbitrary"),
        name="combine",
    )(dest0, dest1, x1, rf, ga2, ys)


def _rope_tables(S):
    pos = jnp.arange(S, dtype=F32)
    inv_freq = ROPE_THETA ** (-jnp.arange(0, ROT_DIM, 2, dtype=F32) / ROT_DIM)
    ang = pos[:, None] * inv_freq[None, :]
    cos, sin = jnp.cos(ang), jnp.sin(ang)
    half = ROT_DIM // 2
    ones = jnp.ones((S, HEAD_DIM - ROT_DIM), F32)
    cos_h = jnp.concatenate([cos, cos, ones], axis=1)
    sin_h = jnp.concatenate([-sin, sin, 0.0 * ones], axis=1)
    return jnp.tile(cos_h, (1, LANES // HEAD_DIM)), jnp.tile(sin_h, (1, LANES // HEAD_DIM))


def kernel(x, c, w_ada, b_ada, g_norm1, g_norm2, w_in, g_q, g_k, conv_w, conv_b,
           w_pa, w_pb, w_o, w_rg, b_rg, w_re, b_re, w1, w3, w2):
    B, S, D = x.shape
    T = B * S
    assert S % ROW_TILE == 0 and S % QUERY_TILE == 0 and QUERY_TILE % (2 * MOBA_BLOCK) == 0
    assert S // MOBA_BLOCK <= LANES - HEAD_DIM
    tiles_per_seq = S // ROW_TILE
    l = 0

    mod = _ada(c, w_ada[l], b_ada[l])
    sh1, sc1, ga1, sh2, sc2, ga2 = [m.reshape(B, 1, D) for m in jnp.split(mod, N_MOD, axis=-1)]

    x2 = x.reshape(T, D)
    z2 = _inproj(x2, g_norm1[l].reshape(1, D), sc1, sh1, w_in[l].astype(BF16), tiles_per_seq)

    cosf, sinf = _rope_tables(S)
    rep = LANES // HEAD_DIM
    ya = _attention(z2.reshape(B, S, -1), cosf, sinf,
                    jnp.tile(g_q[l], rep).reshape(1, LANES), jnp.tile(g_k[l], rep).reshape(1, LANES))

    wr = jnp.zeros((D, LANES), F32).at[:, :N_GROUPS].set(w_rg[l]).at[:, N_GROUPS:N_GROUPS + N_EXPERTS].set(w_re[l])
    br = jnp.zeros((1, LANES), F32).at[0, :N_GROUPS].set(b_rg[l]).at[0, N_GROUPS:N_GROUPS + N_EXPERTS].set(b_re[l])
    wr_hi = wr.astype(BF16)
    wr2 = jnp.concatenate([wr_hi, (wr - wr_hi.astype(F32)).astype(BF16)], axis=1)
    tri = (lax.broadcasted_iota(jnp.int32, (POST_CHUNK, POST_CHUNK), 1)
           < lax.broadcasted_iota(jnp.int32, (POST_CHUNK, POST_CHUNK), 0)).astype(BF16)
    x1, h2, ri, rf, cnt = _post(x2, ya.reshape(T, ATTN_WIDTH), z2, ga1, sc2, sh2,
                                conv_w[l], conv_b[l].reshape(1, CONV_WIDTH),
                                w_pa[l].astype(BF16), w_pb[l].astype(BF16), w_o[l].astype(BF16),
                                g_norm2[l].reshape(1, D), wr2, br, tri, tiles_per_seq)

    rb = EXPERT_ROWS
    counts = cnt[0, :N_EXPERTS].astype(jnp.int32)
    padded = (counts + rb - 1) // rb * rb
    pad_end = jnp.cumsum(padded)
    pad_start = pad_end - padded
    dest = _slots(pad_start.astype(jnp.int32), ri)
    dest0, dest1 = dest[0], dest[1]
    n_blocks = -(-T * TOPK_IN_GROUP // rb) + N_EXPERTS
    n_pad = n_blocks * rb
    blk_start = jnp.arange(n_blocks, dtype=jnp.int32) * rb
    blk_expert = jnp.minimum(jnp.sum(pad_end[None, :] <= blk_start[:, None], axis=-1), N_EXPERTS - 1).astype(jnp.int32)

    xs = _dispatch(dest0, dest1, h2, n_pad)
    ys = _experts(blk_expert, xs, w1[l].astype(BF16), w3[l].astype(BF16), w2[l].astype(BF16))
    out = _combine(dest0, dest1, x1, rf, ga2, ys, tiles_per_seq)
    return out.reshape(B, S, D)
```

```python
import functools

import jax
import jax.numpy as jnp
from jax import lax
from jax.experimental import pallas as pl
from jax.experimental.pallas import tpu as pltpu

F32 = jnp.float32
BF16 = jnp.bfloat16
HIGHEST = lax.Precision.HIGHEST

N_HEADS = 8
HEAD_DIM = 64
ATTN_WIDTH = N_HEADS * HEAD_DIM
CONV_WIDTH = 512
CONV_K = 3
MOBA_BLOCK = 256
MOBA_TOPK = 3
ROPE_THETA = 500000.0
ROT_DIM = HEAD_DIM // 4
N_GROUPS = 4
EXPERTS_PER_GROUP = 8
N_EXPERTS = N_GROUPS * EXPERTS_PER_GROUP
TOPK_IN_GROUP = 2
N_MOD = 6
EPS = 1e-6

LANES = 128
NEG = -1e30
ROW_TILE = 512
POST_CHUNK = 256
QUERY_TILE = 512
DMA_UNROLL = 8
EXPERT_ROWS = 256
VMEM_LIMIT = 56 * 1024 * 1024


def _cparams(*sem):
    return pltpu.CompilerParams(dimension_semantics=sem, vmem_limit_bytes=VMEM_LIMIT)


def _ada_kernel(c_ref, w_ref, b_ref, o_ref):
    c = c_ref[...]
    a = c * jax.nn.sigmoid(c)
    o_ref[...] = jnp.dot(a, w_ref[...], preferred_element_type=F32, precision=HIGHEST) + b_ref[...]


def _ada(c, w_ada, b_ada):
    B, D = c.shape
    N = w_ada.shape[1]
    tn = 1536
    return pl.pallas_call(
        _ada_kernel,
        grid=(N // tn,),
        in_specs=[pl.BlockSpec((B, D), lambda j: (0, 0)),
                  pl.BlockSpec((D, tn), lambda j: (0, j)),
                  pl.BlockSpec((1, tn), lambda j: (0, j))],
        out_specs=pl.BlockSpec((B, tn), lambda j: (0, j)),
        out_shape=jax.ShapeDtypeStruct((B, N), F32),
        compiler_params=_cparams("arbitrary"),
        name="ada",
    )(c, w_ada, b_ada.reshape(1, N))


def _inproj_kernel(x_ref, g_ref, sc_ref, sh_ref, w_ref, z_ref, *, n_chunk):
    x = x_ref[...]
    ms = jnp.mean(x * x, axis=-1, keepdims=True)
    y = x * lax.rsqrt(ms + EPS) * g_ref[...]
    h = (y * (1.0 + sc_ref[0]) + sh_ref[0]).astype(BF16)
    for n in range(0, z_ref.shape[1], n_chunk):
        z_ref[:, n:n + n_chunk] = jnp.dot(h, w_ref[:, n:n + n_chunk],
                                          preferred_element_type=F32).astype(BF16)


def _inproj(x2, g1, sc1, sh1, w_in_bf, tiles_per_seq):
    T, D = x2.shape
    N = w_in_bf.shape[1]
    tm = ROW_TILE
    bmap = lambda i: (i // tiles_per_seq, 0, 0)
    return pl.pallas_call(
        functools.partial(_inproj_kernel, n_chunk=512),
        grid=(T // tm,),
        in_specs=[pl.BlockSpec((tm, D), lambda i: (i, 0)),
                  pl.BlockSpec((1, D), lambda i: (0, 0)),
                  pl.BlockSpec((1, 1, D), bmap),
                  pl.BlockSpec((1, 1, D), bmap),
                  pl.BlockSpec((D, N), lambda i: (0, 0))],
        out_specs=pl.BlockSpec((tm, N), lambda i: (i, 0)),
        out_shape=jax.ShapeDtypeStruct((T, N), BF16),
        compiler_params=_cparams("arbitrary"),
        name="inproj",
    )(x2, g1, sc1, sh1, w_in_bf)


def _fold_rows(x, op):
    parts = [x[r:r + 8] for r in range(0, x.shape[0], 8)]
    while len(parts) > 1:
        parts = [op(parts[i], parts[i + 1]) for i in range(0, len(parts) - 1, 2)] + (
            [parts[-1]] if len(parts) % 2 else [])
    return parts[0]


def _attn_kernel(q_ref, k_ref, v_ref, cos_ref, sin_ref, gq_ref, gk_ref, o_ref,
                 kaug_ref, vt_ref, kmp_ref, s_ref):
    S = q_ref.shape[1]
    blk = MOBA_BLOCK
    qt = QUERY_TILE
    sub = qt // blk
    nb = S // blk
    nbp = kmp_ref.shape[0]

    def norm_rope(xb, g, r0):
        rows = xb.shape[0]
        lane = lax.broadcasted_iota(jnp.int32, (rows, LANES), 1)
        head0 = lane < HEAD_DIM
        x = xb.astype(F32)
        sq = x * x
        s0 = jnp.sum(jnp.where(head0, sq, 0.0), axis=-1, keepdims=True)
        s1 = jnp.sum(jnp.where(head0, 0.0, sq), axis=-1, keepdims=True)
        inv = jnp.where(head0, lax.rsqrt(s0 * (1.0 / HEAD_DIM) + EPS), lax.rsqrt(s1 * (1.0 / HEAD_DIM) + EPS))
        y = x * inv * g
        half = ROT_DIM // 2
        rot_lo = (lane & (HEAD_DIM - 1)) < half
        rot = jnp.where(rot_lo, pltpu.roll(y, LANES - half, 1), pltpu.roll(y, half, 1))
        return y * cos_ref[pl.ds(r0, rows), :] + rot * sin_ref[pl.ds(r0, rows), :]

    kmp_ref[...] = jnp.zeros_like(kmp_ref)
    ones_row = jnp.where(lax.broadcasted_iota(jnp.int32, (16, blk), 0) == 0, 1.0, 0.0).astype(BF16)
    lane_k = lax.broadcasted_iota(jnp.int32, (blk, LANES), 1)
    head0_k = lane_k < HEAD_DIM

    def kbody(j, carry):
        r0 = pl.multiple_of(j * blk, blk)
        kr = norm_rope(k_ref[0, pl.ds(r0, blk), :], gk_ref[...], r0)
        kmp_ref[pl.ds(j, 1), :] = jnp.sum(kr, axis=0, keepdims=True) * (1.0 / blk)
        onehot = jnp.where(lane_k - HEAD_DIM == j, 1.0, 0.0)
        kaug_ref[0, j] = jnp.where(head0_k, kr, onehot).astype(BF16)
        kaug_ref[1, j] = jnp.where(head0_k, pltpu.roll(kr, HEAD_DIM, 1), onehot).astype(BF16)
        vT = v_ref[0, pl.ds(r0, blk), :].astype(F32).T
        for h in range(2):
            vt_ref[h, j, 0:HEAD_DIM, :] = vT[h * HEAD_DIM:(h + 1) * HEAD_DIM].astype(BF16)
            vt_ref[h, j, HEAD_DIM:HEAD_DIM + 16, :] = ones_row
        return carry

    lax.fori_loop(0, nb, kbody, 0)

    key_i = lax.broadcasted_iota(jnp.int32, (blk, qt), 0)
    qry_i = lax.broadcasted_iota(jnp.int32, (blk, qt), 1)
    causal = [(qry_i < u * blk) | (qry_i >= (u + 1) * blk) | (key_i <= qry_i - u * blk) for u in range(sub)]
    rowf = lax.broadcasted_iota(jnp.int32, (nbp, qt), 0).astype(F32)
    subf = (lax.broadcasted_iota(jnp.int32, (nbp, qt), 1) // blk).astype(F32)
    row128 = lax.broadcasted_iota(jnp.int32, (LANES, qt), 0)
    q_scale = (HEAD_DIM ** -0.5) * 1.4426950408889634

    def qbody(qi, carry):
        r0 = pl.multiple_of(qi * qt, qt)
        j0 = qi * sub
        qT = norm_rope(q_ref[0, pl.ds(r0, qt), :], gq_ref[...], r0).T
        cur = lax.convert_element_type(j0, F32) + subf
        qa = []
        for h in range(2):
            qmT = jnp.where(row128 < HEAD_DIM, qT, 0.0) if h == 0 else jnp.where(row128 < HEAD_DIM, 0.0, qT)
            gate = jnp.dot(kmp_ref[...], qmT, precision=HIGHEST, preferred_element_type=F32)
            g = jnp.where(rowf < cur, gate, -jnp.inf)
            keep = rowf == cur
            for r in range(MOBA_TOPK):
                m = jnp.max(g, axis=0, keepdims=True)
                idx = jnp.min(jnp.where(g == m, rowf, 1e9), axis=0, keepdims=True)
                pick = (rowf == idx) & (cur > r)
                keep = keep | pick
                g = jnp.where(pick, -jnp.inf, g)
            bias = jnp.where(keep, 0.0, NEG)
            qs = qT[h * HEAD_DIM:(h + 1) * HEAD_DIM] * q_scale
            pad = jnp.zeros((LANES - HEAD_DIM - nbp, qt), F32)
            qa.append(jnp.concatenate([qs, bias, pad], axis=0).astype(BF16))

        mx = [None, None]
        for u in range(sub):
            for h in range(2):
                sT = jnp.dot(kaug_ref[h, j0 + u], qa[h], preferred_element_type=F32)
                sT = jnp.where(causal[u], sT, NEG)
                s_ref[h, j0 + u] = sT
                f = _fold_rows(sT, jnp.maximum)
                mx[h] = f if mx[h] is None else jnp.maximum(mx[h], f)
        n_pair = j0 // 2

        def p1(p, mx):
            mx = list(mx)
            for u in range(2):
                j = 2 * p + u
                for h in range(2):
                    sT = jnp.dot(kaug_ref[h, j], qa[h], preferred_element_type=F32)
                    s_ref[h, j] = sT
                    mx[h] = jnp.maximum(mx[h], _fold_rows(sT, jnp.maximum))
            return tuple(mx)

        mx = lax.fori_loop(0, n_pair, p1, tuple(mx))
        mcol = [jnp.max(mx[h], axis=0, keepdims=True) for h in range(2)]

        acc = [None, None]
        for u in range(sub):
            for h in range(2):
                pT = jnp.exp2(s_ref[h, j0 + u] - mcol[h]).astype(BF16)
                d = jnp.dot(vt_ref[h, j0 + u], pT, preferred_element_type=F32)
                acc[h] = d if acc[h] is None else acc[h] + d

        def p2(p, acc):
            acc = list(acc)
            for u in range(2):
                j = 2 * p + u
                for h in range(2):
                    pT = jnp.exp2(s_ref[h, j] - mcol[h]).astype(BF16)
                    acc[h] = acc[h] + jnp.dot(vt_ref[h, j], pT, preferred_element_type=F32)
            return tuple(acc)

        acc = lax.fori_loop(0, n_pair, p2, tuple(acc))
        outT = jnp.concatenate([acc[h][0:HEAD_DIM] / acc[h][HEAD_DIM:HEAD_DIM + 1] for h in range(2)], axis=0)
        o_ref[0, pl.ds(r0, qt), :] = outT.T.astype(BF16)
        return carry

    lax.fori_loop(0, S // qt, qbody, 0)


def _attention(z3, cosf, sinf, gq2, gk2):
    B, S, _ = z3.shape
    n_pair = N_HEADS // 2
    kq = ATTN_WIDTH // LANES
    nb = S // MOBA_BLOCK
    nbp = -(-nb // 16) * 16
    return pl.pallas_call(
        _attn_kernel,
        grid=(B, n_pair),
        in_specs=[pl.BlockSpec((1, S, LANES), lambda b, p: (b, 0, p)),
                  pl.BlockSpec((1, S, LANES), lambda b, p: (b, 0, kq + p)),
                  pl.BlockSpec((1, S, LANES), lambda b, p: (b, 0, 2 * kq + p)),
                  pl.BlockSpec((S, LANES), lambda b, p: (0, 0)),
                  pl.BlockSpec((S, LANES), lambda b, p: (0, 0)),
                  pl.BlockSpec((1, LANES), lambda b, p: (0, 0)),
                  pl.BlockSpec((1, LANES), lambda b, p: (0, 0))],
        out_specs=pl.BlockSpec((1, S, LANES), lambda b, p: (b, 0, p)),
        out_shape=jax.ShapeDtypeStruct((B, S, ATTN_WIDTH), BF16),
        scratch_shapes=[pltpu.VMEM((2, nb, MOBA_BLOCK, LANES), BF16),
                        pltpu.VMEM((2, nb, HEAD_DIM + 16, MOBA_BLOCK), BF16),
                        pltpu.VMEM((nbp, LANES), F32),
                        pltpu.VMEM((2, nb, MOBA_BLOCK, QUERY_TILE), F32)],
        compiler_params=_cparams("arbitrary", "arbitrary"),
        name="attn",
    )(z3, z3, z3, cosf, sinf, gq2, gk2)


def _slabs(width):
    return width // LANES


def _load_slabs(ref, row0, rows, n_slab, lead=()):
    return jnp.concatenate([ref[lead + (pl.ds(row0 * n_slab + s, rows, stride=n_slab), slice(None))]
                            for s in range(n_slab)], axis=1)


def _store_slabs(ref, row0, val):
    rows, width = val.shape
    n_slab = _slabs(width)
    for s in range(n_slab):
        ref[pl.ds(row0 * n_slab + s, rows, stride=n_slab), :] = val[:, s * LANES:(s + 1) * LANES]


def _post_kernel(x_ref, ya_ref, xb_ref, bg_ref, cg_ref, gta_ref, gtb_ref, ga1_ref, sc2_ref, sh2_ref,
                 cw_ref, cb_ref, wpa_ref, wpb_ref, wo_ref, g2_ref, wr_ref, br_ref, tri_ref,
                 x1_ref, h2_ref, ri_ref, rf_ref, cnt_ref, ubuf_ref, run_ref, *, tiles_per_seq):
    i = pl.program_id(0)
    tm = x_ref.shape[0]
    rc = POST_CHUNK
    halo = 8

    @pl.when(i == 0)
    def _():
        run_ref[...] = jnp.zeros_like(run_ref)

    @pl.when(i % tiles_per_seq == 0)
    def _():
        ubuf_ref[0:halo, :] = jnp.zeros((halo, CONV_WIDTH), F32)

    ubuf_ref[halo:halo + tm, :] = cg_ref[...].astype(F32) * xb_ref[...].astype(F32)
    cw = cw_ref[...]
    lanef = lax.broadcasted_iota(jnp.int32, (rc, LANES), 1).astype(F32)
    half_d = x_ref.shape[1] // 2
    run = run_ref[0:1, :]

    for c in range(tm // rc):
        rows = pl.ds(c * rc, rc)
        conv = (cw[0:1, :] * ubuf_ref[pl.ds(halo - 2 + c * rc, rc), :]
                + cw[1:2, :] * ubuf_ref[pl.ds(halo - 1 + c * rc, rc), :]
                + cw[2:3, :] * ubuf_ref[pl.ds(halo + c * rc, rc), :])
        y_b = (bg_ref[rows, :].astype(F32) * (conv + cb_ref[...])).astype(BF16)
        pa = jnp.dot(ya_ref[rows, :], wpa_ref[...], preferred_element_type=F32)
        pb = jnp.dot(y_b, wpb_ref[...], preferred_element_type=F32)
        merged = (jax.nn.sigmoid(gta_ref[rows, :].astype(F32)) * pa
                  + jax.nn.sigmoid(gtb_ref[rows, :].astype(F32)) * pb).astype(BF16)
        x1 = x_ref[rows, :] + ga1_ref[0] * jnp.dot(merged, wo_ref[...], preferred_element_type=F32)
        x1_ref[rows, :] = x1

        ms = jnp.mean(x1 * x1, axis=-1, keepdims=True)
        h2 = x1 * lax.rsqrt(ms + EPS) * g2_ref[...]
        h2 = h2 * (1.0 + sc2_ref[0]) + sh2_ref[0]
        h_hi = h2.astype(BF16)
        h_hi32 = h_hi.astype(F32)
        bits = pltpu.bitcast(h_hi32, jnp.uint32)
        _store_slabs(h2_ref, c * rc, (bits[:, :half_d] >> 16) | bits[:, half_d:])

        h_lo = (h2 - h_hi32).astype(BF16)
        r = jnp.dot(h_hi, wr_ref[...], preferred_element_type=F32)
        logit = (r[:, :LANES] + r[:, LANES:]
                 + jnp.dot(h_lo, wr_ref[:, :LANES], preferred_element_type=F32) + br_ref[...])
        gl = jnp.where(lanef < N_GROUPS, logit, -jnp.inf)
        gmax = jnp.max(gl, axis=-1, keepdims=True)
        g_idx = jnp.min(jnp.where(gl == gmax, lanef, 1e9), axis=-1, keepdims=True)
        g_w = 1.0 / jnp.sum(jnp.exp(gl - gmax), axis=-1, keepdims=True)
        e_lo = N_GROUPS + EXPERTS_PER_GROUP * g_idx
        el = jnp.where((lanef >= e_lo) & (lanef < e_lo + EXPERTS_PER_GROUP), logit, -jnp.inf)
        v0 = jnp.max(el, axis=-1, keepdims=True)
        i0 = jnp.min(jnp.where(el == v0, lanef, 1e9), axis=-1, keepdims=True)
        el = jnp.where(lanef == i0, -jnp.inf, el)
        v1 = jnp.max(el, axis=-1, keepdims=True)
        i1 = jnp.min(jnp.where(el == v1, lanef, 1e9), axis=-1, keepdims=True)
        t = jnp.exp(v1 - v0)
        w0 = g_w / (1.0 + t)
        w1 = g_w * t / (1.0 + t)
        e0 = i0 - N_GROUPS
        e1 = i1 - N_GROUPS

        oh0 = lanef == e0
        oh1 = lanef == e1
        oh = jnp.where(oh0 | oh1, 1.0, 0.0)
        before = jnp.dot(tri_ref[...], oh.astype(BF16), preferred_element_type=F32) + run
        r0 = jnp.sum(jnp.where(oh0, before, 0.0), axis=-1, keepdims=True)
        r1 = jnp.sum(jnp.where(oh1, before, 0.0), axis=-1, keepdims=True)
        run = run + jnp.sum(oh, axis=0, keepdims=True)

        ri = jnp.where(lanef == 0, e0, jnp.where(lanef == 1, e1, jnp.where(lanef == 2, r0, jnp.where(lanef == 3, r1, 0.0))))
        ri_ref[:, rows] = ri.astype(jnp.int32).T[0:8]
        rf_ref[rows, :] = jnp.where(lanef == 0, w0, jnp.where(lanef == 1, w1, 0.0))

    ubuf_ref[0:halo, :] = ubuf_ref[tm:tm + halo, :]
    run_ref[...] = jnp.broadcast_to(run, run_ref.shape)
    cnt_ref[...] = jnp.broadcast_to(run, cnt_ref.shape)


def _post(x2, ya2, z2, ga1, sc2, sh2, conv_w, conv_b, wpa, wpb, wo, g2, wr, br, tri, tiles_per_seq):
    T, D = x2.shape
    tm = ROW_TILE
    cw = CONV_WIDTH
    xcol = 3 * ATTN_WIDTH // cw
    gcol = (3 * ATTN_WIDTH + 3 * cw) // D
    bmap = lambda i: (i // tiles_per_seq, 0, 0)
    const = lambda i: (0, 0)
    return pl.pallas_call(
        functools.partial(_post_kernel, tiles_per_seq=tiles_per_seq),
        grid=(T // tm,),
        in_specs=[pl.BlockSpec((tm, D), lambda i: (i, 0)),
                  pl.BlockSpec((tm, ATTN_WIDTH), lambda i: (i, 0)),
                  pl.BlockSpec((tm, cw), lambda i: (i, xcol)),
                  pl.BlockSpec((tm, cw), lambda i: (i, xcol + 1)),
                  pl.BlockSpec((tm, cw), lambda i: (i, xcol + 2)),
                  pl.BlockSpec((tm, D), lambda i: (i, gcol)),
                  pl.BlockSpec((tm, D), lambda i: (i, gcol + 1)),
                  pl.BlockSpec((1, 1, D), bmap),
                  pl.BlockSpec((1, 1, D), bmap),
                  pl.BlockSpec((1, 1, D), bmap),
                  pl.BlockSpec((CONV_K, cw), const),
                  pl.BlockSpec((1, cw), const),
                  pl.BlockSpec((ATTN_WIDTH, D), const),
                  pl.BlockSpec((cw, D), const),
                  pl.BlockSpec((D, D), const),
                  pl.BlockSpec((1, D), const),
                  pl.BlockSpec((D, 2 * LANES), const),
                  pl.BlockSpec((1, LANES), const),
                  pl.BlockSpec((POST_CHUNK, POST_CHUNK), const)],
        out_specs=[pl.BlockSpec((tm, D), lambda i: (i, 0)),
                   pl.BlockSpec((tm * _slabs(D // 2), LANES), lambda i: (i, 0)),
                   pl.BlockSpec((8, tm), lambda i: (0, i)),
                   pl.BlockSpec((tm, LANES), lambda i: (i, 0)),
                   pl.BlockSpec((8, LANES), const)],
        out_shape=[jax.ShapeDtypeStruct((T, D), F32),
                   jax.ShapeDtypeStruct((T * _slabs(D // 2), LANES), jnp.uint32),
                   jax.ShapeDtypeStruct((8, T), jnp.int32),
                   jax.ShapeDtypeStruct((T, LANES), F32),
                   jax.ShapeDtypeStruct((8, LANES), F32)],
        scratch_shapes=[pltpu.VMEM((tm + 16, cw), F32),
                        pltpu.VMEM((8, LANES), F32)],
        compiler_params=_cparams("arbitrary"),
        name="post",
    )(x2, ya2, z2, z2, z2, z2, z2, ga1, sc2, sh2, conv_w, conv_b, wpa, wpb, wo, g2, wr, br, tri)


def _slots_kernel(ps_ref, ri_ref, d_ref, *, n_slab):
    e = ri_ref[0:TOPK_IN_GROUP, :]
    start = jnp.zeros(e.shape, jnp.int32)
    for k in range(N_EXPERTS):
        start = jnp.where(e == k, ps_ref[k], start)
    d_ref[...] = (start + ri_ref[TOPK_IN_GROUP:2 * TOPK_IN_GROUP, :]) * n_slab


def _slots(pad_start, riT, n_slab):
    T = riT.shape[1]
    grid_spec = pltpu.PrefetchScalarGridSpec(
        num_scalar_prefetch=1,
        grid=(1,),
        in_specs=[pl.BlockSpec(riT.shape, lambda i, ps: (0, 0))],
        out_specs=pl.BlockSpec((TOPK_IN_GROUP, T), lambda i, ps: (0, 0)),
    )
    return pl.pallas_call(
        functools.partial(_slots_kernel, n_slab=n_slab),
        grid_spec=grid_spec,
        out_shape=jax.ShapeDtypeStruct((TOPK_IN_GROUP, T), jnp.int32),
        compiler_params=_cparams("arbitrary"),
        name="slots",
    )(pad_start, riT)


def _dispatch_kernel(d0_ref, d1_ref, h_ref, xs_in_ref, xs_ref, sem, *, n_slab):
    del xs_in_ref
    tm = h_ref.shape[0] // n_slab
    base = pl.program_id(0) * tm

    def body(g, carry):
        r8 = pl.multiple_of(g * DMA_UNROLL, DMA_UNROLL)
        for u in range(DMA_UNROLL):
            src = h_ref.at[pl.ds((r8 + u) * n_slab, n_slab), :]
            pltpu.make_async_copy(src, xs_ref.at[pl.ds(d0_ref[base + r8 + u], n_slab), :], sem).start(priority=0)
            pltpu.make_async_copy(src, xs_ref.at[pl.ds(d1_ref[base + r8 + u], n_slab), :], sem).start(priority=1)
        return carry

    lax.fori_loop(0, tm // DMA_UNROLL, body, 0)
    for _ in range(TOPK_IN_GROUP):
        pltpu.make_async_copy(h_ref, xs_ref.at[pl.ds(0, tm * n_slab), :], sem).wait()


def _dispatch(dest0, dest1, h2p, n_pad, n_slab):
    tm = ROW_TILE
    T = h2p.shape[0] // n_slab
    grid_spec = pltpu.PrefetchScalarGridSpec(
        num_scalar_prefetch=2,
        grid=(T // tm,),
        in_specs=[pl.BlockSpec((tm * n_slab, LANES), lambda i, d0, d1: (i, 0)),
                  pl.BlockSpec(memory_space=pl.ANY)],
        out_specs=pl.BlockSpec(memory_space=pl.ANY),
        scratch_shapes=[pltpu.SemaphoreType.DMA(())],
    )
    return pl.pallas_call(
        functools.partial(_dispatch_kernel, n_slab=n_slab),
        grid_spec=grid_spec,
        out_shape=jax.ShapeDtypeStruct((n_pad * n_slab, LANES), h2p.dtype),
        input_output_aliases={3: 0},
        compiler_params=_cparams("arbitrary"),
        name="dispatch",
    )(dest0, dest1, h2p, jnp.zeros((n_pad * n_slab, LANES), h2p.dtype))


def _expert_kernel(be_ref, xs_ref, w1_ref, w3_ref, w2_ref, ys_ref):
    half_d = w1_ref.shape[1] // 2
    xp = _load_slabs(xs_ref, 0, EXPERT_ROWS, _slabs(half_d))
    x_lo = pltpu.bitcast(xp << 16, F32).astype(BF16)
    x_hi = pltpu.bitcast(xp & jnp.uint32(0xFFFF0000), F32).astype(BF16)
    a = (jnp.dot(x_lo, w1_ref[0, :half_d], preferred_element_type=F32)
         + jnp.dot(x_hi, w1_ref[0, half_d:], preferred_element_type=F32))
    b = (jnp.dot(x_lo, w3_ref[0, :half_d], preferred_element_type=F32)
         + jnp.dot(x_hi, w3_ref[0, half_d:], preferred_element_type=F32))
    hid = (a * jax.nn.sigmoid(a) * b).astype(BF16)
    y = jnp.dot(hid, w2_ref[0], preferred_element_type=F32)
    bits = pltpu.bitcast(y.astype(BF16).astype(F32), jnp.uint32)
    _store_slabs(ys_ref, 0, (bits[:, :half_d] >> 16) | bits[:, half_d:])


def _experts(blk_expert, xs, w1b, w3b, w2b):
    _, D, F = w1b.shape
    rb = EXPERT_ROWS
    blk_rows = rb * _slabs(D // 2)
    grid_spec = pltpu.PrefetchScalarGridSpec(
        num_scalar_prefetch=1,
        grid=(xs.shape[0] // blk_rows,),
        in_specs=[pl.BlockSpec((blk_rows, LANES), lambda i, be: (i, 0)),
                  pl.BlockSpec((1, D, F), lambda i, be: (be[i], 0, 0)),
                  pl.BlockSpec((1, D, F), lambda i, be: (be[i], 0, 0)),
                  pl.BlockSpec((1, F, D), lambda i, be: (be[i], 0, 0))],
        out_specs=pl.BlockSpec((blk_rows, LANES), lambda i, be: (i, 0)),
    )
    return pl.pallas_call(
        _expert_kernel,
        grid_spec=grid_spec,
        out_shape=jax.ShapeDtypeStruct(xs.shape, jnp.uint32),
        compiler_params=_cparams("arbitrary"),
        name="experts",
    )(blk_expert, xs, w1b, w3b, w2b)


def _combine_kernel(d0_ref, d1_ref, x1_ref, rf_ref, ga2_ref, ys_ref, o_ref, buf_ref, sem, *, n_slab):
    i = pl.program_id(0)
    n = pl.num_programs(0)
    tm = x1_ref.shape[0]

    def gather(step, slot):
        base = step * tm

        def body(g, carry):
            r8 = pl.multiple_of(g * DMA_UNROLL, DMA_UNROLL)
            for u in range(DMA_UNROLL):
                dst = pl.ds((r8 + u) * n_slab, n_slab)
                pltpu.make_async_copy(ys_ref.at[pl.ds(d0_ref[base + r8 + u], n_slab), :],
                                      buf_ref.at[slot, 0, dst, :], sem.at[slot]).start(priority=0)
                pltpu.make_async_copy(ys_ref.at[pl.ds(d1_ref[base + r8 + u], n_slab), :],
                                      buf_ref.at[slot, 1, dst, :], sem.at[slot]).start(priority=1)
            return carry

        lax.fori_loop(0, tm // DMA_UNROLL, body, 0)

    @pl.when(i == 0)
    def _():
        gather(0, 0)

    @pl.when(i + 1 < n)
    def _():
        gather(i + 1, (i + 1) % 2)

    slot = i % 2
    for k in range(TOPK_IN_GROUP):
        pltpu.make_async_copy(ys_ref.at[pl.ds(0, tm * n_slab), :], buf_ref.at[slot, k], sem.at[slot]).wait()
    rf = rf_ref[...]
    w = [rf[:, k:k + 1] for k in range(TOPK_IN_GROUP)]
    yp = [_load_slabs(buf_ref, 0, tm, n_slab, lead=(slot, k)) for k in range(TOPK_IN_GROUP)]
    half_d = n_slab * LANES
    y_lo = sum(w[k] * pltpu.bitcast(yp[k] << 16, F32) for k in range(TOPK_IN_GROUP))
    y_hi = sum(w[k] * pltpu.bitcast(yp[k] & jnp.uint32(0xFFFF0000), F32) for k in range(TOPK_IN_GROUP))
    ga2 = ga2_ref[0]
    o_ref[:, :half_d] = x1_ref[:, :half_d] + ga2[:, :half_d] * y_lo
    o_ref[:, half_d:] = x1_ref[:, half_d:] + ga2[:, half_d:] * y_hi


def _combine(dest0, dest1, x1, rf, ga2, ys, tiles_per_seq):
    T, D = x1.shape
    n_slab = _slabs(D // 2)
    tm = ROW_TILE
    row = lambda i, d0, d1: (i, 0)
    grid_spec = pltpu.PrefetchScalarGridSpec(
        num_scalar_prefetch=2,
        grid=(T // tm,),
        in_specs=[pl.BlockSpec((tm, D), row),
                  pl.BlockSpec((tm, LANES), row),
                  pl.BlockSpec((1, 1, D), lambda i, d0, d1: (i // tiles_per_seq, 0, 0)),
                  pl.BlockSpec(memory_space=pl.ANY)],
        out_specs=pl.BlockSpec((tm, D), row),
        scratch_shapes=[pltpu.VMEM((2, TOPK_IN_GROUP, tm * n_slab, LANES), ys.dtype),
                        pltpu.SemaphoreType.DMA((2,))],
    )
    return pl.pallas_call(
        functools.partial(_combine_kernel, n_slab=n_slab),
        grid_spec=grid_spec,
        out_shape=jax.ShapeDtypeStruct((T, D), F32),
        compiler_params=_cparams("arbitrary"),
        name="combine",
    )(dest0, dest1, x1, rf, ga2, ys)


def _rope_tables(S):
    pos = jnp.arange(S, dtype=F32)
    inv_freq = ROPE_THETA ** (-jnp.arange(0, ROT_DIM, 2, dtype=F32) / ROT_DIM)
    ang = pos[:, None] * inv_freq[None, :]
    cos, sin = jnp.cos(ang), jnp.sin(ang)
    half = ROT_DIM // 2
    ones = jnp.ones((S, HEAD_DIM - ROT_DIM), F32)
    cos_h = jnp.concatenate([cos, cos, ones], axis=1)
    sin_h = jnp.concatenate([-sin, sin, 0.0 * ones], axis=1)
    return jnp.tile(cos_h, (1, LANES // HEAD_DIM)), jnp.tile(sin_h, (1, LANES // HEAD_DIM))


def kernel(x, c, w_ada, b_ada, g_norm1, g_norm2, w_in, g_q, g_k, conv_w, conv_b,
           w_pa, w_pb, w_o, w_rg, b_rg, w_re, b_re, w1, w3, w2):
    B, S, D = x.shape
    T = B * S
    assert S % ROW_TILE == 0 and S % QUERY_TILE == 0 and QUERY_TILE % (2 * MOBA_BLOCK) == 0
    assert S // MOBA_BLOCK <= LANES - HEAD_DIM
    tiles_per_seq = S // ROW_TILE
    l = 0

    mod = _ada(c, w_ada[l], b_ada[l])
    sh1, sc1, ga1, sh2, sc2, ga2 = [m.reshape(B, 1, D) for m in jnp.split(mod, N_MOD, axis=-1)]

    x2 = x.reshape(T, D)
    z2 = _inproj(x2, g_norm1[l].reshape(1, D), sc1, sh1, w_in[l].astype(BF16), tiles_per_seq)

    cosf, sinf = _rope_tables(S)
    rep = LANES // HEAD_DIM
    ya = _attention(z2.reshape(B, S, -1), cosf, sinf,
                    jnp.tile(g_q[l], rep).reshape(1, LANES), jnp.tile(g_k[l], rep).reshape(1, LANES))

    wr = jnp.zeros((D, LANES), F32).at[:, :N_GROUPS].set(w_rg[l]).at[:, N_GROUPS:N_GROUPS + N_EXPERTS].set(w_re[l])
    br = jnp.zeros((1, LANES), F32).at[0, :N_GROUPS].set(b_rg[l]).at[0, N_GROUPS:N_GROUPS + N_EXPERTS].set(b_re[l])
    wr_hi = wr.astype(BF16)
    wr2 = jnp.concatenate([wr_hi, (wr - wr_hi.astype(F32)).astype(BF16)], axis=1)
    tri = (lax.broadcasted_iota(jnp.int32, (POST_CHUNK, POST_CHUNK), 1)
           < lax.broadcasted_iota(jnp.int32, (POST_CHUNK, POST_CHUNK), 0)).astype(BF16)
    x1, h2, ri, rf, cnt = _post(x2, ya.reshape(T, ATTN_WIDTH), z2, ga1, sc2, sh2,
                                conv_w[l], conv_b[l].reshape(1, CONV_WIDTH),
                                w_pa[l].astype(BF16), w_pb[l].astype(BF16), w_o[l].astype(BF16),
                                g_norm2[l].reshape(1, D), wr2, br, tri, tiles_per_seq)

    rb = EXPERT_ROWS
    counts = cnt[0, :N_EXPERTS].astype(jnp.int32)
    padded = (counts + rb - 1) // rb * rb
    pad_end = jnp.cumsum(padded)
    pad_start = pad_end - padded
    dest = _slots(pad_start.astype(jnp.int32), ri, _slabs(D // 2))
    dest0, dest1 = dest[0], dest[1]
    n_blocks = -(-T * TOPK_IN_GROUP // rb) + N_EXPERTS
    n_pad = n_blocks * rb
    blk_start = jnp.arange(n_blocks, dtype=jnp.int32) * rb
    blk_expert = jnp.minimum(jnp.sum(pad_end[None, :] <= blk_start[:, None], axis=-1), N_EXPERTS - 1).astype(jnp.int32)

    xs = _dispatch(dest0, dest1, h2, n_pad, _slabs(D // 2))
    ys = _experts(blk_expert, xs, w1[l].astype(BF16), w3[l].astype(BF16), w2[l].astype(BF16))
    out = _combine(dest0, dest1, x1, rf, ga2, ys, tiles_per_seq)
    return out.reshape(B, S, D)
```

```python
import functools

import jax
import jax.numpy as jnp
from jax import lax
from jax.experimental import pallas as pl
from jax.experimental.pallas import tpu as pltpu

F32 = jnp.float32
BF16 = jnp.bfloat16
HIGHEST = lax.Precision.HIGHEST

N_HEADS = 8
HEAD_DIM = 64
ATTN_WIDTH = N_HEADS * HEAD_DIM
CONV_WIDTH = 512
CONV_K = 3
MOBA_BLOCK = 256
MOBA_TOPK = 3
ROPE_THETA = 500000.0
ROT_DIM = HEAD_DIM // 4
N_GROUPS = 4
EXPERTS_PER_GROUP = 8
N_EXPERTS = N_GROUPS * EXPERTS_PER_GROUP
TOPK_IN_GROUP = 2
N_MOD = 6
EPS = 1e-6

LANES = 128
NEG = -1e30
ROW_TILE = 512
POST_CHUNK = 256
QUERY_TILE = 512
DMA_UNROLL = 8
EXPERT_ROWS = 256
VMEM_LIMIT = 56 * 1024 * 1024
ATTN_VMEM_LIMIT = 60 * 1024 * 1024


def _cparams(*sem):
    return pltpu.CompilerParams(dimension_semantics=sem, vmem_limit_bytes=VMEM_LIMIT)


def _ada_kernel(c_ref, w_ref, b_ref, o_ref):
    c = c_ref[...]
    a = c * jax.nn.sigmoid(c)
    o_ref[...] = jnp.dot(a, w_ref[...], preferred_element_type=F32, precision=HIGHEST) + b_ref[...]


def _ada(c, w_ada, b_ada):
    B, D = c.shape
    N = w_ada.shape[1]
    tn = 1536
    return pl.pallas_call(
        _ada_kernel,
        grid=(N // tn,),
        in_specs=[pl.BlockSpec((B, D), lambda j: (0, 0)),
                  pl.BlockSpec((D, tn), lambda j: (0, j)),
                  pl.BlockSpec((1, tn), lambda j: (0, j))],
        out_specs=pl.BlockSpec((B, tn), lambda j: (0, j)),
        out_shape=jax.ShapeDtypeStruct((B, N), F32),
        compiler_params=_cparams("arbitrary"),
        name="ada",
    )(c, w_ada, b_ada.reshape(1, N))


def _inproj_kernel(x_ref, g_ref, sc_ref, sh_ref, w_ref, z_ref, *, n_chunk):
    x = x_ref[...]
    ms = jnp.mean(x * x, axis=-1, keepdims=True)
    y = x * lax.rsqrt(ms + EPS) * g_ref[...]
    h = (y * (1.0 + sc_ref[0]) + sh_ref[0]).astype(BF16)
    for n in range(0, z_ref.shape[1], n_chunk):
        z_ref[:, n:n + n_chunk] = jnp.dot(h, w_ref[:, n:n + n_chunk],
                                          preferred_element_type=F32).astype(BF16)


def _inproj(x2, g1, sc1, sh1, w_in_bf, tiles_per_seq):
    T, D = x2.shape
    N = w_in_bf.shape[1]
    tm = ROW_TILE
    bmap = lambda i: (i // tiles_per_seq, 0, 0)
    return pl.pallas_call(
        functools.partial(_inproj_kernel, n_chunk=512),
        grid=(T // tm,),
        in_specs=[pl.BlockSpec((tm, D), lambda i: (i, 0)),
                  pl.BlockSpec((1, D), lambda i: (0, 0)),
                  pl.BlockSpec((1, 1, D), bmap),
                  pl.BlockSpec((1, 1, D), bmap),
                  pl.BlockSpec((D, N), lambda i: (0, 0))],
        out_specs=pl.BlockSpec((tm, N), lambda i: (i, 0)),
        out_shape=jax.ShapeDtypeStruct((T, N), BF16),
        compiler_params=_cparams("arbitrary"),
        name="inproj",
    )(x2, g1, sc1, sh1, w_in_bf)


def _fold_rows(x, op):
    parts = [x[r:r + 8] for r in range(0, x.shape[0], 8)]
    while len(parts) > 1:
        parts = [op(parts[i], parts[i + 1]) for i in range(0, len(parts) - 1, 2)] + (
            [parts[-1]] if len(parts) % 2 else [])
    return parts[0]


def _attn_kernel(q_ref, k_ref, v_ref, cos_ref, sin_ref, gq_ref, gk_ref, o_ref,
                 kaug_ref, vt_ref, kmp_ref, kst_ref, s_ref, mcol_ref):
    S = q_ref.shape[1]
    blk = MOBA_BLOCK
    qt = QUERY_TILE
    sub = qt // blk
    nb = S // blk
    nq = S // qt
    nbp = kmp_ref.shape[0]
    hd = HEAD_DIM

    def norm_rope(xb, g, r0):
        rows = xb.shape[0]
        lane = lax.broadcasted_iota(jnp.int32, (rows, LANES), 1)
        head0 = lane < HEAD_DIM
        x = xb.astype(F32)
        sq = x * x
        s0 = jnp.sum(jnp.where(head0, sq, 0.0), axis=-1, keepdims=True)
        s1 = jnp.sum(jnp.where(head0, 0.0, sq), axis=-1, keepdims=True)
        inv = jnp.where(head0, lax.rsqrt(s0 * (1.0 / HEAD_DIM) + EPS), lax.rsqrt(s1 * (1.0 / HEAD_DIM) + EPS))
        y = x * inv * g
        half = ROT_DIM // 2
        rot_lo = (lane & (HEAD_DIM - 1)) < half
        rot = jnp.where(rot_lo, pltpu.roll(y, LANES - half, 1), pltpu.roll(y, half, 1))
        return y * cos_ref[pl.ds(r0, rows), :] + rot * sin_ref[pl.ds(r0, rows), :]

    kmp_ref[...] = jnp.zeros_like(kmp_ref)
    ones_row = jnp.where(lax.broadcasted_iota(jnp.int32, (16, blk), 0) == 0, 1.0, 0.0).astype(BF16)
    lane_k = lax.broadcasted_iota(jnp.int32, (blk, LANES), 1)
    head0_k = lane_k < HEAD_DIM

    def kbody(j, carry):
        r0 = pl.multiple_of(j * blk, blk)
        kr = norm_rope(k_ref[0, pl.ds(r0, blk), :], gk_ref[...], r0)
        kmp_ref[pl.ds(j, 1), :] = jnp.sum(kr, axis=0, keepdims=True) * (1.0 / blk)
        kaug_ref[0, j] = jnp.where(head0_k, kr, jnp.where(lane_k - hd == j, 1.0, 0.0)).astype(BF16)
        kaug_ref[1, j] = jnp.where(head0_k, jnp.where(lane_k == j, 1.0, 0.0), kr).astype(BF16)
        vT = v_ref[0, pl.ds(r0, blk), :].astype(F32).T
        for h in range(2):
            vt_ref[h, j, 0:HEAD_DIM, :] = vT[h * HEAD_DIM:(h + 1) * HEAD_DIM].astype(BF16)
            vt_ref[h, j, HEAD_DIM:HEAD_DIM + 16, :] = ones_row
        return carry

    lax.fori_loop(0, nb, kbody, 0)

    kmp = kmp_ref[...]
    lane_m = lax.broadcasted_iota(jnp.int32, (nbp, LANES), 1)
    k_hi = kmp.astype(BF16)
    k_lo = (kmp - k_hi.astype(F32)).astype(BF16)
    zero = jnp.zeros((nbp, LANES), BF16)
    parts = []
    for h in range(2):
        mine = (lane_m < hd) if h == 0 else (lane_m >= hd)
        parts += [jnp.where(mine, k_hi, zero), jnp.where(mine, k_lo, zero)]
    parts += [parts[0], parts[2]]
    for n, part in enumerate(parts):
        kst_ref[n * nbp:(n + 1) * nbp, :] = part

    key_i = lax.broadcasted_iota(jnp.int32, (blk, qt), 0)
    qry_i = lax.broadcasted_iota(jnp.int32, (blk, qt), 1)
    causal = [(qry_i < u * blk) | (qry_i >= (u + 1) * blk) | (key_i <= qry_i - u * blk) for u in range(sub)]
    rowf = lax.broadcasted_iota(jnp.int32, (nbp, qt), 0).astype(F32)
    subf = (lax.broadcasted_iota(jnp.int32, (nbp, qt), 1) // blk).astype(F32)
    q_scale = (hd ** -0.5) * 1.4426950408889634

    def query_operands(t):
        r0 = pl.multiple_of(t * qt, qt)
        qT = norm_rope(q_ref[0, pl.ds(r0, qt), :], gq_ref[...], r0).T
        cur = lax.convert_element_type(t * sub, F32) + subf
        q_hi = qT.astype(BF16)
        q_lo = (qT - q_hi.astype(F32)).astype(BF16)
        g1 = jnp.dot(kst_ref[0:4 * nbp, :], q_hi, preferred_element_type=F32)
        g2 = jnp.dot(kst_ref[4 * nbp:6 * nbp, :], q_lo, preferred_element_type=F32)
        qa = []
        for h in range(2):
            gate = g1[2 * h * nbp:(2 * h + 1) * nbp] + g1[(2 * h + 1) * nbp:(2 * h + 2) * nbp] + g2[h * nbp:(h + 1) * nbp]
            g = jnp.where(rowf < cur, gate, -jnp.inf)
            keep = rowf == cur
            for r in range(MOBA_TOPK):
                m = jnp.max(g, axis=0, keepdims=True)
                idx = jnp.min(jnp.where(g == m, rowf, 1e9), axis=0, keepdims=True)
                pick = (rowf == idx) & (cur > r)
                keep = keep | pick
                g = jnp.where(pick, -jnp.inf, g)
            bias = jnp.where(keep, 0.0, NEG)
            qs = qT[h * hd:(h + 1) * hd] * q_scale
            pad = jnp.zeros((LANES - hd - nbp, qt), F32)
            pieces = [qs, bias, pad] if h == 0 else [bias, pad, qs]
            qa.append(jnp.concatenate(pieces, axis=0).astype(BF16))
        return qa

    def pass1_tile(par, h, j, qa_h, mask):
        sT = jnp.dot(kaug_ref[h, j], qa_h, preferred_element_type=F32)
        if mask is not None:
            sT = jnp.where(mask, sT, NEG)
        s_ref[par, h, j] = sT
        return _fold_rows(sT, jnp.maximum)

    def pass2_tile(par, h, j):
        pT = jnp.exp2(s_ref[par, h, j] - mcol_ref[h, 0:1, :]).astype(BF16)
        return jnp.dot(vt_ref[h, j], pT, preferred_element_type=F32)

    def pass1_own(t, par, qa):
        mx = []
        for h in range(2):
            f = [pass1_tile(par, h, t * sub + u, qa[h], causal[u]) for u in range(sub)]
            mx.append(functools.reduce(jnp.maximum, f))
        return mx

    def pass2_own(t, par):
        return [sum(pass2_tile(par, h, t * sub + u) for u in range(sub)) for h in range(2)]

    def pass1_pair(p, par, qa, mx):
        return [functools.reduce(jnp.maximum, [mx[h]] + [pass1_tile(par, h, 2 * p + u, qa[h], None) for u in range(2)])
                for h in range(2)]

    def pass2_pair(p, par, acc):
        return [acc[h] + sum(pass2_tile(par, h, 2 * p + u) for u in range(2)) for h in range(2)]

    def finish_pass1(mx):
        for h in range(2):
            mcol_ref[h] = jnp.broadcast_to(jnp.max(mx[h], axis=0, keepdims=True), mcol_ref.shape[1:])

    def finish_pass2(t, acc):
        outT = jnp.concatenate([acc[h][0:hd] / acc[h][hd:hd + 1] for h in range(2)], axis=0)
        o_ref[0, pl.ds(pl.multiple_of(t * qt, qt), qt), :] = outT.T.astype(BF16)

    def stage(t, par):

        @pl.when(t == 0)
        def _():
            finish_pass1(pass1_own(t, par, query_operands(t)))

        @pl.when((t > 0) & (t < nq))
        def _():
            qa = query_operands(t)
            mx = pass1_own(t, par, qa)
            acc = pass2_own(t - 1, 1 - par)
            n_prev = (t - 1) * sub // 2

            def both(p, c):
                mx, acc = c
                return tuple(pass1_pair(p, par, qa, mx)), tuple(pass2_pair(p, 1 - par, acc))

            mx, acc = lax.fori_loop(0, n_prev, both, (tuple(mx), tuple(acc)))
            mx = lax.fori_loop(n_prev, t * sub // 2, lambda p, m: tuple(pass1_pair(p, par, qa, m)), tuple(mx))
            finish_pass2(t - 1, acc)
            finish_pass1(mx)

        @pl.when(t == nq)
        def _():
            acc = pass2_own(t - 1, 1 - par)
            acc = lax.fori_loop(0, (t - 1) * sub // 2, lambda p, a: tuple(pass2_pair(p, 1 - par, a)), tuple(acc))
            finish_pass2(t - 1, acc)

    def stage_pair(tt, carry):
        stage(2 * tt, 0)
        stage(2 * tt + 1, 1)
        return carry

    lax.fori_loop(0, (nq + 2) // 2, stage_pair, 0)


def _attention(z3, cosf, sinf, gq2, gk2):
    B, S, _ = z3.shape
    n_pair = N_HEADS // 2
    kq = ATTN_WIDTH // LANES
    nb = S // MOBA_BLOCK
    nbp = -(-nb // 16) * 16
    return pl.pallas_call(
        _attn_kernel,
        grid=(B, n_pair),
        in_specs=[pl.BlockSpec((1, S, LANES), lambda b, p: (b, 0, p)),
                  pl.BlockSpec((1, S, LANES), lambda b, p: (b, 0, kq + p)),
                  pl.BlockSpec((1, S, LANES), lambda b, p: (b, 0, 2 * kq + p)),
                  pl.BlockSpec((S, LANES), lambda b, p: (0, 0), pipeline_mode=pl.Buffered(1)),
                  pl.BlockSpec((S, LANES), lambda b, p: (0, 0), pipeline_mode=pl.Buffered(1)),
                  pl.BlockSpec((1, LANES), lambda b, p: (0, 0)),
                  pl.BlockSpec((1, LANES), lambda b, p: (0, 0))],
        out_specs=pl.BlockSpec((1, S, LANES), lambda b, p: (b, 0, p)),
        out_shape=jax.ShapeDtypeStruct((B, S, ATTN_WIDTH), BF16),
        scratch_shapes=[pltpu.VMEM((2, nb, MOBA_BLOCK, LANES), BF16),
                        pltpu.VMEM((2, nb, HEAD_DIM + 16, MOBA_BLOCK), BF16),
                        pltpu.VMEM((nbp, LANES), F32),
                        pltpu.VMEM((6 * nbp, LANES), BF16),
                        pltpu.VMEM((2, 2, nb, MOBA_BLOCK, QUERY_TILE), F32),
                        pltpu.VMEM((2, 8, QUERY_TILE), F32)],
        compiler_params=pltpu.CompilerParams(dimension_semantics=("arbitrary", "arbitrary"),
                                             vmem_limit_bytes=ATTN_VMEM_LIMIT),
        name="attn",
    )(z3, z3, z3, cosf, sinf, gq2, gk2)


def _slabs(width):
    return width // LANES


def _load_slabs(ref, row0, rows, n_slab, lead=()):
    return jnp.concatenate([ref[lead + (pl.ds(row0 * n_slab + s, rows, stride=n_slab), slice(None))]
                            for s in range(n_slab)], axis=1)


def _store_slabs(ref, row0, val):
    rows, width = val.shape
    n_slab = _slabs(width)
    for s in range(n_slab):
        ref[pl.ds(row0 * n_slab + s, rows, stride=n_slab), :] = val[:, s * LANES:(s + 1) * LANES]


def _post_kernel(x_ref, ya_ref, xb_ref, bg_ref, cg_ref, gta_ref, gtb_ref, ga1_ref, sc2_ref, sh2_ref,
                 cw_ref, cb_ref, wpa_ref, wpb_ref, wo_ref, g2_ref, wr_ref, br_ref, tri_ref,
                 x1_ref, h2_ref, ri_ref, rf_ref, cnt_ref, ubuf_ref, run_ref, *, tiles_per_seq):
    i = pl.program_id(0)
    tm = x_ref.shape[0]
    rc = POST_CHUNK
    halo = 8

    @pl.when(i == 0)
    def _():
        run_ref[...] = jnp.zeros_like(run_ref)

    @pl.when(i % tiles_per_seq == 0)
    def _():
        ubuf_ref[0:halo, :] = jnp.zeros((halo, CONV_WIDTH), F32)

    ubuf_ref[halo:halo + tm, :] = cg_ref[...].astype(F32) * xb_ref[...].astype(F32)
    cw = cw_ref[...]
    lanef = lax.broadcasted_iota(jnp.int32, (rc, LANES), 1).astype(F32)
    half_d = x_ref.shape[1] // 2
    run = run_ref[0:1, :]

    for c in range(tm // rc):
        rows = pl.ds(c * rc, rc)
        conv = (cw[0:1, :] * ubuf_ref[pl.ds(halo - 2 + c * rc, rc), :]
                + cw[1:2, :] * ubuf_ref[pl.ds(halo - 1 + c * rc, rc), :]
                + cw[2:3, :] * ubuf_ref[pl.ds(halo + c * rc, rc), :])
        y_b = (bg_ref[rows, :].astype(F32) * (conv + cb_ref[...])).astype(BF16)
        pa = jnp.dot(ya_ref[rows, :], wpa_ref[...], preferred_element_type=F32)
        pb = jnp.dot(y_b, wpb_ref[...], preferred_element_type=F32)
        merged = (jax.nn.sigmoid(gta_ref[rows, :].astype(F32)) * pa
                  + jax.nn.sigmoid(gtb_ref[rows, :].astype(F32)) * pb).astype(BF16)
        x1 = x_ref[rows, :] + ga1_ref[0] * jnp.dot(merged, wo_ref[...], preferred_element_type=F32)
        x1_ref[rows, :] = x1

        ms = jnp.mean(x1 * x1, axis=-1, keepdims=True)
        h2 = x1 * lax.rsqrt(ms + EPS) * g2_ref[...]
        h2 = h2 * (1.0 + sc2_ref[0]) + sh2_ref[0]
        h_hi = h2.astype(BF16)
        h_hi32 = h_hi.astype(F32)
        bits = pltpu.bitcast(h_hi32, jnp.uint32)
        _store_slabs(h2_ref, c * rc, (bits[:, :half_d] >> 16) | bits[:, half_d:])

        h_lo = (h2 - h_hi32).astype(BF16)
        r = jnp.dot(h_hi, wr_ref[...], preferred_element_type=F32)
        logit = (r[:, :LANES] + r[:, LANES:]
                 + jnp.dot(h_lo, wr_ref[:, :LANES], preferred_element_type=F32) + br_ref[...])
        gl = jnp.where(lanef < N_GROUPS, logit, -jnp.inf)
        gmax = jnp.max(gl, axis=-1, keepdims=True)
        g_idx = jnp.min(jnp.where(gl == gmax, lanef, 1e9), axis=-1, keepdims=True)
        g_w = 1.0 / jnp.sum(jnp.exp(gl - gmax), axis=-1, keepdims=True)
        e_lo = N_GROUPS + EXPERTS_PER_GROUP * g_idx
        el = jnp.where((lanef >= e_lo) & (lanef < e_lo + EXPERTS_PER_GROUP), logit, -jnp.inf)
        v0 = jnp.max(el, axis=-1, keepdims=True)
        i0 = jnp.min(jnp.where(el == v0, lanef, 1e9), axis=-1, keepdims=True)
        el = jnp.where(lanef == i0, -jnp.inf, el)
        v1 = jnp.max(el, axis=-1, keepdims=True)
        i1 = jnp.min(jnp.where(el == v1, lanef, 1e9), axis=-1, keepdims=True)
        t = jnp.exp(v1 - v0)
        w0 = g_w / (1.0 + t)
        w1 = g_w * t / (1.0 + t)
        e0 = i0 - N_GROUPS
        e1 = i1 - N_GROUPS

        oh0 = lanef == e0
        oh1 = lanef == e1
        oh = jnp.where(oh0 | oh1, 1.0, 0.0)
        before = jnp.dot(tri_ref[...], oh.astype(BF16), preferred_element_type=F32) + run
        r0 = jnp.sum(jnp.where(oh0, before, 0.0), axis=-1, keepdims=True)
        r1 = jnp.sum(jnp.where(oh1, before, 0.0), axis=-1, keepdims=True)
        run = run + jnp.sum(oh, axis=0, keepdims=True)

        ri = jnp.where(lanef == 0, e0, jnp.where(lanef == 1, e1, jnp.where(lanef == 2, r0, jnp.where(lanef == 3, r1, 0.0))))
        ri_ref[:, rows] = ri.astype(jnp.int32).T[0:8]
        rf_ref[rows, :] = jnp.where(lanef == 0, w0, jnp.where(lanef == 1, w1, 0.0))

    ubuf_ref[0:halo, :] = ubuf_ref[tm:tm + halo, :]
    run_ref[...] = jnp.broadcast_to(run, run_ref.shape)
    cnt_ref[...] = jnp.broadcast_to(run, cnt_ref.shape)


def _post(x2, ya2, z2, ga1, sc2, sh2, conv_w, conv_b, wpa, wpb, wo, g2, wr, br, tri, tiles_per_seq):
    T, D = x2.shape
    tm = ROW_TILE
    cw = CONV_WIDTH
    xcol = 3 * ATTN_WIDTH // cw
    gcol = (3 * ATTN_WIDTH + 3 * cw) // D
    bmap = lambda i: (i // tiles_per_seq, 0, 0)
    const = lambda i: (0, 0)
    return pl.pallas_call(
        functools.partial(_post_kernel, tiles_per_seq=tiles_per_seq),
        grid=(T // tm,),
        in_specs=[pl.BlockSpec((tm, D), lambda i: (i, 0)),
                  pl.BlockSpec((tm, ATTN_WIDTH), lambda i: (i, 0)),
                  pl.BlockSpec((tm, cw), lambda i: (i, xcol)),
                  pl.BlockSpec((tm, cw), lambda i: (i, xcol + 1)),
                  pl.BlockSpec((tm, cw), lambda i: (i, xcol + 2)),
                  pl.BlockSpec((tm, D), lambda i: (i, gcol)),
                  pl.BlockSpec((tm, D), lambda i: (i, gcol + 1)),
                  pl.BlockSpec((1, 1, D), bmap),
                  pl.BlockSpec((1, 1, D), bmap),
                  pl.BlockSpec((1, 1, D), bmap),
                  pl.BlockSpec((CONV_K, cw), const),
                  pl.BlockSpec((1, cw), const),
                  pl.BlockSpec((ATTN_WIDTH, D), const),
                  pl.BlockSpec((cw, D), const),
                  pl.BlockSpec((D, D), const),
                  pl.BlockSpec((1, D), const),
                  pl.BlockSpec((D, 2 * LANES), const),
                  pl.BlockSpec((1, LANES), const),
                  pl.BlockSpec((POST_CHUNK, POST_CHUNK), const)],
        out_specs=[pl.BlockSpec((tm, D), lambda i: (i, 0)),
                   pl.BlockSpec((tm * _slabs(D // 2), LANES), lambda i: (i, 0)),
                   pl.BlockSpec((8, tm), lambda i: (0, i)),
                   pl.BlockSpec((tm, LANES), lambda i: (i, 0)),
                   pl.BlockSpec((8, LANES), const)],
        out_shape=[jax.ShapeDtypeStruct((T, D), F32),
                   jax.ShapeDtypeStruct((T * _slabs(D // 2), LANES), jnp.uint32),
                   jax.ShapeDtypeStruct((8, T), jnp.int32),
                   jax.ShapeDtypeStruct((T, LANES), F32),
                   jax.ShapeDtypeStruct((8, LANES), F32)],
        scratch_shapes=[pltpu.VMEM((tm + 16, cw), F32),
                        pltpu.VMEM((8, LANES), F32)],
        compiler_params=_cparams("arbitrary"),
        name="post",
    )(x2, ya2, z2, z2, z2, z2, z2, ga1, sc2, sh2, conv_w, conv_b, wpa, wpb, wo, g2, wr, br, tri)


def _slots_kernel(ps_ref, ri_ref, d_ref, *, n_slab):
    e = ri_ref[0:TOPK_IN_GROUP, :]
    start = jnp.zeros(e.shape, jnp.int32)
    for k in range(N_EXPERTS):
        start = jnp.where(e == k, ps_ref[k], start)
    d_ref[...] = (start + ri_ref[TOPK_IN_GROUP:2 * TOPK_IN_GROUP, :]) * n_slab


def _slots(pad_start, riT, n_slab):
    T = riT.shape[1]
    grid_spec = pltpu.PrefetchScalarGridSpec(
        num_scalar_prefetch=1,
        grid=(1,),
        in_specs=[pl.BlockSpec(riT.shape, lambda i, ps: (0, 0))],
        out_specs=pl.BlockSpec((TOPK_IN_GROUP, T), lambda i, ps: (0, 0)),
    )
    return pl.pallas_call(
        functools.partial(_slots_kernel, n_slab=n_slab),
        grid_spec=grid_spec,
        out_shape=jax.ShapeDtypeStruct((TOPK_IN_GROUP, T), jnp.int32),
        compiler_params=_cparams("arbitrary"),
        name="slots",
    )(pad_start, riT)


def _dispatch_kernel(d0_ref, d1_ref, h_ref, xs_in_ref, xs_ref, sem, *, n_slab):
    del xs_in_ref
    tm = h_ref.shape[0] // n_slab
    base = pl.program_id(0) * tm

    def body(g, carry):
        r8 = pl.multiple_of(g * DMA_UNROLL, DMA_UNROLL)
        for u in range(DMA_UNROLL):
            src = h_ref.at[pl.ds((r8 + u) * n_slab, n_slab), :]
            pltpu.make_async_copy(src, xs_ref.at[pl.ds(d0_ref[base + r8 + u], n_slab), :], sem).start(priority=0)
            pltpu.make_async_copy(src, xs_ref.at[pl.ds(d1_ref[base + r8 + u], n_slab), :], sem).start(priority=1)
        return carry

    lax.fori_loop(0, tm // DMA_UNROLL, body, 0)
    for _ in range(TOPK_IN_GROUP):
        pltpu.make_async_copy(h_ref, xs_ref.at[pl.ds(0, tm * n_slab), :], sem).wait()


def _dispatch(dest0, dest1, h2p, n_pad, n_slab):
    tm = ROW_TILE
    T = h2p.shape[0] // n_slab
    grid_spec = pltpu.PrefetchScalarGridSpec(
        num_scalar_prefetch=2,
        grid=(T // tm,),
        in_specs=[pl.BlockSpec((tm * n_slab, LANES), lambda i, d0, d1: (i, 0)),
                  pl.BlockSpec(memory_space=pl.ANY)],
        out_specs=pl.BlockSpec(memory_space=pl.ANY),
        scratch_shapes=[pltpu.SemaphoreType.DMA(())],
    )
    return pl.pallas_call(
        functools.partial(_dispatch_kernel, n_slab=n_slab),
        grid_spec=grid_spec,
        out_shape=jax.ShapeDtypeStruct((n_pad * n_slab, LANES), h2p.dtype),
        input_output_aliases={3: 0},
        compiler_params=_cparams("arbitrary"),
        name="dispatch",
    )(dest0, dest1, h2p, jnp.zeros((n_pad * n_slab, LANES), h2p.dtype))


def _expert_kernel(be_ref, xs_ref, w1_ref, w3_ref, w2_ref, ys_ref):
    half_d = w1_ref.shape[1] // 2
    xp = _load_slabs(xs_ref, 0, EXPERT_ROWS, _slabs(half_d))
    x_lo = pltpu.bitcast(xp << 16, F32).astype(BF16)
    x_hi = pltpu.bitcast(xp & jnp.uint32(0xFFFF0000), F32).astype(BF16)
    a = (jnp.dot(x_lo, w1_ref[0, :half_d], preferred_element_type=F32)
         + jnp.dot(x_hi, w1_ref[0, half_d:], preferred_element_type=F32))
    b = (jnp.dot(x_lo, w3_ref[0, :half_d], preferred_element_type=F32)
         + jnp.dot(x_hi, w3_ref[0, half_d:], preferred_element_type=F32))
    hid = (a * jax.nn.sigmoid(a) * b).astype(BF16)
    y = jnp.dot(hid, w2_ref[0], preferred_element_type=F32)
    bits = pltpu.bitcast(y.astype(BF16).astype(F32), jnp.uint32)
    _store_slabs(ys_ref, 0, (bits[:, :half_d] >> 16) | bits[:, half_d:])


def _experts(blk_expert, xs, w1b, w3b, w2b):
    _, D, F = w1b.shape
    rb = EXPERT_ROWS
    blk_rows = rb * _slabs(D // 2)
    grid_spec = pltpu.PrefetchScalarGridSpec(
        num_scalar_prefetch=1,
        grid=(xs.shape[0] // blk_rows,),
        in_specs=[pl.BlockSpec((blk_rows, LANES), lambda i, be: (i, 0)),
                  pl.BlockSpec((1, D, F), lambda i, be: (be[i], 0, 0)),
                  pl.BlockSpec((1, D, F), lambda i, be: (be[i], 0, 0)),
                  pl.BlockSpec((1, F, D), lambda i, be: (be[i], 0, 0))],
        out_specs=pl.BlockSpec((blk_rows, LANES), lambda i, be: (i, 0)),
    )
    return pl.pallas_call(
        _expert_kernel,
        grid_spec=grid_spec,
        out_shape=jax.ShapeDtypeStruct(xs.shape, jnp.uint32),
        compiler_params=_cparams("arbitrary"),
        name="experts",
    )(blk_expert, xs, w1b, w3b, w2b)


def _combine_kernel(d0_ref, d1_ref, x1_ref, rf_ref, ga2_ref, ys_ref, o_ref, buf_ref, sem, *, n_slab):
    i = pl.program_id(0)
    n = pl.num_programs(0)
    tm = x1_ref.shape[0]

    def gather(step, slot):
        base = step * tm

        def body(g, carry):
            r8 = pl.multiple_of(g * DMA_UNROLL, DMA_UNROLL)
            for u in range(DMA_UNROLL):
                dst = pl.ds((r8 + u) * n_slab, n_slab)
                pltpu.make_async_copy(ys_ref.at[pl.ds(d0_ref[base + r8 + u], n_slab), :],
                                      buf_ref.at[slot, 0, dst, :], sem.at[slot]).start(priority=0)
                pltpu.make_async_copy(ys_ref.at[pl.ds(d1_ref[base + r8 + u], n_slab), :],
                                      buf_ref.at[slot, 1, dst, :], sem.at[slot]).start(priority=1)
            return carry

        lax.fori_loop(0, tm // DMA_UNROLL, body, 0)

    @pl.when(i == 0)
    def _():
        gather(0, 0)

    @pl.when(i + 1 < n)
    def _():
        gather(i + 1, (i + 1) % 2)

    slot = i % 2
    for k in range(TOPK_IN_GROUP):
        pltpu.make_async_copy(ys_ref.at[pl.ds(0, tm * n_slab), :], buf_ref.at[slot, k], sem.at[slot]).wait()
    rf = rf_ref[...]
    w = [rf[:, k:k + 1] for k in range(TOPK_IN_GROUP)]
    yp = [_load_slabs(buf_ref, 0, tm, n_slab, lead=(slot, k)) for k in range(TOPK_IN_GROUP)]
    half_d = n_slab * LANES
    y_lo = sum(w[k] * pltpu.bitcast(yp[k] << 16, F32) for k in range(TOPK_IN_GROUP))
    y_hi = sum(w[k] * pltpu.bitcast(yp[k] & jnp.uint32(0xFFFF0000), F32) for k in range(TOPK_IN_GROUP))
    ga2 = ga2_ref[0]
    o_ref[:, :half_d] = x1_ref[:, :half_d] + ga2[:, :half_d] * y_lo
    o_ref[:, half_d:] = x1_ref[:, half_d:] + ga2[:, half_d:] * y_hi


def _combine(dest0, dest1, x1, rf, ga2, ys, tiles_per_seq):
    T, D = x1.shape
    n_slab = _slabs(D // 2)
    tm = ROW_TILE
    row = lambda i, d0, d1: (i, 0)
    grid_spec = pltpu.PrefetchScalarGridSpec(
        num_scalar_prefetch=2,
        grid=(T // tm,),
        in_specs=[pl.BlockSpec((tm, D), row),
                  pl.BlockSpec((tm, LANES), row),
                  pl.BlockSpec((1, 1, D), lambda i, d0, d1: (i // tiles_per_seq, 0, 0)),
                  pl.BlockSpec(memory_space=pl.ANY)],
        out_specs=pl.BlockSpec((tm, D), row),
        scratch_shapes=[pltpu.VMEM((2, TOPK_IN_GROUP, tm * n_slab, LANES), ys.dtype),
                        pltpu.SemaphoreType.DMA((2,))],
    )
    return pl.pallas_call(
        functools.partial(_combine_kernel, n_slab=n_slab),
        grid_spec=grid_spec,
        out_shape=jax.ShapeDtypeStruct((T, D), F32),
        compiler_params=_cparams("arbitrary"),
        name="combine",
    )(dest0, dest1, x1, rf, ga2, ys)


def _rope_tables(S):
    pos = jnp.arange(S, dtype=F32)
    inv_freq = ROPE_THETA ** (-jnp.arange(0, ROT_DIM, 2, dtype=F32) / ROT_DIM)
    ang = pos[:, None] * inv_freq[None, :]
    cos, sin = jnp.cos(ang), jnp.sin(ang)
    half = ROT_DIM // 2
    ones = jnp.ones((S, HEAD_DIM - ROT_DIM), F32)
    cos_h = jnp.concatenate([cos, cos, ones], axis=1)
    sin_h = jnp.concatenate([-sin, sin, 0.0 * ones], axis=1)
    return jnp.tile(cos_h, (1, LANES // HEAD_DIM)), jnp.tile(sin_h, (1, LANES // HEAD_DIM))


def kernel(x, c, w_ada, b_ada, g_norm1, g_norm2, w_in, g_q, g_k, conv_w, conv_b,
           w_pa, w_pb, w_o, w_rg, b_rg, w_re, b_re, w1, w3, w2):
    B, S, D = x.shape
    T = B * S
    assert S % ROW_TILE == 0 and S % QUERY_TILE == 0 and QUERY_TILE % (2 * MOBA_BLOCK) == 0
    assert S // MOBA_BLOCK <= LANES - HEAD_DIM
    tiles_per_seq = S // ROW_TILE
    l = 0

    mod = _ada(c, w_ada[l], b_ada[l])
    sh1, sc1, ga1, sh2, sc2, ga2 = [m.reshape(B, 1, D) for m in jnp.split(mod, N_MOD, axis=-1)]

    x2 = x.reshape(T, D)
    z2 = _inproj(x2, g_norm1[l].reshape(1, D), sc1, sh1, w_in[l].astype(BF16), tiles_per_seq)

    cosf, sinf = _rope_tables(S)
    rep = LANES // HEAD_DIM
    ya = _attention(z2.reshape(B, S, -1), cosf, sinf,
                    jnp.tile(g_q[l], rep).reshape(1, LANES), jnp.tile(g_k[l], rep).reshape(1, LANES))

    wr = jnp.zeros((D, LANES), F32).at[:, :N_GROUPS].set(w_rg[l]).at[:, N_GROUPS:N_GROUPS + N_EXPERTS].set(w_re[l])
    br = jnp.zeros((1, LANES), F32).at[0, :N_GROUPS].set(b_rg[l]).at[0, N_GROUPS:N_GROUPS + N_EXPERTS].set(b_re[l])
    wr_hi = wr.astype(BF16)
    wr2 = jnp.concatenate([wr_hi, (wr - wr_hi.astype(F32)).astype(BF16)], axis=1)
    tri = (lax.broadcasted_iota(jnp.int32, (POST_CHUNK, POST_CHUNK), 1)
           < lax.broadcasted_iota(jnp.int32, (POST_CHUNK, POST_CHUNK), 0)).astype(BF16)
    x1, h2, ri, rf, cnt = _post(x2, ya.reshape(T, ATTN_WIDTH), z2, ga1, sc2, sh2,
                                conv_w[l], conv_b[l].reshape(1, CONV_WIDTH),
                                w_pa[l].astype(BF16), w_pb[l].astype(BF16), w_o[l].astype(BF16),
                                g_norm2[l].reshape(1, D), wr2, br, tri, tiles_per_seq)

    rb = EXPERT_ROWS
    counts = cnt[0, :N_EXPERTS].astype(jnp.int32)
    padded = (counts + rb - 1) // rb * rb
    pad_end = jnp.cumsum(padded)
    pad_start = pad_end - padded
    dest = _slots(pad_start.astype(jnp.int32), ri, _slabs(D // 2))
    dest0, dest1 = dest[0], dest[1]
    n_blocks = -(-T * TOPK_IN_GROUP // rb) + N_EXPERTS
    n_pad = n_blocks * rb
    blk_start = jnp.arange(n_blocks, dtype=jnp.int32) * rb
    blk_expert = jnp.minimum(jnp.sum(pad_end[None, :] <= blk_start[:, None], axis=-1), N_EXPERTS - 1).astype(jnp.int32)

    xs = _dispatch(dest0, dest1, h2, n_pad, _slabs(D // 2))
    ys = _experts(blk_expert, xs, w1[l].astype(BF16), w3[l].astype(BF16), w2[l].astype(BF16))
    out = _combine(dest0, dest1, x1, rf, ga2, ys, tiles_per_seq)
    return out.reshape(B, S, D)
```

```python
import functools

import jax
import jax.numpy as jnp
from jax import lax
from jax.experimental import pallas as pl
from jax.experimental.pallas import tpu as pltpu

F32 = jnp.float32
BF16 = jnp.bfloat16
HIGHEST = lax.Precision.HIGHEST

N_HEADS = 8
HEAD_DIM = 64
ATTN_WIDTH = N_HEADS * HEAD_DIM
CONV_WIDTH = 512
CONV_K = 3
MOBA_BLOCK = 256
MOBA_TOPK = 3
ROPE_THETA = 500000.0
ROT_DIM = HEAD_DIM // 4
N_GROUPS = 4
EXPERTS_PER_GROUP = 8
N_EXPERTS = N_GROUPS * EXPERTS_PER_GROUP
TOPK_IN_GROUP = 2
N_MOD = 6
EPS = 1e-6

LANES = 128
NEG = -1e30
ROW_TILE = 512
POST_CHUNK = 256
QUERY_TILE = 512
DMA_UNROLL = 8
EXPERT_ROWS = 256
VMEM_LIMIT = 56 * 1024 * 1024
ATTN_VMEM_LIMIT = 60 * 1024 * 1024


def _cparams(*sem):
    return pltpu.CompilerParams(dimension_semantics=sem, vmem_limit_bytes=VMEM_LIMIT)


def _ada_kernel(c_ref, w_ref, b_ref, o_ref):
    c = c_ref[...]
    a = c * jax.nn.sigmoid(c)
    o_ref[...] = jnp.dot(a, w_ref[...], preferred_element_type=F32, precision=HIGHEST) + b_ref[...]


def _ada(c, w_ada, b_ada):
    B, D = c.shape
    N = w_ada.shape[1]
    tn = 1536
    return pl.pallas_call(
        _ada_kernel,
        grid=(N // tn,),
        in_specs=[pl.BlockSpec((B, D), lambda j: (0, 0)),
                  pl.BlockSpec((D, tn), lambda j: (0, j)),
                  pl.BlockSpec((1, tn), lambda j: (0, j))],
        out_specs=pl.BlockSpec((B, tn), lambda j: (0, j)),
        out_shape=jax.ShapeDtypeStruct((B, N), F32),
        compiler_params=_cparams("arbitrary"),
        name="ada",
    )(c, w_ada, b_ada.reshape(1, N))


def _inproj_kernel(x_ref, g_ref, sc_ref, sh_ref, w_ref, z_ref, *, n_chunk):
    x = x_ref[...]
    ms = jnp.mean(x * x, axis=-1, keepdims=True)
    y = x * lax.rsqrt(ms + EPS) * g_ref[...]
    h = (y * (1.0 + sc_ref[0]) + sh_ref[0]).astype(BF16)
    for n in range(0, z_ref.shape[1], n_chunk):
        z_ref[:, n:n + n_chunk] = jnp.dot(h, w_ref[:, n:n + n_chunk],
                                          preferred_element_type=F32).astype(BF16)


def _inproj(x2, g1, sc1, sh1, w_in_bf, tiles_per_seq):
    T, D = x2.shape
    N = w_in_bf.shape[1]
    tm = ROW_TILE
    bmap = lambda i: (i // tiles_per_seq, 0, 0)
    return pl.pallas_call(
        functools.partial(_inproj_kernel, n_chunk=512),
        grid=(T // tm,),
        in_specs=[pl.BlockSpec((tm, D), lambda i: (i, 0)),
                  pl.BlockSpec((1, D), lambda i: (0, 0)),
                  pl.BlockSpec((1, 1, D), bmap),
                  pl.BlockSpec((1, 1, D), bmap),
                  pl.BlockSpec((D, N), lambda i: (0, 0))],
        out_specs=pl.BlockSpec((tm, N), lambda i: (i, 0)),
        out_shape=jax.ShapeDtypeStruct((T, N), BF16),
        compiler_params=_cparams("arbitrary"),
        name="inproj",
    )(x2, g1, sc1, sh1, w_in_bf)


def _fold_rows(x, op):
    parts = [x[r:r + 8] for r in range(0, x.shape[0], 8)]
    while len(parts) > 1:
        parts = [op(parts[i], parts[i + 1]) for i in range(0, len(parts) - 1, 2)] + (
            [parts[-1]] if len(parts) % 2 else [])
    return parts[0]


def _attn_kernel(q_ref, k_ref, v_ref, cos_ref, sin_ref, gq_ref, gk_ref, o_ref,
                 kaug_ref, vt_ref, kmp_ref, kst_ref, s_ref, mcol_ref):
    S = q_ref.shape[1]
    blk = MOBA_BLOCK
    qt = QUERY_TILE
    sub = qt // blk
    nb = S // blk
    nq = S // qt
    nbp = kmp_ref.shape[0]
    hd = HEAD_DIM

    def norm_rope(xb, g, r0):
        rows = xb.shape[0]
        lane = lax.broadcasted_iota(jnp.int32, (rows, LANES), 1)
        head0 = lane < HEAD_DIM
        x = xb.astype(F32)
        sq = x * x
        s0 = jnp.sum(jnp.where(head0, sq, 0.0), axis=-1, keepdims=True)
        s1 = jnp.sum(jnp.where(head0, 0.0, sq), axis=-1, keepdims=True)
        inv = jnp.where(head0, lax.rsqrt(s0 * (1.0 / HEAD_DIM) + EPS), lax.rsqrt(s1 * (1.0 / HEAD_DIM) + EPS))
        y = x * inv * g
        half = ROT_DIM // 2
        rot_lo = (lane & (HEAD_DIM - 1)) < half
        rot = jnp.where(rot_lo, pltpu.roll(y, LANES - half, 1), pltpu.roll(y, half, 1))
        return y * cos_ref[pl.ds(r0, rows), :] + rot * sin_ref[pl.ds(r0, rows), :]

    kmp_ref[...] = jnp.zeros_like(kmp_ref)
    ones_row = jnp.where(lax.broadcasted_iota(jnp.int32, (16, blk), 0) == 0, 1.0, 0.0).astype(BF16)
    lane_k = lax.broadcasted_iota(jnp.int32, (blk, LANES), 1)
    head0_k = lane_k < HEAD_DIM

    def kbody(j, carry):
        r0 = pl.multiple_of(j * blk, blk)
        kr = norm_rope(k_ref[0, pl.ds(r0, blk), :], gk_ref[...], r0)
        kmp_ref[pl.ds(j, 1), :] = jnp.sum(kr, axis=0, keepdims=True) * (1.0 / blk)
        kaug_ref[0, j] = jnp.where(head0_k, kr, jnp.where(lane_k - hd == j, 1.0, 0.0)).astype(BF16)
        kaug_ref[1, j] = jnp.where(head0_k, jnp.where(lane_k == j, 1.0, 0.0), kr).astype(BF16)
        vT = v_ref[0, pl.ds(r0, blk), :].astype(F32).T
        for h in range(2):
            vt_ref[h, j, 0:HEAD_DIM, :] = vT[h * HEAD_DIM:(h + 1) * HEAD_DIM].astype(BF16)
            vt_ref[h, j, HEAD_DIM:HEAD_DIM + 16, :] = ones_row
        return carry

    lax.fori_loop(0, nb, kbody, 0)

    kmp = kmp_ref[...]
    lane_m = lax.broadcasted_iota(jnp.int32, (nbp, LANES), 1)
    k_hi = kmp.astype(BF16)
    k_lo = (kmp - k_hi.astype(F32)).astype(BF16)
    zero = jnp.zeros((nbp, LANES), BF16)
    parts = []
    for h in range(2):
        mine = (lane_m < hd) if h == 0 else (lane_m >= hd)
        parts += [jnp.where(mine, k_hi, zero), jnp.where(mine, k_lo, zero)]
    parts += [parts[0], parts[2]]
    for n, part in enumerate(parts):
        kst_ref[n * nbp:(n + 1) * nbp, :] = part

    key_i = lax.broadcasted_iota(jnp.int32, (blk, qt), 0)
    qry_i = lax.broadcasted_iota(jnp.int32, (blk, qt), 1)
    causal = [(qry_i < u * blk) | (qry_i >= (u + 1) * blk) | (key_i <= qry_i - u * blk) for u in range(sub)]
    rowf = lax.broadcasted_iota(jnp.int32, (nbp, qt), 0).astype(F32)
    subf = (lax.broadcasted_iota(jnp.int32, (nbp, qt), 1) // blk).astype(F32)
    q_scale = (hd ** -0.5) * 1.4426950408889634

    def query_operands(t):
        r0 = pl.multiple_of(t * qt, qt)
        qT = norm_rope(q_ref[0, pl.ds(r0, qt), :], gq_ref[...], r0).T
        cur = lax.convert_element_type(t * sub, F32) + subf
        q_hi = qT.astype(BF16)
        q_lo = (qT - q_hi.astype(F32)).astype(BF16)
        g1 = jnp.dot(kst_ref[0:4 * nbp, :], q_hi, preferred_element_type=F32)
        g2 = jnp.dot(kst_ref[4 * nbp:6 * nbp, :], q_lo, preferred_element_type=F32)
        qa = []
        for h in range(2):
            gate = g1[2 * h * nbp:(2 * h + 1) * nbp] + g1[(2 * h + 1) * nbp:(2 * h + 2) * nbp] + g2[h * nbp:(h + 1) * nbp]
            g = jnp.where(rowf < cur, gate, -jnp.inf)
            keep = rowf == cur
            for r in range(MOBA_TOPK):
                m = jnp.max(g, axis=0, keepdims=True)
                idx = jnp.min(jnp.where(g == m, rowf, 1e9), axis=0, keepdims=True)
                pick = (rowf == idx) & (cur > r)
                keep = keep | pick
                g = jnp.where(pick, -jnp.inf, g)
            bias = jnp.where(keep, 0.0, NEG)
            qs = qT[h * hd:(h + 1) * hd] * q_scale
            pad = jnp.zeros((LANES - hd - nbp, qt), F32)
            pieces = [qs, bias, pad] if h == 0 else [bias, pad, qs]
            qa.append(jnp.concatenate(pieces, axis=0).astype(BF16))
        return qa

    def pass1_tile(par, h, j, qa_h, mask):
        sT = jnp.dot(kaug_ref[h, j], qa_h, preferred_element_type=F32)
        if mask is not None:
            sT = jnp.where(mask, sT, NEG)
        s_ref[par, h, j] = sT
        return _fold_rows(sT, jnp.maximum)

    def pass2_tile(par, h, j):
        pT = jnp.exp2(s_ref[par, h, j] - mcol_ref[h, 0:1, :]).astype(BF16)
        return jnp.dot(vt_ref[h, j], pT, preferred_element_type=F32)

    def pass1_own(t, par, qa):
        mx = []
        for h in range(2):
            f = [pass1_tile(par, h, t * sub + u, qa[h], causal[u]) for u in range(sub)]
            mx.append(functools.reduce(jnp.maximum, f))
        return mx

    def pass2_own(t, par):
        return [sum(pass2_tile(par, h, t * sub + u) for u in range(sub)) for h in range(2)]

    def pass1_pair(p, par, qa, mx):
        return [functools.reduce(jnp.maximum, [mx[h]] + [pass1_tile(par, h, 2 * p + u, qa[h], None) for u in range(2)])
                for h in range(2)]

    def pass2_pair(p, par, acc):
        return [acc[h] + sum(pass2_tile(par, h, 2 * p + u) for u in range(2)) for h in range(2)]

    def finish_pass1(mx):
        for h in range(2):
            mcol_ref[h] = jnp.broadcast_to(jnp.max(mx[h], axis=0, keepdims=True), mcol_ref.shape[1:])

    def finish_pass2(t, acc):
        outT = jnp.concatenate([acc[h][0:hd] / acc[h][hd:hd + 1] for h in range(2)], axis=0)
        o_ref[0, pl.ds(pl.multiple_of(t * qt, qt), qt), :] = outT.T.astype(BF16)

    def stage(t, par):

        @pl.when(t == 0)
        def _():
            finish_pass1(pass1_own(t, par, query_operands(t)))

        @pl.when((t > 0) & (t < nq))
        def _():
            qa = query_operands(t)
            mx = pass1_own(t, par, qa)
            acc = pass2_own(t - 1, 1 - par)
            n_prev = (t - 1) * sub // 2

            def both(p, c):
                mx, acc = c
                return tuple(pass1_pair(p, par, qa, mx)), tuple(pass2_pair(p, 1 - par, acc))

            mx, acc = lax.fori_loop(0, n_prev, both, (tuple(mx), tuple(acc)))
            mx = lax.fori_loop(n_prev, t * sub // 2, lambda p, m: tuple(pass1_pair(p, par, qa, m)), tuple(mx))
            finish_pass2(t - 1, acc)
            finish_pass1(mx)

        @pl.when(t == nq)
        def _():
            acc = pass2_own(t - 1, 1 - par)
            acc = lax.fori_loop(0, (t - 1) * sub // 2, lambda p, a: tuple(pass2_pair(p, 1 - par, a)), tuple(acc))
            finish_pass2(t - 1, acc)

    def stage_pair(tt, carry):
        stage(2 * tt, 0)
        stage(2 * tt + 1, 1)
        return carry

    lax.fori_loop(0, (nq + 2) // 2, stage_pair, 0)


def _attention(z3, cosf, sinf, gq2, gk2):
    B, S, _ = z3.shape
    n_pair = N_HEADS // 2
    kq = ATTN_WIDTH // LANES
    nb = S // MOBA_BLOCK
    nbp = -(-nb // 16) * 16
    return pl.pallas_call(
        _attn_kernel,
        grid=(B, n_pair),
        in_specs=[pl.BlockSpec((1, S, LANES), lambda b, p: (b, 0, p)),
                  pl.BlockSpec((1, S, LANES), lambda b, p: (b, 0, kq + p)),
                  pl.BlockSpec((1, S, LANES), lambda b, p: (b, 0, 2 * kq + p)),
                  pl.BlockSpec((S, LANES), lambda b, p: (0, 0), pipeline_mode=pl.Buffered(1)),
                  pl.BlockSpec((S, LANES), lambda b, p: (0, 0), pipeline_mode=pl.Buffered(1)),
                  pl.BlockSpec((1, LANES), lambda b, p: (0, 0)),
                  pl.BlockSpec((1, LANES), lambda b, p: (0, 0))],
        out_specs=pl.BlockSpec((1, S, LANES), lambda b, p: (b, 0, p)),
        out_shape=jax.ShapeDtypeStruct((B, S, ATTN_WIDTH), BF16),
        scratch_shapes=[pltpu.VMEM((2, nb, MOBA_BLOCK, LANES), BF16),
                        pltpu.VMEM((2, nb, HEAD_DIM + 16, MOBA_BLOCK), BF16),
                        pltpu.VMEM((nbp, LANES), F32),
                        pltpu.VMEM((6 * nbp, LANES), BF16),
                        pltpu.VMEM((2, 2, nb, MOBA_BLOCK, QUERY_TILE), F32),
                        pltpu.VMEM((2, 8, QUERY_TILE), F32)],
        compiler_params=pltpu.CompilerParams(dimension_semantics=("arbitrary", "arbitrary"),
                                             vmem_limit_bytes=ATTN_VMEM_LIMIT),
        name="attn",
    )(z3, z3, z3, cosf, sinf, gq2, gk2)


def _slabs(width):
    return width // LANES


def _load_slabs(ref, row0, rows, n_slab, lead=()):
    return jnp.concatenate([ref[lead + (pl.ds(row0 * n_slab + s, rows, stride=n_slab), slice(None))]
                            for s in range(n_slab)], axis=1)


def _store_slabs(ref, row0, val):
    rows, width = val.shape
    n_slab = _slabs(width)
    for s in range(n_slab):
        ref[pl.ds(row0 * n_slab + s, rows, stride=n_slab), :] = val[:, s * LANES:(s + 1) * LANES]


def _post_kernel(x_ref, ya_ref, xb_ref, bg_ref, cg_ref, gta_ref, gtb_ref, ga1_ref, sc2_ref, sh2_ref,
                 cw_ref, cb_ref, wpa_ref, wpb_ref, wo_ref, g2_ref, wr_ref, br_ref, tri_ref,
                 x1_ref, h2_ref, ri_ref, rf_ref, cnt_ref, ubuf_ref, run_ref, *, tiles_per_seq):
    i = pl.program_id(0)
    tm = x_ref.shape[0]
    rc = POST_CHUNK
    halo = 8

    @pl.when(i == 0)
    def _():
        run_ref[...] = jnp.zeros_like(run_ref)

    @pl.when(i % tiles_per_seq == 0)
    def _():
        ubuf_ref[0:halo, :] = jnp.zeros((halo, CONV_WIDTH), F32)

    ubuf_ref[halo:halo + tm, :] = cg_ref[...].astype(F32) * xb_ref[...].astype(F32)
    cw = cw_ref[...]
    lanef = lax.broadcasted_iota(jnp.int32, (rc, LANES), 1).astype(F32)
    half_d = x_ref.shape[1] // 2
    run = run_ref[0:1, :]

    for c in range(tm // rc):
        rows = pl.ds(c * rc, rc)
        conv = (cw[0:1, :] * ubuf_ref[pl.ds(halo - 2 + c * rc, rc), :]
                + cw[1:2, :] * ubuf_ref[pl.ds(halo - 1 + c * rc, rc), :]
                + cw[2:3, :] * ubuf_ref[pl.ds(halo + c * rc, rc), :])
        y_b = (bg_ref[rows, :].astype(F32) * (conv + cb_ref[...])).astype(BF16)
        pa = jnp.dot(ya_ref[rows, :], wpa_ref[...], preferred_element_type=F32)
        pb = jnp.dot(y_b, wpb_ref[...], preferred_element_type=F32)
        merged = (jax.nn.sigmoid(gta_ref[rows, :].astype(F32)) * pa
                  + jax.nn.sigmoid(gtb_ref[rows, :].astype(F32)) * pb).astype(BF16)
        x1 = x_ref[rows, :] + ga1_ref[0] * jnp.dot(merged, wo_ref[...], preferred_element_type=F32)
        x1_ref[rows, :] = x1

        ms = jnp.mean(x1 * x1, axis=-1, keepdims=True)
        h2 = x1 * lax.rsqrt(ms + EPS) * g2_ref[...]
        h2 = h2 * (1.0 + sc2_ref[0]) + sh2_ref[0]
        h_hi = h2.astype(BF16)
        h_hi32 = h_hi.astype(F32)
        bits = pltpu.bitcast(h_hi32, jnp.uint32)
        _store_slabs(h2_ref, c * rc, (bits[:, :half_d] >> 16) | bits[:, half_d:])

        h_lo = (h2 - h_hi32).astype(BF16)
        r = jnp.dot(h_hi, wr_ref[...], preferred_element_type=F32)
        logit = (r[:, :LANES] + r[:, LANES:]
                 + jnp.dot(h_lo, wr_ref[:, :LANES], preferred_element_type=F32) + br_ref[...])
        gl = jnp.where(lanef < N_GROUPS, logit, -jnp.inf)
        gmax = jnp.max(gl, axis=-1, keepdims=True)
        g_idx = jnp.min(jnp.where(gl == gmax, lanef, 1e9), axis=-1, keepdims=True)
        g_w = 1.0 / jnp.sum(jnp.exp(gl - gmax), axis=-1, keepdims=True)
        e_lo = N_GROUPS + EXPERTS_PER_GROUP * g_idx
        el = jnp.where((lanef >= e_lo) & (lanef < e_lo + EXPERTS_PER_GROUP), logit, -jnp.inf)
        v0 = jnp.max(el, axis=-1, keepdims=True)
        i0 = jnp.min(jnp.where(el == v0, lanef, 1e9), axis=-1, keepdims=True)
        el = jnp.where(lanef == i0, -jnp.inf, el)
        v1 = jnp.max(el, axis=-1, keepdims=True)
        i1 = jnp.min(jnp.where(el == v1, lanef, 1e9), axis=-1, keepdims=True)
        t = jnp.exp(v1 - v0)
        w0 = g_w / (1.0 + t)
        w1 = g_w * t / (1.0 + t)
        e0 = i0 - N_GROUPS
        e1 = i1 - N_GROUPS

        oh0 = lanef == e0
        oh1 = lanef == e1
        oh = jnp.where(oh0 | oh1, 1.0, 0.0)
        before = jnp.dot(tri_ref[...], oh.astype(BF16), preferred_element_type=F32) + run
        r0 = jnp.sum(jnp.where(oh0, before, 0.0), axis=-1, keepdims=True)
        r1 = jnp.sum(jnp.where(oh1, before, 0.0), axis=-1, keepdims=True)
        run = run + jnp.sum(oh, axis=0, keepdims=True)

        ri = jnp.where(lanef == 0, e0, jnp.where(lanef == 1, e1, jnp.where(lanef == 2, r0, jnp.where(lanef == 3, r1, 0.0))))
        ri_ref[:, rows] = ri.astype(jnp.int32).T[0:8]
        rf_ref[rows, :] = jnp.where(lanef == 0, w0, jnp.where(lanef == 1, w1, 0.0))

    ubuf_ref[0:halo, :] = ubuf_ref[tm:tm + halo, :]
    run_ref[...] = jnp.broadcast_to(run, run_ref.shape)
    cnt_ref[...] = jnp.broadcast_to(run, cnt_ref.shape)


def _post(x2, ya2, z2, ga1, sc2, sh2, conv_w, conv_b, wpa, wpb, wo, g2, wr, br, tri, tiles_per_seq):
    T, D = x2.shape
    tm = ROW_TILE
    cw = CONV_WIDTH
    xcol = 3 * ATTN_WIDTH // cw
    gcol = (3 * ATTN_WIDTH + 3 * cw) // D
    bmap = lambda i: (i // tiles_per_seq, 0, 0)
    const = lambda i: (0, 0)
    return pl.pallas_call(
        functools.partial(_post_kernel, tiles_per_seq=tiles_per_seq),
        grid=(T // tm,),
        in_specs=[pl.BlockSpec((tm, D), lambda i: (i, 0)),
                  pl.BlockSpec((tm, ATTN_WIDTH), lambda i: (i, 0)),
                  pl.BlockSpec((tm, cw), lambda i: (i, xcol)),
                  pl.BlockSpec((tm, cw), lambda i: (i, xcol + 1)),
                  pl.BlockSpec((tm, cw), lambda i: (i, xcol + 2)),
                  pl.BlockSpec((tm, D), lambda i: (i, gcol)),
                  pl.BlockSpec((tm, D), lambda i: (i, gcol + 1)),
                  pl.BlockSpec((1, 1, D), bmap),
                  pl.BlockSpec((1, 1, D), bmap),
                  pl.BlockSpec((1, 1, D), bmap),
                  pl.BlockSpec((CONV_K, cw), const),
                  pl.BlockSpec((1, cw), const),
                  pl.BlockSpec((ATTN_WIDTH, D), const),
                  pl.BlockSpec((cw, D), const),
                  pl.BlockSpec((D, D), const),
                  pl.BlockSpec((1, D), const),
                  pl.BlockSpec((D, 2 * LANES), const),
                  pl.BlockSpec((1, LANES), const),
                  pl.BlockSpec((POST_CHUNK, POST_CHUNK), const)],
        out_specs=[pl.BlockSpec((tm, D), lambda i: (i, 0)),
                   pl.BlockSpec((tm * _slabs(D // 2), LANES), lambda i: (i, 0)),
                   pl.BlockSpec((8, tm), lambda i: (0, i)),
                   pl.BlockSpec((tm, LANES), lambda i: (i, 0)),
                   pl.BlockSpec((8, LANES), const)],
        out_shape=[jax.ShapeDtypeStruct((T, D), F32),
                   jax.ShapeDtypeStruct((T * _slabs(D // 2), LANES), jnp.uint32),
                   jax.ShapeDtypeStruct((8, T), jnp.int32),
                   jax.ShapeDtypeStruct((T, LANES), F32),
                   jax.ShapeDtypeStruct((8, LANES), F32)],
        scratch_shapes=[pltpu.VMEM((tm + 16, cw), F32),
                        pltpu.VMEM((8, LANES), F32)],
        compiler_params=_cparams("arbitrary"),
        name="post",
    )(x2, ya2, z2, z2, z2, z2, z2, ga1, sc2, sh2, conv_w, conv_b, wpa, wpb, wo, g2, wr, br, tri)


def _slots_kernel(ps_ref, ri_ref, d_ref, *, n_slab):
    e = ri_ref[0:TOPK_IN_GROUP, :]
    start = jnp.zeros(e.shape, jnp.int32)
    for k in range(N_EXPERTS):
        start = jnp.where(e == k, ps_ref[k], start)
    d_ref[...] = (start + ri_ref[TOPK_IN_GROUP:2 * TOPK_IN_GROUP, :]) * n_slab


def _slots(pad_start, riT, n_slab):
    T = riT.shape[1]
    grid_spec = pltpu.PrefetchScalarGridSpec(
        num_scalar_prefetch=1,
        grid=(1,),
        in_specs=[pl.BlockSpec(riT.shape, lambda i, ps: (0, 0))],
        out_specs=pl.BlockSpec((TOPK_IN_GROUP, T), lambda i, ps: (0, 0)),
    )
    return pl.pallas_call(
        functools.partial(_slots_kernel, n_slab=n_slab),
        grid_spec=grid_spec,
        out_shape=jax.ShapeDtypeStruct((TOPK_IN_GROUP, T), jnp.int32),
        compiler_params=_cparams("arbitrary"),
        name="slots",
    )(pad_start, riT)


def _dispatch_kernel(d0_ref, d1_ref, tail_ref, h_ref, xs_ref, zero_ref, sem, zsem, *, n_slab):
    tm = h_ref.shape[0] // n_slab
    base = pl.program_id(0) * tm
    blk_rows = zero_ref.shape[0]

    @pl.when(pl.program_id(0) == 0)
    def _():
        zero_ref[...] = jnp.zeros_like(zero_ref)

        def tail_copy(e):
            start = pl.multiple_of(tail_ref[e] * blk_rows, blk_rows)
            return pltpu.make_async_copy(zero_ref, xs_ref.at[pl.ds(start, blk_rows), :], zsem)

        for e in range(tail_ref.shape[0]):
            @pl.when(tail_ref[e] >= 0)
            def _():
                tail_copy(e).start()
        for e in range(tail_ref.shape[0]):
            @pl.when(tail_ref[e] >= 0)
            def _():
                tail_copy(e).wait()

    def body(g, carry):
        r8 = pl.multiple_of(g * DMA_UNROLL, DMA_UNROLL)
        for u in range(DMA_UNROLL):
            src = h_ref.at[pl.ds((r8 + u) * n_slab, n_slab), :]
            pltpu.make_async_copy(src, xs_ref.at[pl.ds(d0_ref[base + r8 + u], n_slab), :], sem).start(priority=0)
            pltpu.make_async_copy(src, xs_ref.at[pl.ds(d1_ref[base + r8 + u], n_slab), :], sem).start(priority=1)
        return carry

    lax.fori_loop(0, tm // DMA_UNROLL, body, 0)
    for _ in range(TOPK_IN_GROUP):
        pltpu.make_async_copy(h_ref, xs_ref.at[pl.ds(0, tm * n_slab), :], sem).wait()


def _dispatch(dest0, dest1, tail_blk, h2p, n_pad, n_slab):
    tm = ROW_TILE
    T = h2p.shape[0] // n_slab
    grid_spec = pltpu.PrefetchScalarGridSpec(
        num_scalar_prefetch=3,
        grid=(T // tm,),
        in_specs=[pl.BlockSpec((tm * n_slab, LANES), lambda i, d0, d1, tb: (i, 0))],
        out_specs=pl.BlockSpec(memory_space=pl.ANY),
        scratch_shapes=[pltpu.VMEM((EXPERT_ROWS * n_slab, LANES), h2p.dtype),
                        pltpu.SemaphoreType.DMA(()),
                        pltpu.SemaphoreType.DMA(())],
    )
    return pl.pallas_call(
        functools.partial(_dispatch_kernel, n_slab=n_slab),
        grid_spec=grid_spec,
        out_shape=jax.ShapeDtypeStruct((n_pad * n_slab, LANES), h2p.dtype),
        compiler_params=_cparams("arbitrary"),
        name="dispatch",
    )(dest0, dest1, tail_blk, h2p)


def _expert_kernel(be_ref, nv_ref, xs_ref, w1_ref, w3_ref, w2_ref, ys_ref):
    @pl.when(pl.program_id(0) >= nv_ref[0])
    def _():
        ys_ref[...] = jnp.zeros_like(ys_ref)

    @pl.when(pl.program_id(0) < nv_ref[0])
    def _():
        half_d = w1_ref.shape[1] // 2
        xp = _load_slabs(xs_ref, 0, EXPERT_ROWS, _slabs(half_d))
        x_lo = pltpu.bitcast(xp << 16, F32).astype(BF16)
        x_hi = pltpu.bitcast(xp & jnp.uint32(0xFFFF0000), F32).astype(BF16)
        a = (jnp.dot(x_lo, w1_ref[0, :half_d], preferred_element_type=F32)
             + jnp.dot(x_hi, w1_ref[0, half_d:], preferred_element_type=F32))
        b = (jnp.dot(x_lo, w3_ref[0, :half_d], preferred_element_type=F32)
             + jnp.dot(x_hi, w3_ref[0, half_d:], preferred_element_type=F32))
        hid = (a * jax.nn.sigmoid(a) * b).astype(BF16)
        y = jnp.dot(hid, w2_ref[0], preferred_element_type=F32)
        bits = pltpu.bitcast(y.astype(BF16).astype(F32), jnp.uint32)
        _store_slabs(ys_ref, 0, (bits[:, :half_d] >> 16) | bits[:, half_d:])


def _experts(blk_expert, n_valid, xs, w1b, w3b, w2b):
    _, D, F = w1b.shape
    rb = EXPERT_ROWS
    blk_rows = rb * _slabs(D // 2)
    row_blk = lambda i, be, nv: (jnp.minimum(i, nv[0] - 1), 0)
    out_blk = lambda i, be, nv: (i, 0)
    wgt_blk = lambda i, be, nv: (be[i], 0, 0)
    grid_spec = pltpu.PrefetchScalarGridSpec(
        num_scalar_prefetch=2,
        grid=(xs.shape[0] // blk_rows,),
        in_specs=[pl.BlockSpec((blk_rows, LANES), row_blk),
                  pl.BlockSpec((1, D, F), wgt_blk),
                  pl.BlockSpec((1, D, F), wgt_blk),
                  pl.BlockSpec((1, F, D), wgt_blk)],
        out_specs=pl.BlockSpec((blk_rows, LANES), out_blk),
    )
    return pl.pallas_call(
        _expert_kernel,
        grid_spec=grid_spec,
        out_shape=jax.ShapeDtypeStruct(xs.shape, jnp.uint32),
        compiler_params=_cparams("arbitrary"),
        name="experts",
    )(blk_expert, n_valid, xs, w1b, w3b, w2b)


def _combine_kernel(d0_ref, d1_ref, x1_ref, rf_ref, ga2_ref, ys_ref, o_ref, buf_ref, sem, *, n_slab):
    i = pl.program_id(0)
    n = pl.num_programs(0)
    tm = x1_ref.shape[0]

    def gather(step, slot):
        base = step * tm

        def body(g, carry):
            r8 = pl.multiple_of(g * DMA_UNROLL, DMA_UNROLL)
            for u in range(DMA_UNROLL):
                dst = pl.ds((r8 + u) * n_slab, n_slab)
                pltpu.make_async_copy(ys_ref.at[pl.ds(d0_ref[base + r8 + u], n_slab), :],
                                      buf_ref.at[slot, 0, dst, :], sem.at[slot]).start(priority=0)
                pltpu.make_async_copy(ys_ref.at[pl.ds(d1_ref[base + r8 + u], n_slab), :],
                                      buf_ref.at[slot, 1, dst, :], sem.at[slot]).start(priority=1)
            return carry

        lax.fori_loop(0, tm // DMA_UNROLL, body, 0)

    @pl.when(i == 0)
    def _():
        gather(0, 0)

    @pl.when(i + 1 < n)
    def _():
        gather(i + 1, (i + 1) % 2)

    slot = i % 2
    for k in range(TOPK_IN_GROUP):
        pltpu.make_async_copy(ys_ref.at[pl.ds(0, tm * n_slab), :], buf_ref.at[slot, k], sem.at[slot]).wait()
    rf = rf_ref[...]
    w = [rf[:, k:k + 1] for k in range(TOPK_IN_GROUP)]
    yp = [_load_slabs(buf_ref, 0, tm, n_slab, lead=(slot, k)) for k in range(TOPK_IN_GROUP)]
    half_d = n_slab * LANES
    y_lo = sum(w[k] * pltpu.bitcast(yp[k] << 16, F32) for k in range(TOPK_IN_GROUP))
    y_hi = sum(w[k] * pltpu.bitcast(yp[k] & jnp.uint32(0xFFFF0000), F32) for k in range(TOPK_IN_GROUP))
    ga2 = ga2_ref[0]
    o_ref[:, :half_d] = x1_ref[:, :half_d] + ga2[:, :half_d] * y_lo
    o_ref[:, half_d:] = x1_ref[:, half_d:] + ga2[:, half_d:] * y_hi


def _combine(dest0, dest1, x1, rf, ga2, ys, tiles_per_seq):
    T, D = x1.shape
    n_slab = _slabs(D // 2)
    tm = ROW_TILE
    row = lambda i, d0, d1: (i, 0)
    grid_spec = pltpu.PrefetchScalarGridSpec(
        num_scalar_prefetch=2,
        grid=(T // tm,),
        in_specs=[pl.BlockSpec((tm, D), row),
                  pl.BlockSpec((tm, LANES), row),
                  pl.BlockSpec((1, 1, D), lambda i, d0, d1: (i // tiles_per_seq, 0, 0)),
                  pl.BlockSpec(memory_space=pl.ANY)],
        out_specs=pl.BlockSpec((tm, D), row),
        scratch_shapes=[pltpu.VMEM((2, TOPK_IN_GROUP, tm * n_slab, LANES), ys.dtype),
                        pltpu.SemaphoreType.DMA((2,))],
    )
    return pl.pallas_call(
        functools.partial(_combine_kernel, n_slab=n_slab),
        grid_spec=grid_spec,
        out_shape=jax.ShapeDtypeStruct((T, D), F32),
        compiler_params=_cparams("arbitrary"),
        name="combine",
    )(dest0, dest1, x1, rf, ga2, ys)


def _rope_tables(S):
    pos = jnp.arange(S, dtype=F32)
    inv_freq = ROPE_THETA ** (-jnp.arange(0, ROT_DIM, 2, dtype=F32) / ROT_DIM)
    ang = pos[:, None] * inv_freq[None, :]
    cos, sin = jnp.cos(ang), jnp.sin(ang)
    half = ROT_DIM // 2
    ones = jnp.ones((S, HEAD_DIM - ROT_DIM), F32)
    cos_h = jnp.concatenate([cos, cos, ones], axis=1)
    sin_h = jnp.concatenate([-sin, sin, 0.0 * ones], axis=1)
    return jnp.tile(cos_h, (1, LANES // HEAD_DIM)), jnp.tile(sin_h, (1, LANES // HEAD_DIM))


def kernel(x, c, w_ada, b_ada, g_norm1, g_norm2, w_in, g_q, g_k, conv_w, conv_b,
           w_pa, w_pb, w_o, w_rg, b_rg, w_re, b_re, w1, w3, w2):
    B, S, D = x.shape
    T = B * S
    assert S % ROW_TILE == 0 and S % QUERY_TILE == 0 and QUERY_TILE % (2 * MOBA_BLOCK) == 0
    assert S // MOBA_BLOCK <= LANES - HEAD_DIM
    tiles_per_seq = S // ROW_TILE
    l = 0

    mod = _ada(c, w_ada[l], b_ada[l])
    sh1, sc1, ga1, sh2, sc2, ga2 = [m.reshape(B, 1, D) for m in jnp.split(mod, N_MOD, axis=-1)]

    x2 = x.reshape(T, D)
    z2 = _inproj(x2, g_norm1[l].reshape(1, D), sc1, sh1, w_in[l].astype(BF16), tiles_per_seq)

    cosf, sinf = _rope_tables(S)
    rep = LANES // HEAD_DIM
    ya = _attention(z2.reshape(B, S, -1), cosf, sinf,
                    jnp.tile(g_q[l], rep).reshape(1, LANES), jnp.tile(g_k[l], rep).reshape(1, LANES))

    wr = jnp.zeros((D, LANES), F32).at[:, :N_GROUPS].set(w_rg[l]).at[:, N_GROUPS:N_GROUPS + N_EXPERTS].set(w_re[l])
    br = jnp.zeros((1, LANES), F32).at[0, :N_GROUPS].set(b_rg[l]).at[0, N_GROUPS:N_GROUPS + N_EXPERTS].set(b_re[l])
    wr_hi = wr.astype(BF16)
    wr2 = jnp.concatenate([wr_hi, (wr - wr_hi.astype(F32)).astype(BF16)], axis=1)
    tri = (lax.broadcasted_iota(jnp.int32, (POST_CHUNK, POST_CHUNK), 1)
           < lax.broadcasted_iota(jnp.int32, (POST_CHUNK, POST_CHUNK), 0)).astype(BF16)
    x1, h2, ri, rf, cnt = _post(x2, ya.reshape(T, ATTN_WIDTH), z2, ga1, sc2, sh2,
                                conv_w[l], conv_b[l].reshape(1, CONV_WIDTH),
                                w_pa[l].astype(BF16), w_pb[l].astype(BF16), w_o[l].astype(BF16),
                                g_norm2[l].reshape(1, D), wr2, br, tri, tiles_per_seq)

    rb = EXPERT_ROWS
    counts = cnt[0, :N_EXPERTS].astype(jnp.int32)
    padded = (counts + rb - 1) // rb * rb
    pad_end = jnp.cumsum(padded)
    pad_start = pad_end - padded
    dest = _slots(pad_start.astype(jnp.int32), ri, _slabs(D // 2))
    dest0, dest1 = dest[0], dest[1]
    n_blocks = -(-T * TOPK_IN_GROUP // rb) + N_EXPERTS
    n_pad = n_blocks * rb
    n_valid = (pad_end[-1] // rb).astype(jnp.int32)
    blk_start = jnp.minimum(jnp.arange(n_blocks, dtype=jnp.int32), n_valid - 1) * rb
    blk_expert = jnp.sum(pad_end[None, :] <= blk_start[:, None], axis=-1).astype(jnp.int32)
    unused = n_valid + jnp.arange(N_EXPERTS, dtype=jnp.int32)
    tail_blk = jnp.concatenate([jnp.where(padded > 0, pad_end // rb - 1, -1),
                                jnp.where(unused < n_blocks, unused, -1)]).astype(jnp.int32)

    xs = _dispatch(dest0, dest1, tail_blk, h2, n_pad, _slabs(D // 2))
    ys = _experts(blk_expert, n_valid.reshape(1), xs, w1[l].astype(BF16), w3[l].astype(BF16), w2[l].astype(BF16))
    out = _combine(dest0, dest1, x1, rf, ga2, ys, tiles_per_seq)
    return out.reshape(B, S, D)
```

```python
import functools

import jax
import jax.numpy as jnp
from jax import lax
from jax.experimental import pallas as pl
from jax.experimental.pallas import tpu as pltpu

F32 = jnp.float32
BF16 = jnp.bfloat16
HIGHEST = lax.Precision.HIGHEST

N_HEADS = 8
HEAD_DIM = 64
ATTN_WIDTH = N_HEADS * HEAD_DIM
CONV_WIDTH = 512
CONV_K = 3
MOBA_BLOCK = 256
MOBA_TOPK = 3
ROPE_THETA = 500000.0
ROT_DIM = HEAD_DIM // 4
N_GROUPS = 4
EXPERTS_PER_GROUP = 8
N_EXPERTS = N_GROUPS * EXPERTS_PER_GROUP
TOPK_IN_GROUP = 2
N_MOD = 6
EPS = 1e-6

LANES = 128
NEG = -1e30
ROW_TILE = 512
POST_CHUNK = 256
QUERY_TILE = 512
DMA_UNROLL = 8
EXPERT_ROWS = 512
VMEM_LIMIT = 56 * 1024 * 1024
ATTN_VMEM_LIMIT = 60 * 1024 * 1024


def _cparams(*sem):
    return pltpu.CompilerParams(dimension_semantics=sem, vmem_limit_bytes=VMEM_LIMIT)


def _ada_kernel(c_ref, w_ref, b_ref, o_ref):
    c = c_ref[...]
    a = c * jax.nn.sigmoid(c)
    o_ref[...] = jnp.dot(a, w_ref[...], preferred_element_type=F32, precision=HIGHEST) + b_ref[...]


def _ada(c, w_ada, b_ada):
    B, D = c.shape
    N = w_ada.shape[1]
    tn = 1536
    return pl.pallas_call(
        _ada_kernel,
        grid=(N // tn,),
        in_specs=[pl.BlockSpec((B, D), lambda j: (0, 0)),
                  pl.BlockSpec((D, tn), lambda j: (0, j)),
                  pl.BlockSpec((1, tn), lambda j: (0, j))],
        out_specs=pl.BlockSpec((B, tn), lambda j: (0, j)),
        out_shape=jax.ShapeDtypeStruct((B, N), F32),
        compiler_params=_cparams("arbitrary"),
        name="ada",
    )(c, w_ada, b_ada.reshape(1, N))


def _inproj_kernel(x_ref, g_ref, sc_ref, sh_ref, w_ref, z_ref, *, n_chunk):
    x = x_ref[...]
    ms = jnp.mean(x * x, axis=-1, keepdims=True)
    y = x * lax.rsqrt(ms + EPS) * g_ref[...]
    h = (y * (1.0 + sc_ref[0]) + sh_ref[0]).astype(BF16)
    for n in range(0, z_ref.shape[1], n_chunk):
        z_ref[:, n:n + n_chunk] = jnp.dot(h, w_ref[:, n:n + n_chunk],
                                          preferred_element_type=F32).astype(BF16)


def _inproj(x2, g1, sc1, sh1, w_in_bf, tiles_per_seq):
    T, D = x2.shape
    N = w_in_bf.shape[1]
    tm = ROW_TILE
    bmap = lambda i: (i // tiles_per_seq, 0, 0)
    return pl.pallas_call(
        functools.partial(_inproj_kernel, n_chunk=512),
        grid=(T // tm,),
        in_specs=[pl.BlockSpec((tm, D), lambda i: (i, 0)),
                  pl.BlockSpec((1, D), lambda i: (0, 0)),
                  pl.BlockSpec((1, 1, D), bmap),
                  pl.BlockSpec((1, 1, D), bmap),
                  pl.BlockSpec((D, N), lambda i: (0, 0))],
        out_specs=pl.BlockSpec((tm, N), lambda i: (i, 0)),
        out_shape=jax.ShapeDtypeStruct((T, N), BF16),
        compiler_params=_cparams("arbitrary"),
        name="inproj",
    )(x2, g1, sc1, sh1, w_in_bf)


def _fold_rows(x, op):
    parts = [x[r:r + 8] for r in range(0, x.shape[0], 8)]
    while len(parts) > 1:
        parts = [op(parts[i], parts[i + 1]) for i in range(0, len(parts) - 1, 2)] + (
            [parts[-1]] if len(parts) % 2 else [])
    return parts[0]


def _attn_kernel(q_ref, k_ref, v_ref, cos_ref, sin_ref, gq_ref, gk_ref, o_ref,
                 kaug_ref, vt_ref, kmp_ref, kst_ref, s_ref, mcol_ref):
    S = q_ref.shape[1]
    blk = MOBA_BLOCK
    qt = QUERY_TILE
    sub = qt // blk
    nb = S // blk
    nq = S // qt
    nbp = kmp_ref.shape[0]
    hd = HEAD_DIM

    def norm_rope(xb, g, r0):
        rows = xb.shape[0]
        lane = lax.broadcasted_iota(jnp.int32, (rows, LANES), 1)
        head0 = lane < HEAD_DIM
        x = xb.astype(F32)
        sq = x * x
        s0 = jnp.sum(jnp.where(head0, sq, 0.0), axis=-1, keepdims=True)
        s1 = jnp.sum(jnp.where(head0, 0.0, sq), axis=-1, keepdims=True)
        inv = jnp.where(head0, lax.rsqrt(s0 * (1.0 / HEAD_DIM) + EPS), lax.rsqrt(s1 * (1.0 / HEAD_DIM) + EPS))
        y = x * inv * g
        half = ROT_DIM // 2
        rot_lo = (lane & (HEAD_DIM - 1)) < half
        rot = jnp.where(rot_lo, pltpu.roll(y, LANES - half, 1), pltpu.roll(y, half, 1))
        return y * cos_ref[pl.ds(r0, rows), :] + rot * sin_ref[pl.ds(r0, rows), :]

    kmp_ref[...] = jnp.zeros_like(kmp_ref)
    ones_row = jnp.where(lax.broadcasted_iota(jnp.int32, (16, blk), 0) == 0, 1.0, 0.0).astype(BF16)
    lane_k = lax.broadcasted_iota(jnp.int32, (blk, LANES), 1)
    head0_k = lane_k < HEAD_DIM

    def kbody(j, carry):
        r0 = pl.multiple_of(j * blk, blk)
        kr = norm_rope(k_ref[0, pl.ds(r0, blk), :], gk_ref[...], r0)
        kmp_ref[pl.ds(j, 1), :] = jnp.sum(kr, axis=0, keepdims=True) * (1.0 / blk)
        kaug_ref[0, j] = jnp.where(head0_k, kr, jnp.where(lane_k - hd == j, 1.0, 0.0)).astype(BF16)
        kaug_ref[1, j] = jnp.where(head0_k, jnp.where(lane_k == j, 1.0, 0.0), kr).astype(BF16)
        vT = v_ref[0, pl.ds(r0, blk), :].astype(F32).T
        for h in range(2):
            vt_ref[h, j, 0:HEAD_DIM, :] = vT[h * HEAD_DIM:(h + 1) * HEAD_DIM].astype(BF16)
            vt_ref[h, j, HEAD_DIM:HEAD_DIM + 16, :] = ones_row
        return carry

    lax.fori_loop(0, nb, kbody, 0)

    kmp = kmp_ref[...]
    lane_m = lax.broadcasted_iota(jnp.int32, (nbp, LANES), 1)
    k_hi = kmp.astype(BF16)
    k_lo = (kmp - k_hi.astype(F32)).astype(BF16)
    zero = jnp.zeros((nbp, LANES), BF16)
    parts = []
    for h in range(2):
        mine = (lane_m < hd) if h == 0 else (lane_m >= hd)
        parts += [jnp.where(mine, k_hi, zero), jnp.where(mine, k_lo, zero)]
    parts += [parts[0], parts[2]]
    for n, part in enumerate(parts):
        kst_ref[n * nbp:(n + 1) * nbp, :] = part

    key_i = lax.broadcasted_iota(jnp.int32, (blk, qt), 0)
    qry_i = lax.broadcasted_iota(jnp.int32, (blk, qt), 1)
    causal = [(qry_i < u * blk) | (qry_i >= (u + 1) * blk) | (key_i <= qry_i - u * blk) for u in range(sub)]
    rowf = lax.broadcasted_iota(jnp.int32, (nbp, qt), 0).astype(F32)
    subf = (lax.broadcasted_iota(jnp.int32, (nbp, qt), 1) // blk).astype(F32)
    q_scale = (hd ** -0.5) * 1.4426950408889634

    def query_operands(t):
        r0 = pl.multiple_of(t * qt, qt)
        qT = norm_rope(q_ref[0, pl.ds(r0, qt), :], gq_ref[...], r0).T
        cur = lax.convert_element_type(t * sub, F32) + subf
        q_hi = qT.astype(BF16)
        q_lo = (qT - q_hi.astype(F32)).astype(BF16)
        g1 = jnp.dot(kst_ref[0:4 * nbp, :], q_hi, preferred_element_type=F32)
        g2 = jnp.dot(kst_ref[4 * nbp:6 * nbp, :], q_lo, preferred_element_type=F32)
        qa = []
        for h in range(2):
            gate = g1[2 * h * nbp:(2 * h + 1) * nbp] + g1[(2 * h + 1) * nbp:(2 * h + 2) * nbp] + g2[h * nbp:(h + 1) * nbp]
            g = jnp.where(rowf < cur, gate, -jnp.inf)
            keep = rowf == cur
            for r in range(MOBA_TOPK):
                m = jnp.max(g, axis=0, keepdims=True)
                idx = jnp.min(jnp.where(g == m, rowf, 1e9), axis=0, keepdims=True)
                pick = (rowf == idx) & (cur > r)
                keep = keep | pick
                g = jnp.where(pick, -jnp.inf, g)
            bias = jnp.where(keep, 0.0, NEG)
            qs = qT[h * hd:(h + 1) * hd] * q_scale
            pad = jnp.zeros((LANES - hd - nbp, qt), F32)
            pieces = [qs, bias, pad] if h == 0 else [bias, pad, qs]
            qa.append(jnp.concatenate(pieces, axis=0).astype(BF16))
        return qa

    def pass1_tile(par, h, j, qa_h, mask):
        sT = jnp.dot(kaug_ref[h, j], qa_h, preferred_element_type=F32)
        if mask is not None:
            sT = jnp.where(mask, sT, NEG)
        s_ref[par, h, j] = sT
        return _fold_rows(sT, jnp.maximum)

    def pass2_tile(par, h, j):
        pT = jnp.exp2(s_ref[par, h, j] - mcol_ref[h, 0:1, :]).astype(BF16)
        return jnp.dot(vt_ref[h, j], pT, preferred_element_type=F32)

    def pass1_own(t, par, qa):
        mx = []
        for h in range(2):
            f = [pass1_tile(par, h, t * sub + u, qa[h], causal[u]) for u in range(sub)]
            mx.append(functools.reduce(jnp.maximum, f))
        return mx

    def pass2_own(t, par):
        return [sum(pass2_tile(par, h, t * sub + u) for u in range(sub)) for h in range(2)]

    def pass1_pair(p, par, qa, mx):
        return [functools.reduce(jnp.maximum, [mx[h]] + [pass1_tile(par, h, 2 * p + u, qa[h], None) for u in range(2)])
                for h in range(2)]

    def pass2_pair(p, par, acc):
        return [acc[h] + sum(pass2_tile(par, h, 2 * p + u) for u in range(2)) for h in range(2)]

    def finish_pass1(mx):
        for h in range(2):
            mcol_ref[h] = jnp.broadcast_to(jnp.max(mx[h], axis=0, keepdims=True), mcol_ref.shape[1:])

    def finish_pass2(t, acc):
        outT = jnp.concatenate([acc[h][0:hd] / acc[h][hd:hd + 1] for h in range(2)], axis=0)
        o_ref[0, pl.ds(pl.multiple_of(t * qt, qt), qt), :] = outT.T.astype(BF16)

    def stage(t, par):

        @pl.when(t == 0)
        def _():
            finish_pass1(pass1_own(t, par, query_operands(t)))

        @pl.when((t > 0) & (t < nq))
        def _():
            qa = query_operands(t)
            mx = pass1_own(t, par, qa)
            acc = pass2_own(t - 1, 1 - par)
            n_prev = (t - 1) * sub // 2

            def both(p, c):
                mx, acc = c
                return tuple(pass1_pair(p, par, qa, mx)), tuple(pass2_pair(p, 1 - par, acc))

            mx, acc = lax.fori_loop(0, n_prev, both, (tuple(mx), tuple(acc)))
            mx = lax.fori_loop(n_prev, t * sub // 2, lambda p, m: tuple(pass1_pair(p, par, qa, m)), tuple(mx))
            finish_pass2(t - 1, acc)
            finish_pass1(mx)

        @pl.when(t == nq)
        def _():
            acc = pass2_own(t - 1, 1 - par)
            acc = lax.fori_loop(0, (t - 1) * sub // 2, lambda p, a: tuple(pass2_pair(p, 1 - par, a)), tuple(acc))
            finish_pass2(t - 1, acc)

    def stage_pair(tt, carry):
        stage(2 * tt, 0)
        stage(2 * tt + 1, 1)
        return carry

    lax.fori_loop(0, (nq + 2) // 2, stage_pair, 0)


def _attention(z3, cosf, sinf, gq2, gk2):
    B, S, _ = z3.shape
    n_pair = N_HEADS // 2
    kq = ATTN_WIDTH // LANES
    nb = S // MOBA_BLOCK
    nbp = -(-nb // 16) * 16
    return pl.pallas_call(
        _attn_kernel,
        grid=(B, n_pair),
        in_specs=[pl.BlockSpec((1, S, LANES), lambda b, p: (b, 0, p)),
                  pl.BlockSpec((1, S, LANES), lambda b, p: (b, 0, kq + p)),
                  pl.BlockSpec((1, S, LANES), lambda b, p: (b, 0, 2 * kq + p)),
                  pl.BlockSpec((S, LANES), lambda b, p: (0, 0), pipeline_mode=pl.Buffered(1)),
                  pl.BlockSpec((S, LANES), lambda b, p: (0, 0), pipeline_mode=pl.Buffered(1)),
                  pl.BlockSpec((1, LANES), lambda b, p: (0, 0)),
                  pl.BlockSpec((1, LANES), lambda b, p: (0, 0))],
        out_specs=pl.BlockSpec((1, S, LANES), lambda b, p: (b, 0, p)),
        out_shape=jax.ShapeDtypeStruct((B, S, ATTN_WIDTH), BF16),
        scratch_shapes=[pltpu.VMEM((2, nb, MOBA_BLOCK, LANES), BF16),
                        pltpu.VMEM((2, nb, HEAD_DIM + 16, MOBA_BLOCK), BF16),
                        pltpu.VMEM((nbp, LANES), F32),
                        pltpu.VMEM((6 * nbp, LANES), BF16),
                        pltpu.VMEM((2, 2, nb, MOBA_BLOCK, QUERY_TILE), F32),
                        pltpu.VMEM((2, 8, QUERY_TILE), F32)],
        compiler_params=pltpu.CompilerParams(dimension_semantics=("arbitrary", "arbitrary"),
                                             vmem_limit_bytes=ATTN_VMEM_LIMIT),
        name="attn",
    )(z3, z3, z3, cosf, sinf, gq2, gk2)


def _slabs(width):
    return width // LANES


def _load_slabs(ref, row0, rows, n_slab, lead=()):
    return jnp.concatenate([ref[lead + (pl.ds(row0 * n_slab + s, rows, stride=n_slab), slice(None))]
                            for s in range(n_slab)], axis=1)


def _store_slabs(ref, row0, val):
    rows, width = val.shape
    n_slab = _slabs(width)
    for s in range(n_slab):
        ref[pl.ds(row0 * n_slab + s, rows, stride=n_slab), :] = val[:, s * LANES:(s + 1) * LANES]


def _post_kernel(x_ref, ya_ref, xb_ref, bg_ref, cg_ref, gta_ref, gtb_ref, ga1_ref, sc2_ref, sh2_ref,
                 cw_ref, cb_ref, wpa_ref, wpb_ref, wo_ref, g2_ref, wr_ref, br_ref, tri_ref,
                 x1_ref, h2_ref, ri_ref, rf_ref, cnt_ref, ubuf_ref, run_ref, *, tiles_per_seq):
    i = pl.program_id(0)
    tm = x_ref.shape[0]
    rc = POST_CHUNK
    halo = 8

    @pl.when(i == 0)
    def _():
        run_ref[...] = jnp.zeros_like(run_ref)

    @pl.when(i % tiles_per_seq == 0)
    def _():
        ubuf_ref[0:halo, :] = jnp.zeros((halo, CONV_WIDTH), F32)

    ubuf_ref[halo:halo + tm, :] = cg_ref[...].astype(F32) * xb_ref[...].astype(F32)
    cw = cw_ref[...]
    lanef = lax.broadcasted_iota(jnp.int32, (rc, LANES), 1).astype(F32)
    half_d = x_ref.shape[1] // 2
    run = run_ref[0:1, :]

    for c in range(tm // rc):
        rows = pl.ds(c * rc, rc)
        conv = (cw[0:1, :] * ubuf_ref[pl.ds(halo - 2 + c * rc, rc), :]
                + cw[1:2, :] * ubuf_ref[pl.ds(halo - 1 + c * rc, rc), :]
                + cw[2:3, :] * ubuf_ref[pl.ds(halo + c * rc, rc), :])
        y_b = (bg_ref[rows, :].astype(F32) * (conv + cb_ref[...])).astype(BF16)
        pa = jnp.dot(ya_ref[rows, :], wpa_ref[...], preferred_element_type=F32)
        pb = jnp.dot(y_b, wpb_ref[...], preferred_element_type=F32)
        merged = (jax.nn.sigmoid(gta_ref[rows, :].astype(F32)) * pa
                  + jax.nn.sigmoid(gtb_ref[rows, :].astype(F32)) * pb).astype(BF16)
        x1 = x_ref[rows, :] + ga1_ref[0] * jnp.dot(merged, wo_ref[...], preferred_element_type=F32)
        x1_ref[rows, :] = x1

        ms = jnp.mean(x1 * x1, axis=-1, keepdims=True)
        h2 = x1 * lax.rsqrt(ms + EPS) * g2_ref[...]
        h2 = h2 * (1.0 + sc2_ref[0]) + sh2_ref[0]
        h_hi = h2.astype(BF16)
        h_hi32 = h_hi.astype(F32)
        bits = pltpu.bitcast(h_hi32, jnp.uint32)
        _store_slabs(h2_ref, c * rc, (bits[:, :half_d] >> 16) | bits[:, half_d:])

        h_lo = (h2 - h_hi32).astype(BF16)
        r = jnp.dot(h_hi, wr_ref[...], preferred_element_type=F32)
        logit = (r[:, :LANES] + r[:, LANES:]
                 + jnp.dot(h_lo, wr_ref[:, :LANES], preferred_element_type=F32) + br_ref[...])
        gl = jnp.where(lanef < N_GROUPS, logit, -jnp.inf)
        gmax = jnp.max(gl, axis=-1, keepdims=True)
        g_idx = jnp.min(jnp.where(gl == gmax, lanef, 1e9), axis=-1, keepdims=True)
        g_w = 1.0 / jnp.sum(jnp.exp(gl - gmax), axis=-1, keepdims=True)
        e_lo = N_GROUPS + EXPERTS_PER_GROUP * g_idx
        el = jnp.where((lanef >= e_lo) & (lanef < e_lo + EXPERTS_PER_GROUP), logit, -jnp.inf)
        v0 = jnp.max(el, axis=-1, keepdims=True)
        i0 = jnp.min(jnp.where(el == v0, lanef, 1e9), axis=-1, keepdims=True)
        el = jnp.where(lanef == i0, -jnp.inf, el)
        v1 = jnp.max(el, axis=-1, keepdims=True)
        i1 = jnp.min(jnp.where(el == v1, lanef, 1e9), axis=-1, keepdims=True)
        t = jnp.exp(v1 - v0)
        w0 = g_w / (1.0 + t)
        w1 = g_w * t / (1.0 + t)
        e0 = i0 - N_GROUPS
        e1 = i1 - N_GROUPS

        oh0 = lanef == e0
        oh1 = lanef == e1
        oh = jnp.where(oh0 | oh1, 1.0, 0.0)
        before = jnp.dot(tri_ref[...], oh.astype(BF16), preferred_element_type=F32) + run
        r0 = jnp.sum(jnp.where(oh0, before, 0.0), axis=-1, keepdims=True)
        r1 = jnp.sum(jnp.where(oh1, before, 0.0), axis=-1, keepdims=True)
        run = run + jnp.sum(oh, axis=0, keepdims=True)

        ri = jnp.where(lanef == 0, e0, jnp.where(lanef == 1, e1, jnp.where(lanef == 2, r0, jnp.where(lanef == 3, r1, 0.0))))
        ri_ref[:, rows] = ri.astype(jnp.int32).T[0:8]
        rf_ref[rows, :] = jnp.where(lanef == 0, w0, jnp.where(lanef == 1, w1, 0.0))

    ubuf_ref[0:halo, :] = ubuf_ref[tm:tm + halo, :]
    run_ref[...] = jnp.broadcast_to(run, run_ref.shape)
    cnt_ref[...] = jnp.broadcast_to(run, cnt_ref.shape)


def _post(x2, ya2, z2, ga1, sc2, sh2, conv_w, conv_b, wpa, wpb, wo, g2, wr, br, tri, tiles_per_seq):
    T, D = x2.shape
    tm = ROW_TILE
    cw = CONV_WIDTH
    xcol = 3 * ATTN_WIDTH // cw
    gcol = (3 * ATTN_WIDTH + 3 * cw) // D
    bmap = lambda i: (i // tiles_per_seq, 0, 0)
    const = lambda i: (0, 0)
    return pl.pallas_call(
        functools.partial(_post_kernel, tiles_per_seq=tiles_per_seq),
        grid=(T // tm,),
        in_specs=[pl.BlockSpec((tm, D), lambda i: (i, 0)),
                  pl.BlockSpec((tm, ATTN_WIDTH), lambda i: (i, 0)),
                  pl.BlockSpec((tm, cw), lambda i: (i, xcol)),
                  pl.BlockSpec((tm, cw), lambda i: (i, xcol + 1)),
                  pl.BlockSpec((tm, cw), lambda i: (i, xcol + 2)),
                  pl.BlockSpec((tm, D), lambda i: (i, gcol)),
                  pl.BlockSpec((tm, D), lambda i: (i, gcol + 1)),
                  pl.BlockSpec((1, 1, D), bmap),
                  pl.BlockSpec((1, 1, D), bmap),
                  pl.BlockSpec((1, 1, D), bmap),
                  pl.BlockSpec((CONV_K, cw), const),
                  pl.BlockSpec((1, cw), const),
                  pl.BlockSpec((ATTN_WIDTH, D), const),
                  pl.BlockSpec((cw, D), const),
                  pl.BlockSpec((D, D), const),
                  pl.BlockSpec((1, D), const),
                  pl.BlockSpec((D, 2 * LANES), const),
                  pl.BlockSpec((1, LANES), const),
                  pl.BlockSpec((POST_CHUNK, POST_CHUNK), const)],
        out_specs=[pl.BlockSpec((tm, D), lambda i: (i, 0)),
                   pl.BlockSpec((tm * _slabs(D // 2), LANES), lambda i: (i, 0)),
                   pl.BlockSpec((8, tm), lambda i: (0, i)),
                   pl.BlockSpec((tm, LANES), lambda i: (i, 0)),
                   pl.BlockSpec((8, LANES), const)],
        out_shape=[jax.ShapeDtypeStruct((T, D), F32),
                   jax.ShapeDtypeStruct((T * _slabs(D // 2), LANES), jnp.uint32),
                   jax.ShapeDtypeStruct((8, T), jnp.int32),
                   jax.ShapeDtypeStruct((T, LANES), F32),
                   jax.ShapeDtypeStruct((8, LANES), F32)],
        scratch_shapes=[pltpu.VMEM((tm + 16, cw), F32),
                        pltpu.VMEM((8, LANES), F32)],
        compiler_params=_cparams("arbitrary"),
        name="post",
    )(x2, ya2, z2, z2, z2, z2, z2, ga1, sc2, sh2, conv_w, conv_b, wpa, wpb, wo, g2, wr, br, tri)


def _slots_kernel(ps_ref, ri_ref, d_ref, *, n_slab):
    e = ri_ref[0:TOPK_IN_GROUP, :]
    start = jnp.zeros(e.shape, jnp.int32)
    for k in range(N_EXPERTS):
        start = jnp.where(e == k, ps_ref[k], start)
    d_ref[...] = (start + ri_ref[TOPK_IN_GROUP:2 * TOPK_IN_GROUP, :]) * n_slab


def _slots(pad_start, riT, n_slab):
    T = riT.shape[1]
    grid_spec = pltpu.PrefetchScalarGridSpec(
        num_scalar_prefetch=1,
        grid=(1,),
        in_specs=[pl.BlockSpec(riT.shape, lambda i, ps: (0, 0))],
        out_specs=pl.BlockSpec((TOPK_IN_GROUP, T), lambda i, ps: (0, 0)),
    )
    return pl.pallas_call(
        functools.partial(_slots_kernel, n_slab=n_slab),
        grid_spec=grid_spec,
        out_shape=jax.ShapeDtypeStruct((TOPK_IN_GROUP, T), jnp.int32),
        compiler_params=_cparams("arbitrary"),
        name="slots",
    )(pad_start, riT)


def _dispatch_kernel(d0_ref, d1_ref, tail_ref, h_ref, xs_ref, zero_ref, sem, zsem, *, n_slab):
    tm = h_ref.shape[0] // n_slab
    base = pl.program_id(0) * tm
    blk_rows = zero_ref.shape[0]

    @pl.when(pl.program_id(0) == 0)
    def _():
        zero_ref[...] = jnp.zeros_like(zero_ref)

        def tail_copy(e):
            start = pl.multiple_of(tail_ref[e] * blk_rows, blk_rows)
            return pltpu.make_async_copy(zero_ref, xs_ref.at[pl.ds(start, blk_rows), :], zsem)

        for e in range(tail_ref.shape[0]):
            @pl.when(tail_ref[e] >= 0)
            def _():
                tail_copy(e).start()
        for e in range(tail_ref.shape[0]):
            @pl.when(tail_ref[e] >= 0)
            def _():
                tail_copy(e).wait()

    def body(g, carry):
        r8 = pl.multiple_of(g * DMA_UNROLL, DMA_UNROLL)
        for u in range(DMA_UNROLL):
            src = h_ref.at[pl.ds((r8 + u) * n_slab, n_slab), :]
            pltpu.make_async_copy(src, xs_ref.at[pl.ds(d0_ref[base + r8 + u], n_slab), :], sem).start(priority=0)
            pltpu.make_async_copy(src, xs_ref.at[pl.ds(d1_ref[base + r8 + u], n_slab), :], sem).start(priority=1)
        return carry

    lax.fori_loop(0, tm // DMA_UNROLL, body, 0)
    for _ in range(TOPK_IN_GROUP):
        pltpu.make_async_copy(h_ref, xs_ref.at[pl.ds(0, tm * n_slab), :], sem).wait()


def _dispatch(dest0, dest1, tail_blk, h2p, n_pad, n_slab):
    tm = ROW_TILE
    T = h2p.shape[0] // n_slab
    grid_spec = pltpu.PrefetchScalarGridSpec(
        num_scalar_prefetch=3,
        grid=(T // tm,),
        in_specs=[pl.BlockSpec((tm * n_slab, LANES), lambda i, d0, d1, tb: (i, 0))],
        out_specs=pl.BlockSpec(memory_space=pl.ANY),
        scratch_shapes=[pltpu.VMEM((EXPERT_ROWS * n_slab, LANES), h2p.dtype),
                        pltpu.SemaphoreType.DMA(()),
                        pltpu.SemaphoreType.DMA(())],
    )
    return pl.pallas_call(
        functools.partial(_dispatch_kernel, n_slab=n_slab),
        grid_spec=grid_spec,
        out_shape=jax.ShapeDtypeStruct((n_pad * n_slab, LANES), h2p.dtype),
        compiler_params=_cparams("arbitrary"),
        name="dispatch",
    )(dest0, dest1, tail_blk, h2p)


def _expert_kernel(be_ref, nv_ref, xs_ref, w1_ref, w3_ref, w2_ref, ys_ref):
    @pl.when(pl.program_id(0) >= nv_ref[0])
    def _():
        ys_ref[...] = jnp.zeros_like(ys_ref)

    @pl.when(pl.program_id(0) < nv_ref[0])
    def _():
        half_d = w1_ref.shape[1] // 2
        xp = _load_slabs(xs_ref, 0, EXPERT_ROWS, _slabs(half_d))
        x_lo = pltpu.bitcast(xp << 16, F32).astype(BF16)
        x_hi = pltpu.bitcast(xp & jnp.uint32(0xFFFF0000), F32).astype(BF16)
        a = (jnp.dot(x_lo, w1_ref[0, :half_d], preferred_element_type=F32)
             + jnp.dot(x_hi, w1_ref[0, half_d:], preferred_element_type=F32))
        b = (jnp.dot(x_lo, w3_ref[0, :half_d], preferred_element_type=F32)
             + jnp.dot(x_hi, w3_ref[0, half_d:], preferred_element_type=F32))
        hid = (a * jax.nn.sigmoid(a) * b).astype(BF16)
        y = jnp.dot(hid, w2_ref[0], preferred_element_type=F32)
        bits = pltpu.bitcast(y.astype(BF16).astype(F32), jnp.uint32)
        _store_slabs(ys_ref, 0, (bits[:, :half_d] >> 16) | bits[:, half_d:])


def _experts(blk_expert, n_valid, xs, w1b, w3b, w2b):
    _, D, F = w1b.shape
    rb = EXPERT_ROWS
    blk_rows = rb * _slabs(D // 2)
    row_blk = lambda i, be, nv: (jnp.minimum(i, nv[0] - 1), 0)
    out_blk = lambda i, be, nv: (i, 0)
    wgt_blk = lambda i, be, nv: (be[i], 0, 0)
    grid_spec = pltpu.PrefetchScalarGridSpec(
        num_scalar_prefetch=2,
        grid=(xs.shape[0] // blk_rows,),
        in_specs=[pl.BlockSpec((blk_rows, LANES), row_blk),
                  pl.BlockSpec((1, D, F), wgt_blk),
                  pl.BlockSpec((1, D, F), wgt_blk),
                  pl.BlockSpec((1, F, D), wgt_blk)],
        out_specs=pl.BlockSpec((blk_rows, LANES), out_blk),
    )
    return pl.pallas_call(
        _expert_kernel,
        grid_spec=grid_spec,
        out_shape=jax.ShapeDtypeStruct(xs.shape, jnp.uint32),
        compiler_params=_cparams("arbitrary"),
        name="experts",
    )(blk_expert, n_valid, xs, w1b, w3b, w2b)


def _combine_kernel(d0_ref, d1_ref, x1_ref, rf_ref, ga2_ref, ys_ref, o_ref, buf_ref, sem, *, n_slab):
    i = pl.program_id(0)
    n = pl.num_programs(0)
    tm = x1_ref.shape[0]

    def gather(step, slot):
        base = step * tm

        def body(g, carry):
            r8 = pl.multiple_of(g * DMA_UNROLL, DMA_UNROLL)
            for u in range(DMA_UNROLL):
                dst = pl.ds((r8 + u) * n_slab, n_slab)
                pltpu.make_async_copy(ys_ref.at[pl.ds(d0_ref[base + r8 + u], n_slab), :],
                                      buf_ref.at[slot, 0, dst, :], sem.at[slot]).start(priority=0)
                pltpu.make_async_copy(ys_ref.at[pl.ds(d1_ref[base + r8 + u], n_slab), :],
                                      buf_ref.at[slot, 1, dst, :], sem.at[slot]).start(priority=1)
            return carry

        lax.fori_loop(0, tm // DMA_UNROLL, body, 0)

    @pl.when(i == 0)
    def _():
        gather(0, 0)

    @pl.when(i + 1 < n)
    def _():
        gather(i + 1, (i + 1) % 2)

    slot = i % 2
    for k in range(TOPK_IN_GROUP):
        pltpu.make_async_copy(ys_ref.at[pl.ds(0, tm * n_slab), :], buf_ref.at[slot, k], sem.at[slot]).wait()
    rf = rf_ref[...]
    w = [rf[:, k:k + 1] for k in range(TOPK_IN_GROUP)]
    yp = [_load_slabs(buf_ref, 0, tm, n_slab, lead=(slot, k)) for k in range(TOPK_IN_GROUP)]
    half_d = n_slab * LANES
    y_lo = sum(w[k] * pltpu.bitcast(yp[k] << 16, F32) for k in range(TOPK_IN_GROUP))
    y_hi = sum(w[k] * pltpu.bitcast(yp[k] & jnp.uint32(0xFFFF0000), F32) for k in range(TOPK_IN_GROUP))
    ga2 = ga2_ref[0]
    o_ref[:, :half_d] = x1_ref[:, :half_d] + ga2[:, :half_d] * y_lo
    o_ref[:, half_d:] = x1_ref[:, half_d:] + ga2[:, half_d:] * y_hi


def _combine(dest0, dest1, x1, rf, ga2, ys, tiles_per_seq):
    T, D = x1.shape
    n_slab = _slabs(D // 2)
    tm = ROW_TILE
    row = lambda i, d0, d1: (i, 0)
    grid_spec = pltpu.PrefetchScalarGridSpec(
        num_scalar_prefetch=2,
        grid=(T // tm,),
        in_specs=[pl.BlockSpec((tm, D), row),
                  pl.BlockSpec((tm, LANES), row),
                  pl.BlockSpec((1, 1, D), lambda i, d0, d1: (i // tiles_per_seq, 0, 0)),
                  pl.BlockSpec(memory_space=pl.ANY)],
        out_specs=pl.BlockSpec((tm, D), row),
        scratch_shapes=[pltpu.VMEM((2, TOPK_IN_GROUP, tm * n_slab, LANES), ys.dtype),
                        pltpu.SemaphoreType.DMA((2,))],
    )
    return pl.pallas_call(
        functools.partial(_combine_kernel, n_slab=n_slab),
        grid_spec=grid_spec,
        out_shape=jax.ShapeDtypeStruct((T, D), F32),
        compiler_params=_cparams("arbitrary"),
        name="combine",
    )(dest0, dest1, x1, rf, ga2, ys)


def _rope_tables(S):
    pos = jnp.arange(S, dtype=F32)
    inv_freq = ROPE_THETA ** (-jnp.arange(0, ROT_DIM, 2, dtype=F32) / ROT_DIM)
    ang = pos[:, None] * inv_freq[None, :]
    cos, sin = jnp.cos(ang), jnp.sin(ang)
    half = ROT_DIM // 2
    ones = jnp.ones((S, HEAD_DIM - ROT_DIM), F32)
    cos_h = jnp.concatenate([cos, cos, ones], axis=1)
    sin_h = jnp.concatenate([-sin, sin, 0.0 * ones], axis=1)
    return jnp.tile(cos_h, (1, LANES // HEAD_DIM)), jnp.tile(sin_h, (1, LANES // HEAD_DIM))


def kernel(x, c, w_ada, b_ada, g_norm1, g_norm2, w_in, g_q, g_k, conv_w, conv_b,
           w_pa, w_pb, w_o, w_rg, b_rg, w_re, b_re, w1, w3, w2):
    B, S, D = x.shape
    T = B * S
    assert S % ROW_TILE == 0 and S % QUERY_TILE == 0 and QUERY_TILE % (2 * MOBA_BLOCK) == 0
    assert S // MOBA_BLOCK <= LANES - HEAD_DIM
    tiles_per_seq = S // ROW_TILE
    l = 0

    mod = _ada(c, w_ada[l], b_ada[l])
    sh1, sc1, ga1, sh2, sc2, ga2 = [m.reshape(B, 1, D) for m in jnp.split(mod, N_MOD, axis=-1)]

    x2 = x.reshape(T, D)
    z2 = _inproj(x2, g_norm1[l].reshape(1, D), sc1, sh1, w_in[l].astype(BF16), tiles_per_seq)

    cosf, sinf = _rope_tables(S)
    rep = LANES // HEAD_DIM
    ya = _attention(z2.reshape(B, S, -1), cosf, sinf,
                    jnp.tile(g_q[l], rep).reshape(1, LANES), jnp.tile(g_k[l], rep).reshape(1, LANES))

    wr = jnp.zeros((D, LANES), F32).at[:, :N_GROUPS].set(w_rg[l]).at[:, N_GROUPS:N_GROUPS + N_EXPERTS].set(w_re[l])
    br = jnp.zeros((1, LANES), F32).at[0, :N_GROUPS].set(b_rg[l]).at[0, N_GROUPS:N_GROUPS + N_EXPERTS].set(b_re[l])
    wr_hi = wr.astype(BF16)
    wr2 = jnp.concatenate([wr_hi, (wr - wr_hi.astype(F32)).astype(BF16)], axis=1)
    tri = (lax.broadcasted_iota(jnp.int32, (POST_CHUNK, POST_CHUNK), 1)
           < lax.broadcasted_iota(jnp.int32, (POST_CHUNK, POST_CHUNK), 0)).astype(BF16)
    x1, h2, ri, rf, cnt = _post(x2, ya.reshape(T, ATTN_WIDTH), z2, ga1, sc2, sh2,
                                conv_w[l], conv_b[l].reshape(1, CONV_WIDTH),
                                w_pa[l].astype(BF16), w_pb[l].astype(BF16), w_o[l].astype(BF16),
                                g_norm2[l].reshape(1, D), wr2, br, tri, tiles_per_seq)

    rb = EXPERT_ROWS
    counts = cnt[0, :N_EXPERTS].astype(jnp.int32)
    padded = (counts + rb - 1) // rb * rb
    pad_end = jnp.cumsum(padded)
    pad_start = pad_end - padded
    dest = _slots(pad_start.astype(jnp.int32), ri, _slabs(D // 2))
    dest0, dest1 = dest[0], dest[1]
    n_blocks = -(-T * TOPK_IN_GROUP // rb) + N_EXPERTS
    n_pad = n_blocks * rb
    n_valid = (pad_end[-1] // rb).astype(jnp.int32)
    blk_start = jnp.minimum(jnp.arange(n_blocks, dtype=jnp.int32), n_valid - 1) * rb
    blk_expert = jnp.sum(pad_end[None, :] <= blk_start[:, None], axis=-1).astype(jnp.int32)
    unused = n_valid + jnp.arange(N_EXPERTS, dtype=jnp.int32)
    tail_blk = jnp.concatenate([jnp.where(padded > 0, pad_end // rb - 1, -1),
                                jnp.where(unused < n_blocks, unused, -1)]).astype(jnp.int32)

    xs = _dispatch(dest0, dest1, tail_blk, h2, n_pad, _slabs(D // 2))
    ys = _experts(blk_expert, n_valid.reshape(1), xs, w1[l].astype(BF16), w3[l].astype(BF16), w2[l].astype(BF16))
    out = _combine(dest0, dest1, x1, rf, ga2, ys, tiles_per_seq)
    return out.reshape(B, S, D)
```

```python
import functools

import jax
import jax.numpy as jnp
from jax import lax
from jax.experimental import pallas as pl
from jax.experimental.pallas import tpu as pltpu

F32 = jnp.float32
BF16 = jnp.bfloat16
HIGHEST = lax.Precision.HIGHEST

N_HEADS = 8
HEAD_DIM = 64
ATTN_WIDTH = N_HEADS * HEAD_DIM
CONV_WIDTH = 512
CONV_K = 3
MOBA_BLOCK = 256
MOBA_TOPK = 3
ROPE_THETA = 500000.0
ROT_DIM = HEAD_DIM // 4
N_GROUPS = 4
EXPERTS_PER_GROUP = 8
N_EXPERTS = N_GROUPS * EXPERTS_PER_GROUP
TOPK_IN_GROUP = 2
N_MOD = 6
EPS = 1e-6

LANES = 128
NEG = -1e30
ROW_TILE = 512
POST_CHUNK = 256
QUERY_TILE = 512
DMA_UNROLL = 8
EXPERT_ROWS = 512
VMEM_LIMIT = 56 * 1024 * 1024
ATTN_VMEM_LIMIT = 60 * 1024 * 1024


def _cparams(*sem):
    return pltpu.CompilerParams(dimension_semantics=sem, vmem_limit_bytes=VMEM_LIMIT)


def _ada_kernel(c_ref, w_ref, b_ref, o_ref):
    c = c_ref[...]
    a = c * jax.nn.sigmoid(c)
    o_ref[...] = jnp.dot(a, w_ref[...], preferred_element_type=F32, precision=HIGHEST) + b_ref[...]


def _ada(c, w_ada, b_ada):
    B, D = c.shape
    N = w_ada.shape[1]
    tn = 1536
    return pl.pallas_call(
        _ada_kernel,
        grid=(N // tn,),
        in_specs=[pl.BlockSpec((B, D), lambda j: (0, 0)),
                  pl.BlockSpec((D, tn), lambda j: (0, j)),
                  pl.BlockSpec((1, tn), lambda j: (0, j))],
        out_specs=pl.BlockSpec((B, tn), lambda j: (0, j)),
        out_shape=jax.ShapeDtypeStruct((B, N), F32),
        compiler_params=_cparams("arbitrary"),
        name="ada",
    )(c, w_ada, b_ada.reshape(1, N))


def _inproj_kernel(x_ref, g_ref, sc_ref, sh_ref, w_ref, z_ref, *, n_chunk):
    x = x_ref[...]
    ms = jnp.mean(x * x, axis=-1, keepdims=True)
    y = x * lax.rsqrt(ms + EPS) * g_ref[...]
    h = (y * (1.0 + sc_ref[0]) + sh_ref[0]).astype(BF16)
    for n in range(0, z_ref.shape[1], n_chunk):
        z_ref[:, n:n + n_chunk] = jnp.dot(h, w_ref[:, n:n + n_chunk],
                                          preferred_element_type=F32).astype(BF16)


def _inproj(x2, g1, sc1, sh1, w_in_bf, tiles_per_seq):
    T, D = x2.shape
    N = w_in_bf.shape[1]
    tm = ROW_TILE
    bmap = lambda i: (i // tiles_per_seq, 0, 0)
    return pl.pallas_call(
        functools.partial(_inproj_kernel, n_chunk=512),
        grid=(T // tm,),
        in_specs=[pl.BlockSpec((tm, D), lambda i: (i, 0)),
                  pl.BlockSpec((1, D), lambda i: (0, 0)),
                  pl.BlockSpec((1, 1, D), bmap),
                  pl.BlockSpec((1, 1, D), bmap),
                  pl.BlockSpec((D, N), lambda i: (0, 0))],
        out_specs=pl.BlockSpec((tm, N), lambda i: (i, 0)),
        out_shape=jax.ShapeDtypeStruct((T, N), BF16),
        compiler_params=_cparams("arbitrary"),
        name="inproj",
    )(x2, g1, sc1, sh1, w_in_bf)


def _fold_rows(x, op):
    parts = [x[r:r + 8] for r in range(0, x.shape[0], 8)]
    while len(parts) > 1:
        parts = [op(parts[i], parts[i + 1]) for i in range(0, len(parts) - 1, 2)] + (
            [parts[-1]] if len(parts) % 2 else [])
    return parts[0]


def _attn_kernel(q_ref, k_ref, v_ref, cos_ref, sin_ref, gq_ref, gk_ref, o_ref,
                 kaug_ref, vt_ref, kmp_ref, kst_ref, s_ref, mcol_ref, qa_ref):
    S = q_ref.shape[1]
    blk = MOBA_BLOCK
    qt = QUERY_TILE
    sub = qt // blk
    nb = S // blk
    nq = S // qt
    nbp = kmp_ref.shape[0]
    hd = HEAD_DIM

    def norm_rope(xb, g, r0):
        rows = xb.shape[0]
        lane = lax.broadcasted_iota(jnp.int32, (rows, LANES), 1)
        head0 = lane < HEAD_DIM
        x = xb.astype(F32)
        sq = x * x
        s0 = jnp.sum(jnp.where(head0, sq, 0.0), axis=-1, keepdims=True)
        s1 = jnp.sum(jnp.where(head0, 0.0, sq), axis=-1, keepdims=True)
        inv = jnp.where(head0, lax.rsqrt(s0 * (1.0 / HEAD_DIM) + EPS), lax.rsqrt(s1 * (1.0 / HEAD_DIM) + EPS))
        y = x * inv * g
        half = ROT_DIM // 2
        rot_lo = (lane & (HEAD_DIM - 1)) < half
        rot = jnp.where(rot_lo, pltpu.roll(y, LANES - half, 1), pltpu.roll(y, half, 1))
        return y * cos_ref[pl.ds(r0, rows), :] + rot * sin_ref[pl.ds(r0, rows), :]

    kmp_ref[...] = jnp.zeros_like(kmp_ref)
    ones_row = jnp.where(lax.broadcasted_iota(jnp.int32, (16, blk), 0) == 0, 1.0, 0.0).astype(BF16)
    lane_k = lax.broadcasted_iota(jnp.int32, (blk, LANES), 1)
    head0_k = lane_k < HEAD_DIM

    def kbody(j, carry):
        r0 = pl.multiple_of(j * blk, blk)
        kr = norm_rope(k_ref[0, pl.ds(r0, blk), :], gk_ref[...], r0)
        kmp_ref[pl.ds(j, 1), :] = jnp.sum(kr, axis=0, keepdims=True) * (1.0 / blk)
        kaug_ref[0, j] = jnp.where(head0_k, kr, jnp.where(lane_k - hd == j, 1.0, 0.0)).astype(BF16)
        kaug_ref[1, j] = jnp.where(head0_k, jnp.where(lane_k == j, 1.0, 0.0), kr).astype(BF16)
        vT = v_ref[0, pl.ds(r0, blk), :].astype(F32).T
        for h in range(2):
            vt_ref[h, j, 0:HEAD_DIM, :] = vT[h * HEAD_DIM:(h + 1) * HEAD_DIM].astype(BF16)
            vt_ref[h, j, HEAD_DIM:HEAD_DIM + 16, :] = ones_row
        return carry

    lax.fori_loop(0, nb, kbody, 0)

    kmp = kmp_ref[...]
    lane_m = lax.broadcasted_iota(jnp.int32, (nbp, LANES), 1)
    k_hi = kmp.astype(BF16)
    k_lo = (kmp - k_hi.astype(F32)).astype(BF16)
    zero = jnp.zeros((nbp, LANES), BF16)
    parts = []
    for h in range(2):
        mine = (lane_m < hd) if h == 0 else (lane_m >= hd)
        parts += [jnp.where(mine, k_hi, zero), jnp.where(mine, k_lo, zero)]
    parts += [parts[0], parts[2]]
    for n, part in enumerate(parts):
        kst_ref[n * nbp:(n + 1) * nbp, :] = part

    key_i = lax.broadcasted_iota(jnp.int32, (blk, qt), 0)
    qry_i = lax.broadcasted_iota(jnp.int32, (blk, qt), 1)
    causal = [(qry_i < u * blk) | (qry_i >= (u + 1) * blk) | (key_i <= qry_i - u * blk) for u in range(sub)]
    rowf = lax.broadcasted_iota(jnp.int32, (nbp, qt), 0).astype(F32)
    subf = (lax.broadcasted_iota(jnp.int32, (nbp, qt), 1) // blk).astype(F32)
    q_scale = (hd ** -0.5) * 1.4426950408889634

    def query_operands(t):
        r0 = pl.multiple_of(t * qt, qt)
        qT = norm_rope(q_ref[0, pl.ds(r0, qt), :], gq_ref[...], r0).T
        cur = lax.convert_element_type(t * sub, F32) + subf
        q_hi = qT.astype(BF16)
        q_lo = (qT - q_hi.astype(F32)).astype(BF16)
        g1 = jnp.dot(kst_ref[0:4 * nbp, :], q_hi, preferred_element_type=F32)
        g2 = jnp.dot(kst_ref[4 * nbp:6 * nbp, :], q_lo, preferred_element_type=F32)
        qa = []
        for h in range(2):
            gate = g1[2 * h * nbp:(2 * h + 1) * nbp] + g1[(2 * h + 1) * nbp:(2 * h + 2) * nbp] + g2[h * nbp:(h + 1) * nbp]
            g = jnp.where(rowf < cur, gate, -jnp.inf)
            keep = rowf == cur
            for r in range(MOBA_TOPK):
                m = jnp.max(g, axis=0, keepdims=True)
                idx = jnp.min(jnp.where(g == m, rowf, 1e9), axis=0, keepdims=True)
                pick = (rowf == idx) & (cur > r)
                keep = keep | pick
                g = jnp.where(pick, -jnp.inf, g)
            bias = jnp.where(keep, 0.0, NEG)
            qs = qT[h * hd:(h + 1) * hd] * q_scale
            pad = jnp.zeros((LANES - hd - nbp, qt), F32)
            pieces = [qs, bias, pad] if h == 0 else [bias, pad, qs]
            qa.append(jnp.concatenate(pieces, axis=0).astype(BF16))
        return qa

    def pass1_tile(par, h, j, qa_h, mask):
        sT = jnp.dot(kaug_ref[h, j], qa_h, preferred_element_type=F32)
        if mask is not None:
            sT = jnp.where(mask, sT, NEG)
        s_ref[par, h, j] = sT
        return _fold_rows(sT, jnp.maximum)

    def pass2_tile(par, h, j):
        pT = jnp.exp2(s_ref[par, h, j] - mcol_ref[h, 0:1, :]).astype(BF16)
        return jnp.dot(vt_ref[h, j], pT, preferred_element_type=F32)

    def pass1_own(t, par, qa):
        mx = []
        for h in range(2):
            f = [pass1_tile(par, h, t * sub + u, qa[h], causal[u]) for u in range(sub)]
            mx.append(functools.reduce(jnp.maximum, f))
        return mx

    def pass2_own(t, par):
        return [sum(pass2_tile(par, h, t * sub + u) for u in range(sub)) for h in range(2)]

    def pass1_pair(p, par, qa, mx):
        return [functools.reduce(jnp.maximum, [mx[h]] + [pass1_tile(par, h, 2 * p + u, qa[h], None) for u in range(2)])
                for h in range(2)]

    def pass2_pair(p, par, acc):
        return [acc[h] + sum(pass2_tile(par, h, 2 * p + u) for u in range(2)) for h in range(2)]

    def finish_pass1(mx):
        for h in range(2):
            mcol_ref[h] = jnp.broadcast_to(jnp.max(mx[h], axis=0, keepdims=True), mcol_ref.shape[1:])

    def finish_pass2(t, acc):
        outT = jnp.concatenate([acc[h][0:hd] / acc[h][hd:hd + 1] for h in range(2)], axis=0)
        o_ref[0, pl.ds(pl.multiple_of(t * qt, qt), qt), :] = outT.T.astype(BF16)

    def stage(t, par):

        def prepare_next():
            nxt = query_operands(jnp.minimum(t + 1, nq - 1))
            for h in range(2):
                qa_ref[1 - par, h] = nxt[h]

        @pl.when(t == 0)
        def _():
            finish_pass1(pass1_own(t, par, query_operands(t)))
            prepare_next()

        @pl.when((t > 0) & (t < nq))
        def _():
            qa = [qa_ref[par, h] for h in range(2)]
            mx = pass1_own(t, par, qa)
            acc = pass2_own(t - 1, 1 - par)
            prepare_next()
            n_prev = (t - 1) * sub // 2

            def both(p, c):
                mx, acc = c
                return tuple(pass1_pair(p, par, qa, mx)), tuple(pass2_pair(p, 1 - par, acc))

            mx, acc = lax.fori_loop(0, n_prev, both, (tuple(mx), tuple(acc)))
            mx = lax.fori_loop(n_prev, t * sub // 2, lambda p, m: tuple(pass1_pair(p, par, qa, m)), tuple(mx))
            finish_pass2(t - 1, acc)
            finish_pass1(mx)

        @pl.when(t == nq)
        def _():
            acc = pass2_own(t - 1, 1 - par)
            acc = lax.fori_loop(0, (t - 1) * sub // 2, lambda p, a: tuple(pass2_pair(p, 1 - par, a)), tuple(acc))
            finish_pass2(t - 1, acc)

    def stage_pair(tt, carry):
        stage(2 * tt, 0)
        stage(2 * tt + 1, 1)
        return carry

    lax.fori_loop(0, (nq + 2) // 2, stage_pair, 0)


def _attention(z3, cosf, sinf, gq2, gk2):
    B, S, _ = z3.shape
    n_pair = N_HEADS // 2
    kq = ATTN_WIDTH // LANES
    nb = S // MOBA_BLOCK
    nbp = -(-nb // 16) * 16
    return pl.pallas_call(
        _attn_kernel,
        grid=(B, n_pair),
        in_specs=[pl.BlockSpec((1, S, LANES), lambda b, p: (b, 0, p)),
                  pl.BlockSpec((1, S, LANES), lambda b, p: (b, 0, kq + p)),
                  pl.BlockSpec((1, S, LANES), lambda b, p: (b, 0, 2 * kq + p)),
                  pl.BlockSpec((S, LANES), lambda b, p: (0, 0), pipeline_mode=pl.Buffered(1)),
                  pl.BlockSpec((S, LANES), lambda b, p: (0, 0), pipeline_mode=pl.Buffered(1)),
                  pl.BlockSpec((1, LANES), lambda b, p: (0, 0)),
                  pl.BlockSpec((1, LANES), lambda b, p: (0, 0))],
        out_specs=pl.BlockSpec((1, S, LANES), lambda b, p: (b, 0, p)),
        out_shape=jax.ShapeDtypeStruct((B, S, ATTN_WIDTH), BF16),
        scratch_shapes=[pltpu.VMEM((2, nb, MOBA_BLOCK, LANES), BF16),
                        pltpu.VMEM((2, nb, HEAD_DIM + 16, MOBA_BLOCK), BF16),
                        pltpu.VMEM((nbp, LANES), F32),
                        pltpu.VMEM((6 * nbp, LANES), BF16),
                        pltpu.VMEM((2, 2, nb, MOBA_BLOCK, QUERY_TILE), F32),
                        pltpu.VMEM((2, 8, QUERY_TILE), F32),
                        pltpu.VMEM((2, 2, LANES, QUERY_TILE), BF16)],
        compiler_params=pltpu.CompilerParams(dimension_semantics=("arbitrary", "arbitrary"),
                                             vmem_limit_bytes=ATTN_VMEM_LIMIT),
        name="attn",
    )(z3, z3, z3, cosf, sinf, gq2, gk2)


def _slabs(width):
    return width // LANES


def _load_slabs(ref, row0, rows, n_slab, lead=()):
    return jnp.concatenate([ref[lead + (pl.ds(row0 * n_slab + s, rows, stride=n_slab), slice(None))]
                            for s in range(n_slab)], axis=1)


def _store_slabs(ref, row0, val):
    rows, width = val.shape
    n_slab = _slabs(width)
    for s in range(n_slab):
        ref[pl.ds(row0 * n_slab + s, rows, stride=n_slab), :] = val[:, s * LANES:(s + 1) * LANES]


def _post_kernel(x_ref, ya_ref, xb_ref, bg_ref, cg_ref, gta_ref, gtb_ref, ga1_ref, sc2_ref, sh2_ref,
                 cw_ref, cb_ref, wpa_ref, wpb_ref, wo_ref, g2_ref, wr_ref, br_ref, tri_ref,
                 x1_ref, h2_ref, ri_ref, rf_ref, cnt_ref, ubuf_ref, run_ref, *, tiles_per_seq):
    i = pl.program_id(0)
    tm = x_ref.shape[0]
    rc = POST_CHUNK
    halo = 8

    @pl.when(i == 0)
    def _():
        run_ref[...] = jnp.zeros_like(run_ref)

    @pl.when(i % tiles_per_seq == 0)
    def _():
        ubuf_ref[0:halo, :] = jnp.zeros((halo, CONV_WIDTH), F32)

    ubuf_ref[halo:halo + tm, :] = cg_ref[...].astype(F32) * xb_ref[...].astype(F32)
    cw = cw_ref[...]
    lanef = lax.broadcasted_iota(jnp.int32, (rc, LANES), 1).astype(F32)
    half_d = x_ref.shape[1] // 2
    run = run_ref[0:1, :]

    for c in range(tm // rc):
        rows = pl.ds(c * rc, rc)
        conv = (cw[0:1, :] * ubuf_ref[pl.ds(halo - 2 + c * rc, rc), :]
                + cw[1:2, :] * ubuf_ref[pl.ds(halo - 1 + c * rc, rc), :]
                + cw[2:3, :] * ubuf_ref[pl.ds(halo + c * rc, rc), :])
        y_b = (bg_ref[rows, :].astype(F32) * (conv + cb_ref[...])).astype(BF16)
        pa = jnp.dot(ya_ref[rows, :], wpa_ref[...], preferred_element_type=F32)
        pb = jnp.dot(y_b, wpb_ref[...], preferred_element_type=F32)
        merged = (jax.nn.sigmoid(gta_ref[rows, :].astype(F32)) * pa
                  + jax.nn.sigmoid(gtb_ref[rows, :].astype(F32)) * pb).astype(BF16)
        x1 = x_ref[rows, :] + ga1_ref[0] * jnp.dot(merged, wo_ref[...], preferred_element_type=F32)
        x1_ref[rows, :] = x1

        ms = jnp.mean(x1 * x1, axis=-1, keepdims=True)
        h2 = x1 * lax.rsqrt(ms + EPS) * g2_ref[...]
        h2 = h2 * (1.0 + sc2_ref[0]) + sh2_ref[0]
        h_hi = h2.astype(BF16)
        h_hi32 = h_hi.astype(F32)
        bits = pltpu.bitcast(h_hi32, jnp.uint32)
        _store_slabs(h2_ref, c * rc, (bits[:, :half_d] >> 16) | bits[:, half_d:])

        h_lo = (h2 - h_hi32).astype(BF16)
        r = jnp.dot(h_hi, wr_ref[...], preferred_element_type=F32)
        logit = (r[:, :LANES] + r[:, LANES:]
                 + jnp.dot(h_lo, wr_ref[:, :LANES], preferred_element_type=F32) + br_ref[...])
        gl = jnp.where(lanef < N_GROUPS, logit, -jnp.inf)
        gmax = jnp.max(gl, axis=-1, keepdims=True)
        g_idx = jnp.min(jnp.where(gl == gmax, lanef, 1e9), axis=-1, keepdims=True)
        g_w = 1.0 / jnp.sum(jnp.exp(gl - gmax), axis=-1, keepdims=True)
        e_lo = N_GROUPS + EXPERTS_PER_GROUP * g_idx
        el = jnp.where((lanef >= e_lo) & (lanef < e_lo + EXPERTS_PER_GROUP), logit, -jnp.inf)
        v0 = jnp.max(el, axis=-1, keepdims=True)
        i0 = jnp.min(jnp.where(el == v0, lanef, 1e9), axis=-1, keepdims=True)
        el = jnp.where(lanef == i0, -jnp.inf, el)
        v1 = jnp.max(el, axis=-1, keepdims=True)
        i1 = jnp.min(jnp.where(el == v1, lanef, 1e9), axis=-1, keepdims=True)
        t = jnp.exp(v1 - v0)
        w0 = g_w / (1.0 + t)
        w1 = g_w * t / (1.0 + t)
        e0 = i0 - N_GROUPS
        e1 = i1 - N_GROUPS

        oh0 = lanef == e0
        oh1 = lanef == e1
        oh = jnp.where(oh0 | oh1, 1.0, 0.0)
        before = jnp.dot(tri_ref[...], oh.astype(BF16), preferred_element_type=F32) + run
        r0 = jnp.sum(jnp.where(oh0, before, 0.0), axis=-1, keepdims=True)
        r1 = jnp.sum(jnp.where(oh1, before, 0.0), axis=-1, keepdims=True)
        run = run + jnp.sum(oh, axis=0, keepdims=True)

        ri = jnp.where(lanef == 0, e0, jnp.where(lanef == 1, e1, jnp.where(lanef == 2, r0, jnp.where(lanef == 3, r1, 0.0))))
        ri_ref[:, rows] = ri.astype(jnp.int32).T[0:8]
        rf_ref[rows, :] = jnp.where(lanef == 0, w0, jnp.where(lanef == 1, w1, 0.0))

    ubuf_ref[0:halo, :] = ubuf_ref[tm:tm + halo, :]
    run_ref[...] = jnp.broadcast_to(run, run_ref.shape)
    cnt_ref[...] = jnp.broadcast_to(run, cnt_ref.shape)


def _post(x2, ya2, z2, ga1, sc2, sh2, conv_w, conv_b, wpa, wpb, wo, g2, wr, br, tri, tiles_per_seq):
    T, D = x2.shape
    tm = ROW_TILE
    cw = CONV_WIDTH
    xcol = 3 * ATTN_WIDTH // cw
    gcol = (3 * ATTN_WIDTH + 3 * cw) // D
    bmap = lambda i: (i // tiles_per_seq, 0, 0)
    const = lambda i: (0, 0)
    return pl.pallas_call(
        functools.partial(_post_kernel, tiles_per_seq=tiles_per_seq),
        grid=(T // tm,),
        in_specs=[pl.BlockSpec((tm, D), lambda i: (i, 0)),
                  pl.BlockSpec((tm, ATTN_WIDTH), lambda i: (i, 0)),
                  pl.BlockSpec((tm, cw), lambda i: (i, xcol)),
                  pl.BlockSpec((tm, cw), lambda i: (i, xcol + 1)),
                  pl.BlockSpec((tm, cw), lambda i: (i, xcol + 2)),
                  pl.BlockSpec((tm, D), lambda i: (i, gcol)),
                  pl.BlockSpec((tm, D), lambda i: (i, gcol + 1)),
                  pl.BlockSpec((1, 1, D), bmap),
                  pl.BlockSpec((1, 1, D), bmap),
                  pl.BlockSpec((1, 1, D), bmap),
                  pl.BlockSpec((CONV_K, cw), const),
                  pl.BlockSpec((1, cw), const),
                  pl.BlockSpec((ATTN_WIDTH, D), const),
                  pl.BlockSpec((cw, D), const),
                  pl.BlockSpec((D, D), const),
                  pl.BlockSpec((1, D), const),
                  pl.BlockSpec((D, 2 * LANES), const),
                  pl.BlockSpec((1, LANES), const),
                  pl.BlockSpec((POST_CHUNK, POST_CHUNK), const)],
        out_specs=[pl.BlockSpec((tm, D), lambda i: (i, 0)),
                   pl.BlockSpec((tm * _slabs(D // 2), LANES), lambda i: (i, 0)),
                   pl.BlockSpec((8, tm), lambda i: (0, i)),
                   pl.BlockSpec((tm, LANES), lambda i: (i, 0)),
                   pl.BlockSpec((8, LANES), const)],
        out_shape=[jax.ShapeDtypeStruct((T, D), F32),
                   jax.ShapeDtypeStruct((T * _slabs(D // 2), LANES), jnp.uint32),
                   jax.ShapeDtypeStruct((8, T), jnp.int32),
                   jax.ShapeDtypeStruct((T, LANES), F32),
                   jax.ShapeDtypeStruct((8, LANES), F32)],
        scratch_shapes=[pltpu.VMEM((tm + 16, cw), F32),
                        pltpu.VMEM((8, LANES), F32)],
        compiler_params=_cparams("arbitrary"),
        name="post",
    )(x2, ya2, z2, z2, z2, z2, z2, ga1, sc2, sh2, conv_w, conv_b, wpa, wpb, wo, g2, wr, br, tri)


def _slots_kernel(ps_ref, ri_ref, d_ref, *, n_slab):
    e = ri_ref[0:TOPK_IN_GROUP, :]
    start = jnp.zeros(e.shape, jnp.int32)
    for k in range(N_EXPERTS):
        start = jnp.where(e == k, ps_ref[k], start)
    d_ref[...] = (start + ri_ref[TOPK_IN_GROUP:2 * TOPK_IN_GROUP, :]) * n_slab


def _slots(pad_start, riT, n_slab):
    T = riT.shape[1]
    grid_spec = pltpu.PrefetchScalarGridSpec(
        num_scalar_prefetch=1,
        grid=(1,),
        in_specs=[pl.BlockSpec(riT.shape, lambda i, ps: (0, 0))],
        out_specs=pl.BlockSpec((TOPK_IN_GROUP, T), lambda i, ps: (0, 0)),
    )
    return pl.pallas_call(
        functools.partial(_slots_kernel, n_slab=n_slab),
        grid_spec=grid_spec,
        out_shape=jax.ShapeDtypeStruct((TOPK_IN_GROUP, T), jnp.int32),
        compiler_params=_cparams("arbitrary"),
        name="slots",
    )(pad_start, riT)


def _dispatch_kernel(d0_ref, d1_ref, tail_ref, h_ref, xs_ref, zero_ref, sem, zsem, *, n_slab):
    tm = h_ref.shape[0] // n_slab
    base = pl.program_id(0) * tm
    blk_rows = zero_ref.shape[0]

    @pl.when(pl.program_id(0) == 0)
    def _():
        zero_ref[...] = jnp.zeros_like(zero_ref)

        def tail_copy(e):
            start = pl.multiple_of(tail_ref[e] * blk_rows, blk_rows)
            return pltpu.make_async_copy(zero_ref, xs_ref.at[pl.ds(start, blk_rows), :], zsem)

        for e in range(tail_ref.shape[0]):
            @pl.when(tail_ref[e] >= 0)
            def _():
                tail_copy(e).start()
        for e in range(tail_ref.shape[0]):
            @pl.when(tail_ref[e] >= 0)
            def _():
                tail_copy(e).wait()

    def body(g, carry):
        r8 = pl.multiple_of(g * DMA_UNROLL, DMA_UNROLL)
        for u in range(DMA_UNROLL):
            src = h_ref.at[pl.ds((r8 + u) * n_slab, n_slab), :]
            pltpu.make_async_copy(src, xs_ref.at[pl.ds(d0_ref[base + r8 + u], n_slab), :], sem).start(priority=0)
            pltpu.make_async_copy(src, xs_ref.at[pl.ds(d1_ref[base + r8 + u], n_slab), :], sem).start(priority=1)
        return carry

    lax.fori_loop(0, tm // DMA_UNROLL, body, 0)
    for _ in range(TOPK_IN_GROUP):
        pltpu.make_async_copy(h_ref, xs_ref.at[pl.ds(0, tm * n_slab), :], sem).wait()


def _dispatch(dest0, dest1, tail_blk, h2p, n_pad, n_slab):
    tm = ROW_TILE
    T = h2p.shape[0] // n_slab
    grid_spec = pltpu.PrefetchScalarGridSpec(
        num_scalar_prefetch=3,
        grid=(T // tm,),
        in_specs=[pl.BlockSpec((tm * n_slab, LANES), lambda i, d0, d1, tb: (i, 0))],
        out_specs=pl.BlockSpec(memory_space=pl.ANY),
        scratch_shapes=[pltpu.VMEM((EXPERT_ROWS * n_slab, LANES), h2p.dtype),
                        pltpu.SemaphoreType.DMA(()),
                        pltpu.SemaphoreType.DMA(())],
    )
    return pl.pallas_call(
        functools.partial(_dispatch_kernel, n_slab=n_slab),
        grid_spec=grid_spec,
        out_shape=jax.ShapeDtypeStruct((n_pad * n_slab, LANES), h2p.dtype),
        compiler_params=_cparams("arbitrary"),
        name="dispatch",
    )(dest0, dest1, tail_blk, h2p)


def _expert_kernel(be_ref, nv_ref, xs_ref, w1_ref, w3_ref, w2_ref, ys_ref):
    @pl.when(pl.program_id(0) >= nv_ref[0])
    def _():
        ys_ref[...] = jnp.zeros_like(ys_ref)

    @pl.when(pl.program_id(0) < nv_ref[0])
    def _():
        half_d = w1_ref.shape[1] // 2
        xp = _load_slabs(xs_ref, 0, EXPERT_ROWS, _slabs(half_d))
        x_lo = pltpu.bitcast(xp << 16, F32).astype(BF16)
        x_hi = pltpu.bitcast(xp & jnp.uint32(0xFFFF0000), F32).astype(BF16)
        a = (jnp.dot(x_lo, w1_ref[0, :half_d], preferred_element_type=F32)
             + jnp.dot(x_hi, w1_ref[0, half_d:], preferred_element_type=F32))
        b = (jnp.dot(x_lo, w3_ref[0, :half_d], preferred_element_type=F32)
             + jnp.dot(x_hi, w3_ref[0, half_d:], preferred_element_type=F32))
        hid = (a * jax.nn.sigmoid(a) * b).astype(BF16)
        y = jnp.dot(hid, w2_ref[0], preferred_element_type=F32)
        bits = pltpu.bitcast(y.astype(BF16).astype(F32), jnp.uint32)
        _store_slabs(ys_ref, 0, (bits[:, :half_d] >> 16) | bits[:, half_d:])


def _experts(blk_expert, n_valid, xs, w1b, w3b, w2b):
    _, D, F = w1b.shape
    rb = EXPERT_ROWS
    blk_rows = rb * _slabs(D // 2)
    row_blk = lambda i, be, nv: (jnp.minimum(i, nv[0] - 1), 0)
    out_blk = lambda i, be, nv: (i, 0)
    wgt_blk = lambda i, be, nv: (be[i], 0, 0)
    grid_spec = pltpu.PrefetchScalarGridSpec(
        num_scalar_prefetch=2,
        grid=(xs.shape[0] // blk_rows,),
        in_specs=[pl.BlockSpec((blk_rows, LANES), row_blk),
                  pl.BlockSpec((1, D, F), wgt_blk),
                  pl.BlockSpec((1, D, F), wgt_blk),
                  pl.BlockSpec((1, F, D), wgt_blk)],
        out_specs=pl.BlockSpec((blk_rows, LANES), out_blk),
    )
    return pl.pallas_call(
        _expert_kernel,
        grid_spec=grid_spec,
        out_shape=jax.ShapeDtypeStruct(xs.shape, jnp.uint32),
        compiler_params=_cparams("arbitrary"),
        name="experts",
    )(blk_expert, n_valid, xs, w1b, w3b, w2b)


def _combine_kernel(d0_ref, d1_ref, x1_ref, rf_ref, ga2_ref, ys_ref, o_ref, buf_ref, sem, *, n_slab):
    i = pl.program_id(0)
    n = pl.num_programs(0)
    tm = x1_ref.shape[0]

    def gather(step, slot):
        base = step * tm

        def body(g, carry):
            r8 = pl.multiple_of(g * DMA_UNROLL, DMA_UNROLL)
            for u in range(DMA_UNROLL):
                dst = pl.ds((r8 + u) * n_slab, n_slab)
                pltpu.make_async_copy(ys_ref.at[pl.ds(d0_ref[base + r8 + u], n_slab), :],
                                      buf_ref.at[slot, 0, dst, :], sem.at[slot]).start(priority=0)
                pltpu.make_async_copy(ys_ref.at[pl.ds(d1_ref[base + r8 + u], n_slab), :],
                                      buf_ref.at[slot, 1, dst, :], sem.at[slot]).start(priority=1)
            return carry

        lax.fori_loop(0, tm // DMA_UNROLL, body, 0)

    @pl.when(i == 0)
    def _():
        gather(0, 0)

    @pl.when(i + 1 < n)
    def _():
        gather(i + 1, (i + 1) % 2)

    slot = i % 2
    for k in range(TOPK_IN_GROUP):
        pltpu.make_async_copy(ys_ref.at[pl.ds(0, tm * n_slab), :], buf_ref.at[slot, k], sem.at[slot]).wait()
    rf = rf_ref[...]
    w = [rf[:, k:k + 1] for k in range(TOPK_IN_GROUP)]
    yp = [_load_slabs(buf_ref, 0, tm, n_slab, lead=(slot, k)) for k in range(TOPK_IN_GROUP)]
    half_d = n_slab * LANES
    y_lo = sum(w[k] * pltpu.bitcast(yp[k] << 16, F32) for k in range(TOPK_IN_GROUP))
    y_hi = sum(w[k] * pltpu.bitcast(yp[k] & jnp.uint32(0xFFFF0000), F32) for k in range(TOPK_IN_GROUP))
    ga2 = ga2_ref[0]
    o_ref[:, :half_d] = x1_ref[:, :half_d] + ga2[:, :half_d] * y_lo
    o_ref[:, half_d:] = x1_ref[:, half_d:] + ga2[:, half_d:] * y_hi


def _combine(dest0, dest1, x1, rf, ga2, ys, tiles_per_seq):
    T, D = x1.shape
    n_slab = _slabs(D // 2)
    tm = ROW_TILE
    row = lambda i, d0, d1: (i, 0)
    grid_spec = pltpu.PrefetchScalarGridSpec(
        num_scalar_prefetch=2,
        grid=(T // tm,),
        in_specs=[pl.BlockSpec((tm, D), row),
                  pl.BlockSpec((tm, LANES), row),
                  pl.BlockSpec((1, 1, D), lambda i, d0, d1: (i // tiles_per_seq, 0, 0)),
                  pl.BlockSpec(memory_space=pl.ANY)],
        out_specs=pl.BlockSpec((tm, D), row),
        scratch_shapes=[pltpu.VMEM((2, TOPK_IN_GROUP, tm * n_slab, LANES), ys.dtype),
                        pltpu.SemaphoreType.DMA((2,))],
    )
    return pl.pallas_call(
        functools.partial(_combine_kernel, n_slab=n_slab),
        grid_spec=grid_spec,
        out_shape=jax.ShapeDtypeStruct((T, D), F32),
        compiler_params=_cparams("arbitrary"),
        name="combine",
    )(dest0, dest1, x1, rf, ga2, ys)


def _rope_tables(S):
    pos = jnp.arange(S, dtype=F32)
    inv_freq = ROPE_THETA ** (-jnp.arange(0, ROT_DIM, 2, dtype=F32) / ROT_DIM)
    ang = pos[:, None] * inv_freq[None, :]
    cos, sin = jnp.cos(ang), jnp.sin(ang)
    half = ROT_DIM // 2
    ones = jnp.ones((S, HEAD_DIM - ROT_DIM), F32)
    cos_h = jnp.concatenate([cos, cos, ones], axis=1)
    sin_h = jnp.concatenate([-sin, sin, 0.0 * ones], axis=1)
    return jnp.tile(cos_h, (1, LANES // HEAD_DIM)), jnp.tile(sin_h, (1, LANES // HEAD_DIM))


def kernel(x, c, w_ada, b_ada, g_norm1, g_norm2, w_in, g_q, g_k, conv_w, conv_b,
           w_pa, w_pb, w_o, w_rg, b_rg, w_re, b_re, w1, w3, w2):
    B, S, D = x.shape
    T = B * S
    assert S % ROW_TILE == 0 and S % QUERY_TILE == 0 and QUERY_TILE % (2 * MOBA_BLOCK) == 0
    assert S // MOBA_BLOCK <= LANES - HEAD_DIM
    tiles_per_seq = S // ROW_TILE
    l = 0

    mod = _ada(c, w_ada[l], b_ada[l])
    sh1, sc1, ga1, sh2, sc2, ga2 = [m.reshape(B, 1, D) for m in jnp.split(mod, N_MOD, axis=-1)]

    x2 = x.reshape(T, D)
    z2 = _inproj(x2, g_norm1[l].reshape(1, D), sc1, sh1, w_in[l].astype(BF16), tiles_per_seq)

    cosf, sinf = _rope_tables(S)
    rep = LANES // HEAD_DIM
    ya = _attention(z2.reshape(B, S, -1), cosf, sinf,
                    jnp.tile(g_q[l], rep).reshape(1, LANES), jnp.tile(g_k[l], rep).reshape(1, LANES))

    wr = jnp.zeros((D, LANES), F32).at[:, :N_GROUPS].set(w_rg[l]).at[:, N_GROUPS:N_GROUPS + N_EXPERTS].set(w_re[l])
    br = jnp.zeros((1, LANES), F32).at[0, :N_GROUPS].set(b_rg[l]).at[0, N_GROUPS:N_GROUPS + N_EXPERTS].set(b_re[l])
    wr_hi = wr.astype(BF16)
    wr2 = jnp.concatenate([wr_hi, (wr - wr_hi.astype(F32)).astype(BF16)], axis=1)
    tri = (lax.broadcasted_iota(jnp.int32, (POST_CHUNK, POST_CHUNK), 1)
           < lax.broadcasted_iota(jnp.int32, (POST_CHUNK, POST_CHUNK), 0)).astype(BF16)
    x1, h2, ri, rf, cnt = _post(x2, ya.reshape(T, ATTN_WIDTH), z2, ga1, sc2, sh2,
                                conv_w[l], conv_b[l].reshape(1, CONV_WIDTH),
                                w_pa[l].astype(BF16), w_pb[l].astype(BF16), w_o[l].astype(BF16),
                                g_norm2[l].reshape(1, D), wr2, br, tri, tiles_per_seq)

    rb = EXPERT_ROWS
    counts = cnt[0, :N_EXPERTS].astype(jnp.int32)
    padded = (counts + rb - 1) // rb * rb
    pad_end = jnp.cumsum(padded)
    pad_start = pad_end - padded
    dest = _slots(pad_start.astype(jnp.int32), ri, _slabs(D // 2))
    dest0, dest1 = dest[0], dest[1]
    n_blocks = -(-T * TOPK_IN_GROUP // rb) + N_EXPERTS
    n_pad = n_blocks * rb
    n_valid = (pad_end[-1] // rb).astype(jnp.int32)
    blk_start = jnp.minimum(jnp.arange(n_blocks, dtype=jnp.int32), n_valid - 1) * rb
    blk_expert = jnp.sum(pad_end[None, :] <= blk_start[:, None], axis=-1).astype(jnp.int32)
    unused = n_valid + jnp.arange(N_EXPERTS, dtype=jnp.int32)
    tail_blk = jnp.concatenate([jnp.where(padded > 0, pad_end // rb - 1, -1),
                                jnp.where(unused < n_blocks, unused, -1)]).astype(jnp.int32)

    xs = _dispatch(dest0, dest1, tail_blk, h2, n_pad, _slabs(D // 2))
    ys = _experts(blk_expert, n_valid.reshape(1), xs, w1[l].astype(BF16), w3[l].astype(BF16), w2[l].astype(BF16))
    out = _combine(dest0, dest1, x1, rf, ga2, ys, tiles_per_seq)
    return out.reshape(B, S, D)
```

```python
import functools

import jax
import jax.numpy as jnp
from jax import lax
from jax.experimental import pallas as pl
from jax.experimental.pallas import tpu as pltpu

F32 = jnp.float32
BF16 = jnp.bfloat16
HIGHEST = lax.Precision.HIGHEST

N_HEADS = 8
HEAD_DIM = 64
ATTN_WIDTH = N_HEADS * HEAD_DIM
CONV_WIDTH = 512
CONV_K = 3
MOBA_BLOCK = 256
MOBA_TOPK = 3
ROPE_THETA = 500000.0
ROT_DIM = HEAD_DIM // 4
N_GROUPS = 4
EXPERTS_PER_GROUP = 8
N_EXPERTS = N_GROUPS * EXPERTS_PER_GROUP
TOPK_IN_GROUP = 2
N_MOD = 6
EPS = 1e-6

LANES = 128
NEG = -1e30
ROW_TILE = 512
POST_CHUNK = 256
QUERY_TILE = 512
DMA_UNROLL = 8
EXPERT_ROWS = 512
VMEM_LIMIT = 56 * 1024 * 1024
ATTN_VMEM_LIMIT = 60 * 1024 * 1024


def _cparams(*sem):
    return pltpu.CompilerParams(dimension_semantics=sem, vmem_limit_bytes=VMEM_LIMIT)


def _ada_kernel(c_ref, w_ref, b_ref, o_ref):
    c = c_ref[...]
    a = c * jax.nn.sigmoid(c)
    o_ref[...] = jnp.dot(a, w_ref[...], preferred_element_type=F32, precision=HIGHEST) + b_ref[...]


def _ada(c, w_ada, b_ada):
    B, D = c.shape
    N = w_ada.shape[1]
    tn = 1536
    return pl.pallas_call(
        _ada_kernel,
        grid=(N // tn,),
        in_specs=[pl.BlockSpec((B, D), lambda j: (0, 0)),
                  pl.BlockSpec((D, tn), lambda j: (0, j)),
                  pl.BlockSpec((1, tn), lambda j: (0, j))],
        out_specs=pl.BlockSpec((B, tn), lambda j: (0, j)),
        out_shape=jax.ShapeDtypeStruct((B, N), F32),
        compiler_params=_cparams("arbitrary"),
        name="ada",
    )(c, w_ada, b_ada.reshape(1, N))


def _inproj_kernel(x_ref, g_ref, sc_ref, sh_ref, w_ref, z_ref, *, n_chunk):
    x = x_ref[...]
    ms = jnp.mean(x * x, axis=-1, keepdims=True)
    y = x * lax.rsqrt(ms + EPS) * g_ref[...]
    h = (y * (1.0 + sc_ref[0]) + sh_ref[0]).astype(BF16)
    for n in range(0, z_ref.shape[1], n_chunk):
        z_ref[:, n:n + n_chunk] = jnp.dot(h, w_ref[:, n:n + n_chunk],
                                          preferred_element_type=F32).astype(BF16)


def _inproj(x2, g1, sc1, sh1, w_in_bf, tiles_per_seq):
    T, D = x2.shape
    N = w_in_bf.shape[1]
    tm = ROW_TILE
    bmap = lambda i: (i // tiles_per_seq, 0, 0)
    return pl.pallas_call(
        functools.partial(_inproj_kernel, n_chunk=512),
        grid=(T // tm,),
        in_specs=[pl.BlockSpec((tm, D), lambda i: (i, 0)),
                  pl.BlockSpec((1, D), lambda i: (0, 0)),
                  pl.BlockSpec((1, 1, D), bmap),
                  pl.BlockSpec((1, 1, D), bmap),
                  pl.BlockSpec((D, N), lambda i: (0, 0))],
        out_specs=pl.BlockSpec((tm, N), lambda i: (i, 0)),
        out_shape=jax.ShapeDtypeStruct((T, N), BF16),
        compiler_params=_cparams("arbitrary"),
        name="inproj",
    )(x2, g1, sc1, sh1, w_in_bf)


def _fold_rows(x, op):
    parts = [x[r:r + 8] for r in range(0, x.shape[0], 8)]
    while len(parts) > 1:
        parts = [op(parts[i], parts[i + 1]) for i in range(0, len(parts) - 1, 2)] + (
            [parts[-1]] if len(parts) % 2 else [])
    return parts[0]


def _attn_kernel(q_ref, k_ref, v_ref, cos_ref, sin_ref, cost_ref, sint_ref, gq_ref, gk_ref, o_ref,
                 kaug_ref, vt_ref, kmp_ref, kst_ref, s_ref, mcol_ref, qa_ref):
    S = q_ref.shape[1]
    blk = MOBA_BLOCK
    qt = QUERY_TILE
    sub = qt // blk
    nb = S // blk
    nq = S // qt
    nbp = kst_ref.shape[0] // 6
    hd = HEAD_DIM

    half = ROT_DIM // 2
    lane_r = lax.broadcasted_iota(jnp.int32, (blk, LANES), 1)
    rot_lo = (lane_r & (hd - 1)) < half
    same_head = jnp.where((lax.broadcasted_iota(jnp.int32, (LANES, LANES), 0) < hd)
                          == (lax.broadcasted_iota(jnp.int32, (LANES, LANES), 1) < hd), 1.0, 0.0).astype(BF16)

    def norm_rope_keys(xb, r0):
        x = xb.astype(F32)
        sq = x * x
        sq_hi = sq.astype(BF16)
        sq_lo = (sq - sq_hi.astype(F32)).astype(BF16)
        ssq = (jnp.dot(sq_hi, same_head, preferred_element_type=F32)
               + jnp.dot(sq_lo, same_head, preferred_element_type=F32))
        y = x * lax.rsqrt(ssq * (1.0 / hd) + EPS) * gk_ref[...]
        rot = jnp.where(rot_lo, pltpu.roll(y, LANES - half, 1), pltpu.roll(y, half, 1))
        return y * cos_ref[pl.ds(r0, blk), :] + rot * sin_ref[pl.ds(r0, blk), :]

    def norm_rope_queries_t(xb, r0):
        xT = xb.astype(F32).T
        sq = xT * xT
        cos = cost_ref[:, pl.ds(r0, qt)]
        sin = sint_ref[:, pl.ds(r0, qt)]
        rows = []
        for h in range(2):
            lo, hi = h * hd, (h + 1) * hd
            ssq = jnp.sum(_fold_rows(sq[lo:hi], jnp.add), axis=0, keepdims=True)
            y = xT[lo:hi] * lax.rsqrt(ssq * (1.0 / hd) + EPS) * gq_ref[lo:hi, :]
            y1, y2 = y[0:half], y[half:2 * half]
            rows += [y1 * cos - y2 * sin, y2 * cos + y1 * sin, y[2 * half:]]
        return jnp.concatenate(rows, axis=0)

    kmp_ref[...] = jnp.zeros_like(kmp_ref)
    ones_row = jnp.where(lax.broadcasted_iota(jnp.int32, (16, blk), 0) == 0, 1.0, 0.0).astype(BF16)
    lane_k = lax.broadcasted_iota(jnp.int32, (blk, LANES), 1)
    head0_k = lane_k < HEAD_DIM
    lane_m = lax.broadcasted_iota(jnp.int32, (nbp, LANES), 1)

    def prepare_keys(t):
        for u in range(sub):
            j = t * sub + u
            r0 = pl.multiple_of(jnp.minimum(j, nb - 1) * blk, blk)
            kr = norm_rope_keys(k_ref[0, pl.ds(r0, blk), :], r0)
            kmp_ref[pl.ds(j, 1), :] = jnp.sum(kr, axis=0, keepdims=True) * (1.0 / blk)
            kaug_ref[0, j] = jnp.where(head0_k, kr, jnp.where(lane_k - hd == j, 1.0, 0.0)).astype(BF16)
            kaug_ref[1, j] = jnp.where(head0_k, jnp.where(lane_k == j, 1.0, 0.0), kr).astype(BF16)
            vT = v_ref[0, pl.ds(r0, blk), :].astype(F32).T
            for h in range(2):
                vt_ref[h, j, 0:HEAD_DIM, :] = vT[h * HEAD_DIM:(h + 1) * HEAD_DIM].astype(BF16)
                vt_ref[h, j, HEAD_DIM:HEAD_DIM + 16, :] = ones_row
        kmp = kmp_ref[0:nbp, :]
        k_hi = kmp.astype(BF16)
        k_lo = (kmp - k_hi.astype(F32)).astype(BF16)
        zero = jnp.zeros((nbp, LANES), BF16)
        parts = []
        for h in range(2):
            mine = (lane_m < hd) if h == 0 else (lane_m >= hd)
            parts += [jnp.where(mine, k_hi, zero), jnp.where(mine, k_lo, zero)]
        parts += [parts[0], parts[2]]
        for n, part in enumerate(parts):
            kst_ref[n * nbp:(n + 1) * nbp, :] = part

    prepare_keys(0)

    key_i = lax.broadcasted_iota(jnp.int32, (blk, qt), 0)
    qry_i = lax.broadcasted_iota(jnp.int32, (blk, qt), 1)
    causal = [(qry_i < u * blk) | (qry_i >= (u + 1) * blk) | (key_i <= qry_i - u * blk) for u in range(sub)]
    rowf = lax.broadcasted_iota(jnp.int32, (nbp, qt), 0).astype(F32)
    subf = (lax.broadcasted_iota(jnp.int32, (nbp, qt), 1) // blk).astype(F32)
    q_scale = (hd ** -0.5) * 1.4426950408889634

    def query_operands(t):
        r0 = pl.multiple_of(t * qt, qt)
        qT = norm_rope_queries_t(q_ref[0, pl.ds(r0, qt), :], r0)
        cur = lax.convert_element_type(t * sub, F32) + subf
        q_hi = qT.astype(BF16)
        q_lo = (qT - q_hi.astype(F32)).astype(BF16)
        g1 = jnp.dot(kst_ref[0:4 * nbp, :], q_hi, preferred_element_type=F32)
        g2 = jnp.dot(kst_ref[4 * nbp:6 * nbp, :], q_lo, preferred_element_type=F32)
        qa = []
        for h in range(2):
            gate = g1[2 * h * nbp:(2 * h + 1) * nbp] + g1[(2 * h + 1) * nbp:(2 * h + 2) * nbp] + g2[h * nbp:(h + 1) * nbp]
            g = jnp.where(rowf < cur, gate, -jnp.inf)
            keep = rowf == cur
            for r in range(MOBA_TOPK):
                m = jnp.max(g, axis=0, keepdims=True)
                idx = jnp.min(jnp.where(g == m, rowf, 1e9), axis=0, keepdims=True)
                pick = (rowf == idx) & (cur > r)
                keep = keep | pick
                g = jnp.where(pick, -jnp.inf, g)
            bias = jnp.where(keep, 0.0, NEG)
            qs = qT[h * hd:(h + 1) * hd] * q_scale
            pad = jnp.zeros((LANES - hd - nbp, qt), F32)
            pieces = [qs, bias, pad] if h == 0 else [bias, pad, qs]
            qa.append(jnp.concatenate(pieces, axis=0).astype(BF16))
        return qa

    def pass1_tile(par, h, j, qa_h, mask):
        sT = jnp.dot(kaug_ref[h, j], qa_h, preferred_element_type=F32)
        if mask is not None:
            sT = jnp.where(mask, sT, NEG)
        s_ref[par, h, j] = sT
        return _fold_rows(sT, jnp.maximum)

    def pass2_tile(par, h, j):
        pT = jnp.exp2(s_ref[par, h, j] - mcol_ref[h, 0:1, :]).astype(BF16)
        return jnp.dot(vt_ref[h, j], pT, preferred_element_type=F32)

    def pass1_own(t, par, qa):
        mx = []
        for h in range(2):
            f = [pass1_tile(par, h, t * sub + u, qa[h], causal[u]) for u in range(sub)]
            mx.append(functools.reduce(jnp.maximum, f))
        return mx

    def pass2_own(t, par):
        return [sum(pass2_tile(par, h, t * sub + u) for u in range(sub)) for h in range(2)]

    def pass1_pair(p, par, qa, mx):
        return [functools.reduce(jnp.maximum, [mx[h]] + [pass1_tile(par, h, 2 * p + u, qa[h], None) for u in range(2)])
                for h in range(2)]

    def pass2_pair(p, par, acc):
        return [acc[h] + sum(pass2_tile(par, h, 2 * p + u) for u in range(2)) for h in range(2)]

    def finish_pass1(mx):
        for h in range(2):
            mcol_ref[h] = jnp.broadcast_to(jnp.max(mx[h], axis=0, keepdims=True), mcol_ref.shape[1:])

    def finish_pass2(t, acc):
        outT = jnp.concatenate([acc[h][0:hd] / acc[h][hd:hd + 1] for h in range(2)], axis=0)
        o_ref[0, pl.ds(pl.multiple_of(t * qt, qt), qt), :] = outT.T.astype(BF16)

    def stage(t, par):

        def prepare_next():
            prepare_keys(t + 1)
            nxt = query_operands(jnp.minimum(t + 1, nq - 1))
            for h in range(2):
                qa_ref[1 - par, h] = nxt[h]

        @pl.when(t == 0)
        def _():
            finish_pass1(pass1_own(t, par, query_operands(t)))
            prepare_next()

        @pl.when((t > 0) & (t < nq))
        def _():
            qa = [qa_ref[par, h] for h in range(2)]
            mx = pass1_own(t, par, qa)
            acc = pass2_own(t - 1, 1 - par)
            prepare_next()
            n_prev = (t - 1) * sub // 2

            def both(p, c):
                mx, acc = c
                return tuple(pass1_pair(p, par, qa, mx)), tuple(pass2_pair(p, 1 - par, acc))

            mx, acc = lax.fori_loop(0, n_prev, both, (tuple(mx), tuple(acc)))
            mx = lax.fori_loop(n_prev, t * sub // 2, lambda p, m: tuple(pass1_pair(p, par, qa, m)), tuple(mx))
            finish_pass2(t - 1, acc)
            finish_pass1(mx)

        @pl.when(t == nq)
        def _():
            acc = pass2_own(t - 1, 1 - par)
            acc = lax.fori_loop(0, (t - 1) * sub // 2, lambda p, a: tuple(pass2_pair(p, 1 - par, a)), tuple(acc))
            finish_pass2(t - 1, acc)

    def stage_pair(tt, carry):
        stage(2 * tt, 0)
        stage(2 * tt + 1, 1)
        return carry

    lax.fori_loop(0, (nq + 2) // 2, stage_pair, 0)


def _attention(z3, cosf, sinf, cost, sint, gq_cols, gk2):
    B, S, _ = z3.shape
    n_pair = N_HEADS // 2
    kq = ATTN_WIDTH // LANES
    nb = S // MOBA_BLOCK
    nbp = -(-nb // 16) * 16
    sub = QUERY_TILE // MOBA_BLOCK
    assert sub <= 8
    return pl.pallas_call(
        _attn_kernel,
        grid=(B, n_pair),
        in_specs=[pl.BlockSpec((1, S, LANES), lambda b, p: (b, 0, p)),
                  pl.BlockSpec((1, S, LANES), lambda b, p: (b, 0, kq + p)),
                  pl.BlockSpec((1, S, LANES), lambda b, p: (b, 0, 2 * kq + p)),
                  pl.BlockSpec((S, LANES), lambda b, p: (0, 0), pipeline_mode=pl.Buffered(1)),
                  pl.BlockSpec((S, LANES), lambda b, p: (0, 0), pipeline_mode=pl.Buffered(1)),
                  pl.BlockSpec(cost.shape, lambda b, p: (0, 0)),
                  pl.BlockSpec(sint.shape, lambda b, p: (0, 0)),
                  pl.BlockSpec((LANES, QUERY_TILE), lambda b, p: (0, 0)),
                  pl.BlockSpec((1, LANES), lambda b, p: (0, 0))],
        out_specs=pl.BlockSpec((1, S, LANES), lambda b, p: (b, 0, p)),
        out_shape=jax.ShapeDtypeStruct((B, S, ATTN_WIDTH), BF16),
        scratch_shapes=[pltpu.VMEM((2, nb + sub, MOBA_BLOCK, LANES), BF16),
                        pltpu.VMEM((2, nb + sub, HEAD_DIM + 16, MOBA_BLOCK), BF16),
                        pltpu.VMEM((nbp + 8, LANES), F32),
                        pltpu.VMEM((6 * nbp, LANES), BF16),
                        pltpu.VMEM((2, 2, nb, MOBA_BLOCK, QUERY_TILE), F32),
                        pltpu.VMEM((2, 8, QUERY_TILE), F32),
                        pltpu.VMEM((2, 2, LANES, QUERY_TILE), BF16)],
        compiler_params=pltpu.CompilerParams(dimension_semantics=("arbitrary", "arbitrary"),
                                             vmem_limit_bytes=ATTN_VMEM_LIMIT),
        name="attn",
    )(z3, z3, z3, cosf, sinf, cost, sint, gq_cols, gk2)


def _slabs(width):
    return width // LANES


def _load_slabs(ref, row0, rows, n_slab, lead=()):
    return jnp.concatenate([ref[lead + (pl.ds(row0 * n_slab + s, rows, stride=n_slab), slice(None))]
                            for s in range(n_slab)], axis=1)


def _store_slabs(ref, row0, val):
    rows, width = val.shape
    n_slab = _slabs(width)
    for s in range(n_slab):
        ref[pl.ds(row0 * n_slab + s, rows, stride=n_slab), :] = val[:, s * LANES:(s + 1) * LANES]


def _post_kernel(x_ref, ya_ref, xb_ref, bg_ref, cg_ref, gta_ref, gtb_ref, ga1_ref, sc2_ref, sh2_ref,
                 cw_ref, cb_ref, wpa_ref, wpb_ref, wo_ref, g2_ref, wr_ref, br_ref, tri_ref,
                 x1_ref, h2_ref, ri_ref, rf_ref, cnt_ref, ubuf_ref, run_ref, *, tiles_per_seq):
    i = pl.program_id(0)
    tm = x_ref.shape[0]
    rc = POST_CHUNK
    halo = 8

    @pl.when(i == 0)
    def _():
        run_ref[...] = jnp.zeros_like(run_ref)

    @pl.when(i % tiles_per_seq == 0)
    def _():
        ubuf_ref[0:halo, :] = jnp.zeros((halo, CONV_WIDTH), F32)

    ubuf_ref[halo:halo + tm, :] = cg_ref[...].astype(F32) * xb_ref[...].astype(F32)
    cw = cw_ref[...]
    lanef = lax.broadcasted_iota(jnp.int32, (rc, LANES), 1).astype(F32)
    half_d = x_ref.shape[1] // 2
    run = run_ref[0:1, :]

    for c in range(tm // rc):
        rows = pl.ds(c * rc, rc)
        conv = (cw[0:1, :] * ubuf_ref[pl.ds(halo - 2 + c * rc, rc), :]
                + cw[1:2, :] * ubuf_ref[pl.ds(halo - 1 + c * rc, rc), :]
                + cw[2:3, :] * ubuf_ref[pl.ds(halo + c * rc, rc), :])
        y_b = (bg_ref[rows, :].astype(F32) * (conv + cb_ref[...])).astype(BF16)
        pa = jnp.dot(ya_ref[rows, :], wpa_ref[...], preferred_element_type=F32)
        pb = jnp.dot(y_b, wpb_ref[...], preferred_element_type=F32)
        merged = (jax.nn.sigmoid(gta_ref[rows, :].astype(F32)) * pa
                  + jax.nn.sigmoid(gtb_ref[rows, :].astype(F32)) * pb).astype(BF16)
        x1 = x_ref[rows, :] + ga1_ref[0] * jnp.dot(merged, wo_ref[...], preferred_element_type=F32)
        x1_ref[rows, :] = x1

        ms = jnp.mean(x1 * x1, axis=-1, keepdims=True)
        h2 = x1 * lax.rsqrt(ms + EPS) * g2_ref[...]
        h2 = h2 * (1.0 + sc2_ref[0]) + sh2_ref[0]
        h_hi = h2.astype(BF16)
        h_hi32 = h_hi.astype(F32)
        bits = pltpu.bitcast(h_hi32, jnp.uint32)
        _store_slabs(h2_ref, c * rc, (bits[:, :half_d] >> 16) | bits[:, half_d:])

        h_lo = (h2 - h_hi32).astype(BF16)
        r = jnp.dot(h_hi, wr_ref[...], preferred_element_type=F32)
        logit = (r[:, :LANES] + r[:, LANES:]
                 + jnp.dot(h_lo, wr_ref[:, :LANES], preferred_element_type=F32) + br_ref[...])
        gl = jnp.where(lanef < N_GROUPS, logit, -jnp.inf)
        gmax = jnp.max(gl, axis=-1, keepdims=True)
        g_idx = jnp.min(jnp.where(gl == gmax, lanef, 1e9), axis=-1, keepdims=True)
        g_w = 1.0 / jnp.sum(jnp.exp(gl - gmax), axis=-1, keepdims=True)
        e_lo = N_GROUPS + EXPERTS_PER_GROUP * g_idx
        el = jnp.where((lanef >= e_lo) & (lanef < e_lo + EXPERTS_PER_GROUP), logit, -jnp.inf)
        v0 = jnp.max(el, axis=-1, keepdims=True)
        i0 = jnp.min(jnp.where(el == v0, lanef, 1e9), axis=-1, keepdims=True)
        el = jnp.where(lanef == i0, -jnp.inf, el)
        v1 = jnp.max(el, axis=-1, keepdims=True)
        i1 = jnp.min(jnp.where(el == v1, lanef, 1e9), axis=-1, keepdims=True)
        t = jnp.exp(v1 - v0)
        w0 = g_w / (1.0 + t)
        w1 = g_w * t / (1.0 + t)
        e0 = i0 - N_GROUPS
        e1 = i1 - N_GROUPS

        oh0 = lanef == e0
        oh1 = lanef == e1
        oh = jnp.where(oh0 | oh1, 1.0, 0.0)
        before = jnp.dot(tri_ref[...], oh.astype(BF16), preferred_element_type=F32) + run
        r0 = jnp.sum(jnp.where(oh0, before, 0.0), axis=-1, keepdims=True)
        r1 = jnp.sum(jnp.where(oh1, before, 0.0), axis=-1, keepdims=True)
        run = run + jnp.sum(oh, axis=0, keepdims=True)

        ri = jnp.where(lanef == 0, e0, jnp.where(lanef == 1, e1, jnp.where(lanef == 2, r0, jnp.where(lanef == 3, r1, 0.0))))
        ri_ref[:, rows] = ri.astype(jnp.int32).T[0:8]
        rf_ref[rows, :] = jnp.where(lanef == 0, w0, jnp.where(lanef == 1, w1, 0.0))

    ubuf_ref[0:halo, :] = ubuf_ref[tm:tm + halo, :]
    run_ref[...] = jnp.broadcast_to(run, run_ref.shape)
    cnt_ref[...] = jnp.broadcast_to(run, cnt_ref.shape)


def _post(x2, ya2, z2, ga1, sc2, sh2, conv_w, conv_b, wpa, wpb, wo, g2, wr, br, tri, tiles_per_seq):
    T, D = x2.shape
    tm = ROW_TILE
    cw = CONV_WIDTH
    xcol = 3 * ATTN_WIDTH // cw
    gcol = (3 * ATTN_WIDTH + 3 * cw) // D
    bmap = lambda i: (i // tiles_per_seq, 0, 0)
    const = lambda i: (0, 0)
    return pl.pallas_call(
        functools.partial(_post_kernel, tiles_per_seq=tiles_per_seq),
        grid=(T // tm,),
        in_specs=[pl.BlockSpec((tm, D), lambda i: (i, 0)),
                  pl.BlockSpec((tm, ATTN_WIDTH), lambda i: (i, 0)),
                  pl.BlockSpec((tm, cw), lambda i: (i, xcol)),
                  pl.BlockSpec((tm, cw), lambda i: (i, xcol + 1)),
                  pl.BlockSpec((tm, cw), lambda i: (i, xcol + 2)),
                  pl.BlockSpec((tm, D), lambda i: (i, gcol)),
                  pl.BlockSpec((tm, D), lambda i: (i, gcol + 1)),
                  pl.BlockSpec((1, 1, D), bmap),
                  pl.BlockSpec((1, 1, D), bmap),
                  pl.BlockSpec((1, 1, D), bmap),
                  pl.BlockSpec((CONV_K, cw), const),
                  pl.BlockSpec((1, cw), const),
                  pl.BlockSpec((ATTN_WIDTH, D), const),
                  pl.BlockSpec((cw, D), const),
                  pl.BlockSpec((D, D), const),
                  pl.BlockSpec((1, D), const),
                  pl.BlockSpec((D, 2 * LANES), const),
                  pl.BlockSpec((1, LANES), const),
                  pl.BlockSpec((POST_CHUNK, POST_CHUNK), const)],
        out_specs=[pl.BlockSpec((tm, D), lambda i: (i, 0)),
                   pl.BlockSpec((tm * _slabs(D // 2), LANES), lambda i: (i, 0)),
                   pl.BlockSpec((8, tm), lambda i: (0, i)),
                   pl.BlockSpec((tm, LANES), lambda i: (i, 0)),
                   pl.BlockSpec((8, LANES), const)],
        out_shape=[jax.ShapeDtypeStruct((T, D), F32),
                   jax.ShapeDtypeStruct((T * _slabs(D // 2), LANES), jnp.uint32),
                   jax.ShapeDtypeStruct((8, T), jnp.int32),
                   jax.ShapeDtypeStruct((T, LANES), F32),
                   jax.ShapeDtypeStruct((8, LANES), F32)],
        scratch_shapes=[pltpu.VMEM((tm + 16, cw), F32),
                        pltpu.VMEM((8, LANES), F32)],
        compiler_params=_cparams("arbitrary"),
        name="post",
    )(x2, ya2, z2, z2, z2, z2, z2, ga1, sc2, sh2, conv_w, conv_b, wpa, wpb, wo, g2, wr, br, tri)


def _slots_kernel(ps_ref, ri_ref, d_ref, *, n_slab):
    e = ri_ref[0:TOPK_IN_GROUP, :]
    start = jnp.zeros(e.shape, jnp.int32)
    for k in range(N_EXPERTS):
        start = jnp.where(e == k, ps_ref[k], start)
    d_ref[...] = (start + ri_ref[TOPK_IN_GROUP:2 * TOPK_IN_GROUP, :]) * n_slab


def _slots(pad_start, riT, n_slab):
    T = riT.shape[1]
    grid_spec = pltpu.PrefetchScalarGridSpec(
        num_scalar_prefetch=1,
        grid=(1,),
        in_specs=[pl.BlockSpec(riT.shape, lambda i, ps: (0, 0))],
        out_specs=pl.BlockSpec((TOPK_IN_GROUP, T), lambda i, ps: (0, 0)),
    )
    return pl.pallas_call(
        functools.partial(_slots_kernel, n_slab=n_slab),
        grid_spec=grid_spec,
        out_shape=jax.ShapeDtypeStruct((TOPK_IN_GROUP, T), jnp.int32),
        compiler_params=_cparams("arbitrary"),
        name="slots",
    )(pad_start, riT)


def _dispatch_kernel(d0_ref, d1_ref, tail_ref, h_ref, xs_ref, zero_ref, sem, zsem, *, n_slab):
    tm = h_ref.shape[0] // n_slab
    base = pl.program_id(0) * tm
    blk_rows = zero_ref.shape[0]

    @pl.when(pl.program_id(0) == 0)
    def _():
        zero_ref[...] = jnp.zeros_like(zero_ref)

        def tail_copy(e):
            start = pl.multiple_of(tail_ref[e] * blk_rows, blk_rows)
            return pltpu.make_async_copy(zero_ref, xs_ref.at[pl.ds(start, blk_rows), :], zsem)

        for e in range(tail_ref.shape[0]):
            @pl.when(tail_ref[e] >= 0)
            def _():
                tail_copy(e).start()
        for e in range(tail_ref.shape[0]):
            @pl.when(tail_ref[e] >= 0)
            def _():
                tail_copy(e).wait()

    def body(g, carry):
        r8 = pl.multiple_of(g * DMA_UNROLL, DMA_UNROLL)
        for u in range(DMA_UNROLL):
            src = h_ref.at[pl.ds((r8 + u) * n_slab, n_slab), :]
            pltpu.make_async_copy(src, xs_ref.at[pl.ds(d0_ref[base + r8 + u], n_slab), :], sem).start(priority=0)
            pltpu.make_async_copy(src, xs_ref.at[pl.ds(d1_ref[base + r8 + u], n_slab), :], sem).start(priority=1)
        return carry

    lax.fori_loop(0, tm // DMA_UNROLL, body, 0)
    for _ in range(TOPK_IN_GROUP):
        pltpu.make_async_copy(h_ref, xs_ref.at[pl.ds(0, tm * n_slab), :], sem).wait()


def _dispatch(dest0, dest1, tail_blk, h2p, n_pad, n_slab):
    tm = ROW_TILE
    T = h2p.shape[0] // n_slab
    grid_spec = pltpu.PrefetchScalarGridSpec(
        num_scalar_prefetch=3,
        grid=(T // tm,),
        in_specs=[pl.BlockSpec((tm * n_slab, LANES), lambda i, d0, d1, tb: (i, 0))],
        out_specs=pl.BlockSpec(memory_space=pl.ANY),
        scratch_shapes=[pltpu.VMEM((EXPERT_ROWS * n_slab, LANES), h2p.dtype),
                        pltpu.SemaphoreType.DMA(()),
                        pltpu.SemaphoreType.DMA(())],
    )
    return pl.pallas_call(
        functools.partial(_dispatch_kernel, n_slab=n_slab),
        grid_spec=grid_spec,
        out_shape=jax.ShapeDtypeStruct((n_pad * n_slab, LANES), h2p.dtype),
        compiler_params=_cparams("arbitrary"),
        name="dispatch",
    )(dest0, dest1, tail_blk, h2p)


def _expert_kernel(be_ref, nv_ref, xs_ref, w1_ref, w3_ref, w2_ref, ys_ref):
    @pl.when(pl.program_id(0) >= nv_ref[0])
    def _():
        ys_ref[...] = jnp.zeros_like(ys_ref)

    @pl.when(pl.program_id(0) < nv_ref[0])
    def _():
        half_d = w1_ref.shape[1] // 2
        xp = _load_slabs(xs_ref, 0, EXPERT_ROWS, _slabs(half_d))
        x_lo = pltpu.bitcast(xp << 16, F32).astype(BF16)
        x_hi = pltpu.bitcast(xp & jnp.uint32(0xFFFF0000), F32).astype(BF16)
        a = (jnp.dot(x_lo, w1_ref[0, :half_d], preferred_element_type=F32)
             + jnp.dot(x_hi, w1_ref[0, half_d:], preferred_element_type=F32))
        b = (jnp.dot(x_lo, w3_ref[0, :half_d], preferred_element_type=F32)
             + jnp.dot(x_hi, w3_ref[0, half_d:], preferred_element_type=F32))
        hid = (a * jax.nn.sigmoid(a) * b).astype(BF16)
        y = jnp.dot(hid, w2_ref[0], preferred_element_type=F32)
        bits = pltpu.bitcast(y.astype(BF16).astype(F32), jnp.uint32)
        _store_slabs(ys_ref, 0, (bits[:, :half_d] >> 16) | bits[:, half_d:])


def _experts(blk_expert, n_valid, xs, w1b, w3b, w2b):
    _, D, F = w1b.shape
    rb = EXPERT_ROWS
    blk_rows = rb * _slabs(D // 2)
    row_blk = lambda i, be, nv: (jnp.minimum(i, nv[0] - 1), 0)
    out_blk = lambda i, be, nv: (i, 0)
    wgt_blk = lambda i, be, nv: (be[i], 0, 0)
    grid_spec = pltpu.PrefetchScalarGridSpec(
        num_scalar_prefetch=2,
        grid=(xs.shape[0] // blk_rows,),
        in_specs=[pl.BlockSpec((blk_rows, LANES), row_blk),
                  pl.BlockSpec((1, D, F), wgt_blk),
                  pl.BlockSpec((1, D, F), wgt_blk),
                  pl.BlockSpec((1, F, D), wgt_blk)],
        out_specs=pl.BlockSpec((blk_rows, LANES), out_blk),
    )
    return pl.pallas_call(
        _expert_kernel,
        grid_spec=grid_spec,
        out_shape=jax.ShapeDtypeStruct(xs.shape, jnp.uint32),
        compiler_params=_cparams("arbitrary"),
        name="experts",
    )(blk_expert, n_valid, xs, w1b, w3b, w2b)


def _combine_kernel(d0_ref, d1_ref, x1_ref, rf_ref, ga2_ref, ys_ref, o_ref, buf_ref, sem, *, n_slab):
    i = pl.program_id(0)
    n = pl.num_programs(0)
    tm = x1_ref.shape[0]

    def gather(step, slot):
        base = step * tm

        def body(g, carry):
            r8 = pl.multiple_of(g * DMA_UNROLL, DMA_UNROLL)
            for u in range(DMA_UNROLL):
                dst = pl.ds((r8 + u) * n_slab, n_slab)
                pltpu.make_async_copy(ys_ref.at[pl.ds(d0_ref[base + r8 + u], n_slab), :],
                                      buf_ref.at[slot, 0, dst, :], sem.at[slot]).start(priority=0)
                pltpu.make_async_copy(ys_ref.at[pl.ds(d1_ref[base + r8 + u], n_slab), :],
                                      buf_ref.at[slot, 1, dst, :], sem.at[slot]).start(priority=1)
            return carry

        lax.fori_loop(0, tm // DMA_UNROLL, body, 0)

    @pl.when(i == 0)
    def _():
        gather(0, 0)

    @pl.when(i + 1 < n)
    def _():
        gather(i + 1, (i + 1) % 2)

    slot = i % 2
    for k in range(TOPK_IN_GROUP):
        pltpu.make_async_copy(ys_ref.at[pl.ds(0, tm * n_slab), :], buf_ref.at[slot, k], sem.at[slot]).wait()
    rf = rf_ref[...]
    w = [rf[:, k:k + 1] for k in range(TOPK_IN_GROUP)]
    yp = [_load_slabs(buf_ref, 0, tm, n_slab, lead=(slot, k)) for k in range(TOPK_IN_GROUP)]
    half_d = n_slab * LANES
    y_lo = sum(w[k] * pltpu.bitcast(yp[k] << 16, F32) for k in range(TOPK_IN_GROUP))
    y_hi = sum(w[k] * pltpu.bitcast(yp[k] & jnp.uint32(0xFFFF0000), F32) for k in range(TOPK_IN_GROUP))
    ga2 = ga2_ref[0]
    o_ref[:, :half_d] = x1_ref[:, :half_d] + ga2[:, :half_d] * y_lo
    o_ref[:, half_d:] = x1_ref[:, half_d:] + ga2[:, half_d:] * y_hi


def _combine(dest0, dest1, x1, rf, ga2, ys, tiles_per_seq):
    T, D = x1.shape
    n_slab = _slabs(D // 2)
    tm = ROW_TILE
    row = lambda i, d0, d1: (i, 0)
    grid_spec = pltpu.PrefetchScalarGridSpec(
        num_scalar_prefetch=2,
        grid=(T // tm,),
        in_specs=[pl.BlockSpec((tm, D), row),
                  pl.BlockSpec((tm, LANES), row),
                  pl.BlockSpec((1, 1, D), lambda i, d0, d1: (i // tiles_per_seq, 0, 0)),
                  pl.BlockSpec(memory_space=pl.ANY)],
        out_specs=pl.BlockSpec((tm, D), row),
        scratch_shapes=[pltpu.VMEM((2, TOPK_IN_GROUP, tm * n_slab, LANES), ys.dtype),
                        pltpu.SemaphoreType.DMA((2,))],
    )
    return pl.pallas_call(
        functools.partial(_combine_kernel, n_slab=n_slab),
        grid_spec=grid_spec,
        out_shape=jax.ShapeDtypeStruct((T, D), F32),
        compiler_params=_cparams("arbitrary"),
        name="combine",
    )(dest0, dest1, x1, rf, ga2, ys)


def _rope_tables(S):
    pos = jnp.arange(S, dtype=F32)
    inv_freq = ROPE_THETA ** (-jnp.arange(0, ROT_DIM, 2, dtype=F32) / ROT_DIM)
    ang = pos[:, None] * inv_freq[None, :]
    cos, sin = jnp.cos(ang), jnp.sin(ang)
    half = ROT_DIM // 2
    ones = jnp.ones((S, HEAD_DIM - ROT_DIM), F32)
    cos_h = jnp.concatenate([cos, cos, ones], axis=1)
    sin_h = jnp.concatenate([-sin, sin, 0.0 * ones], axis=1)
    return jnp.tile(cos_h, (1, LANES // HEAD_DIM)), jnp.tile(sin_h, (1, LANES // HEAD_DIM)), cos.T, sin.T


def kernel(x, c, w_ada, b_ada, g_norm1, g_norm2, w_in, g_q, g_k, conv_w, conv_b,
           w_pa, w_pb, w_o, w_rg, b_rg, w_re, b_re, w1, w3, w2):
    B, S, D = x.shape
    T = B * S
    assert S % ROW_TILE == 0 and S % QUERY_TILE == 0 and QUERY_TILE % (2 * MOBA_BLOCK) == 0
    assert S // MOBA_BLOCK <= LANES - HEAD_DIM
    tiles_per_seq = S // ROW_TILE
    l = 0

    mod = _ada(c, w_ada[l], b_ada[l])
    sh1, sc1, ga1, sh2, sc2, ga2 = [m.reshape(B, 1, D) for m in jnp.split(mod, N_MOD, axis=-1)]

    x2 = x.reshape(T, D)
    z2 = _inproj(x2, g_norm1[l].reshape(1, D), sc1, sh1, w_in[l].astype(BF16), tiles_per_seq)

    cosf, sinf, cost, sint = _rope_tables(S)
    rep = LANES // HEAD_DIM
    gq_cols = jnp.broadcast_to(jnp.tile(g_q[l], rep).reshape(LANES, 1), (LANES, QUERY_TILE))
    ya = _attention(z2.reshape(B, S, -1), cosf, sinf, cost, sint, gq_cols, jnp.tile(g_k[l], rep).reshape(1, LANES))

    wr = jnp.zeros((D, LANES), F32).at[:, :N_GROUPS].set(w_rg[l]).at[:, N_GROUPS:N_GROUPS + N_EXPERTS].set(w_re[l])
    br = jnp.zeros((1, LANES), F32).at[0, :N_GROUPS].set(b_rg[l]).at[0, N_GROUPS:N_GROUPS + N_EXPERTS].set(b_re[l])
    wr_hi = wr.astype(BF16)
    wr2 = jnp.concatenate([wr_hi, (wr - wr_hi.astype(F32)).astype(BF16)], axis=1)
    tri = (lax.broadcasted_iota(jnp.int32, (POST_CHUNK, POST_CHUNK), 1)
           < lax.broadcasted_iota(jnp.int32, (POST_CHUNK, POST_CHUNK), 0)).astype(BF16)
    x1, h2, ri, rf, cnt = _post(x2, ya.reshape(T, ATTN_WIDTH), z2, ga1, sc2, sh2,
                                conv_w[l], conv_b[l].reshape(1, CONV_WIDTH),
                                w_pa[l].astype(BF16), w_pb[l].astype(BF16), w_o[l].astype(BF16),
                                g_norm2[l].reshape(1, D), wr2, br, tri, tiles_per_seq)

    rb = EXPERT_ROWS
    counts = cnt[0, :N_EXPERTS].astype(jnp.int32)
    padded = (counts + rb - 1) // rb * rb
    pad_end = jnp.cumsum(padded)
    pad_start = pad_end - padded
    dest = _slots(pad_start.astype(jnp.int32), ri, _slabs(D // 2))
    dest0, dest1 = dest[0], dest[1]
    n_blocks = -(-T * TOPK_IN_GROUP // rb) + N_EXPERTS
    n_pad = n_blocks * rb
    n_valid = (pad_end[-1] // rb).astype(jnp.int32)
    blk_start = jnp.minimum(jnp.arange(n_blocks, dtype=jnp.int32), n_valid - 1) * rb
    blk_expert = jnp.sum(pad_end[None, :] <= blk_start[:, None], axis=-1).astype(jnp.int32)
    unused = n_valid + jnp.arange(N_EXPERTS, dtype=jnp.int32)
    tail_blk = jnp.concatenate([jnp.where(padded > 0, pad_end // rb - 1, -1),
                                jnp.where(unused < n_blocks, unused, -1)]).astype(jnp.int32)

    xs = _dispatch(dest0, dest1, tail_blk, h2, n_pad, _slabs(D // 2))
    ys = _experts(blk_expert, n_valid.reshape(1), xs, w1[l].astype(BF16), w3[l].astype(BF16), w2[l].astype(BF16))
    out = _combine(dest0, dest1, x1, rf, ga2, ys, tiles_per_seq)
    return out.reshape(B, S, D)
```

```python
import functools

import jax
import jax.numpy as jnp
from jax import lax
from jax.experimental import pallas as pl
from jax.experimental.pallas import tpu as pltpu

F32 = jnp.float32
BF16 = jnp.bfloat16
HIGHEST = lax.Precision.HIGHEST

N_HEADS = 8
HEAD_DIM = 64
ATTN_WIDTH = N_HEADS * HEAD_DIM
CONV_WIDTH = 512
CONV_K = 3
MOBA_BLOCK = 256
MOBA_TOPK = 3
ROPE_THETA = 500000.0
ROT_DIM = HEAD_DIM // 4
N_GROUPS = 4
EXPERTS_PER_GROUP = 8
N_EXPERTS = N_GROUPS * EXPERTS_PER_GROUP
TOPK_IN_GROUP = 2
N_MOD = 6
EPS = 1e-6

LANES = 128
NEG = -1e30
ROW_TILE = 512
INPROJ_TILE = 1024
DISPATCH_TILE = 2048
POST_CHUNK = 256
QUERY_TILE = 512
DMA_UNROLL = 8
EXPERT_ROWS = 512
VMEM_LIMIT = 56 * 1024 * 1024
ATTN_VMEM_LIMIT = 60 * 1024 * 1024


def _cparams(*sem):
    return pltpu.CompilerParams(dimension_semantics=sem, vmem_limit_bytes=VMEM_LIMIT)


def _ada_kernel(c_ref, w_ref, b_ref, o_ref):
    c = c_ref[...]
    a = c * jax.nn.sigmoid(c)
    o_ref[...] = jnp.dot(a, w_ref[...], preferred_element_type=F32, precision=HIGHEST) + b_ref[...]


def _ada(c, w_ada, b_ada):
    B, D = c.shape
    N = w_ada.shape[1]
    tn = 1536
    return pl.pallas_call(
        _ada_kernel,
        grid=(N // tn,),
        in_specs=[pl.BlockSpec((B, D), lambda j: (0, 0)),
                  pl.BlockSpec((D, tn), lambda j: (0, j)),
                  pl.BlockSpec((1, tn), lambda j: (0, j))],
        out_specs=pl.BlockSpec((B, tn), lambda j: (0, j)),
        out_shape=jax.ShapeDtypeStruct((B, N), F32),
        compiler_params=_cparams("arbitrary"),
        name="ada",
    )(c, w_ada, b_ada.reshape(1, N))


def _inproj_kernel(x_ref, g_ref, sc_ref, sh_ref, w_ref, z_ref, *, n_chunk):
    x = x_ref[...]
    ms = jnp.mean(x * x, axis=-1, keepdims=True)
    y = x * lax.rsqrt(ms + EPS) * g_ref[...]
    h = (y * (1.0 + sc_ref[0]) + sh_ref[0]).astype(BF16)
    for n in range(0, z_ref.shape[1], n_chunk):
        z_ref[:, n:n + n_chunk] = jnp.dot(h, w_ref[:, n:n + n_chunk],
                                          preferred_element_type=F32).astype(BF16)


def _inproj(x2, g1, sc1, sh1, w_in_bf, seq_len):
    T, D = x2.shape
    N = w_in_bf.shape[1]
    tm = INPROJ_TILE
    assert seq_len % tm == 0
    bmap = lambda i: (i // (seq_len // tm), 0, 0)
    return pl.pallas_call(
        functools.partial(_inproj_kernel, n_chunk=512),
        grid=(T // tm,),
        in_specs=[pl.BlockSpec((tm, D), lambda i: (i, 0)),
                  pl.BlockSpec((1, D), lambda i: (0, 0)),
                  pl.BlockSpec((1, 1, D), bmap),
                  pl.BlockSpec((1, 1, D), bmap),
                  pl.BlockSpec((D, N), lambda i: (0, 0), pipeline_mode=pl.Buffered(1))],
        out_specs=pl.BlockSpec((tm, N), lambda i: (i, 0)),
        out_shape=jax.ShapeDtypeStruct((T, N), BF16),
        compiler_params=_cparams("arbitrary"),
        name="inproj",
    )(x2, g1, sc1, sh1, w_in_bf)


def _fold_rows(x, op):
    parts = [x[r:r + 8] for r in range(0, x.shape[0], 8)]
    while len(parts) > 1:
        parts = [op(parts[i], parts[i + 1]) for i in range(0, len(parts) - 1, 2)] + (
            [parts[-1]] if len(parts) % 2 else [])
    return parts[0]


def _attn_kernel(q_ref, k_ref, v_ref, cos_ref, sin_ref, cost_ref, sint_ref, gq_ref, gk_ref, o_ref,
                 kaug_ref, vt_ref, kmp_ref, kst_ref, s_ref, mcol_ref, qa_ref):
    S = q_ref.shape[1]
    blk = MOBA_BLOCK
    qt = QUERY_TILE
    sub = qt // blk
    nb = S // blk
    nq = S // qt
    nbp = kst_ref.shape[0] // 6
    hd = HEAD_DIM

    half = ROT_DIM // 2
    lane_r = lax.broadcasted_iota(jnp.int32, (blk, LANES), 1)
    rot_lo = (lane_r & (hd - 1)) < half
    same_head = jnp.where((lax.broadcasted_iota(jnp.int32, (LANES, LANES), 0) < hd)
                          == (lax.broadcasted_iota(jnp.int32, (LANES, LANES), 1) < hd), 1.0, 0.0).astype(BF16)

    def norm_rope_keys(xb, r0):
        x = xb.astype(F32)
        sq = x * x
        sq_hi = sq.astype(BF16)
        sq_lo = (sq - sq_hi.astype(F32)).astype(BF16)
        ssq = (jnp.dot(sq_hi, same_head, preferred_element_type=F32)
               + jnp.dot(sq_lo, same_head, preferred_element_type=F32))
        y = x * lax.rsqrt(ssq * (1.0 / hd) + EPS) * gk_ref[...]
        rot = jnp.where(rot_lo, pltpu.roll(y, LANES - half, 1), pltpu.roll(y, half, 1))
        return y * cos_ref[pl.ds(r0, blk), :] + rot * sin_ref[pl.ds(r0, blk), :]

    def norm_rope_queries_t(xb, r0):
        xT = xb.astype(F32).T
        sq = xT * xT
        cos = cost_ref[:, pl.ds(r0, qt)]
        sin = sint_ref[:, pl.ds(r0, qt)]
        rows = []
        for h in range(2):
            lo, hi = h * hd, (h + 1) * hd
            ssq = jnp.sum(_fold_rows(sq[lo:hi], jnp.add), axis=0, keepdims=True)
            y = xT[lo:hi] * lax.rsqrt(ssq * (1.0 / hd) + EPS) * gq_ref[lo:hi, :]
            y1, y2 = y[0:half], y[half:2 * half]
            rows += [y1 * cos - y2 * sin, y2 * cos + y1 * sin, y[2 * half:]]
        return jnp.concatenate(rows, axis=0)

    kmp_ref[...] = jnp.zeros_like(kmp_ref)
    ones_row = jnp.where(lax.broadcasted_iota(jnp.int32, (16, blk), 0) == 0, 1.0, 0.0).astype(BF16)
    lane_k = lax.broadcasted_iota(jnp.int32, (blk, LANES), 1)
    head0_k = lane_k < HEAD_DIM
    lane_m = lax.broadcasted_iota(jnp.int32, (nbp, LANES), 1)

    def prepare_keys(t):
        for u in range(sub):
            j = t * sub + u
            r0 = pl.multiple_of(jnp.minimum(j, nb - 1) * blk, blk)
            kr = norm_rope_keys(k_ref[0, pl.ds(r0, blk), :], r0)
            kmp_ref[pl.ds(j, 1), :] = jnp.sum(kr, axis=0, keepdims=True) * (1.0 / blk)
            kaug_ref[0, j] = jnp.where(head0_k, kr, jnp.where(lane_k - hd == j, 1.0, 0.0)).astype(BF16)
            kaug_ref[1, j] = jnp.where(head0_k, jnp.where(lane_k == j, 1.0, 0.0), kr).astype(BF16)
            vT = v_ref[0, pl.ds(r0, blk), :].astype(F32).T
            for h in range(2):
                vt_ref[h, j, 0:HEAD_DIM, :] = vT[h * HEAD_DIM:(h + 1) * HEAD_DIM].astype(BF16)
                vt_ref[h, j, HEAD_DIM:HEAD_DIM + 16, :] = ones_row
        kmp = kmp_ref[0:nbp, :]
        k_hi = kmp.astype(BF16)
        k_lo = (kmp - k_hi.astype(F32)).astype(BF16)
        zero = jnp.zeros((nbp, LANES), BF16)
        parts = []
        for h in range(2):
            mine = (lane_m < hd) if h == 0 else (lane_m >= hd)
            parts += [jnp.where(mine, k_hi, zero), jnp.where(mine, k_lo, zero)]
        parts += [parts[0], parts[2]]
        for n, part in enumerate(parts):
            kst_ref[n * nbp:(n + 1) * nbp, :] = part

    prepare_keys(0)

    key_i = lax.broadcasted_iota(jnp.int32, (blk, qt), 0)
    qry_i = lax.broadcasted_iota(jnp.int32, (blk, qt), 1)
    causal = [(qry_i < u * blk) | (qry_i >= (u + 1) * blk) | (key_i <= qry_i - u * blk) for u in range(sub)]
    rowf = lax.broadcasted_iota(jnp.int32, (nbp, qt), 0).astype(F32)
    subf = (lax.broadcasted_iota(jnp.int32, (nbp, qt), 1) // blk).astype(F32)
    q_scale = (hd ** -0.5) * 1.4426950408889634

    def query_operands(t):
        r0 = pl.multiple_of(t * qt, qt)
        qT = norm_rope_queries_t(q_ref[0, pl.ds(r0, qt), :], r0)
        cur = lax.convert_element_type(t * sub, F32) + subf
        q_hi = qT.astype(BF16)
        q_lo = (qT - q_hi.astype(F32)).astype(BF16)
        g1 = jnp.dot(kst_ref[0:4 * nbp, :], q_hi, preferred_element_type=F32)
        g2 = jnp.dot(kst_ref[4 * nbp:6 * nbp, :], q_lo, preferred_element_type=F32)
        qa = []
        for h in range(2):
            gate = g1[2 * h * nbp:(2 * h + 1) * nbp] + g1[(2 * h + 1) * nbp:(2 * h + 2) * nbp] + g2[h * nbp:(h + 1) * nbp]
            g = jnp.where(rowf < cur, gate, -jnp.inf)
            keep = rowf == cur
            for r in range(MOBA_TOPK):
                m = jnp.max(g, axis=0, keepdims=True)
                idx = jnp.min(jnp.where(g == m, rowf, 1e9), axis=0, keepdims=True)
                pick = (rowf == idx) & (cur > r)
                keep = keep | pick
                g = jnp.where(pick, -jnp.inf, g)
            bias = jnp.where(keep, 0.0, NEG)
            qs = qT[h * hd:(h + 1) * hd] * q_scale
            pad = jnp.zeros((LANES - hd - nbp, qt), F32)
            pieces = [qs, bias, pad] if h == 0 else [bias, pad, qs]
            qa.append(jnp.concatenate(pieces, axis=0).astype(BF16))
        return qa

    def pass1_tile(par, h, j, qa_h, mask):
        sT = jnp.dot(kaug_ref[h, j], qa_h, preferred_element_type=F32)
        if mask is not None:
            sT = jnp.where(mask, sT, NEG)
        s_ref[par, h, j] = sT
        return _fold_rows(sT, jnp.maximum)

    def pass2_tile(par, h, j):
        pT = jnp.exp2(s_ref[par, h, j] - mcol_ref[h, 0:1, :]).astype(BF16)
        return jnp.dot(vt_ref[h, j], pT, preferred_element_type=F32)

    def pass1_own(t, par, qa):
        mx = []
        for h in range(2):
            f = [pass1_tile(par, h, t * sub + u, qa[h], causal[u]) for u in range(sub)]
            mx.append(functools.reduce(jnp.maximum, f))
        return mx

    def pass2_own(t, par):
        return [sum(pass2_tile(par, h, t * sub + u) for u in range(sub)) for h in range(2)]

    def pass1_pair(p, par, qa, mx):
        return [functools.reduce(jnp.maximum, [mx[h]] + [pass1_tile(par, h, 2 * p + u, qa[h], None) for u in range(2)])
                for h in range(2)]

    def pass2_pair(p, par, acc):
        return [acc[h] + sum(pass2_tile(par, h, 2 * p + u) for u in range(2)) for h in range(2)]

    def finish_pass1(mx):
        for h in range(2):
            mcol_ref[h] = jnp.broadcast_to(jnp.max(mx[h], axis=0, keepdims=True), mcol_ref.shape[1:])

    def finish_pass2(t, acc):
        outT = jnp.concatenate([acc[h][0:hd] / acc[h][hd:hd + 1] for h in range(2)], axis=0)
        o_ref[0, pl.ds(pl.multiple_of(t * qt, qt), qt), :] = outT.T.astype(BF16)

    def stage(t, par):

        def prepare_next():
            prepare_keys(t + 1)
            nxt = query_operands(jnp.minimum(t + 1, nq - 1))
            for h in range(2):
                qa_ref[1 - par, h] = nxt[h]

        @pl.when(t == 0)
        def _():
            finish_pass1(pass1_own(t, par, query_operands(t)))
            prepare_next()

        @pl.when((t > 0) & (t < nq))
        def _():
            qa = [qa_ref[par, h] for h in range(2)]
            mx = pass1_own(t, par, qa)
            acc = pass2_own(t - 1, 1 - par)
            prepare_next()
            n_prev = (t - 1) * sub // 2

            def both(p, c):
                mx, acc = c
                return tuple(pass1_pair(p, par, qa, mx)), tuple(pass2_pair(p, 1 - par, acc))

            def both_twice(p2, c):
                return both(2 * p2 + 1, both(2 * p2, c))

            c = lax.fori_loop(0, n_prev // 2, both_twice, (tuple(mx), tuple(acc)))
            mx, acc = lax.fori_loop(n_prev // 2 * 2, n_prev, both, c)
            mx = lax.fori_loop(n_prev, t * sub // 2, lambda p, m: tuple(pass1_pair(p, par, qa, m)), tuple(mx))
            finish_pass2(t - 1, acc)
            finish_pass1(mx)

        @pl.when(t == nq)
        def _():
            acc = pass2_own(t - 1, 1 - par)
            acc = lax.fori_loop(0, (t - 1) * sub // 2, lambda p, a: tuple(pass2_pair(p, 1 - par, a)), tuple(acc))
            finish_pass2(t - 1, acc)

    def stage_pair(tt, carry):
        stage(2 * tt, 0)
        stage(2 * tt + 1, 1)
        return carry

    lax.fori_loop(0, (nq + 2) // 2, stage_pair, 0)


def _attention(z3, cosf, sinf, cost, sint, gq_cols, gk2):
    B, S, _ = z3.shape
    n_pair = N_HEADS // 2
    kq = ATTN_WIDTH // LANES
    nb = S // MOBA_BLOCK
    nbp = -(-nb // 16) * 16
    sub = QUERY_TILE // MOBA_BLOCK
    assert sub <= 8
    return pl.pallas_call(
        _attn_kernel,
        grid=(B, n_pair),
        in_specs=[pl.BlockSpec((1, S, LANES), lambda b, p: (b, 0, p)),
                  pl.BlockSpec((1, S, LANES), lambda b, p: (b, 0, kq + p)),
                  pl.BlockSpec((1, S, LANES), lambda b, p: (b, 0, 2 * kq + p)),
                  pl.BlockSpec((S, LANES), lambda b, p: (0, 0), pipeline_mode=pl.Buffered(1)),
                  pl.BlockSpec((S, LANES), lambda b, p: (0, 0), pipeline_mode=pl.Buffered(1)),
                  pl.BlockSpec(cost.shape, lambda b, p: (0, 0)),
                  pl.BlockSpec(sint.shape, lambda b, p: (0, 0)),
                  pl.BlockSpec((LANES, QUERY_TILE), lambda b, p: (0, 0)),
                  pl.BlockSpec((1, LANES), lambda b, p: (0, 0))],
        out_specs=pl.BlockSpec((1, S, LANES), lambda b, p: (b, 0, p)),
        out_shape=jax.ShapeDtypeStruct((B, S, ATTN_WIDTH), BF16),
        scratch_shapes=[pltpu.VMEM((2, nb + sub, MOBA_BLOCK, LANES), BF16),
                        pltpu.VMEM((2, nb + sub, HEAD_DIM + 16, MOBA_BLOCK), BF16),
                        pltpu.VMEM((nbp + 8, LANES), F32),
                        pltpu.VMEM((6 * nbp, LANES), BF16),
                        pltpu.VMEM((2, 2, nb, MOBA_BLOCK, QUERY_TILE), F32),
                        pltpu.VMEM((2, 8, QUERY_TILE), F32),
                        pltpu.VMEM((2, 2, LANES, QUERY_TILE), BF16)],
        compiler_params=pltpu.CompilerParams(dimension_semantics=("arbitrary", "arbitrary"),
                                             vmem_limit_bytes=ATTN_VMEM_LIMIT),
        name="attn",
    )(z3, z3, z3, cosf, sinf, cost, sint, gq_cols, gk2)


def _slabs(width):
    return width // LANES


def _load_slabs(ref, row0, rows, n_slab, lead=()):
    return jnp.concatenate([ref[lead + (pl.ds(row0 * n_slab + s, rows, stride=n_slab), slice(None))]
                            for s in range(n_slab)], axis=1)


def _store_slabs(ref, row0, val):
    rows, width = val.shape
    n_slab = _slabs(width)
    for s in range(n_slab):
        ref[pl.ds(row0 * n_slab + s, rows, stride=n_slab), :] = val[:, s * LANES:(s + 1) * LANES]


def _post_kernel(x_ref, ya_ref, xb_ref, bg_ref, cg_ref, gta_ref, gtb_ref, ga1_ref, sc2_ref, sh2_ref,
                 cw_ref, cb_ref, wpa_ref, wpb_ref, wo_ref, g2_ref, wr_ref, br_ref, tri_ref,
                 x1_ref, h2_ref, ri_ref, rf_ref, cnt_ref, ubuf_ref, run_ref, *, tiles_per_seq):
    i = pl.program_id(0)
    tm = x_ref.shape[0]
    rc = POST_CHUNK
    halo = 8

    @pl.when(i == 0)
    def _():
        run_ref[...] = jnp.zeros_like(run_ref)

    @pl.when(i % tiles_per_seq == 0)
    def _():
        ubuf_ref[0:halo, :] = jnp.zeros((halo, CONV_WIDTH), F32)

    ubuf_ref[halo:halo + tm, :] = cg_ref[...].astype(F32) * xb_ref[...].astype(F32)
    cw = cw_ref[...]
    lanef = lax.broadcasted_iota(jnp.int32, (rc, LANES), 1).astype(F32)
    half_d = x_ref.shape[1] // 2
    run = run_ref[0:1, :]

    for c in range(tm // rc):
        rows = pl.ds(c * rc, rc)
        conv = (cw[0:1, :] * ubuf_ref[pl.ds(halo - 2 + c * rc, rc), :]
                + cw[1:2, :] * ubuf_ref[pl.ds(halo - 1 + c * rc, rc), :]
                + cw[2:3, :] * ubuf_ref[pl.ds(halo + c * rc, rc), :])
        y_b = (bg_ref[rows, :].astype(F32) * (conv + cb_ref[...])).astype(BF16)
        pa = jnp.dot(ya_ref[rows, :], wpa_ref[...], preferred_element_type=F32)
        pb = jnp.dot(y_b, wpb_ref[...], preferred_element_type=F32)
        merged = (jax.nn.sigmoid(gta_ref[rows, :].astype(F32)) * pa
                  + jax.nn.sigmoid(gtb_ref[rows, :].astype(F32)) * pb).astype(BF16)
        x1 = x_ref[rows, :] + ga1_ref[0] * jnp.dot(merged, wo_ref[...], preferred_element_type=F32)
        x1_ref[rows, :] = x1

        ms = jnp.mean(x1 * x1, axis=-1, keepdims=True)
        h2 = x1 * lax.rsqrt(ms + EPS) * g2_ref[...]
        h2 = h2 * (1.0 + sc2_ref[0]) + sh2_ref[0]
        h_hi = h2.astype(BF16)
        h_hi32 = h_hi.astype(F32)
        bits = pltpu.bitcast(h_hi32, jnp.uint32)
        _store_slabs(h2_ref, c * rc, (bits[:, :half_d] >> 16) | bits[:, half_d:])

        h_lo = (h2 - h_hi32).astype(BF16)
        r = jnp.dot(h_hi, wr_ref[...], preferred_element_type=F32)
        logit = (r[:, :LANES] + r[:, LANES:]
                 + jnp.dot(h_lo, wr_ref[:, :LANES], preferred_element_type=F32) + br_ref[...])
        gl = jnp.where(lanef < N_GROUPS, logit, -jnp.inf)
        gmax = jnp.max(gl, axis=-1, keepdims=True)
        g_idx = jnp.min(jnp.where(gl == gmax, lanef, 1e9), axis=-1, keepdims=True)
        g_w = 1.0 / jnp.sum(jnp.exp(gl - gmax), axis=-1, keepdims=True)
        e_lo = N_GROUPS + EXPERTS_PER_GROUP * g_idx
        el = jnp.where((lanef >= e_lo) & (lanef < e_lo + EXPERTS_PER_GROUP), logit, -jnp.inf)
        v0 = jnp.max(el, axis=-1, keepdims=True)
        i0 = jnp.min(jnp.where(el == v0, lanef, 1e9), axis=-1, keepdims=True)
        el = jnp.where(lanef == i0, -jnp.inf, el)
        v1 = jnp.max(el, axis=-1, keepdims=True)
        i1 = jnp.min(jnp.where(el == v1, lanef, 1e9), axis=-1, keepdims=True)
        t = jnp.exp(v1 - v0)
        w0 = g_w / (1.0 + t)
        w1 = g_w * t / (1.0 + t)
        e0 = i0 - N_GROUPS
        e1 = i1 - N_GROUPS

        oh0 = lanef == e0
        oh1 = lanef == e1
        oh = jnp.where(oh0 | oh1, 1.0, 0.0)
        before = jnp.dot(tri_ref[...], oh.astype(BF16), preferred_element_type=F32) + run
        r0 = jnp.sum(jnp.where(oh0, before, 0.0), axis=-1, keepdims=True)
        r1 = jnp.sum(jnp.where(oh1, before, 0.0), axis=-1, keepdims=True)
        run = run + jnp.sum(oh, axis=0, keepdims=True)

        ri = jnp.where(lanef == 0, e0, jnp.where(lanef == 1, e1, jnp.where(lanef == 2, r0, jnp.where(lanef == 3, r1, 0.0))))
        ri_ref[:, rows] = ri.astype(jnp.int32).T[0:8]
        rf_ref[rows, :] = jnp.where(lanef == 0, w0, jnp.where(lanef == 1, w1, 0.0))

    ubuf_ref[0:halo, :] = ubuf_ref[tm:tm + halo, :]
    run_ref[...] = jnp.broadcast_to(run, run_ref.shape)
    cnt_ref[...] = jnp.broadcast_to(run, cnt_ref.shape)


def _post(x2, ya2, z2, ga1, sc2, sh2, conv_w, conv_b, wpa, wpb, wo, g2, wr, br, tri, tiles_per_seq):
    T, D = x2.shape
    tm = ROW_TILE
    cw = CONV_WIDTH
    xcol = 3 * ATTN_WIDTH // cw
    gcol = (3 * ATTN_WIDTH + 3 * cw) // D
    bmap = lambda i: (i // tiles_per_seq, 0, 0)
    const = lambda i: (0, 0)
    return pl.pallas_call(
        functools.partial(_post_kernel, tiles_per_seq=tiles_per_seq),
        grid=(T // tm,),
        in_specs=[pl.BlockSpec((tm, D), lambda i: (i, 0)),
                  pl.BlockSpec((tm, ATTN_WIDTH), lambda i: (i, 0)),
                  pl.BlockSpec((tm, cw), lambda i: (i, xcol)),
                  pl.BlockSpec((tm, cw), lambda i: (i, xcol + 1)),
                  pl.BlockSpec((tm, cw), lambda i: (i, xcol + 2)),
                  pl.BlockSpec((tm, D), lambda i: (i, gcol)),
                  pl.BlockSpec((tm, D), lambda i: (i, gcol + 1)),
                  pl.BlockSpec((1, 1, D), bmap),
                  pl.BlockSpec((1, 1, D), bmap),
                  pl.BlockSpec((1, 1, D), bmap),
                  pl.BlockSpec((CONV_K, cw), const),
                  pl.BlockSpec((1, cw), const),
                  pl.BlockSpec((ATTN_WIDTH, D), const),
                  pl.BlockSpec((cw, D), const),
                  pl.BlockSpec((D, D), const),
                  pl.BlockSpec((1, D), const),
                  pl.BlockSpec((D, 2 * LANES), const),
                  pl.BlockSpec((1, LANES), const),
                  pl.BlockSpec((POST_CHUNK, POST_CHUNK), const)],
        out_specs=[pl.BlockSpec((tm, D), lambda i: (i, 0)),
                   pl.BlockSpec((tm * _slabs(D // 2), LANES), lambda i: (i, 0)),
                   pl.BlockSpec((8, tm), lambda i: (0, i)),
                   pl.BlockSpec((tm, LANES), lambda i: (i, 0)),
                   pl.BlockSpec((8, LANES), const)],
        out_shape=[jax.ShapeDtypeStruct((T, D), F32),
                   jax.ShapeDtypeStruct((T * _slabs(D // 2), LANES), jnp.uint32),
                   jax.ShapeDtypeStruct((8, T), jnp.int32),
                   jax.ShapeDtypeStruct((T, LANES), F32),
                   jax.ShapeDtypeStruct((8, LANES), F32)],
        scratch_shapes=[pltpu.VMEM((tm + 16, cw), F32),
                        pltpu.VMEM((8, LANES), F32)],
        compiler_params=_cparams("arbitrary"),
        name="post",
    )(x2, ya2, z2, z2, z2, z2, z2, ga1, sc2, sh2, conv_w, conv_b, wpa, wpb, wo, g2, wr, br, tri)


def _slots_kernel(ps_ref, ri_ref, d_ref, *, n_slab):
    e = ri_ref[0:TOPK_IN_GROUP, :]
    start = jnp.zeros(e.shape, jnp.int32)
    for k in range(N_EXPERTS):
        start = jnp.where(e == k, ps_ref[k], start)
    d_ref[...] = (start + ri_ref[TOPK_IN_GROUP:2 * TOPK_IN_GROUP, :]) * n_slab


def _slots(pad_start, riT, n_slab):
    T = riT.shape[1]
    grid_spec = pltpu.PrefetchScalarGridSpec(
        num_scalar_prefetch=1,
        grid=(1,),
        in_specs=[pl.BlockSpec(riT.shape, lambda i, ps: (0, 0))],
        out_specs=pl.BlockSpec((TOPK_IN_GROUP, T), lambda i, ps: (0, 0)),
    )
    return pl.pallas_call(
        functools.partial(_slots_kernel, n_slab=n_slab),
        grid_spec=grid_spec,
        out_shape=jax.ShapeDtypeStruct((TOPK_IN_GROUP, T), jnp.int32),
        compiler_params=_cparams("arbitrary"),
        name="slots",
    )(pad_start, riT)


def _dispatch_kernel(d0_ref, d1_ref, tail_ref, h_ref, xs_ref, zero_ref, sem, zsem, *, n_slab):
    tm = h_ref.shape[0] // n_slab
    base = pl.program_id(0) * tm
    blk_rows = zero_ref.shape[0]

    @pl.when(pl.program_id(0) == 0)
    def _():
        zero_ref[...] = jnp.zeros_like(zero_ref)

        def tail_copy(e):
            start = pl.multiple_of(tail_ref[e] * blk_rows, blk_rows)
            return pltpu.make_async_copy(zero_ref, xs_ref.at[pl.ds(start, blk_rows), :], zsem)

        for e in range(tail_ref.shape[0]):
            @pl.when(tail_ref[e] >= 0)
            def _():
                tail_copy(e).start()
        for e in range(tail_ref.shape[0]):
            @pl.when(tail_ref[e] >= 0)
            def _():
                tail_copy(e).wait()

    def body(g, carry):
        r8 = pl.multiple_of(g * DMA_UNROLL, DMA_UNROLL)
        for u in range(DMA_UNROLL):
            src = h_ref.at[pl.ds((r8 + u) * n_slab, n_slab), :]
            pltpu.make_async_copy(src, xs_ref.at[pl.ds(d0_ref[base + r8 + u], n_slab), :], sem).start(priority=0)
            pltpu.make_async_copy(src, xs_ref.at[pl.ds(d1_ref[base + r8 + u], n_slab), :], sem).start(priority=1)
        return carry

    lax.fori_loop(0, tm // DMA_UNROLL, body, 0)
    for _ in range(TOPK_IN_GROUP):
        pltpu.make_async_copy(h_ref, xs_ref.at[pl.ds(0, tm * n_slab), :], sem).wait()


def _dispatch(dest0, dest1, tail_blk, h2p, n_pad, n_slab):
    tm = DISPATCH_TILE
    T = h2p.shape[0] // n_slab
    grid_spec = pltpu.PrefetchScalarGridSpec(
        num_scalar_prefetch=3,
        grid=(T // tm,),
        in_specs=[pl.BlockSpec((tm * n_slab, LANES), lambda i, d0, d1, tb: (i, 0))],
        out_specs=pl.BlockSpec(memory_space=pl.ANY),
        scratch_shapes=[pltpu.VMEM((EXPERT_ROWS * n_slab, LANES), h2p.dtype),
                        pltpu.SemaphoreType.DMA(()),
                        pltpu.SemaphoreType.DMA(())],
    )
    return pl.pallas_call(
        functools.partial(_dispatch_kernel, n_slab=n_slab),
        grid_spec=grid_spec,
        out_shape=jax.ShapeDtypeStruct((n_pad * n_slab, LANES), h2p.dtype),
        compiler_params=_cparams("arbitrary"),
        name="dispatch",
    )(dest0, dest1, tail_blk, h2p)


def _expert_kernel(be_ref, nv_ref, xs_ref, w1_ref, w3_ref, w2_ref, ys_ref):
    @pl.when(pl.program_id(0) >= nv_ref[0])
    def _():
        ys_ref[...] = jnp.zeros_like(ys_ref)

    @pl.when(pl.program_id(0) < nv_ref[0])
    def _():
        half_d = w1_ref.shape[1] // 2
        xp = _load_slabs(xs_ref, 0, EXPERT_ROWS, _slabs(half_d))
        x_lo = pltpu.bitcast(xp << 16, F32).astype(BF16)
        x_hi = pltpu.bitcast(xp & jnp.uint32(0xFFFF0000), F32).astype(BF16)
        a = (jnp.dot(x_lo, w1_ref[0, :half_d], preferred_element_type=F32)
             + jnp.dot(x_hi, w1_ref[0, half_d:], preferred_element_type=F32))
        b = (jnp.dot(x_lo, w3_ref[0, :half_d], preferred_element_type=F32)
             + jnp.dot(x_hi, w3_ref[0, half_d:], preferred_element_type=F32))
        hid = (a * jax.nn.sigmoid(a) * b).astype(BF16)
        y = jnp.dot(hid, w2_ref[0], preferred_element_type=F32)
        bits = pltpu.bitcast(y.astype(BF16).astype(F32), jnp.uint32)
        _store_slabs(ys_ref, 0, (bits[:, :half_d] >> 16) | bits[:, half_d:])


def _experts(blk_expert, n_valid, xs, w1b, w3b, w2b):
    _, D, F = w1b.shape
    rb = EXPERT_ROWS
    blk_rows = rb * _slabs(D // 2)
    row_blk = lambda i, be, nv: (jnp.minimum(i, nv[0] - 1), 0)
    out_blk = lambda i, be, nv: (i, 0)
    wgt_blk = lambda i, be, nv: (be[i], 0, 0)
    grid_spec = pltpu.PrefetchScalarGridSpec(
        num_scalar_prefetch=2,
        grid=(xs.shape[0] // blk_rows,),
        in_specs=[pl.BlockSpec((blk_rows, LANES), row_blk),
                  pl.BlockSpec((1, D, F), wgt_blk),
                  pl.BlockSpec((1, D, F), wgt_blk),
                  pl.BlockSpec((1, F, D), wgt_blk)],
        out_specs=pl.BlockSpec((blk_rows, LANES), out_blk),
    )
    return pl.pallas_call(
        _expert_kernel,
        grid_spec=grid_spec,
        out_shape=jax.ShapeDtypeStruct(xs.shape, jnp.uint32),
        compiler_params=_cparams("arbitrary"),
        name="experts",
    )(blk_expert, n_valid, xs, w1b, w3b, w2b)


def _combine_kernel(d0_ref, d1_ref, x1_ref, rf_ref, ga2_ref, ys_ref, o_ref, buf_ref, sem, *, n_slab):
    i = pl.program_id(0)
    n = pl.num_programs(0)
    tm = x1_ref.shape[0]

    def gather(step, slot):
        base = step * tm

        def body(g, carry):
            r8 = pl.multiple_of(g * DMA_UNROLL, DMA_UNROLL)
            for u in range(DMA_UNROLL):
                dst = pl.ds((r8 + u) * n_slab, n_slab)
                pltpu.make_async_copy(ys_ref.at[pl.ds(d0_ref[base + r8 + u], n_slab), :],
                                      buf_ref.at[slot, 0, dst, :], sem.at[slot]).start(priority=0)
                pltpu.make_async_copy(ys_ref.at[pl.ds(d1_ref[base + r8 + u], n_slab), :],
                                      buf_ref.at[slot, 1, dst, :], sem.at[slot]).start(priority=1)
            return carry

        lax.fori_loop(0, tm // DMA_UNROLL, body, 0)

    @pl.when(i == 0)
    def _():
        gather(0, 0)

    @pl.when(i + 1 < n)
    def _():
        gather(i + 1, (i + 1) % 2)

    slot = i % 2
    for k in range(TOPK_IN_GROUP):
        pltpu.make_async_copy(ys_ref.at[pl.ds(0, tm * n_slab), :], buf_ref.at[slot, k], sem.at[slot]).wait()
    rf = rf_ref[...]
    w = [rf[:, k:k + 1] for k in range(TOPK_IN_GROUP)]
    yp = [_load_slabs(buf_ref, 0, tm, n_slab, lead=(slot, k)) for k in range(TOPK_IN_GROUP)]
    half_d = n_slab * LANES
    y_lo = sum(w[k] * pltpu.bitcast(yp[k] << 16, F32) for k in range(TOPK_IN_GROUP))
    y_hi = sum(w[k] * pltpu.bitcast(yp[k] & jnp.uint32(0xFFFF0000), F32) for k in range(TOPK_IN_GROUP))
    ga2 = ga2_ref[0]
    o_ref[:, :half_d] = x1_ref[:, :half_d] + ga2[:, :half_d] * y_lo
    o_ref[:, half_d:] = x1_ref[:, half_d:] + ga2[:, half_d:] * y_hi


def _combine(dest0, dest1, x1, rf, ga2, ys, tiles_per_seq):
    T, D = x1.shape
    n_slab = _slabs(D // 2)
    tm = ROW_TILE
    row = lambda i, d0, d1: (i, 0)
    grid_spec = pltpu.PrefetchScalarGridSpec(
        num_scalar_prefetch=2,
        grid=(T // tm,),
        in_specs=[pl.BlockSpec((tm, D), row),
                  pl.BlockSpec((tm, LANES), row),
                  pl.BlockSpec((1, 1, D), lambda i, d0, d1: (i // tiles_per_seq, 0, 0)),
                  pl.BlockSpec(memory_space=pl.ANY)],
        out_specs=pl.BlockSpec((tm, D), row),
        scratch_shapes=[pltpu.VMEM((2, TOPK_IN_GROUP, tm * n_slab, LANES), ys.dtype),
                        pltpu.SemaphoreType.DMA((2,))],
    )
    return pl.pallas_call(
        functools.partial(_combine_kernel, n_slab=n_slab),
        grid_spec=grid_spec,
        out_shape=jax.ShapeDtypeStruct((T, D), F32),
        compiler_params=_cparams("arbitrary"),
        name="combine",
    )(dest0, dest1, x1, rf, ga2, ys)


def _rope_tables(S):
    pos = jnp.arange(S, dtype=F32)
    inv_freq = ROPE_THETA ** (-jnp.arange(0, ROT_DIM, 2, dtype=F32) / ROT_DIM)
    ang = pos[:, None] * inv_freq[None, :]
    cos, sin = jnp.cos(ang), jnp.sin(ang)
    half = ROT_DIM // 2
    ones = jnp.ones((S, HEAD_DIM - ROT_DIM), F32)
    cos_h = jnp.concatenate([cos, cos, ones], axis=1)
    sin_h = jnp.concatenate([-sin, sin, 0.0 * ones], axis=1)
    return jnp.tile(cos_h, (1, LANES // HEAD_DIM)), jnp.tile(sin_h, (1, LANES // HEAD_DIM)), cos.T, sin.T


def kernel(x, c, w_ada, b_ada, g_norm1, g_norm2, w_in, g_q, g_k, conv_w, conv_b,
           w_pa, w_pb, w_o, w_rg, b_rg, w_re, b_re, w1, w3, w2):
    B, S, D = x.shape
    T = B * S
    assert S % ROW_TILE == 0 and T % DISPATCH_TILE == 0 and S % QUERY_TILE == 0 and QUERY_TILE % (2 * MOBA_BLOCK) == 0
    assert S // MOBA_BLOCK <= LANES - HEAD_DIM
    tiles_per_seq = S // ROW_TILE
    l = 0

    mod = _ada(c, w_ada[l], b_ada[l])
    sh1, sc1, ga1, sh2, sc2, ga2 = [m.reshape(B, 1, D) for m in jnp.split(mod, N_MOD, axis=-1)]

    x2 = x.reshape(T, D)
    z2 = _inproj(x2, g_norm1[l].reshape(1, D), sc1, sh1, w_in[l].astype(BF16), S)

    cosf, sinf, cost, sint = _rope_tables(S)
    rep = LANES // HEAD_DIM
    gq_cols = jnp.broadcast_to(jnp.tile(g_q[l], rep).reshape(LANES, 1), (LANES, QUERY_TILE))
    ya = _attention(z2.reshape(B, S, -1), cosf, sinf, cost, sint, gq_cols, jnp.tile(g_k[l], rep).reshape(1, LANES))

    wr = jnp.zeros((D, LANES), F32).at[:, :N_GROUPS].set(w_rg[l]).at[:, N_GROUPS:N_GROUPS + N_EXPERTS].set(w_re[l])
    br = jnp.zeros((1, LANES), F32).at[0, :N_GROUPS].set(b_rg[l]).at[0, N_GROUPS:N_GROUPS + N_EXPERTS].set(b_re[l])
    wr_hi = wr.astype(BF16)
    wr2 = jnp.concatenate([wr_hi, (wr - wr_hi.astype(F32)).astype(BF16)], axis=1)
    tri = (lax.broadcasted_iota(jnp.int32, (POST_CHUNK, POST_CHUNK), 1)
           < lax.broadcasted_iota(jnp.int32, (POST_CHUNK, POST_CHUNK), 0)).astype(BF16)
    x1, h2, ri, rf, cnt = _post(x2, ya.reshape(T, ATTN_WIDTH), z2, ga1, sc2, sh2,
                                conv_w[l], conv_b[l].reshape(1, CONV_WIDTH),
                                w_pa[l].astype(BF16), w_pb[l].astype(BF16), w_o[l].astype(BF16),
                                g_norm2[l].reshape(1, D), wr2, br, tri, tiles_per_seq)

    rb = EXPERT_ROWS
    counts = cnt[0, :N_EXPERTS].astype(jnp.int32)
    padded = (counts + rb - 1) // rb * rb
    pad_end = jnp.cumsum(padded)
    pad_start = pad_end - padded
    dest = _slots(pad_start.astype(jnp.int32), ri, _slabs(D // 2))
    dest0, dest1 = dest[0], dest[1]
    n_blocks = -(-T * TOPK_IN_GROUP // rb) + N_EXPERTS
    n_pad = n_blocks * rb
    n_valid = (pad_end[-1] // rb).astype(jnp.int32)
    blk_start = jnp.minimum(jnp.arange(n_blocks, dtype=jnp.int32), n_valid - 1) * rb
    blk_expert = jnp.sum(pad_end[None, :] <= blk_start[:, None], axis=-1).astype(jnp.int32)
    unused = n_valid + jnp.arange(N_EXPERTS, dtype=jnp.int32)
    tail_blk = jnp.concatenate([jnp.where(padded > 0, pad_end // rb - 1, -1),
                                jnp.where(unused < n_blocks, unused, -1)]).astype(jnp.int32)

    xs = _dispatch(dest0, dest1, tail_blk, h2, n_pad, _slabs(D // 2))
    ys = _experts(blk_expert, n_valid.reshape(1), xs, w1[l].astype(BF16), w3[l].astype(BF16), w2[l].astype(BF16))
    out = _combine(dest0, dest1, x1, rf, ga2, ys, tiles_per_seq)
    return out.reshape(B, S, D)
```

```python
import functools

import jax
import jax.numpy as jnp
from jax import lax
from jax.experimental import pallas as pl
from jax.experimental.pallas import tpu as pltpu

F32 = jnp.float32
BF16 = jnp.bfloat16
HIGHEST = lax.Precision.HIGHEST

N_HEADS = 8
HEAD_DIM = 64
ATTN_WIDTH = N_HEADS * HEAD_DIM
CONV_WIDTH = 512
CONV_K = 3
MOBA_BLOCK = 256
MOBA_TOPK = 3
ROPE_THETA = 500000.0
ROT_DIM = HEAD_DIM // 4
N_GROUPS = 4
EXPERTS_PER_GROUP = 8
N_EXPERTS = N_GROUPS * EXPERTS_PER_GROUP
TOPK_IN_GROUP = 2
N_MOD = 6
EPS = 1e-6

LANES = 128
NEG = -1e30
ROW_TILE = 512
INPROJ_TILE = 1024
DISPATCH_TILE = 2048
POST_CHUNK = 256
QUERY_TILE = 512
DMA_UNROLL = 8
EXPERT_ROWS = 512
VMEM_LIMIT = 56 * 1024 * 1024
ATTN_VMEM_LIMIT = 60 * 1024 * 1024


def _cparams(*sem):
    return pltpu.CompilerParams(dimension_semantics=sem, vmem_limit_bytes=VMEM_LIMIT)


def _ada_kernel(c_ref, w_ref, b_ref, o_ref):
    c = c_ref[...]
    a = c * jax.nn.sigmoid(c)
    o_ref[...] = jnp.dot(a, w_ref[...], preferred_element_type=F32, precision=HIGHEST) + b_ref[...]


def _ada(c, w_ada, b_ada):
    B, D = c.shape
    N = w_ada.shape[1]
    tn = 1536
    return pl.pallas_call(
        _ada_kernel,
        grid=(N // tn,),
        in_specs=[pl.BlockSpec((B, D), lambda j: (0, 0)),
                  pl.BlockSpec((D, tn), lambda j: (0, j)),
                  pl.BlockSpec((1, tn), lambda j: (0, j))],
        out_specs=pl.BlockSpec((B, tn), lambda j: (0, j)),
        out_shape=jax.ShapeDtypeStruct((B, N), F32),
        compiler_params=_cparams("arbitrary"),
        name="ada",
    )(c, w_ada, b_ada.reshape(1, N))


def _inproj_kernel(x_ref, g_ref, sc_ref, sh_ref, w_ref, z_ref, *, n_chunk):
    x = x_ref[...]
    ms = jnp.mean(x * x, axis=-1, keepdims=True)
    y = x * lax.rsqrt(ms + EPS) * g_ref[...]
    h = (y * (1.0 + sc_ref[0]) + sh_ref[0]).astype(BF16)
    for n in range(0, z_ref.shape[1], n_chunk):
        z_ref[:, n:n + n_chunk] = jnp.dot(h, w_ref[:, n:n + n_chunk],
                                          preferred_element_type=F32).astype(BF16)


def _inproj(x2, g1, sc1, sh1, w_in_bf, seq_len):
    T, D = x2.shape
    N = w_in_bf.shape[1]
    tm = INPROJ_TILE
    assert seq_len % tm == 0
    bmap = lambda i: (i // (seq_len // tm), 0, 0)
    return pl.pallas_call(
        functools.partial(_inproj_kernel, n_chunk=512),
        grid=(T // tm,),
        in_specs=[pl.BlockSpec((tm, D), lambda i: (i, 0)),
                  pl.BlockSpec((1, D), lambda i: (0, 0)),
                  pl.BlockSpec((1, 1, D), bmap),
                  pl.BlockSpec((1, 1, D), bmap),
                  pl.BlockSpec((D, N), lambda i: (0, 0), pipeline_mode=pl.Buffered(1))],
        out_specs=pl.BlockSpec((tm, N), lambda i: (i, 0)),
        out_shape=jax.ShapeDtypeStruct((T, N), BF16),
        compiler_params=_cparams("arbitrary"),
        name="inproj",
    )(x2, g1, sc1, sh1, w_in_bf)


def _fold_rows(x, op):
    parts = [x[r:r + 8] for r in range(0, x.shape[0], 8)]
    while len(parts) > 1:
        parts = [op(parts[i], parts[i + 1]) for i in range(0, len(parts) - 1, 2)] + (
            [parts[-1]] if len(parts) % 2 else [])
    return parts[0]


def _attn_kernel(q_ref, k_ref, v_ref, cos_ref, sin_ref, cost_ref, sint_ref, gq_ref, gk_ref, o_ref,
                 kaug_ref, vt_ref, kmp_ref, kst_ref, s_ref, mcol_ref, qa_ref):
    S = q_ref.shape[1]
    blk = MOBA_BLOCK
    qt = QUERY_TILE
    sub = qt // blk
    nb = S // blk
    nq = S // qt
    nbp = kst_ref.shape[0] // 6
    hd = HEAD_DIM

    half = ROT_DIM // 2
    lane_r = lax.broadcasted_iota(jnp.int32, (blk, LANES), 1)
    rot_lo = (lane_r & (hd - 1)) < half
    same_head = jnp.where((lax.broadcasted_iota(jnp.int32, (LANES, LANES), 0) < hd)
                          == (lax.broadcasted_iota(jnp.int32, (LANES, LANES), 1) < hd), 1.0, 0.0).astype(BF16)

    def norm_rope_keys(xb, r0):
        x = xb.astype(F32)
        sq = x * x
        sq_hi = sq.astype(BF16)
        sq_lo = (sq - sq_hi.astype(F32)).astype(BF16)
        ssq = (jnp.dot(sq_hi, same_head, preferred_element_type=F32)
               + jnp.dot(sq_lo, same_head, preferred_element_type=F32))
        y = x * lax.rsqrt(ssq * (1.0 / hd) + EPS) * gk_ref[...]
        rot = jnp.where(rot_lo, pltpu.roll(y, LANES - half, 1), pltpu.roll(y, half, 1))
        return y * cos_ref[pl.ds(r0, blk), :] + rot * sin_ref[pl.ds(r0, blk), :]

    def norm_rope_queries_t(xb, r0):
        xT = xb.astype(F32).T
        sq = xT * xT
        cos = cost_ref[:, pl.ds(r0, qt)]
        sin = sint_ref[:, pl.ds(r0, qt)]
        rows = []
        for h in range(2):
            lo, hi = h * hd, (h + 1) * hd
            ssq = jnp.sum(_fold_rows(sq[lo:hi], jnp.add), axis=0, keepdims=True)
            y = xT[lo:hi] * lax.rsqrt(ssq * (1.0 / hd) + EPS) * gq_ref[lo:hi, :]
            y1, y2 = y[0:half], y[half:2 * half]
            rows += [y1 * cos - y2 * sin, y2 * cos + y1 * sin, y[2 * half:]]
        return jnp.concatenate(rows, axis=0)

    kmp_ref[...] = jnp.zeros_like(kmp_ref)
    ones_row = jnp.where(lax.broadcasted_iota(jnp.int32, (16, blk), 0) == 0, 1.0, 0.0).astype(BF16)
    lane_k = lax.broadcasted_iota(jnp.int32, (blk, LANES), 1)
    head0_k = lane_k < HEAD_DIM
    lane_m = lax.broadcasted_iota(jnp.int32, (nbp, LANES), 1)

    def prepare_keys(t):
        for u in range(sub):
            j = t * sub + u
            r0 = pl.multiple_of(jnp.minimum(j, nb - 1) * blk, blk)
            kr = norm_rope_keys(k_ref[0, pl.ds(r0, blk), :], r0)
            kmp_ref[pl.ds(j, 1), :] = jnp.sum(kr, axis=0, keepdims=True) * (1.0 / blk)
            kaug_ref[0, j] = jnp.where(head0_k, kr, jnp.where(lane_k - hd == j, 1.0, 0.0)).astype(BF16)
            kaug_ref[1, j] = jnp.where(head0_k, jnp.where(lane_k == j, 1.0, 0.0), kr).astype(BF16)
            vT = v_ref[0, pl.ds(r0, blk), :].astype(F32).T
            for h in range(2):
                vt_ref[h, j, 0:HEAD_DIM, :] = vT[h * HEAD_DIM:(h + 1) * HEAD_DIM].astype(BF16)
                vt_ref[h, j, HEAD_DIM:HEAD_DIM + 16, :] = ones_row
        kmp = kmp_ref[0:nbp, :]
        k_hi = kmp.astype(BF16)
        k_lo = (kmp - k_hi.astype(F32)).astype(BF16)
        zero = jnp.zeros((nbp, LANES), BF16)
        parts = []
        for h in range(2):
            mine = (lane_m < hd) if h == 0 else (lane_m >= hd)
            parts += [jnp.where(mine, k_hi, zero), jnp.where(mine, k_lo, zero)]
        parts += [parts[0], parts[2]]
        for n, part in enumerate(parts):
            kst_ref[n * nbp:(n + 1) * nbp, :] = part

    prepare_keys(0)

    key_i = lax.broadcasted_iota(jnp.int32, (blk, qt), 0)
    qry_i = lax.broadcasted_iota(jnp.int32, (blk, qt), 1)
    causal = [(qry_i < u * blk) | (qry_i >= (u + 1) * blk) | (key_i <= qry_i - u * blk) for u in range(sub)]
    rowf = lax.broadcasted_iota(jnp.int32, (nbp, qt), 0).astype(F32)
    subf = (lax.broadcasted_iota(jnp.int32, (nbp, qt), 1) // blk).astype(F32)
    q_scale = (hd ** -0.5) * 1.4426950408889634

    def query_operands(t):
        r0 = pl.multiple_of(t * qt, qt)
        qT = norm_rope_queries_t(q_ref[0, pl.ds(r0, qt), :], r0)
        cur = lax.convert_element_type(t * sub, F32) + subf
        q_hi = qT.astype(BF16)
        q_lo = (qT - q_hi.astype(F32)).astype(BF16)
        g1 = jnp.dot(kst_ref[0:4 * nbp, :], q_hi, preferred_element_type=F32)
        g2 = jnp.dot(kst_ref[4 * nbp:6 * nbp, :], q_lo, preferred_element_type=F32)
        qa = []
        for h in range(2):
            gate = g1[2 * h * nbp:(2 * h + 1) * nbp] + g1[(2 * h + 1) * nbp:(2 * h + 2) * nbp] + g2[h * nbp:(h + 1) * nbp]
            g = jnp.where(rowf < cur, gate, -jnp.inf)
            keep = rowf == cur
            for r in range(MOBA_TOPK):
                m = jnp.max(g, axis=0, keepdims=True)
                idx = jnp.min(jnp.where(g == m, rowf, 1e9), axis=0, keepdims=True)
                pick = (rowf == idx) & (cur > r)
                keep = keep | pick
                g = jnp.where(pick, -jnp.inf, g)
            bias = jnp.where(keep, 0.0, NEG)
            qs = qT[h * hd:(h + 1) * hd] * q_scale
            pad = jnp.zeros((LANES - hd - nbp, qt), F32)
            pieces = [qs, bias, pad] if h == 0 else [bias, pad, qs]
            qa.append(jnp.concatenate(pieces, axis=0).astype(BF16))
        return qa

    def pass1_tile(par, h, j, qa_h, mask):
        sT = jnp.dot(kaug_ref[h, j], qa_h, preferred_element_type=F32)
        if mask is not None:
            sT = jnp.where(mask, sT, NEG)
        s_ref[par, h, j] = sT
        return _fold_rows(sT, jnp.maximum)

    def pass2_tile(par, h, j):
        pT = jnp.exp2(s_ref[par, h, j] - mcol_ref[h, 0:1, :]).astype(BF16)
        return jnp.dot(vt_ref[h, j], pT, preferred_element_type=F32)

    def pass1_own(t, par, qa):
        mx = []
        for h in range(2):
            f = [pass1_tile(par, h, t * sub + u, qa[h], causal[u]) for u in range(sub)]
            mx.append(functools.reduce(jnp.maximum, f))
        return mx

    def pass2_own(t, par):
        return [sum(pass2_tile(par, h, t * sub + u) for u in range(sub)) for h in range(2)]

    def pass1_pair(p, par, qa, mx):
        return [functools.reduce(jnp.maximum, [mx[h]] + [pass1_tile(par, h, 2 * p + u, qa[h], None) for u in range(2)])
                for h in range(2)]

    def pass2_pair(p, par, acc):
        return [acc[h] + sum(pass2_tile(par, h, 2 * p + u) for u in range(2)) for h in range(2)]

    def finish_pass1(mx):
        for h in range(2):
            mcol_ref[h] = jnp.broadcast_to(jnp.max(mx[h], axis=0, keepdims=True), mcol_ref.shape[1:])

    def finish_pass2(t, acc):
        outT = jnp.concatenate([acc[h][0:hd] / acc[h][hd:hd + 1] for h in range(2)], axis=0)
        o_ref[0, pl.ds(pl.multiple_of(t * qt, qt), qt), :] = outT.T.astype(BF16)

    def stage(t, par):

        def prepare_next():
            prepare_keys(t + 1)
            nxt = query_operands(jnp.minimum(t + 1, nq - 1))
            for h in range(2):
                qa_ref[1 - par, h] = nxt[h]

        @pl.when(t == 0)
        def _():
            finish_pass1(pass1_own(t, par, query_operands(t)))
            prepare_next()

        @pl.when((t > 0) & (t < nq))
        def _():
            qa = [qa_ref[par, h] for h in range(2)]
            mx = pass1_own(t, par, qa)
            acc = pass2_own(t - 1, 1 - par)
            prepare_next()
            n_prev = (t - 1) * sub // 2

            def both(p, c):
                mx, acc = c
                return tuple(pass1_pair(p, par, qa, mx)), tuple(pass2_pair(p, 1 - par, acc))

            def both_twice(p2, c):
                return both(2 * p2 + 1, both(2 * p2, c))

            c = lax.fori_loop(0, n_prev // 2, both_twice, (tuple(mx), tuple(acc)))
            mx, acc = lax.fori_loop(n_prev // 2 * 2, n_prev, both, c)
            mx = lax.fori_loop(n_prev, t * sub // 2, lambda p, m: tuple(pass1_pair(p, par, qa, m)), tuple(mx))
            finish_pass2(t - 1, acc)
            finish_pass1(mx)

        @pl.when(t == nq)
        def _():
            acc = pass2_own(t - 1, 1 - par)
            acc = lax.fori_loop(0, (t - 1) * sub // 2, lambda p, a: tuple(pass2_pair(p, 1 - par, a)), tuple(acc))
            finish_pass2(t - 1, acc)

    def stage_pair(tt, carry):
        stage(2 * tt, 0)
        stage(2 * tt + 1, 1)
        return carry

    lax.fori_loop(0, (nq + 2) // 2, stage_pair, 0)


def _attention(z3, cosf, sinf, cost, sint, gq_cols, gk2):
    B, S, _ = z3.shape
    n_pair = N_HEADS // 2
    kq = ATTN_WIDTH // LANES
    nb = S // MOBA_BLOCK
    nbp = -(-nb // 16) * 16
    sub = QUERY_TILE // MOBA_BLOCK
    assert sub <= 8
    return pl.pallas_call(
        _attn_kernel,
        grid=(B, n_pair),
        in_specs=[pl.BlockSpec((1, S, LANES), lambda b, p: (b, 0, p)),
                  pl.BlockSpec((1, S, LANES), lambda b, p: (b, 0, kq + p)),
                  pl.BlockSpec((1, S, LANES), lambda b, p: (b, 0, 2 * kq + p)),
                  pl.BlockSpec((S, LANES), lambda b, p: (0, 0), pipeline_mode=pl.Buffered(1)),
                  pl.BlockSpec((S, LANES), lambda b, p: (0, 0), pipeline_mode=pl.Buffered(1)),
                  pl.BlockSpec(cost.shape, lambda b, p: (0, 0)),
                  pl.BlockSpec(sint.shape, lambda b, p: (0, 0)),
                  pl.BlockSpec((LANES, QUERY_TILE), lambda b, p: (0, 0)),
                  pl.BlockSpec((1, LANES), lambda b, p: (0, 0))],
        out_specs=pl.BlockSpec((1, S, LANES), lambda b, p: (b, 0, p)),
        out_shape=jax.ShapeDtypeStruct((B, S, ATTN_WIDTH), BF16),
        scratch_shapes=[pltpu.VMEM((2, nb + sub, MOBA_BLOCK, LANES), BF16),
                        pltpu.VMEM((2, nb + sub, HEAD_DIM + 16, MOBA_BLOCK), BF16),
                        pltpu.VMEM((nbp + 8, LANES), F32),
                        pltpu.VMEM((6 * nbp, LANES), BF16),
                        pltpu.VMEM((2, 2, nb, MOBA_BLOCK, QUERY_TILE), F32),
                        pltpu.VMEM((2, 8, QUERY_TILE), F32),
                        pltpu.VMEM((2, 2, LANES, QUERY_TILE), BF16)],
        compiler_params=pltpu.CompilerParams(dimension_semantics=("arbitrary", "arbitrary"),
                                             vmem_limit_bytes=ATTN_VMEM_LIMIT),
        name="attn",
    )(z3, z3, z3, cosf, sinf, cost, sint, gq_cols, gk2)


def _slabs(width):
    return width // LANES


def _load_slabs(ref, row0, rows, n_slab, lead=()):
    return jnp.concatenate([ref[lead + (pl.ds(row0 * n_slab + s, rows, stride=n_slab), slice(None))]
                            for s in range(n_slab)], axis=1)


def _store_slabs(ref, row0, val):
    rows, width = val.shape
    n_slab = _slabs(width)
    for s in range(n_slab):
        ref[pl.ds(row0 * n_slab + s, rows, stride=n_slab), :] = val[:, s * LANES:(s + 1) * LANES]


def _post_kernel(x_ref, ya_ref, xb_ref, bg_ref, cg_ref, gta_ref, gtb_ref, ga1_ref, sc2_ref, sh2_ref,
                 cw_ref, cb_ref, wpa_ref, wpb_ref, wo_ref, g2_ref, wr_ref, br_ref, tri_ref,
                 x1_ref, h2_ref, ri_ref, rf_ref, cnt_ref, ubuf_ref, run_ref, *, tiles_per_seq):
    i = pl.program_id(0)
    tm = x_ref.shape[0]
    rc = POST_CHUNK
    halo = 8

    @pl.when(i == 0)
    def _():
        run_ref[...] = jnp.zeros_like(run_ref)

    @pl.when(i % tiles_per_seq == 0)
    def _():
        ubuf_ref[0:halo, :] = jnp.zeros((halo, CONV_WIDTH), F32)

    ubuf_ref[halo:halo + tm, :] = cg_ref[...].astype(F32) * xb_ref[...].astype(F32)
    cw = cw_ref[...]
    lanef = lax.broadcasted_iota(jnp.int32, (rc, LANES), 1).astype(F32)
    half_d = x_ref.shape[1] // 2
    run = run_ref[0:1, :]

    for c in range(tm // rc):
        rows = pl.ds(c * rc, rc)
        conv = (cw[0:1, :] * ubuf_ref[pl.ds(halo - 2 + c * rc, rc), :]
                + cw[1:2, :] * ubuf_ref[pl.ds(halo - 1 + c * rc, rc), :]
                + cw[2:3, :] * ubuf_ref[pl.ds(halo + c * rc, rc), :])
        y_b = (bg_ref[rows, :].astype(F32) * (conv + cb_ref[...])).astype(BF16)
        pa = jnp.dot(ya_ref[rows, :], wpa_ref[...], preferred_element_type=F32)
        pb = jnp.dot(y_b, wpb_ref[...], preferred_element_type=F32)
        merged = (jax.nn.sigmoid(gta_ref[rows, :].astype(F32)) * pa
                  + jax.nn.sigmoid(gtb_ref[rows, :].astype(F32)) * pb).astype(BF16)
        x1 = x_ref[rows, :] + ga1_ref[0] * jnp.dot(merged, wo_ref[...], preferred_element_type=F32)
        x1_ref[rows, :] = x1

        ms = jnp.mean(x1 * x1, axis=-1, keepdims=True)
        h2 = x1 * lax.rsqrt(ms + EPS) * g2_ref[...]
        h2 = h2 * (1.0 + sc2_ref[0]) + sh2_ref[0]
        h_hi = h2.astype(BF16)
        h_hi32 = h_hi.astype(F32)
        bits = pltpu.bitcast(h_hi32, jnp.uint32)
        _store_slabs(h2_ref, c * rc, (bits[:, :half_d] >> 16) | bits[:, half_d:])

        h_lo = (h2 - h_hi32).astype(BF16)
        r = jnp.dot(h_hi, wr_ref[...], preferred_element_type=F32)
        logit = (r[:, :LANES] + r[:, LANES:]
                 + jnp.dot(h_lo, wr_ref[:, :LANES], preferred_element_type=F32) + br_ref[...])
        gl = jnp.where(lanef < N_GROUPS, logit, -jnp.inf)
        gmax = jnp.max(gl, axis=-1, keepdims=True)
        g_idx = jnp.min(jnp.where(gl == gmax, lanef, 1e9), axis=-1, keepdims=True)
        g_w = 1.0 / jnp.sum(jnp.exp(gl - gmax), axis=-1, keepdims=True)
        e_lo = N_GROUPS + EXPERTS_PER_GROUP * g_idx
        el = jnp.where((lanef >= e_lo) & (lanef < e_lo + EXPERTS_PER_GROUP), logit, -jnp.inf)
        v0 = jnp.max(el, axis=-1, keepdims=True)
        i0 = jnp.min(jnp.where(el == v0, lanef, 1e9), axis=-1, keepdims=True)
        el = jnp.where(lanef == i0, -jnp.inf, el)
        v1 = jnp.max(el, axis=-1, keepdims=True)
        i1 = jnp.min(jnp.where(el == v1, lanef, 1e9), axis=-1, keepdims=True)
        t = jnp.exp(v1 - v0)
        w0 = g_w / (1.0 + t)
        w1 = g_w * t / (1.0 + t)
        e0 = i0 - N_GROUPS
        e1 = i1 - N_GROUPS

        oh0 = lanef == e0
        oh1 = lanef == e1
        oh = jnp.where(oh0 | oh1, 1.0, 0.0)
        before = jnp.dot(tri_ref[...], oh.astype(BF16), preferred_element_type=F32) + run
        r0 = jnp.sum(jnp.where(oh0, before, 0.0), axis=-1, keepdims=True)
        r1 = jnp.sum(jnp.where(oh1, before, 0.0), axis=-1, keepdims=True)
        run = run + jnp.sum(oh, axis=0, keepdims=True)

        ri = jnp.where(lanef == 0, e0, jnp.where(lanef == 1, e1, jnp.where(lanef == 2, r0, jnp.where(lanef == 3, r1, 0.0))))
        ri_ref[:, rows] = ri.astype(jnp.int32).T[0:8]
        rf_ref[rows, :] = jnp.where(lanef == 0, w0, jnp.where(lanef == 1, w1, 0.0))

    ubuf_ref[0:halo, :] = ubuf_ref[tm:tm + halo, :]
    run_ref[...] = jnp.broadcast_to(run, run_ref.shape)
    cnt_ref[...] = jnp.broadcast_to(run, cnt_ref.shape)


def _post(x2, ya2, z2, ga1, sc2, sh2, conv_w, conv_b, wpa, wpb, wo, g2, wr, br, tri, tiles_per_seq):
    T, D = x2.shape
    tm = ROW_TILE
    cw = CONV_WIDTH
    xcol = 3 * ATTN_WIDTH // cw
    gcol = (3 * ATTN_WIDTH + 3 * cw) // D
    bmap = lambda i: (i // tiles_per_seq, 0, 0)
    const = lambda i: (0, 0)
    return pl.pallas_call(
        functools.partial(_post_kernel, tiles_per_seq=tiles_per_seq),
        grid=(T // tm,),
        in_specs=[pl.BlockSpec((tm, D), lambda i: (i, 0)),
                  pl.BlockSpec((tm, ATTN_WIDTH), lambda i: (i, 0)),
                  pl.BlockSpec((tm, cw), lambda i: (i, xcol)),
                  pl.BlockSpec((tm, cw), lambda i: (i, xcol + 1)),
                  pl.BlockSpec((tm, cw), lambda i: (i, xcol + 2)),
                  pl.BlockSpec((tm, D), lambda i: (i, gcol)),
                  pl.BlockSpec((tm, D), lambda i: (i, gcol + 1)),
                  pl.BlockSpec((1, 1, D), bmap),
                  pl.BlockSpec((1, 1, D), bmap),
                  pl.BlockSpec((1, 1, D), bmap),
                  pl.BlockSpec((CONV_K, cw), const),
                  pl.BlockSpec((1, cw), const),
                  pl.BlockSpec((ATTN_WIDTH, D), const),
                  pl.BlockSpec((cw, D), const),
                  pl.BlockSpec((D, D), const),
                  pl.BlockSpec((1, D), const),
                  pl.BlockSpec((D, 2 * LANES), const),
                  pl.BlockSpec((1, LANES), const),
                  pl.BlockSpec((POST_CHUNK, POST_CHUNK), const)],
        out_specs=[pl.BlockSpec((tm, D), lambda i: (i, 0)),
                   pl.BlockSpec((tm * _slabs(D // 2), LANES), lambda i: (i, 0)),
                   pl.BlockSpec((8, tm), lambda i: (0, i)),
                   pl.BlockSpec((tm, LANES), lambda i: (i, 0)),
                   pl.BlockSpec((8, LANES), const)],
        out_shape=[jax.ShapeDtypeStruct((T, D), F32),
                   jax.ShapeDtypeStruct((T * _slabs(D // 2), LANES), jnp.uint32),
                   jax.ShapeDtypeStruct((8, T), jnp.int32),
                   jax.ShapeDtypeStruct((T, LANES), F32),
                   jax.ShapeDtypeStruct((8, LANES), F32)],
        scratch_shapes=[pltpu.VMEM((tm + 16, cw), F32),
                        pltpu.VMEM((8, LANES), F32)],
        compiler_params=_cparams("arbitrary"),
        name="post",
    )(x2, ya2, z2, z2, z2, z2, z2, ga1, sc2, sh2, conv_w, conv_b, wpa, wpb, wo, g2, wr, br, tri)


def _slots_kernel(ps_ref, ri_ref, d_ref, *, n_slab):
    e = ri_ref[0:TOPK_IN_GROUP, :]
    start = jnp.zeros(e.shape, jnp.int32)
    for k in range(N_EXPERTS):
        start = jnp.where(e == k, ps_ref[k], start)
    d_ref[...] = (start + ri_ref[TOPK_IN_GROUP:2 * TOPK_IN_GROUP, :]) * n_slab


def _slots(pad_start, riT, n_slab):
    T = riT.shape[1]
    grid_spec = pltpu.PrefetchScalarGridSpec(
        num_scalar_prefetch=1,
        grid=(1,),
        in_specs=[pl.BlockSpec(riT.shape, lambda i, ps: (0, 0))],
        out_specs=pl.BlockSpec((TOPK_IN_GROUP, T), lambda i, ps: (0, 0)),
    )
    return pl.pallas_call(
        functools.partial(_slots_kernel, n_slab=n_slab),
        grid_spec=grid_spec,
        out_shape=jax.ShapeDtypeStruct((TOPK_IN_GROUP, T), jnp.int32),
        compiler_params=_cparams("arbitrary"),
        name="slots",
    )(pad_start, riT)


def _dispatch_kernel(d0_ref, d1_ref, tail_ref, h_ref, xs_ref, zero_ref, sem, zsem, *, n_slab):
    tm = h_ref.shape[0] // n_slab
    base = pl.program_id(0) * tm
    blk_rows = zero_ref.shape[0]

    @pl.when(pl.program_id(0) == 0)
    def _():
        zero_ref[...] = jnp.zeros_like(zero_ref)

        def tail_copy(e):
            start = pl.multiple_of(tail_ref[e] * blk_rows, blk_rows)
            return pltpu.make_async_copy(zero_ref, xs_ref.at[pl.ds(start, blk_rows), :], zsem)

        for e in range(tail_ref.shape[0]):
            @pl.when(tail_ref[e] >= 0)
            def _():
                tail_copy(e).start()
        for e in range(tail_ref.shape[0]):
            @pl.when(tail_ref[e] >= 0)
            def _():
                tail_copy(e).wait()

    def body(g, carry):
        r8 = pl.multiple_of(g * DMA_UNROLL, DMA_UNROLL)
        for u in range(DMA_UNROLL):
            src = h_ref.at[pl.ds((r8 + u) * n_slab, n_slab), :]
            pltpu.make_async_copy(src, xs_ref.at[pl.ds(d0_ref[base + r8 + u], n_slab), :], sem).start(priority=0)
            pltpu.make_async_copy(src, xs_ref.at[pl.ds(d1_ref[base + r8 + u], n_slab), :], sem).start(priority=1)
        return carry

    lax.fori_loop(0, tm // DMA_UNROLL, body, 0)
    for _ in range(TOPK_IN_GROUP):
        pltpu.make_async_copy(h_ref, xs_ref.at[pl.ds(0, tm * n_slab), :], sem).wait()


def _dispatch(dest0, dest1, tail_blk, h2p, n_pad, n_slab):
    tm = DISPATCH_TILE
    T = h2p.shape[0] // n_slab
    grid_spec = pltpu.PrefetchScalarGridSpec(
        num_scalar_prefetch=3,
        grid=(T // tm,),
        in_specs=[pl.BlockSpec((tm * n_slab, LANES), lambda i, d0, d1, tb: (i, 0))],
        out_specs=pl.BlockSpec(memory_space=pl.ANY),
        scratch_shapes=[pltpu.VMEM((EXPERT_ROWS * n_slab, LANES), h2p.dtype),
                        pltpu.SemaphoreType.DMA(()),
                        pltpu.SemaphoreType.DMA(())],
    )
    return pl.pallas_call(
        functools.partial(_dispatch_kernel, n_slab=n_slab),
        grid_spec=grid_spec,
        out_shape=jax.ShapeDtypeStruct((n_pad * n_slab, LANES), h2p.dtype),
        compiler_params=_cparams("arbitrary"),
        name="dispatch",
    )(dest0, dest1, tail_blk, h2p)


def _expert_kernel(be_ref, nv_ref, xs_ref, w1f_ref, w3f_ref, w2f_ref, ys_ref, w1_ref, w3_ref, w2_ref):
    i = pl.program_id(0)

    @pl.when((i == 0) | (be_ref[i] != be_ref[jnp.maximum(i - 1, 0)]))
    def _():
        w1_ref[...] = w1f_ref[...].astype(BF16)
        w3_ref[...] = w3f_ref[...].astype(BF16)
        w2_ref[...] = w2f_ref[...].astype(BF16)

    @pl.when(i >= nv_ref[0])
    def _():
        ys_ref[...] = jnp.zeros_like(ys_ref)

    @pl.when(i < nv_ref[0])
    def _():
        half_d = w1_ref.shape[1] // 2
        xp = _load_slabs(xs_ref, 0, EXPERT_ROWS, _slabs(half_d))
        x_lo = pltpu.bitcast(xp << 16, F32).astype(BF16)
        x_hi = pltpu.bitcast(xp & jnp.uint32(0xFFFF0000), F32).astype(BF16)
        a = (jnp.dot(x_lo, w1_ref[0, :half_d], preferred_element_type=F32)
             + jnp.dot(x_hi, w1_ref[0, half_d:], preferred_element_type=F32))
        b = (jnp.dot(x_lo, w3_ref[0, :half_d], preferred_element_type=F32)
             + jnp.dot(x_hi, w3_ref[0, half_d:], preferred_element_type=F32))
        hid = (a * jax.nn.sigmoid(a) * b).astype(BF16)
        y = jnp.dot(hid, w2_ref[0], preferred_element_type=F32)
        bits = pltpu.bitcast(y.astype(BF16).astype(F32), jnp.uint32)
        _store_slabs(ys_ref, 0, (bits[:, :half_d] >> 16) | bits[:, half_d:])


def _experts(blk_expert, n_valid, xs, w1f, w3f, w2f):
    _, D, F = w1f.shape
    rb = EXPERT_ROWS
    blk_rows = rb * _slabs(D // 2)
    row_blk = lambda i, be, nv: (jnp.minimum(i, nv[0] - 1), 0)
    out_blk = lambda i, be, nv: (i, 0)
    wgt_blk = lambda i, be, nv: (be[i], 0, 0)
    grid_spec = pltpu.PrefetchScalarGridSpec(
        num_scalar_prefetch=2,
        grid=(xs.shape[0] // blk_rows,),
        in_specs=[pl.BlockSpec((blk_rows, LANES), row_blk),
                  pl.BlockSpec((1, D, F), wgt_blk),
                  pl.BlockSpec((1, D, F), wgt_blk),
                  pl.BlockSpec((1, F, D), wgt_blk)],
        out_specs=pl.BlockSpec((blk_rows, LANES), out_blk),
        scratch_shapes=[pltpu.VMEM((1, D, F), BF16), pltpu.VMEM((1, D, F), BF16), pltpu.VMEM((1, F, D), BF16)],
    )
    return pl.pallas_call(
        _expert_kernel,
        grid_spec=grid_spec,
        out_shape=jax.ShapeDtypeStruct(xs.shape, jnp.uint32),
        compiler_params=_cparams("arbitrary"),
        name="experts",
    )(blk_expert, n_valid, xs, w1f, w3f, w2f)


def _combine_kernel(d0_ref, d1_ref, x1_ref, rf_ref, ga2_ref, ys_ref, o_ref, buf_ref, sem, *, n_slab):
    i = pl.program_id(0)
    n = pl.num_programs(0)
    tm = x1_ref.shape[0]

    def gather(step, slot):
        base = step * tm

        def body(g, carry):
            r8 = pl.multiple_of(g * DMA_UNROLL, DMA_UNROLL)
            for u in range(DMA_UNROLL):
                dst = pl.ds((r8 + u) * n_slab, n_slab)
                pltpu.make_async_copy(ys_ref.at[pl.ds(d0_ref[base + r8 + u], n_slab), :],
                                      buf_ref.at[slot, 0, dst, :], sem.at[slot]).start(priority=0)
                pltpu.make_async_copy(ys_ref.at[pl.ds(d1_ref[base + r8 + u], n_slab), :],
                                      buf_ref.at[slot, 1, dst, :], sem.at[slot]).start(priority=1)
            return carry

        lax.fori_loop(0, tm // DMA_UNROLL, body, 0)

    @pl.when(i == 0)
    def _():
        gather(0, 0)

    @pl.when(i + 1 < n)
    def _():
        gather(i + 1, (i + 1) % 2)

    slot = i % 2
    for k in range(TOPK_IN_GROUP):
        pltpu.make_async_copy(ys_ref.at[pl.ds(0, tm * n_slab), :], buf_ref.at[slot, k], sem.at[slot]).wait()
    rf = rf_ref[...]
    w = [rf[:, k:k + 1] for k in range(TOPK_IN_GROUP)]
    yp = [_load_slabs(buf_ref, 0, tm, n_slab, lead=(slot, k)) for k in range(TOPK_IN_GROUP)]
    half_d = n_slab * LANES
    y_lo = sum(w[k] * pltpu.bitcast(yp[k] << 16, F32) for k in range(TOPK_IN_GROUP))
    y_hi = sum(w[k] * pltpu.bitcast(yp[k] & jnp.uint32(0xFFFF0000), F32) for k in range(TOPK_IN_GROUP))
    ga2 = ga2_ref[0]
    o_ref[:, :half_d] = x1_ref[:, :half_d] + ga2[:, :half_d] * y_lo
    o_ref[:, half_d:] = x1_ref[:, half_d:] + ga2[:, half_d:] * y_hi


def _combine(dest0, dest1, x1, rf, ga2, ys, tiles_per_seq):
    T, D = x1.shape
    n_slab = _slabs(D // 2)
    tm = ROW_TILE
    row = lambda i, d0, d1: (i, 0)
    grid_spec = pltpu.PrefetchScalarGridSpec(
        num_scalar_prefetch=2,
        grid=(T // tm,),
        in_specs=[pl.BlockSpec((tm, D), row),
                  pl.BlockSpec((tm, LANES), row),
                  pl.BlockSpec((1, 1, D), lambda i, d0, d1: (i // tiles_per_seq, 0, 0)),
                  pl.BlockSpec(memory_space=pl.ANY)],
        out_specs=pl.BlockSpec((tm, D), row),
        scratch_shapes=[pltpu.VMEM((2, TOPK_IN_GROUP, tm * n_slab, LANES), ys.dtype),
                        pltpu.SemaphoreType.DMA((2,))],
    )
    return pl.pallas_call(
        functools.partial(_combine_kernel, n_slab=n_slab),
        grid_spec=grid_spec,
        out_shape=jax.ShapeDtypeStruct((T, D), F32),
        compiler_params=_cparams("arbitrary"),
        name="combine",
    )(dest0, dest1, x1, rf, ga2, ys)


def _rope_tables(S):
    pos = jnp.arange(S, dtype=F32)
    inv_freq = ROPE_THETA ** (-jnp.arange(0, ROT_DIM, 2, dtype=F32) / ROT_DIM)
    ang = pos[:, None] * inv_freq[None, :]
    cos, sin = jnp.cos(ang), jnp.sin(ang)
    half = ROT_DIM // 2
    ones = jnp.ones((S, HEAD_DIM - ROT_DIM), F32)
    cos_h = jnp.concatenate([cos, cos, ones], axis=1)
    sin_h = jnp.concatenate([-sin, sin, 0.0 * ones], axis=1)
    return jnp.tile(cos_h, (1, LANES // HEAD_DIM)), jnp.tile(sin_h, (1, LANES // HEAD_DIM)), cos.T, sin.T


def kernel(x, c, w_ada, b_ada, g_norm1, g_norm2, w_in, g_q, g_k, conv_w, conv_b,
           w_pa, w_pb, w_o, w_rg, b_rg, w_re, b_re, w1, w3, w2):
    B, S, D = x.shape
    T = B * S
    assert S % ROW_TILE == 0 and T % DISPATCH_TILE == 0 and S % QUERY_TILE == 0 and QUERY_TILE % (2 * MOBA_BLOCK) == 0
    assert S // MOBA_BLOCK <= LANES - HEAD_DIM
    tiles_per_seq = S // ROW_TILE
    l = 0

    mod = _ada(c, w_ada[l], b_ada[l])
    sh1, sc1, ga1, sh2, sc2, ga2 = [m.reshape(B, 1, D) for m in jnp.split(mod, N_MOD, axis=-1)]

    x2 = x.reshape(T, D)
    z2 = _inproj(x2, g_norm1[l].reshape(1, D), sc1, sh1, w_in[l].astype(BF16), S)

    cosf, sinf, cost, sint = _rope_tables(S)
    rep = LANES // HEAD_DIM
    gq_cols = jnp.broadcast_to(jnp.tile(g_q[l], rep).reshape(LANES, 1), (LANES, QUERY_TILE))
    ya = _attention(z2.reshape(B, S, -1), cosf, sinf, cost, sint, gq_cols, jnp.tile(g_k[l], rep).reshape(1, LANES))

    wr = jnp.zeros((D, LANES), F32).at[:, :N_GROUPS].set(w_rg[l]).at[:, N_GROUPS:N_GROUPS + N_EXPERTS].set(w_re[l])
    br = jnp.zeros((1, LANES), F32).at[0, :N_GROUPS].set(b_rg[l]).at[0, N_GROUPS:N_GROUPS + N_EXPERTS].set(b_re[l])
    wr_hi = wr.astype(BF16)
    wr2 = jnp.concatenate([wr_hi, (wr - wr_hi.astype(F32)).astype(BF16)], axis=1)
    tri = (lax.broadcasted_iota(jnp.int32, (POST_CHUNK, POST_CHUNK), 1)
           < lax.broadcasted_iota(jnp.int32, (POST_CHUNK, POST_CHUNK), 0)).astype(BF16)
    x1, h2, ri, rf, cnt = _post(x2, ya.reshape(T, ATTN_WIDTH), z2, ga1, sc2, sh2,
                                conv_w[l], conv_b[l].reshape(1, CONV_WIDTH),
                                w_pa[l].astype(BF16), w_pb[l].astype(BF16), w_o[l].astype(BF16),
                                g_norm2[l].reshape(1, D), wr2, br, tri, tiles_per_seq)

    rb = EXPERT_ROWS
    counts = cnt[0, :N_EXPERTS].astype(jnp.int32)
    padded = (counts + rb - 1) // rb * rb
    pad_end = jnp.cumsum(padded)
    pad_start = pad_end - padded
    dest = _slots(pad_start.astype(jnp.int32), ri, _slabs(D // 2))
    dest0, dest1 = dest[0], dest[1]
    n_blocks = -(-T * TOPK_IN_GROUP // rb) + N_EXPERTS
    n_pad = n_blocks * rb
    n_valid = (pad_end[-1] // rb).astype(jnp.int32)
    blk_start = jnp.minimum(jnp.arange(n_blocks, dtype=jnp.int32), n_valid - 1) * rb
    blk_expert = jnp.sum(pad_end[None, :] <= blk_start[:, None], axis=-1).astype(jnp.int32)
    unused = n_valid + jnp.arange(N_EXPERTS, dtype=jnp.int32)
    tail_blk = jnp.concatenate([jnp.where(padded > 0, pad_end // rb - 1, -1),
                                jnp.where(unused < n_blocks, unused, -1)]).astype(jnp.int32)

    xs = _dispatch(dest0, dest1, tail_blk, h2, n_pad, _slabs(D // 2))
    ys = _experts(blk_expert, n_valid.reshape(1), xs, w1[l], w3[l], w2[l])
    out = _combine(dest0, dest1, x1, rf, ga2, ys, tiles_per_seq)
    return out.reshape(B, S, D)
```

```python
import functools

import jax
import jax.numpy as jnp
from jax import lax
from jax.experimental import pallas as pl
from jax.experimental.pallas import tpu as pltpu

F32 = jnp.float32
BF16 = jnp.bfloat16
HIGHEST = lax.Precision.HIGHEST

N_HEADS = 8
HEAD_DIM = 64
ATTN_WIDTH = N_HEADS * HEAD_DIM
CONV_WIDTH = 512
CONV_K = 3
MOBA_BLOCK = 256
MOBA_TOPK = 3
ROPE_THETA = 500000.0
ROT_DIM = HEAD_DIM // 4
N_GROUPS = 4
EXPERTS_PER_GROUP = 8
N_EXPERTS = N_GROUPS * EXPERTS_PER_GROUP
TOPK_IN_GROUP = 2
N_MOD = 6
EPS = 1e-6

LANES = 128
NEG = -1e30
ROW_TILE = 1024
INPROJ_TILE = 1024
DISPATCH_TILE = 2048
POST_CHUNK = 256
QUERY_TILE = 512
DMA_UNROLL = 8
EXPERT_ROWS = 512
VMEM_LIMIT = 56 * 1024 * 1024
ATTN_VMEM_LIMIT = 60 * 1024 * 1024


def _cparams(*sem):
    return pltpu.CompilerParams(dimension_semantics=sem, vmem_limit_bytes=VMEM_LIMIT)


def _ada_kernel(c_ref, w_ref, b_ref, o_ref):
    c = c_ref[...]
    a = c * jax.nn.sigmoid(c)
    o_ref[...] = jnp.dot(a, w_ref[...], preferred_element_type=F32, precision=HIGHEST) + b_ref[...]


def _ada(c, w_ada, b_ada):
    B, D = c.shape
    N = w_ada.shape[1]
    tn = 1536
    return pl.pallas_call(
        _ada_kernel,
        grid=(N // tn,),
        in_specs=[pl.BlockSpec((B, D), lambda j: (0, 0)),
                  pl.BlockSpec((D, tn), lambda j: (0, j)),
                  pl.BlockSpec((1, tn), lambda j: (0, j))],
        out_specs=pl.BlockSpec((B, tn), lambda j: (0, j)),
        out_shape=jax.ShapeDtypeStruct((B, N), F32),
        compiler_params=_cparams("arbitrary"),
        name="ada",
    )(c, w_ada, b_ada.reshape(1, N))


def _inproj_kernel(x_ref, g_ref, sc_ref, sh_ref, w_ref, z_ref, *, n_chunk):
    x = x_ref[...]
    ms = jnp.mean(x * x, axis=-1, keepdims=True)
    y = x * lax.rsqrt(ms + EPS) * g_ref[...]
    h = (y * (1.0 + sc_ref[0]) + sh_ref[0]).astype(BF16)
    for n in range(0, z_ref.shape[1], n_chunk):
        z_ref[:, n:n + n_chunk] = jnp.dot(h, w_ref[:, n:n + n_chunk],
                                          preferred_element_type=F32).astype(BF16)


def _inproj(x2, g1, sc1, sh1, w_in_bf, seq_len):
    T, D = x2.shape
    N = w_in_bf.shape[1]
    tm = INPROJ_TILE
    assert seq_len % tm == 0
    bmap = lambda i: (i // (seq_len // tm), 0, 0)
    return pl.pallas_call(
        functools.partial(_inproj_kernel, n_chunk=512),
        grid=(T // tm,),
        in_specs=[pl.BlockSpec((tm, D), lambda i: (i, 0)),
                  pl.BlockSpec((1, D), lambda i: (0, 0)),
                  pl.BlockSpec((1, 1, D), bmap),
                  pl.BlockSpec((1, 1, D), bmap),
                  pl.BlockSpec((D, N), lambda i: (0, 0), pipeline_mode=pl.Buffered(1))],
        out_specs=pl.BlockSpec((tm, N), lambda i: (i, 0)),
        out_shape=jax.ShapeDtypeStruct((T, N), BF16),
        compiler_params=_cparams("arbitrary"),
        name="inproj",
    )(x2, g1, sc1, sh1, w_in_bf)


def _fold_rows(x, op):
    parts = [x[r:r + 8] for r in range(0, x.shape[0], 8)]
    while len(parts) > 1:
        parts = [op(parts[i], parts[i + 1]) for i in range(0, len(parts) - 1, 2)] + (
            [parts[-1]] if len(parts) % 2 else [])
    return parts[0]


def _attn_kernel(q_ref, k_ref, v_ref, cos_ref, sin_ref, cost_ref, sint_ref, gq_ref, gk_ref, o_ref,
                 kaug_ref, vt_ref, kmp_ref, kst_ref, s_ref, mcol_ref, qa_ref):
    S = q_ref.shape[1]
    blk = MOBA_BLOCK
    qt = QUERY_TILE
    sub = qt // blk
    nb = S // blk
    nq = S // qt
    nbp = kst_ref.shape[0] // 6
    hd = HEAD_DIM

    half = ROT_DIM // 2
    lane_r = lax.broadcasted_iota(jnp.int32, (blk, LANES), 1)
    rot_lo = (lane_r & (hd - 1)) < half
    same_head = jnp.where((lax.broadcasted_iota(jnp.int32, (LANES, LANES), 0) < hd)
                          == (lax.broadcasted_iota(jnp.int32, (LANES, LANES), 1) < hd), 1.0, 0.0).astype(BF16)

    def norm_rope_keys(xb, r0):
        x = xb.astype(F32)
        sq = x * x
        sq_hi = sq.astype(BF16)
        sq_lo = (sq - sq_hi.astype(F32)).astype(BF16)
        ssq = (jnp.dot(sq_hi, same_head, preferred_element_type=F32)
               + jnp.dot(sq_lo, same_head, preferred_element_type=F32))
        y = x * lax.rsqrt(ssq * (1.0 / hd) + EPS) * gk_ref[...]
        rot = jnp.where(rot_lo, pltpu.roll(y, LANES - half, 1), pltpu.roll(y, half, 1))
        return y * cos_ref[pl.ds(r0, blk), :] + rot * sin_ref[pl.ds(r0, blk), :]

    def norm_rope_queries_t(xb, r0):
        xT = xb.astype(F32).T
        sq = xT * xT
        cos = cost_ref[:, pl.ds(r0, qt)]
        sin = sint_ref[:, pl.ds(r0, qt)]
        rows = []
        for h in range(2):
            lo, hi = h * hd, (h + 1) * hd
            ssq = jnp.sum(_fold_rows(sq[lo:hi], jnp.add), axis=0, keepdims=True)
            y = xT[lo:hi] * lax.rsqrt(ssq * (1.0 / hd) + EPS) * gq_ref[lo:hi, :]
            y1, y2 = y[0:half], y[half:2 * half]
            rows += [y1 * cos - y2 * sin, y2 * cos + y1 * sin, y[2 * half:]]
        return jnp.concatenate(rows, axis=0)

    kmp_ref[...] = jnp.zeros_like(kmp_ref)
    ones_row = jnp.where(lax.broadcasted_iota(jnp.int32, (16, blk), 0) == 0, 1.0, 0.0).astype(BF16)
    lane_k = lax.broadcasted_iota(jnp.int32, (blk, LANES), 1)
    head0_k = lane_k < HEAD_DIM
    lane_m = lax.broadcasted_iota(jnp.int32, (nbp, LANES), 1)

    def prepare_keys(t):
        for u in range(sub):
            j = t * sub + u
            r0 = pl.multiple_of(jnp.minimum(j, nb - 1) * blk, blk)
            kr = norm_rope_keys(k_ref[0, pl.ds(r0, blk), :], r0)
            kmp_ref[pl.ds(j, 1), :] = jnp.sum(kr, axis=0, keepdims=True) * (1.0 / blk)
            kaug_ref[0, j] = jnp.where(head0_k, kr, jnp.where(lane_k - hd == j, 1.0, 0.0)).astype(BF16)
            kaug_ref[1, j] = jnp.where(head0_k, jnp.where(lane_k == j, 1.0, 0.0), kr).astype(BF16)
            vT = v_ref[0, pl.ds(r0, blk), :].astype(F32).T
            for h in range(2):
                vt_ref[h, j, 0:HEAD_DIM, :] = vT[h * HEAD_DIM:(h + 1) * HEAD_DIM].astype(BF16)
                vt_ref[h, j, HEAD_DIM:HEAD_DIM + 16, :] = ones_row
        kmp = kmp_ref[0:nbp, :]
        k_hi = kmp.astype(BF16)
        k_lo = (kmp - k_hi.astype(F32)).astype(BF16)
        zero = jnp.zeros((nbp, LANES), BF16)
        parts = []
        for h in range(2):
            mine = (lane_m < hd) if h == 0 else (lane_m >= hd)
            parts += [jnp.where(mine, k_hi, zero), jnp.where(mine, k_lo, zero)]
        parts += [parts[0], parts[2]]
        for n, part in enumerate(parts):
            kst_ref[n * nbp:(n + 1) * nbp, :] = part

    prepare_keys(0)

    key_i = lax.broadcasted_iota(jnp.int32, (blk, qt), 0)
    qry_i = lax.broadcasted_iota(jnp.int32, (blk, qt), 1)
    causal = [(qry_i < u * blk) | (qry_i >= (u + 1) * blk) | (key_i <= qry_i - u * blk) for u in range(sub)]
    rowf = lax.broadcasted_iota(jnp.int32, (nbp, qt), 0).astype(F32)
    subf = (lax.broadcasted_iota(jnp.int32, (nbp, qt), 1) // blk).astype(F32)
    q_scale = (hd ** -0.5) * 1.4426950408889634

    def query_operands(t):
        r0 = pl.multiple_of(t * qt, qt)
        qT = norm_rope_queries_t(q_ref[0, pl.ds(r0, qt), :], r0)
        cur = lax.convert_element_type(t * sub, F32) + subf
        q_hi = qT.astype(BF16)
        q_lo = (qT - q_hi.astype(F32)).astype(BF16)
        g1 = jnp.dot(kst_ref[0:4 * nbp, :], q_hi, preferred_element_type=F32)
        g2 = jnp.dot(kst_ref[4 * nbp:6 * nbp, :], q_lo, preferred_element_type=F32)
        qa = []
        for h in range(2):
            gate = g1[2 * h * nbp:(2 * h + 1) * nbp] + g1[(2 * h + 1) * nbp:(2 * h + 2) * nbp] + g2[h * nbp:(h + 1) * nbp]
            g = jnp.where(rowf < cur, gate, -jnp.inf)
            keep = rowf == cur
            for r in range(MOBA_TOPK):
                m = jnp.max(g, axis=0, keepdims=True)
                idx = jnp.min(jnp.where(g == m, rowf, 1e9), axis=0, keepdims=True)
                pick = (rowf == idx) & (cur > r)
                keep = keep | pick
                g = jnp.where(pick, -jnp.inf, g)
            bias = jnp.where(keep, 0.0, NEG)
            qs = qT[h * hd:(h + 1) * hd] * q_scale
            pad = jnp.zeros((LANES - hd - nbp, qt), F32)
            pieces = [qs, bias, pad] if h == 0 else [bias, pad, qs]
            qa.append(jnp.concatenate(pieces, axis=0).astype(BF16))
        return qa

    def pass1_tile(par, h, j, qa_h, mask):
        sT = jnp.dot(kaug_ref[h, j], qa_h, preferred_element_type=F32)
        if mask is not None:
            sT = jnp.where(mask, sT, NEG)
        s_ref[par, h, j] = sT
        return _fold_rows(sT, jnp.maximum)

    def pass2_tile(par, h, j):
        pT = jnp.exp2(s_ref[par, h, j] - mcol_ref[h, 0:1, :]).astype(BF16)
        return jnp.dot(vt_ref[h, j], pT, preferred_element_type=F32)

    def pass1_own(t, par, qa):
        mx = []
        for h in range(2):
            f = [pass1_tile(par, h, t * sub + u, qa[h], causal[u]) for u in range(sub)]
            mx.append(functools.reduce(jnp.maximum, f))
        return mx

    def pass2_own(t, par):
        return [sum(pass2_tile(par, h, t * sub + u) for u in range(sub)) for h in range(2)]

    def pass1_pair(p, par, qa, mx):
        return [functools.reduce(jnp.maximum, [mx[h]] + [pass1_tile(par, h, 2 * p + u, qa[h], None) for u in range(2)])
                for h in range(2)]

    def pass2_pair(p, par, acc):
        return [acc[h] + sum(pass2_tile(par, h, 2 * p + u) for u in range(2)) for h in range(2)]

    def finish_pass1(mx):
        for h in range(2):
            mcol_ref[h] = jnp.broadcast_to(jnp.max(mx[h], axis=0, keepdims=True), mcol_ref.shape[1:])

    def finish_pass2(t, acc):
        outT = jnp.concatenate([acc[h][0:hd] / acc[h][hd:hd + 1] for h in range(2)], axis=0)
        o_ref[0, pl.ds(pl.multiple_of(t * qt, qt), qt), :] = outT.T.astype(BF16)

    def stage(t, par):

        def prepare_next():
            prepare_keys(t + 1)
            nxt = query_operands(jnp.minimum(t + 1, nq - 1))
            for h in range(2):
                qa_ref[1 - par, h] = nxt[h]

        @pl.when(t == 0)
        def _():
            finish_pass1(pass1_own(t, par, query_operands(t)))
            prepare_next()

        @pl.when((t > 0) & (t < nq))
        def _():
            qa = [qa_ref[par, h] for h in range(2)]
            mx = pass1_own(t, par, qa)
            acc = pass2_own(t - 1, 1 - par)
            prepare_next()
            n_prev = (t - 1) * sub // 2

            def both(p, c):
                mx, acc = c
                return tuple(pass1_pair(p, par, qa, mx)), tuple(pass2_pair(p, 1 - par, acc))

            def both_twice(p2, c):
                return both(2 * p2 + 1, both(2 * p2, c))

            c = lax.fori_loop(0, n_prev // 2, both_twice, (tuple(mx), tuple(acc)))
            mx, acc = lax.fori_loop(n_prev // 2 * 2, n_prev, both, c)
            for p in range(sub // 2):
                mx = pass1_pair(n_prev + p, par, qa, mx)
            finish_pass2(t - 1, acc)
            finish_pass1(mx)

        @pl.when(t == nq)
        def _():
            acc = pass2_own(t - 1, 1 - par)
            acc = lax.fori_loop(0, (t - 1) * sub // 2, lambda p, a: tuple(pass2_pair(p, 1 - par, a)), tuple(acc))
            finish_pass2(t - 1, acc)

    def stage_pair(tt, carry):
        stage(2 * tt, 0)
        stage(2 * tt + 1, 1)
        return carry

    lax.fori_loop(0, (nq + 2) // 2, stage_pair, 0)


def _attention(z3, cosf, sinf, cost, sint, gq_cols, gk2):
    B, S, _ = z3.shape
    n_pair = N_HEADS // 2
    kq = ATTN_WIDTH // LANES
    nb = S // MOBA_BLOCK
    nbp = -(-nb // 16) * 16
    sub = QUERY_TILE // MOBA_BLOCK
    assert sub <= 8
    return pl.pallas_call(
        _attn_kernel,
        grid=(B, n_pair),
        in_specs=[pl.BlockSpec((1, S, LANES), lambda b, p: (b, 0, p)),
                  pl.BlockSpec((1, S, LANES), lambda b, p: (b, 0, kq + p)),
                  pl.BlockSpec((1, S, LANES), lambda b, p: (b, 0, 2 * kq + p)),
                  pl.BlockSpec((S, LANES), lambda b, p: (0, 0), pipeline_mode=pl.Buffered(1)),
                  pl.BlockSpec((S, LANES), lambda b, p: (0, 0), pipeline_mode=pl.Buffered(1)),
                  pl.BlockSpec(cost.shape, lambda b, p: (0, 0)),
                  pl.BlockSpec(sint.shape, lambda b, p: (0, 0)),
                  pl.BlockSpec((LANES, QUERY_TILE), lambda b, p: (0, 0)),
                  pl.BlockSpec((1, LANES), lambda b, p: (0, 0))],
        out_specs=pl.BlockSpec((1, S, LANES), lambda b, p: (b, 0, p)),
        out_shape=jax.ShapeDtypeStruct((B, S, ATTN_WIDTH), BF16),
        scratch_shapes=[pltpu.VMEM((2, nb + sub, MOBA_BLOCK, LANES), BF16),
                        pltpu.VMEM((2, nb + sub, HEAD_DIM + 16, MOBA_BLOCK), BF16),
                        pltpu.VMEM((nbp + 8, LANES), F32),
                        pltpu.VMEM((6 * nbp, LANES), BF16),
                        pltpu.VMEM((2, 2, nb, MOBA_BLOCK, QUERY_TILE), F32),
                        pltpu.VMEM((2, 8, QUERY_TILE), F32),
                        pltpu.VMEM((2, 2, LANES, QUERY_TILE), BF16)],
        compiler_params=pltpu.CompilerParams(dimension_semantics=("arbitrary", "arbitrary"),
                                             vmem_limit_bytes=ATTN_VMEM_LIMIT),
        name="attn",
    )(z3, z3, z3, cosf, sinf, cost, sint, gq_cols, gk2)


def _slabs(width):
    return width // LANES


def _load_slabs(ref, row0, rows, n_slab, lead=()):
    return jnp.concatenate([ref[lead + (pl.ds(row0 * n_slab + s, rows, stride=n_slab), slice(None))]
                            for s in range(n_slab)], axis=1)


def _store_slabs(ref, row0, val):
    rows, width = val.shape
    n_slab = _slabs(width)
    for s in range(n_slab):
        ref[pl.ds(row0 * n_slab + s, rows, stride=n_slab), :] = val[:, s * LANES:(s + 1) * LANES]


def _post_kernel(x_ref, ya_ref, xb_ref, bg_ref, cg_ref, gta_ref, gtb_ref, ga1_ref, sc2_ref, sh2_ref,
                 cw_ref, cb_ref, wpa_ref, wpb_ref, wo_ref, g2_ref, wr_ref, br_ref, tri_ref,
                 x1_ref, h2_ref, ri_ref, rf_ref, cnt_ref, ubuf_ref, run_ref, *, tiles_per_seq):
    i = pl.program_id(0)
    tm = x_ref.shape[0]
    rc = POST_CHUNK
    halo = 8

    @pl.when(i == 0)
    def _():
        run_ref[...] = jnp.zeros_like(run_ref)

    @pl.when(i % tiles_per_seq == 0)
    def _():
        ubuf_ref[0:halo, :] = jnp.zeros((halo, CONV_WIDTH), F32)

    ubuf_ref[halo:halo + tm, :] = cg_ref[...].astype(F32) * xb_ref[...].astype(F32)
    cw = cw_ref[...]
    lanef = lax.broadcasted_iota(jnp.int32, (rc, LANES), 1).astype(F32)
    half_d = x_ref.shape[1] // 2
    run = run_ref[0:1, :]

    for c in range(tm // rc):
        rows = pl.ds(c * rc, rc)
        conv = (cw[0:1, :] * ubuf_ref[pl.ds(halo - 2 + c * rc, rc), :]
                + cw[1:2, :] * ubuf_ref[pl.ds(halo - 1 + c * rc, rc), :]
                + cw[2:3, :] * ubuf_ref[pl.ds(halo + c * rc, rc), :])
        y_b = (bg_ref[rows, :].astype(F32) * (conv + cb_ref[...])).astype(BF16)
        pa = jnp.dot(ya_ref[rows, :], wpa_ref[...], preferred_element_type=F32)
        pb = jnp.dot(y_b, wpb_ref[...], preferred_element_type=F32)
        merged = (jax.nn.sigmoid(gta_ref[rows, :].astype(F32)) * pa
                  + jax.nn.sigmoid(gtb_ref[rows, :].astype(F32)) * pb).astype(BF16)
        x1 = x_ref[rows, :] + ga1_ref[0] * jnp.dot(merged, wo_ref[...], preferred_element_type=F32)
        x1_ref[rows, :] = x1

        ms = jnp.mean(x1 * x1, axis=-1, keepdims=True)
        h2 = x1 * lax.rsqrt(ms + EPS) * g2_ref[...]
        h2 = h2 * (1.0 + sc2_ref[0]) + sh2_ref[0]
        h_hi = h2.astype(BF16)
        h_hi32 = h_hi.astype(F32)
        bits = pltpu.bitcast(h_hi32, jnp.uint32)
        _store_slabs(h2_ref, c * rc, (bits[:, :half_d] >> 16) | bits[:, half_d:])

        h_lo = (h2 - h_hi32).astype(BF16)
        r = jnp.dot(h_hi, wr_ref[...], preferred_element_type=F32)
        logit = (r[:, :LANES] + r[:, LANES:]
                 + jnp.dot(h_lo, wr_ref[:, :LANES], preferred_element_type=F32) + br_ref[...])
        gl = jnp.where(lanef < N_GROUPS, logit, -jnp.inf)
        gmax = jnp.max(gl, axis=-1, keepdims=True)
        g_idx = jnp.min(jnp.where(gl == gmax, lanef, 1e9), axis=-1, keepdims=True)
        g_w = 1.0 / jnp.sum(jnp.exp(gl - gmax), axis=-1, keepdims=True)
        e_lo = N_GROUPS + EXPERTS_PER_GROUP * g_idx
        el = jnp.where((lanef >= e_lo) & (lanef < e_lo + EXPERTS_PER_GROUP), logit, -jnp.inf)
        v0 = jnp.max(el, axis=-1, keepdims=True)
        i0 = jnp.min(jnp.where(el == v0, lanef, 1e9), axis=-1, keepdims=True)
        el = jnp.where(lanef == i0, -jnp.inf, el)
        v1 = jnp.max(el, axis=-1, keepdims=True)
        i1 = jnp.min(jnp.where(el == v1, lanef, 1e9), axis=-1, keepdims=True)
        t = jnp.exp(v1 - v0)
        w0 = g_w / (1.0 + t)
        w1 = g_w * t / (1.0 + t)
        e0 = i0 - N_GROUPS
        e1 = i1 - N_GROUPS

        oh0 = lanef == e0
        oh1 = lanef == e1
        oh = jnp.where(oh0 | oh1, 1.0, 0.0)
        before = jnp.dot(tri_ref[...], oh.astype(BF16), preferred_element_type=F32) + run
        r0 = jnp.sum(jnp.where(oh0, before, 0.0), axis=-1, keepdims=True)
        r1 = jnp.sum(jnp.where(oh1, before, 0.0), axis=-1, keepdims=True)
        run = run + jnp.sum(oh, axis=0, keepdims=True)

        ri = jnp.where(lanef == 0, e0, jnp.where(lanef == 1, e1, jnp.where(lanef == 2, r0, jnp.where(lanef == 3, r1, 0.0))))
        ri_ref[:, rows] = ri.astype(jnp.int32).T[0:8]
        rf_ref[rows, :] = jnp.where(lanef == 0, w0, jnp.where(lanef == 1, w1, 0.0))

    ubuf_ref[0:halo, :] = ubuf_ref[tm:tm + halo, :]
    run_ref[...] = jnp.broadcast_to(run, run_ref.shape)
    cnt_ref[...] = jnp.broadcast_to(run, cnt_ref.shape)


def _post(x2, ya2, z2, ga1, sc2, sh2, conv_w, conv_b, wpa, wpb, wo, g2, wr, br, tri, tiles_per_seq):
    T, D = x2.shape
    tm = ROW_TILE
    cw = CONV_WIDTH
    xcol = 3 * ATTN_WIDTH // cw
    gcol = (3 * ATTN_WIDTH + 3 * cw) // D
    bmap = lambda i: (i // tiles_per_seq, 0, 0)
    const = lambda i: (0, 0)
    return pl.pallas_call(
        functools.partial(_post_kernel, tiles_per_seq=tiles_per_seq),
        grid=(T // tm,),
        in_specs=[pl.BlockSpec((tm, D), lambda i: (i, 0)),
                  pl.BlockSpec((tm, ATTN_WIDTH), lambda i: (i, 0)),
                  pl.BlockSpec((tm, cw), lambda i: (i, xcol)),
                  pl.BlockSpec((tm, cw), lambda i: (i, xcol + 1)),
                  pl.BlockSpec((tm, cw), lambda i: (i, xcol + 2)),
                  pl.BlockSpec((tm, D), lambda i: (i, gcol)),
                  pl.BlockSpec((tm, D), lambda i: (i, gcol + 1)),
                  pl.BlockSpec((1, 1, D), bmap),
                  pl.BlockSpec((1, 1, D), bmap),
                  pl.BlockSpec((1, 1, D), bmap),
                  pl.BlockSpec((CONV_K, cw), const),
                  pl.BlockSpec((1, cw), const),
                  pl.BlockSpec((ATTN_WIDTH, D), const),
                  pl.BlockSpec((cw, D), const),
                  pl.BlockSpec((D, D), const),
                  pl.BlockSpec((1, D), const),
                  pl.BlockSpec((D, 2 * LANES), const),
                  pl.BlockSpec((1, LANES), const),
                  pl.BlockSpec((POST_CHUNK, POST_CHUNK), const)],
        out_specs=[pl.BlockSpec((tm, D), lambda i: (i, 0)),
                   pl.BlockSpec((tm * _slabs(D // 2), LANES), lambda i: (i, 0)),
                   pl.BlockSpec((8, tm), lambda i: (0, i)),
                   pl.BlockSpec((tm, LANES), lambda i: (i, 0)),
                   pl.BlockSpec((8, LANES), const)],
        out_shape=[jax.ShapeDtypeStruct((T, D), F32),
                   jax.ShapeDtypeStruct((T * _slabs(D // 2), LANES), jnp.uint32),
                   jax.ShapeDtypeStruct((8, T), jnp.int32),
                   jax.ShapeDtypeStruct((T, LANES), F32),
                   jax.ShapeDtypeStruct((8, LANES), F32)],
        scratch_shapes=[pltpu.VMEM((tm + 16, cw), F32),
                        pltpu.VMEM((8, LANES), F32)],
        compiler_params=_cparams("arbitrary"),
        name="post",
    )(x2, ya2, z2, z2, z2, z2, z2, ga1, sc2, sh2, conv_w, conv_b, wpa, wpb, wo, g2, wr, br, tri)


def _slots_kernel(ps_ref, ri_ref, d_ref, *, n_slab):
    e = ri_ref[0:TOPK_IN_GROUP, :]
    start = jnp.zeros(e.shape, jnp.int32)
    for k in range(N_EXPERTS):
        start = jnp.where(e == k, ps_ref[k], start)
    d_ref[...] = (start + ri_ref[TOPK_IN_GROUP:2 * TOPK_IN_GROUP, :]) * n_slab


def _slots(pad_start, riT, n_slab):
    T = riT.shape[1]
    grid_spec = pltpu.PrefetchScalarGridSpec(
        num_scalar_prefetch=1,
        grid=(1,),
        in_specs=[pl.BlockSpec(riT.shape, lambda i, ps: (0, 0))],
        out_specs=pl.BlockSpec((TOPK_IN_GROUP, T), lambda i, ps: (0, 0)),
    )
    return pl.pallas_call(
        functools.partial(_slots_kernel, n_slab=n_slab),
        grid_spec=grid_spec,
        out_shape=jax.ShapeDtypeStruct((TOPK_IN_GROUP, T), jnp.int32),
        compiler_params=_cparams("arbitrary"),
        name="slots",
    )(pad_start, riT)


def _dispatch_kernel(d0_ref, d1_ref, tail_ref, h_ref, xs_ref, zero_ref, sem, zsem, *, n_slab):
    tm = h_ref.shape[0] // n_slab
    base = pl.program_id(0) * tm
    blk_rows = zero_ref.shape[0]

    @pl.when(pl.program_id(0) == 0)
    def _():
        zero_ref[...] = jnp.zeros_like(zero_ref)

        def tail_copy(e):
            start = pl.multiple_of(tail_ref[e] * blk_rows, blk_rows)
            return pltpu.make_async_copy(zero_ref, xs_ref.at[pl.ds(start, blk_rows), :], zsem)

        for e in range(tail_ref.shape[0]):
            @pl.when(tail_ref[e] >= 0)
            def _():
                tail_copy(e).start()
        for e in range(tail_ref.shape[0]):
            @pl.when(tail_ref[e] >= 0)
            def _():
                tail_copy(e).wait()

    def body(g, carry):
        r8 = pl.multiple_of(g * DMA_UNROLL, DMA_UNROLL)
        for u in range(DMA_UNROLL):
            src = h_ref.at[pl.ds((r8 + u) * n_slab, n_slab), :]
            pltpu.make_async_copy(src, xs_ref.at[pl.ds(d0_ref[base + r8 + u], n_slab), :], sem).start(priority=0)
            pltpu.make_async_copy(src, xs_ref.at[pl.ds(d1_ref[base + r8 + u], n_slab), :], sem).start(priority=1)
        return carry

    lax.fori_loop(0, tm // DMA_UNROLL, body, 0)
    for _ in range(TOPK_IN_GROUP):
        pltpu.make_async_copy(h_ref, xs_ref.at[pl.ds(0, tm * n_slab), :], sem).wait()


def _dispatch(dest0, dest1, tail_blk, h2p, n_pad, n_slab):
    tm = DISPATCH_TILE
    T = h2p.shape[0] // n_slab
    grid_spec = pltpu.PrefetchScalarGridSpec(
        num_scalar_prefetch=3,
        grid=(T // tm,),
        in_specs=[pl.BlockSpec((tm * n_slab, LANES), lambda i, d0, d1, tb: (i, 0))],
        out_specs=pl.BlockSpec(memory_space=pl.ANY),
        scratch_shapes=[pltpu.VMEM((EXPERT_ROWS * n_slab, LANES), h2p.dtype),
                        pltpu.SemaphoreType.DMA(()),
                        pltpu.SemaphoreType.DMA(())],
    )
    return pl.pallas_call(
        functools.partial(_dispatch_kernel, n_slab=n_slab),
        grid_spec=grid_spec,
        out_shape=jax.ShapeDtypeStruct((n_pad * n_slab, LANES), h2p.dtype),
        compiler_params=_cparams("arbitrary"),
        name="dispatch",
    )(dest0, dest1, tail_blk, h2p)


def _expert_kernel(be_ref, nv_ref, xs_ref, w1f_ref, w3f_ref, w2f_ref, ys_ref, w1_ref, w3_ref, w2_ref):
    i = pl.program_id(0)

    @pl.when((i == 0) | (be_ref[i] != be_ref[jnp.maximum(i - 1, 0)]))
    def _():
        w1_ref[...] = w1f_ref[...].astype(BF16)
        w3_ref[...] = w3f_ref[...].astype(BF16)
        w2_ref[...] = w2f_ref[...].astype(BF16)

    @pl.when(i >= nv_ref[0])
    def _():
        ys_ref[...] = jnp.zeros_like(ys_ref)

    @pl.when(i < nv_ref[0])
    def _():
        half_d = w1_ref.shape[1] // 2
        xp = _load_slabs(xs_ref, 0, EXPERT_ROWS, _slabs(half_d))
        x_lo = pltpu.bitcast(xp << 16, F32).astype(BF16)
        x_hi = pltpu.bitcast(xp & jnp.uint32(0xFFFF0000), F32).astype(BF16)
        a = (jnp.dot(x_lo, w1_ref[0, :half_d], preferred_element_type=F32)
             + jnp.dot(x_hi, w1_ref[0, half_d:], preferred_element_type=F32))
        b = (jnp.dot(x_lo, w3_ref[0, :half_d], preferred_element_type=F32)
             + jnp.dot(x_hi, w3_ref[0, half_d:], preferred_element_type=F32))
        hid = (a * jax.nn.sigmoid(a) * b).astype(BF16)
        y = jnp.dot(hid, w2_ref[0], preferred_element_type=F32)
        bits = pltpu.bitcast(y.astype(BF16).astype(F32), jnp.uint32)
        _store_slabs(ys_ref, 0, (bits[:, :half_d] >> 16) | bits[:, half_d:])


def _experts(blk_expert, n_valid, xs, w1f, w3f, w2f):
    _, D, F = w1f.shape
    rb = EXPERT_ROWS
    blk_rows = rb * _slabs(D // 2)
    row_blk = lambda i, be, nv: (jnp.minimum(i, nv[0] - 1), 0)
    out_blk = lambda i, be, nv: (i, 0)
    wgt_blk = lambda i, be, nv: (be[i], 0, 0)
    grid_spec = pltpu.PrefetchScalarGridSpec(
        num_scalar_prefetch=2,
        grid=(xs.shape[0] // blk_rows,),
        in_specs=[pl.BlockSpec((blk_rows, LANES), row_blk),
                  pl.BlockSpec((1, D, F), wgt_blk),
                  pl.BlockSpec((1, D, F), wgt_blk),
                  pl.BlockSpec((1, F, D), wgt_blk)],
        out_specs=pl.BlockSpec((blk_rows, LANES), out_blk),
        scratch_shapes=[pltpu.VMEM((1, D, F), BF16), pltpu.VMEM((1, D, F), BF16), pltpu.VMEM((1, F, D), BF16)],
    )
    return pl.pallas_call(
        _expert_kernel,
        grid_spec=grid_spec,
        out_shape=jax.ShapeDtypeStruct(xs.shape, jnp.uint32),
        compiler_params=_cparams("arbitrary"),
        name="experts",
    )(blk_expert, n_valid, xs, w1f, w3f, w2f)


def _combine_kernel(d0_ref, d1_ref, x1_ref, rf_ref, ga2_ref, ys_ref, o_ref, buf_ref, sem, *, n_slab):
    i = pl.program_id(0)
    n = pl.num_programs(0)
    tm = x1_ref.shape[0]

    def gather(step, slot):
        base = step * tm

        def body(g, carry):
            r8 = pl.multiple_of(g * DMA_UNROLL, DMA_UNROLL)
            for u in range(DMA_UNROLL):
                dst = pl.ds((r8 + u) * n_slab, n_slab)
                pltpu.make_async_copy(ys_ref.at[pl.ds(d0_ref[base + r8 + u], n_slab), :],
                                      buf_ref.at[slot, 0, dst, :], sem.at[slot]).start(priority=0)
                pltpu.make_async_copy(ys_ref.at[pl.ds(d1_ref[base + r8 + u], n_slab), :],
                                      buf_ref.at[slot, 1, dst, :], sem.at[slot]).start(priority=1)
            return carry

        lax.fori_loop(0, tm // DMA_UNROLL, body, 0)

    @pl.when(i == 0)
    def _():
        gather(0, 0)

    @pl.when(i + 1 < n)
    def _():
        gather(i + 1, (i + 1) % 2)

    slot = i % 2
    for k in range(TOPK_IN_GROUP):
        pltpu.make_async_copy(ys_ref.at[pl.ds(0, tm * n_slab), :], buf_ref.at[slot, k], sem.at[slot]).wait()
    rf = rf_ref[...]
    w = [rf[:, k:k + 1] for k in range(TOPK_IN_GROUP)]
    yp = [_load_slabs(buf_ref, 0, tm, n_slab, lead=(slot, k)) for k in range(TOPK_IN_GROUP)]
    half_d = n_slab * LANES
    y_lo = sum(w[k] * pltpu.bitcast(yp[k] << 16, F32) for k in range(TOPK_IN_GROUP))
    y_hi = sum(w[k] * pltpu.bitcast(yp[k] & jnp.uint32(0xFFFF0000), F32) for k in range(TOPK_IN_GROUP))
    ga2 = ga2_ref[0]
    o_ref[:, :half_d] = x1_ref[:, :half_d] + ga2[:, :half_d] * y_lo
    o_ref[:, half_d:] = x1_ref[:, half_d:] + ga2[:, half_d:] * y_hi


def _combine(dest0, dest1, x1, rf, ga2, ys, tiles_per_seq):
    T, D = x1.shape
    n_slab = _slabs(D // 2)
    tm = ROW_TILE
    row = lambda i, d0, d1: (i, 0)
    grid_spec = pltpu.PrefetchScalarGridSpec(
        num_scalar_prefetch=2,
        grid=(T // tm,),
        in_specs=[pl.BlockSpec((tm, D), row),
                  pl.BlockSpec((tm, LANES), row),
                  pl.BlockSpec((1, 1, D), lambda i, d0, d1: (i // tiles_per_seq, 0, 0)),
                  pl.BlockSpec(memory_space=pl.ANY)],
        out_specs=pl.BlockSpec((tm, D), row),
        scratch_shapes=[pltpu.VMEM((2, TOPK_IN_GROUP, tm * n_slab, LANES), ys.dtype),
                        pltpu.SemaphoreType.DMA((2,))],
    )
    return pl.pallas_call(
        functools.partial(_combine_kernel, n_slab=n_slab),
        grid_spec=grid_spec,
        out_shape=jax.ShapeDtypeStruct((T, D), F32),
        compiler_params=_cparams("arbitrary"),
        name="combine",
    )(dest0, dest1, x1, rf, ga2, ys)


def _rope_tables(S):
    pos = jnp.arange(S, dtype=F32)
    inv_freq = ROPE_THETA ** (-jnp.arange(0, ROT_DIM, 2, dtype=F32) / ROT_DIM)
    ang = pos[:, None] * inv_freq[None, :]
    cos, sin = jnp.cos(ang), jnp.sin(ang)
    half = ROT_DIM // 2
    ones = jnp.ones((S, HEAD_DIM - ROT_DIM), F32)
    cos_h = jnp.concatenate([cos, cos, ones], axis=1)
    sin_h = jnp.concatenate([-sin, sin, 0.0 * ones], axis=1)
    return jnp.tile(cos_h, (1, LANES // HEAD_DIM)), jnp.tile(sin_h, (1, LANES // HEAD_DIM)), cos.T, sin.T


def kernel(x, c, w_ada, b_ada, g_norm1, g_norm2, w_in, g_q, g_k, conv_w, conv_b,
           w_pa, w_pb, w_o, w_rg, b_rg, w_re, b_re, w1, w3, w2):
    B, S, D = x.shape
    T = B * S
    assert S % ROW_TILE == 0 and T % DISPATCH_TILE == 0 and S % QUERY_TILE == 0 and QUERY_TILE % (2 * MOBA_BLOCK) == 0
    assert S // MOBA_BLOCK <= LANES - HEAD_DIM
    tiles_per_seq = S // ROW_TILE
    l = 0

    mod = _ada(c, w_ada[l], b_ada[l])
    sh1, sc1, ga1, sh2, sc2, ga2 = [m.reshape(B, 1, D) for m in jnp.split(mod, N_MOD, axis=-1)]

    x2 = x.reshape(T, D)
    z2 = _inproj(x2, g_norm1[l].reshape(1, D), sc1, sh1, w_in[l].astype(BF16), S)

    cosf, sinf, cost, sint = _rope_tables(S)
    rep = LANES // HEAD_DIM
    gq_cols = jnp.broadcast_to(jnp.tile(g_q[l], rep).reshape(LANES, 1), (LANES, QUERY_TILE))
    ya = _attention(z2.reshape(B, S, -1), cosf, sinf, cost, sint, gq_cols, jnp.tile(g_k[l], rep).reshape(1, LANES))

    wr = jnp.zeros((D, LANES), F32).at[:, :N_GROUPS].set(w_rg[l]).at[:, N_GROUPS:N_GROUPS + N_EXPERTS].set(w_re[l])
    br = jnp.zeros((1, LANES), F32).at[0, :N_GROUPS].set(b_rg[l]).at[0, N_GROUPS:N_GROUPS + N_EXPERTS].set(b_re[l])
    wr_hi = wr.astype(BF16)
    wr2 = jnp.concatenate([wr_hi, (wr - wr_hi.astype(F32)).astype(BF16)], axis=1)
    tri = (lax.broadcasted_iota(jnp.int32, (POST_CHUNK, POST_CHUNK), 1)
           < lax.broadcasted_iota(jnp.int32, (POST_CHUNK, POST_CHUNK), 0)).astype(BF16)
    x1, h2, ri, rf, cnt = _post(x2, ya.reshape(T, ATTN_WIDTH), z2, ga1, sc2, sh2,
                                conv_w[l], conv_b[l].reshape(1, CONV_WIDTH),
                                w_pa[l].astype(BF16), w_pb[l].astype(BF16), w_o[l].astype(BF16),
                                g_norm2[l].reshape(1, D), wr2, br, tri, tiles_per_seq)

    rb = EXPERT_ROWS
    counts = cnt[0, :N_EXPERTS].astype(jnp.int32)
    padded = (counts + rb - 1) // rb * rb
    pad_end = jnp.cumsum(padded)
    pad_start = pad_end - padded
    dest = _slots(pad_start.astype(jnp.int32), ri, _slabs(D // 2))
    dest0, dest1 = dest[0], dest[1]
    n_blocks = -(-T * TOPK_IN_GROUP // rb) + N_EXPERTS
    n_pad = n_blocks * rb
    n_valid = (pad_end[-1] // rb).astype(jnp.int32)
    blk_start = jnp.minimum(jnp.arange(n_blocks, dtype=jnp.int32), n_valid - 1) * rb
    blk_expert = jnp.sum(pad_end[None, :] <= blk_start[:, None], axis=-1).astype(jnp.int32)
    unused = n_valid + jnp.arange(N_EXPERTS, dtype=jnp.int32)
    tail_blk = jnp.concatenate([jnp.where(padded > 0, pad_end // rb - 1, -1),
                                jnp.where(unused < n_blocks, unused, -1)]).astype(jnp.int32)

    xs = _dispatch(dest0, dest1, tail_blk, h2, n_pad, _slabs(D // 2))
    ys = _experts(blk_expert, n_valid.reshape(1), xs, w1[l], w3[l], w2[l])
    out = _combine(dest0, dest1, x1, rf, ga2, ys, tiles_per_seq)
    return out.reshape(B, S, D)
```

```python
import functools

import jax
import jax.numpy as jnp
from jax import lax
from jax.experimental import pallas as pl
from jax.experimental.pallas import tpu as pltpu

F32 = jnp.float32
BF16 = jnp.bfloat16
HIGHEST = lax.Precision.HIGHEST

N_HEADS = 8
HEAD_DIM = 64
ATTN_WIDTH = N_HEADS * HEAD_DIM
CONV_WIDTH = 512
CONV_K = 3
MOBA_BLOCK = 256
MOBA_TOPK = 3
ROPE_THETA = 500000.0
ROT_DIM = HEAD_DIM // 4
N_GROUPS = 4
EXPERTS_PER_GROUP = 8
N_EXPERTS = N_GROUPS * EXPERTS_PER_GROUP
TOPK_IN_GROUP = 2
N_MOD = 6
EPS = 1e-6

LANES = 128
NEG = -1e30
POST_TILE = 1024
COMBINE_TILE = 512
INPROJ_TILE = 1024
DISPATCH_TILE = 2048
POST_CHUNK = 256
QUERY_TILE = 512
DMA_UNROLL = 8
EXPERT_ROWS = 512
VMEM_LIMIT = 56 * 1024 * 1024
ATTN_VMEM_LIMIT = 60 * 1024 * 1024


def _cparams(*sem):
    return pltpu.CompilerParams(dimension_semantics=sem, vmem_limit_bytes=VMEM_LIMIT)


def _ada_kernel(c_ref, w_ref, b_ref, o_ref):
    c = c_ref[...]
    a = c * jax.nn.sigmoid(c)
    o_ref[...] = jnp.dot(a, w_ref[...], preferred_element_type=F32, precision=HIGHEST) + b_ref[...]


def _ada(c, w_ada, b_ada):
    B, D = c.shape
    N = w_ada.shape[1]
    tn = 1536
    return pl.pallas_call(
        _ada_kernel,
        grid=(N // tn,),
        in_specs=[pl.BlockSpec((B, D), lambda j: (0, 0)),
                  pl.BlockSpec((D, tn), lambda j: (0, j)),
                  pl.BlockSpec((1, tn), lambda j: (0, j))],
        out_specs=pl.BlockSpec((B, tn), lambda j: (0, j)),
        out_shape=jax.ShapeDtypeStruct((B, N), F32),
        compiler_params=_cparams("arbitrary"),
        name="ada",
    )(c, w_ada, b_ada.reshape(1, N))


def _inproj_kernel(x_ref, g_ref, sc_ref, sh_ref, w_ref, z_ref, *, n_chunk):
    x = x_ref[...]
    ms = jnp.mean(x * x, axis=-1, keepdims=True)
    y = x * lax.rsqrt(ms + EPS) * g_ref[...]
    h = (y * (1.0 + sc_ref[0]) + sh_ref[0]).astype(BF16)
    for n in range(0, z_ref.shape[1], n_chunk):
        z_ref[:, n:n + n_chunk] = jnp.dot(h, w_ref[:, n:n + n_chunk],
                                          preferred_element_type=F32).astype(BF16)


def _inproj(x2, g1, sc1, sh1, w_in_bf, seq_len):
    T, D = x2.shape
    N = w_in_bf.shape[1]
    tm = INPROJ_TILE
    assert seq_len % tm == 0
    bmap = lambda i: (i // (seq_len // tm), 0, 0)
    return pl.pallas_call(
        functools.partial(_inproj_kernel, n_chunk=512),
        grid=(T // tm,),
        in_specs=[pl.BlockSpec((tm, D), lambda i: (i, 0)),
                  pl.BlockSpec((1, D), lambda i: (0, 0)),
                  pl.BlockSpec((1, 1, D), bmap),
                  pl.BlockSpec((1, 1, D), bmap),
                  pl.BlockSpec((D, N), lambda i: (0, 0), pipeline_mode=pl.Buffered(1))],
        out_specs=pl.BlockSpec((tm, N), lambda i: (i, 0)),
        out_shape=jax.ShapeDtypeStruct((T, N), BF16),
        compiler_params=_cparams("arbitrary"),
        name="inproj",
    )(x2, g1, sc1, sh1, w_in_bf)


def _fold_rows(x, op):
    parts = [x[r:r + 8] for r in range(0, x.shape[0], 8)]
    while len(parts) > 1:
        parts = [op(parts[i], parts[i + 1]) for i in range(0, len(parts) - 1, 2)] + (
            [parts[-1]] if len(parts) % 2 else [])
    return parts[0]


def _attn_kernel(q_ref, k_ref, v_ref, cos_ref, sin_ref, cost_ref, sint_ref, gq_ref, gk_ref, o_ref,
                 kaug_ref, vt_ref, kmp_ref, kst_ref, s_ref, mcol_ref, qa_ref):
    S = q_ref.shape[1]
    blk = MOBA_BLOCK
    qt = QUERY_TILE
    sub = qt // blk
    nb = S // blk
    nq = S // qt
    nbp = kst_ref.shape[0] // 6
    hd = HEAD_DIM

    half = ROT_DIM // 2
    lane_r = lax.broadcasted_iota(jnp.int32, (blk, LANES), 1)
    rot_lo = (lane_r & (hd - 1)) < half
    same_head = jnp.where((lax.broadcasted_iota(jnp.int32, (LANES, LANES), 0) < hd)
                          == (lax.broadcasted_iota(jnp.int32, (LANES, LANES), 1) < hd), 1.0, 0.0).astype(BF16)

    def norm_rope_keys(xb, r0):
        x = xb.astype(F32)
        sq = x * x
        sq_hi = sq.astype(BF16)
        sq_lo = (sq - sq_hi.astype(F32)).astype(BF16)
        ssq = (jnp.dot(sq_hi, same_head, preferred_element_type=F32)
               + jnp.dot(sq_lo, same_head, preferred_element_type=F32))
        y = x * lax.rsqrt(ssq * (1.0 / hd) + EPS) * gk_ref[...]
        rot = jnp.where(rot_lo, pltpu.roll(y, LANES - half, 1), pltpu.roll(y, half, 1))
        return y * cos_ref[pl.ds(r0, blk), :] + rot * sin_ref[pl.ds(r0, blk), :]

    def norm_rope_queries_t(xb, r0):
        xT = xb.astype(F32).T
        sq = xT * xT
        cos = cost_ref[:, pl.ds(r0, qt)]
        sin = sint_ref[:, pl.ds(r0, qt)]
        rows = []
        for h in range(2):
            lo, hi = h * hd, (h + 1) * hd
            ssq = jnp.sum(_fold_rows(sq[lo:hi], jnp.add), axis=0, keepdims=True)
            y = xT[lo:hi] * lax.rsqrt(ssq * (1.0 / hd) + EPS) * gq_ref[lo:hi, :]
            y1, y2 = y[0:half], y[half:2 * half]
            rows += [y1 * cos - y2 * sin, y2 * cos + y1 * sin, y[2 * half:]]
        return jnp.concatenate(rows, axis=0)

    kmp_ref[...] = jnp.zeros_like(kmp_ref)
    ones_row = jnp.where(lax.broadcasted_iota(jnp.int32, (16, blk), 0) == 0, 1.0, 0.0).astype(BF16)
    lane_k = lax.broadcasted_iota(jnp.int32, (blk, LANES), 1)
    head0_k = lane_k < HEAD_DIM
    lane_m = lax.broadcasted_iota(jnp.int32, (nbp, LANES), 1)

    def prepare_keys(t):
        for u in range(sub):
            j = t * sub + u
            r0 = pl.multiple_of(jnp.minimum(j, nb - 1) * blk, blk)
            kr = norm_rope_keys(k_ref[0, pl.ds(r0, blk), :], r0)
            kmp_ref[pl.ds(j, 1), :] = jnp.sum(kr, axis=0, keepdims=True) * (1.0 / blk)
            kaug_ref[0, j] = jnp.where(head0_k, kr, jnp.where(lane_k - hd == j, 1.0, 0.0)).astype(BF16)
            kaug_ref[1, j] = jnp.where(head0_k, jnp.where(lane_k == j, 1.0, 0.0), kr).astype(BF16)
            vT = v_ref[0, pl.ds(r0, blk), :].astype(F32).T
            for h in range(2):
                vt_ref[h, j, 0:HEAD_DIM, :] = vT[h * HEAD_DIM:(h + 1) * HEAD_DIM].astype(BF16)
                vt_ref[h, j, HEAD_DIM:HEAD_DIM + 16, :] = ones_row
        kmp = kmp_ref[0:nbp, :]
        k_hi = kmp.astype(BF16)
        k_lo = (kmp - k_hi.astype(F32)).astype(BF16)
        zero = jnp.zeros((nbp, LANES), BF16)
        parts = []
        for h in range(2):
            mine = (lane_m < hd) if h == 0 else (lane_m >= hd)
            parts += [jnp.where(mine, k_hi, zero), jnp.where(mine, k_lo, zero)]
        parts += [parts[0], parts[2]]
        for n, part in enumerate(parts):
            kst_ref[n * nbp:(n + 1) * nbp, :] = part

    prepare_keys(0)

    key_i = lax.broadcasted_iota(jnp.int32, (blk, qt), 0)
    qry_i = lax.broadcasted_iota(jnp.int32, (blk, qt), 1)
    causal = [(qry_i < u * blk) | (qry_i >= (u + 1) * blk) | (key_i <= qry_i - u * blk) for u in range(sub)]
    rowf = lax.broadcasted_iota(jnp.int32, (nbp, qt), 0).astype(F32)
    subf = (lax.broadcasted_iota(jnp.int32, (nbp, qt), 1) // blk).astype(F32)
    q_scale = (hd ** -0.5) * 1.4426950408889634

    def query_operands(t):
        r0 = pl.multiple_of(t * qt, qt)
        qT = norm_rope_queries_t(q_ref[0, pl.ds(r0, qt), :], r0)
        cur = lax.convert_element_type(t * sub, F32) + subf
        q_hi = qT.astype(BF16)
        q_lo = (qT - q_hi.astype(F32)).astype(BF16)
        g1 = jnp.dot(kst_ref[0:4 * nbp, :], q_hi, preferred_element_type=F32)
        g2 = jnp.dot(kst_ref[4 * nbp:6 * nbp, :], q_lo, preferred_element_type=F32)
        qa = []
        for h in range(2):
            gate = g1[2 * h * nbp:(2 * h + 1) * nbp] + g1[(2 * h + 1) * nbp:(2 * h + 2) * nbp] + g2[h * nbp:(h + 1) * nbp]
            g = jnp.where(rowf < cur, gate, -jnp.inf)
            keep = rowf == cur
            for r in range(MOBA_TOPK):
                m = jnp.max(g, axis=0, keepdims=True)
                idx = jnp.min(jnp.where(g == m, rowf, 1e9), axis=0, keepdims=True)
                pick = (rowf == idx) & (cur > r)
                keep = keep | pick
                g = jnp.where(pick, -jnp.inf, g)
            bias = jnp.where(keep, 0.0, NEG)
            qs = qT[h * hd:(h + 1) * hd] * q_scale
            pad = jnp.zeros((LANES - hd - nbp, qt), F32)
            pieces = [qs, bias, pad] if h == 0 else [bias, pad, qs]
            qa.append(jnp.concatenate(pieces, axis=0).astype(BF16))
        return qa

    def pass1_tile(par, h, j, qa_h, mask):
        sT = jnp.dot(kaug_ref[h, j], qa_h, preferred_element_type=F32)
        if mask is not None:
            sT = jnp.where(mask, sT, NEG)
        s_ref[par, h, j] = sT
        return _fold_rows(sT, jnp.maximum)

    def pass2_tile(par, h, j):
        pT = jnp.exp2(s_ref[par, h, j] - mcol_ref[h, 0:1, :]).astype(BF16)
        return jnp.dot(vt_ref[h, j], pT, preferred_element_type=F32)

    def pass1_own(t, par, qa):
        mx = []
        for h in range(2):
            f = [pass1_tile(par, h, t * sub + u, qa[h], causal[u]) for u in range(sub)]
            mx.append(functools.reduce(jnp.maximum, f))
        return mx

    def pass2_own(t, par):
        return [sum(pass2_tile(par, h, t * sub + u) for u in range(sub)) for h in range(2)]

    def pass1_pair(p, par, qa, mx):
        return [functools.reduce(jnp.maximum, [mx[h]] + [pass1_tile(par, h, 2 * p + u, qa[h], None) for u in range(2)])
                for h in range(2)]

    def pass2_pair(p, par, acc):
        return [acc[h] + sum(pass2_tile(par, h, 2 * p + u) for u in range(2)) for h in range(2)]

    def finish_pass1(mx):
        for h in range(2):
            mcol_ref[h] = jnp.broadcast_to(jnp.max(mx[h], axis=0, keepdims=True), mcol_ref.shape[1:])

    def finish_pass2(t, acc):
        outT = jnp.concatenate([acc[h][0:hd] / acc[h][hd:hd + 1] for h in range(2)], axis=0)
        o_ref[0, pl.ds(pl.multiple_of(t * qt, qt), qt), :] = outT.T.astype(BF16)

    def stage(t, par):

        def prepare_next():
            prepare_keys(t + 1)
            nxt = query_operands(jnp.minimum(t + 1, nq - 1))
            for h in range(2):
                qa_ref[1 - par, h] = nxt[h]

        @pl.when(t == 0)
        def _():
            finish_pass1(pass1_own(t, par, query_operands(t)))
            prepare_next()

        @pl.when((t > 0) & (t < nq))
        def _():
            qa = [qa_ref[par, h] for h in range(2)]
            mx = pass1_own(t, par, qa)
            acc = pass2_own(t - 1, 1 - par)
            prepare_next()
            n_prev = (t - 1) * sub // 2

            def both(p, c):
                mx, acc = c
                return tuple(pass1_pair(p, par, qa, mx)), tuple(pass2_pair(p, 1 - par, acc))

            def both_twice(p2, c):
                return both(2 * p2 + 1, both(2 * p2, c))

            c = lax.fori_loop(0, n_prev // 2, both_twice, (tuple(mx), tuple(acc)))
            mx, acc = lax.fori_loop(n_prev // 2 * 2, n_prev, both, c)
            for p in range(sub // 2):
                mx = pass1_pair(n_prev + p, par, qa, mx)
            finish_pass2(t - 1, acc)
            finish_pass1(mx)

        @pl.when(t == nq)
        def _():
            acc = pass2_own(t - 1, 1 - par)
            acc = lax.fori_loop(0, (t - 1) * sub // 2, lambda p, a: tuple(pass2_pair(p, 1 - par, a)), tuple(acc))
            finish_pass2(t - 1, acc)

    def stage_pair(tt, carry):
        stage(2 * tt, 0)
        stage(2 * tt + 1, 1)
        return carry

    lax.fori_loop(0, (nq + 2) // 2, stage_pair, 0)


def _attention(z3, cosf, sinf, cost, sint, gq_cols, gk2):
    B, S, _ = z3.shape
    n_pair = N_HEADS // 2
    kq = ATTN_WIDTH // LANES
    nb = S // MOBA_BLOCK
    nbp = -(-nb // 16) * 16
    sub = QUERY_TILE // MOBA_BLOCK
    assert sub <= 8
    return pl.pallas_call(
        _attn_kernel,
        grid=(B, n_pair),
        in_specs=[pl.BlockSpec((1, S, LANES), lambda b, p: (b, 0, p)),
                  pl.BlockSpec((1, S, LANES), lambda b, p: (b, 0, kq + p)),
                  pl.BlockSpec((1, S, LANES), lambda b, p: (b, 0, 2 * kq + p)),
                  pl.BlockSpec((S, LANES), lambda b, p: (0, 0), pipeline_mode=pl.Buffered(1)),
                  pl.BlockSpec((S, LANES), lambda b, p: (0, 0), pipeline_mode=pl.Buffered(1)),
                  pl.BlockSpec(cost.shape, lambda b, p: (0, 0)),
                  pl.BlockSpec(sint.shape, lambda b, p: (0, 0)),
                  pl.BlockSpec((LANES, QUERY_TILE), lambda b, p: (0, 0)),
                  pl.BlockSpec((1, LANES), lambda b, p: (0, 0))],
        out_specs=pl.BlockSpec((1, S, LANES), lambda b, p: (b, 0, p)),
        out_shape=jax.ShapeDtypeStruct((B, S, ATTN_WIDTH), BF16),
        scratch_shapes=[pltpu.VMEM((2, nb + sub, MOBA_BLOCK, LANES), BF16),
                        pltpu.VMEM((2, nb + sub, HEAD_DIM + 16, MOBA_BLOCK), BF16),
                        pltpu.VMEM((nbp + 8, LANES), F32),
                        pltpu.VMEM((6 * nbp, LANES), BF16),
                        pltpu.VMEM((2, 2, nb, MOBA_BLOCK, QUERY_TILE), F32),
                        pltpu.VMEM((2, 8, QUERY_TILE), F32),
                        pltpu.VMEM((2, 2, LANES, QUERY_TILE), BF16)],
        compiler_params=pltpu.CompilerParams(dimension_semantics=("arbitrary", "arbitrary"),
                                             vmem_limit_bytes=ATTN_VMEM_LIMIT),
        name="attn",
    )(z3, z3, z3, cosf, sinf, cost, sint, gq_cols, gk2)


def _slabs(width):
    return width // LANES


def _load_slabs(ref, row0, rows, n_slab, lead=()):
    return jnp.concatenate([ref[lead + (pl.ds(row0 * n_slab + s, rows, stride=n_slab), slice(None))]
                            for s in range(n_slab)], axis=1)


def _store_slabs(ref, row0, val):
    rows, width = val.shape
    n_slab = _slabs(width)
    for s in range(n_slab):
        ref[pl.ds(row0 * n_slab + s, rows, stride=n_slab), :] = val[:, s * LANES:(s + 1) * LANES]


def _post_kernel(x_ref, ya_ref, xb_ref, bg_ref, cg_ref, gta_ref, gtb_ref, ga1_ref, sc2_ref, sh2_ref,
                 cw_ref, cb_ref, wpa_ref, wpb_ref, wo_ref, g2_ref, wr_ref, br_ref, tri_ref,
                 x1_ref, h2_ref, ri_ref, rf_ref, cnt_ref, ubuf_ref, run_ref, *, tiles_per_seq):
    i = pl.program_id(0)
    tm = x_ref.shape[0]
    rc = POST_CHUNK
    halo = 8

    @pl.when(i == 0)
    def _():
        run_ref[...] = jnp.zeros_like(run_ref)

    @pl.when(i % tiles_per_seq == 0)
    def _():
        ubuf_ref[0:halo, :] = jnp.zeros((halo, CONV_WIDTH), F32)

    ubuf_ref[halo:halo + tm, :] = cg_ref[...].astype(F32) * xb_ref[...].astype(F32)
    cw = cw_ref[...]
    lanef = lax.broadcasted_iota(jnp.int32, (rc, LANES), 1).astype(F32)
    half_d = x_ref.shape[1] // 2
    run = run_ref[0:1, :]

    for c in range(tm // rc):
        rows = pl.ds(c * rc, rc)
        conv = (cw[0:1, :] * ubuf_ref[pl.ds(halo - 2 + c * rc, rc), :]
                + cw[1:2, :] * ubuf_ref[pl.ds(halo - 1 + c * rc, rc), :]
                + cw[2:3, :] * ubuf_ref[pl.ds(halo + c * rc, rc), :])
        y_b = (bg_ref[rows, :].astype(F32) * (conv + cb_ref[...])).astype(BF16)
        pa = jnp.dot(ya_ref[rows, :], wpa_ref[...], preferred_element_type=F32)
        pb = jnp.dot(y_b, wpb_ref[...], preferred_element_type=F32)
        merged = (jax.nn.sigmoid(gta_ref[rows, :].astype(F32)) * pa
                  + jax.nn.sigmoid(gtb_ref[rows, :].astype(F32)) * pb).astype(BF16)
        x1 = x_ref[rows, :] + ga1_ref[0] * jnp.dot(merged, wo_ref[...], preferred_element_type=F32)
        x1_ref[rows, :] = x1

        ms = jnp.mean(x1 * x1, axis=-1, keepdims=True)
        h2 = x1 * lax.rsqrt(ms + EPS) * g2_ref[...]
        h2 = h2 * (1.0 + sc2_ref[0]) + sh2_ref[0]
        h_hi = h2.astype(BF16)
        h_hi32 = h_hi.astype(F32)
        bits = pltpu.bitcast(h_hi32, jnp.uint32)
        _store_slabs(h2_ref, c * rc, (bits[:, :half_d] >> 16) | bits[:, half_d:])

        h_lo = (h2 - h_hi32).astype(BF16)
        r = jnp.dot(h_hi, wr_ref[...], preferred_element_type=F32)
        logit = (r[:, :LANES] + r[:, LANES:]
                 + jnp.dot(h_lo, wr_ref[:, :LANES], preferred_element_type=F32) + br_ref[...])
        gl = jnp.where(lanef < N_GROUPS, logit, -jnp.inf)
        gmax = jnp.max(gl, axis=-1, keepdims=True)
        g_idx = jnp.min(jnp.where(gl == gmax, lanef, 1e9), axis=-1, keepdims=True)
        g_w = 1.0 / jnp.sum(jnp.exp(gl - gmax), axis=-1, keepdims=True)
        e_lo = N_GROUPS + EXPERTS_PER_GROUP * g_idx
        el = jnp.where((lanef >= e_lo) & (lanef < e_lo + EXPERTS_PER_GROUP), logit, -jnp.inf)
        v0 = jnp.max(el, axis=-1, keepdims=True)
        i0 = jnp.min(jnp.where(el == v0, lanef, 1e9), axis=-1, keepdims=True)
        el = jnp.where(lanef == i0, -jnp.inf, el)
        v1 = jnp.max(el, axis=-1, keepdims=True)
        i1 = jnp.min(jnp.where(el == v1, lanef, 1e9), axis=-1, keepdims=True)
        t = jnp.exp(v1 - v0)
        w0 = g_w / (1.0 + t)
        w1 = g_w * t / (1.0 + t)
        e0 = i0 - N_GROUPS
        e1 = i1 - N_GROUPS

        oh0 = lanef == e0
        oh1 = lanef == e1
        oh = jnp.where(oh0 | oh1, 1.0, 0.0)
        before = jnp.dot(tri_ref[...], oh.astype(BF16), preferred_element_type=F32) + run
        r0 = jnp.sum(jnp.where(oh0, before, 0.0), axis=-1, keepdims=True)
        r1 = jnp.sum(jnp.where(oh1, before, 0.0), axis=-1, keepdims=True)
        run = run + jnp.sum(oh, axis=0, keepdims=True)

        ri = jnp.where(lanef == 0, e0, jnp.where(lanef == 1, e1, jnp.where(lanef == 2, r0, jnp.where(lanef == 3, r1, 0.0))))
        ri_ref[:, rows] = ri.astype(jnp.int32).T[0:8]
        rf_ref[rows, :] = jnp.where(lanef == 0, w0, jnp.where(lanef == 1, w1, 0.0))

    ubuf_ref[0:halo, :] = ubuf_ref[tm:tm + halo, :]
    run_ref[...] = jnp.broadcast_to(run, run_ref.shape)
    cnt_ref[...] = jnp.broadcast_to(run, cnt_ref.shape)


def _post(x2, ya2, z2, ga1, sc2, sh2, conv_w, conv_b, wpa, wpb, wo, g2, wr, br, tri, tiles_per_seq):
    T, D = x2.shape
    tm = POST_TILE
    cw = CONV_WIDTH
    xcol = 3 * ATTN_WIDTH // cw
    gcol = (3 * ATTN_WIDTH + 3 * cw) // D
    bmap = lambda i: (i // tiles_per_seq, 0, 0)
    const = lambda i: (0, 0)
    return pl.pallas_call(
        functools.partial(_post_kernel, tiles_per_seq=tiles_per_seq),
        grid=(T // tm,),
        in_specs=[pl.BlockSpec((tm, D), lambda i: (i, 0)),
                  pl.BlockSpec((tm, ATTN_WIDTH), lambda i: (i, 0)),
                  pl.BlockSpec((tm, cw), lambda i: (i, xcol)),
                  pl.BlockSpec((tm, cw), lambda i: (i, xcol + 1)),
                  pl.BlockSpec((tm, cw), lambda i: (i, xcol + 2)),
                  pl.BlockSpec((tm, D), lambda i: (i, gcol)),
                  pl.BlockSpec((tm, D), lambda i: (i, gcol + 1)),
                  pl.BlockSpec((1, 1, D), bmap),
                  pl.BlockSpec((1, 1, D), bmap),
                  pl.BlockSpec((1, 1, D), bmap),
                  pl.BlockSpec((CONV_K, cw), const),
                  pl.BlockSpec((1, cw), const),
                  pl.BlockSpec((ATTN_WIDTH, D), const),
                  pl.BlockSpec((cw, D), const),
                  pl.BlockSpec((D, D), const),
                  pl.BlockSpec((1, D), const),
                  pl.BlockSpec((D, 2 * LANES), const),
                  pl.BlockSpec((1, LANES), const),
                  pl.BlockSpec((POST_CHUNK, POST_CHUNK), const)],
        out_specs=[pl.BlockSpec((tm, D), lambda i: (i, 0)),
                   pl.BlockSpec((tm * _slabs(D // 2), LANES), lambda i: (i, 0)),
                   pl.BlockSpec((8, tm), lambda i: (0, i)),
                   pl.BlockSpec((tm, LANES), lambda i: (i, 0)),
                   pl.BlockSpec((8, LANES), const)],
        out_shape=[jax.ShapeDtypeStruct((T, D), F32),
                   jax.ShapeDtypeStruct((T * _slabs(D // 2), LANES), jnp.uint32),
                   jax.ShapeDtypeStruct((8, T), jnp.int32),
                   jax.ShapeDtypeStruct((T, LANES), F32),
                   jax.ShapeDtypeStruct((8, LANES), F32)],
        scratch_shapes=[pltpu.VMEM((tm + 16, cw), F32),
                        pltpu.VMEM((8, LANES), F32)],
        compiler_params=_cparams("arbitrary"),
        name="post",
    )(x2, ya2, z2, z2, z2, z2, z2, ga1, sc2, sh2, conv_w, conv_b, wpa, wpb, wo, g2, wr, br, tri)


def _slots_kernel(ps_ref, ri_ref, d_ref, *, n_slab):
    e = ri_ref[0:TOPK_IN_GROUP, :]
    start = jnp.zeros(e.shape, jnp.int32)
    for k in range(N_EXPERTS):
        start = jnp.where(e == k, ps_ref[k], start)
    d_ref[...] = (start + ri_ref[TOPK_IN_GROUP:2 * TOPK_IN_GROUP, :]) * n_slab


def _slots(pad_start, riT, n_slab):
    T = riT.shape[1]
    grid_spec = pltpu.PrefetchScalarGridSpec(
        num_scalar_prefetch=1,
        grid=(1,),
        in_specs=[pl.BlockSpec(riT.shape, lambda i, ps: (0, 0))],
        out_specs=pl.BlockSpec((TOPK_IN_GROUP, T), lambda i, ps: (0, 0)),
    )
    return pl.pallas_call(
        functools.partial(_slots_kernel, n_slab=n_slab),
        grid_spec=grid_spec,
        out_shape=jax.ShapeDtypeStruct((TOPK_IN_GROUP, T), jnp.int32),
        compiler_params=_cparams("arbitrary"),
        name="slots",
    )(pad_start, riT)


def _dispatch_kernel(d0_ref, d1_ref, tail_ref, h_ref, xs_ref, zero_ref, sem, zsem, *, n_slab):
    tm = h_ref.shape[0] // n_slab
    base = pl.program_id(0) * tm
    blk_rows = zero_ref.shape[0]

    @pl.when(pl.program_id(0) == 0)
    def _():
        zero_ref[...] = jnp.zeros_like(zero_ref)

        def tail_copy(e):
            start = pl.multiple_of(tail_ref[e] * blk_rows, blk_rows)
            return pltpu.make_async_copy(zero_ref, xs_ref.at[pl.ds(start, blk_rows), :], zsem)

        for e in range(tail_ref.shape[0]):
            @pl.when(tail_ref[e] >= 0)
            def _():
                tail_copy(e).start()
        for e in range(tail_ref.shape[0]):
            @pl.when(tail_ref[e] >= 0)
            def _():
                tail_copy(e).wait()

    def body(g, carry):
        r8 = pl.multiple_of(g * DMA_UNROLL, DMA_UNROLL)
        for u in range(DMA_UNROLL):
            src = h_ref.at[pl.ds((r8 + u) * n_slab, n_slab), :]
            pltpu.make_async_copy(src, xs_ref.at[pl.ds(d0_ref[base + r8 + u], n_slab), :], sem).start(priority=0)
            pltpu.make_async_copy(src, xs_ref.at[pl.ds(d1_ref[base + r8 + u], n_slab), :], sem).start(priority=1)
        return carry

    lax.fori_loop(0, tm // DMA_UNROLL, body, 0)
    for _ in range(TOPK_IN_GROUP):
        pltpu.make_async_copy(h_ref, xs_ref.at[pl.ds(0, tm * n_slab), :], sem).wait()


def _dispatch(dest0, dest1, tail_blk, h2p, n_pad, n_slab):
    tm = DISPATCH_TILE
    T = h2p.shape[0] // n_slab
    grid_spec = pltpu.PrefetchScalarGridSpec(
        num_scalar_prefetch=3,
        grid=(T // tm,),
        in_specs=[pl.BlockSpec((tm * n_slab, LANES), lambda i, d0, d1, tb: (i, 0))],
        out_specs=pl.BlockSpec(memory_space=pl.ANY),
        scratch_shapes=[pltpu.VMEM((EXPERT_ROWS * n_slab, LANES), h2p.dtype),
                        pltpu.SemaphoreType.DMA(()),
                        pltpu.SemaphoreType.DMA(())],
    )
    return pl.pallas_call(
        functools.partial(_dispatch_kernel, n_slab=n_slab),
        grid_spec=grid_spec,
        out_shape=jax.ShapeDtypeStruct((n_pad * n_slab, LANES), h2p.dtype),
        compiler_params=_cparams("arbitrary"),
        name="dispatch",
    )(dest0, dest1, tail_blk, h2p)


def _expert_kernel(be_ref, nv_ref, xs_ref, w1f_ref, w3f_ref, w2f_ref, ys_ref, w1_ref, w3_ref, w2_ref):
    i = pl.program_id(0)

    @pl.when((i == 0) | (be_ref[i] != be_ref[jnp.maximum(i - 1, 0)]))
    def _():
        w1_ref[...] = w1f_ref[...].astype(BF16)
        w3_ref[...] = w3f_ref[...].astype(BF16)
        w2_ref[...] = w2f_ref[...].astype(BF16)

    @pl.when(i >= nv_ref[0])
    def _():
        ys_ref[...] = jnp.zeros_like(ys_ref)

    @pl.when(i < nv_ref[0])
    def _():
        half_d = w1_ref.shape[1] // 2
        xp = _load_slabs(xs_ref, 0, EXPERT_ROWS, _slabs(half_d))
        x_lo = pltpu.bitcast(xp << 16, F32).astype(BF16)
        x_hi = pltpu.bitcast(xp & jnp.uint32(0xFFFF0000), F32).astype(BF16)
        a = (jnp.dot(x_lo, w1_ref[0, :half_d], preferred_element_type=F32)
             + jnp.dot(x_hi, w1_ref[0, half_d:], preferred_element_type=F32))
        b = (jnp.dot(x_lo, w3_ref[0, :half_d], preferred_element_type=F32)
             + jnp.dot(x_hi, w3_ref[0, half_d:], preferred_element_type=F32))
        hid = (a * jax.nn.sigmoid(a) * b).astype(BF16)
        y = jnp.dot(hid, w2_ref[0], preferred_element_type=F32)
        bits = pltpu.bitcast(y.astype(BF16).astype(F32), jnp.uint32)
        _store_slabs(ys_ref, 0, (bits[:, :half_d] >> 16) | bits[:, half_d:])


def _experts(blk_expert, n_valid, xs, w1f, w3f, w2f):
    _, D, F = w1f.shape
    rb = EXPERT_ROWS
    blk_rows = rb * _slabs(D // 2)
    row_blk = lambda i, be, nv: (jnp.minimum(i, nv[0] - 1), 0)
    out_blk = lambda i, be, nv: (i, 0)
    wgt_blk = lambda i, be, nv: (be[i], 0, 0)
    grid_spec = pltpu.PrefetchScalarGridSpec(
        num_scalar_prefetch=2,
        grid=(xs.shape[0] // blk_rows,),
        in_specs=[pl.BlockSpec((blk_rows, LANES), row_blk),
                  pl.BlockSpec((1, D, F), wgt_blk),
                  pl.BlockSpec((1, D, F), wgt_blk),
                  pl.BlockSpec((1, F, D), wgt_blk)],
        out_specs=pl.BlockSpec((blk_rows, LANES), out_blk),
        scratch_shapes=[pltpu.VMEM((1, D, F), BF16), pltpu.VMEM((1, D, F), BF16), pltpu.VMEM((1, F, D), BF16)],
    )
    return pl.pallas_call(
        _expert_kernel,
        grid_spec=grid_spec,
        out_shape=jax.ShapeDtypeStruct(xs.shape, jnp.uint32),
        compiler_params=_cparams("arbitrary"),
        name="experts",
    )(blk_expert, n_valid, xs, w1f, w3f, w2f)


def _combine_kernel(d0_ref, d1_ref, x1_ref, rf_ref, ga2_ref, ys_ref, o_ref, buf_ref, sem, *, n_slab):
    i = pl.program_id(0)
    n = pl.num_programs(0)
    tm = x1_ref.shape[0]

    def gather(step, slot):
        base = step * tm

        def body(g, carry):
            r8 = pl.multiple_of(g * DMA_UNROLL, DMA_UNROLL)
            for u in range(DMA_UNROLL):
                dst = pl.ds((r8 + u) * n_slab, n_slab)
                pltpu.make_async_copy(ys_ref.at[pl.ds(d0_ref[base + r8 + u], n_slab), :],
                                      buf_ref.at[slot, 0, dst, :], sem.at[slot]).start(priority=0)
                pltpu.make_async_copy(ys_ref.at[pl.ds(d1_ref[base + r8 + u], n_slab), :],
                                      buf_ref.at[slot, 1, dst, :], sem.at[slot]).start(priority=1)
            return carry

        lax.fori_loop(0, tm // DMA_UNROLL, body, 0)

    @pl.when(i == 0)
    def _():
        gather(0, 0)

    @pl.when(i + 1 < n)
    def _():
        gather(i + 1, (i + 1) % 2)

    slot = i % 2
    for k in range(TOPK_IN_GROUP):
        pltpu.make_async_copy(ys_ref.at[pl.ds(0, tm * n_slab), :], buf_ref.at[slot, k], sem.at[slot]).wait()
    rf = rf_ref[...]
    w = [rf[:, k:k + 1] for k in range(TOPK_IN_GROUP)]
    yp = [_load_slabs(buf_ref, 0, tm, n_slab, lead=(slot, k)) for k in range(TOPK_IN_GROUP)]
    half_d = n_slab * LANES
    y_lo = sum(w[k] * pltpu.bitcast(yp[k] << 16, F32) for k in range(TOPK_IN_GROUP))
    y_hi = sum(w[k] * pltpu.bitcast(yp[k] & jnp.uint32(0xFFFF0000), F32) for k in range(TOPK_IN_GROUP))
    ga2 = ga2_ref[0]
    o_ref[:, :half_d] = x1_ref[:, :half_d] + ga2[:, :half_d] * y_lo
    o_ref[:, half_d:] = x1_ref[:, half_d:] + ga2[:, half_d:] * y_hi


def _combine(dest0, dest1, x1, rf, ga2, ys, tiles_per_seq):
    T, D = x1.shape
    n_slab = _slabs(D // 2)
    tm = COMBINE_TILE
    row = lambda i, d0, d1: (i, 0)
    grid_spec = pltpu.PrefetchScalarGridSpec(
        num_scalar_prefetch=2,
        grid=(T // tm,),
        in_specs=[pl.BlockSpec((tm, D), row),
                  pl.BlockSpec((tm, LANES), row),
                  pl.BlockSpec((1, 1, D), lambda i, d0, d1: (i // tiles_per_seq, 0, 0)),
                  pl.BlockSpec(memory_space=pl.ANY)],
        out_specs=pl.BlockSpec((tm, D), row),
        scratch_shapes=[pltpu.VMEM((2, TOPK_IN_GROUP, tm * n_slab, LANES), ys.dtype),
                        pltpu.SemaphoreType.DMA((2,))],
    )
    return pl.pallas_call(
        functools.partial(_combine_kernel, n_slab=n_slab),
        grid_spec=grid_spec,
        out_shape=jax.ShapeDtypeStruct((T, D), F32),
        compiler_params=_cparams("arbitrary"),
        name="combine",
    )(dest0, dest1, x1, rf, ga2, ys)


def _rope_tables(S):
    pos = jnp.arange(S, dtype=F32)
    inv_freq = ROPE_THETA ** (-jnp.arange(0, ROT_DIM, 2, dtype=F32) / ROT_DIM)
    ang = pos[:, None] * inv_freq[None, :]
    cos, sin = jnp.cos(ang), jnp.sin(ang)
    half = ROT_DIM // 2
    ones = jnp.ones((S, HEAD_DIM - ROT_DIM), F32)
    cos_h = jnp.concatenate([cos, cos, ones], axis=1)
    sin_h = jnp.concatenate([-sin, sin, 0.0 * ones], axis=1)
    return jnp.tile(cos_h, (1, LANES // HEAD_DIM)), jnp.tile(sin_h, (1, LANES // HEAD_DIM)), cos.T, sin.T


def kernel(x, c, w_ada, b_ada, g_norm1, g_norm2, w_in, g_q, g_k, conv_w, conv_b,
           w_pa, w_pb, w_o, w_rg, b_rg, w_re, b_re, w1, w3, w2):
    B, S, D = x.shape
    T = B * S
    assert S % POST_TILE == 0 and S % COMBINE_TILE == 0 and T % DISPATCH_TILE == 0 and S % QUERY_TILE == 0 and QUERY_TILE % (2 * MOBA_BLOCK) == 0
    assert S // MOBA_BLOCK <= LANES - HEAD_DIM
    l = 0

    mod = _ada(c, w_ada[l], b_ada[l])
    sh1, sc1, ga1, sh2, sc2, ga2 = [m.reshape(B, 1, D) for m in jnp.split(mod, N_MOD, axis=-1)]

    x2 = x.reshape(T, D)
    z2 = _inproj(x2, g_norm1[l].reshape(1, D), sc1, sh1, w_in[l].astype(BF16), S)

    cosf, sinf, cost, sint = _rope_tables(S)
    rep = LANES // HEAD_DIM
    gq_cols = jnp.broadcast_to(jnp.tile(g_q[l], rep).reshape(LANES, 1), (LANES, QUERY_TILE))
    ya = _attention(z2.reshape(B, S, -1), cosf, sinf, cost, sint, gq_cols, jnp.tile(g_k[l], rep).reshape(1, LANES))

    wr = jnp.zeros((D, LANES), F32).at[:, :N_GROUPS].set(w_rg[l]).at[:, N_GROUPS:N_GROUPS + N_EXPERTS].set(w_re[l])
    br = jnp.zeros((1, LANES), F32).at[0, :N_GROUPS].set(b_rg[l]).at[0, N_GROUPS:N_GROUPS + N_EXPERTS].set(b_re[l])
    wr_hi = wr.astype(BF16)
    wr2 = jnp.concatenate([wr_hi, (wr - wr_hi.astype(F32)).astype(BF16)], axis=1)
    tri = (lax.broadcasted_iota(jnp.int32, (POST_CHUNK, POST_CHUNK), 1)
           < lax.broadcasted_iota(jnp.int32, (POST_CHUNK, POST_CHUNK), 0)).astype(BF16)
    x1, h2, ri, rf, cnt = _post(x2, ya.reshape(T, ATTN_WIDTH), z2, ga1, sc2, sh2,
                                conv_w[l], conv_b[l].reshape(1, CONV_WIDTH),
                                w_pa[l].astype(BF16), w_pb[l].astype(BF16), w_o[l].astype(BF16),
                                g_norm2[l].reshape(1, D), wr2, br, tri, S // POST_TILE)

    rb = EXPERT_ROWS
    counts = cnt[0, :N_EXPERTS].astype(jnp.int32)
    padded = (counts + rb - 1) // rb * rb
    pad_end = jnp.cumsum(padded)
    pad_start = pad_end - padded
    dest = _slots(pad_start.astype(jnp.int32), ri, _slabs(D // 2))
    dest0, dest1 = dest[0], dest[1]
    n_blocks = -(-T * TOPK_IN_GROUP // rb) + N_EXPERTS
    n_pad = n_blocks * rb
    n_valid = (pad_end[-1] // rb).astype(jnp.int32)
    blk_start = jnp.minimum(jnp.arange(n_blocks, dtype=jnp.int32), n_valid - 1) * rb
    blk_expert = jnp.sum(pad_end[None, :] <= blk_start[:, None], axis=-1).astype(jnp.int32)
    unused = n_valid + jnp.arange(N_EXPERTS, dtype=jnp.int32)
    tail_blk = jnp.concatenate([jnp.where(padded > 0, pad_end // rb - 1, -1),
                                jnp.where(unused < n_blocks, unused, -1)]).astype(jnp.int32)

    xs = _dispatch(dest0, dest1, tail_blk, h2, n_pad, _slabs(D // 2))
    ys = _experts(blk_expert, n_valid.reshape(1), xs, w1[l], w3[l], w2[l])
    out = _combine(dest0, dest1, x1, rf, ga2, ys, S // COMBINE_TILE)
    return out.reshape(B, S, D)
```

```python
import functools

import jax
import jax.numpy as jnp
from jax import lax
from jax.experimental import pallas as pl
from jax.experimental.pallas import tpu as pltpu

F32 = jnp.float32
BF16 = jnp.bfloat16
HIGHEST = lax.Precision.HIGHEST

N_HEADS = 8
HEAD_DIM = 64
ATTN_WIDTH = N_HEADS * HEAD_DIM
CONV_WIDTH = 512
CONV_K = 3
MOBA_BLOCK = 256
MOBA_TOPK = 3
ROPE_THETA = 500000.0
ROT_DIM = HEAD_DIM // 4
N_GROUPS = 4
EXPERTS_PER_GROUP = 8
N_EXPERTS = N_GROUPS * EXPERTS_PER_GROUP
TOPK_IN_GROUP = 2
N_MOD = 6
EPS = 1e-6

LANES = 128
NEG = -1e30
POST_TILE = 1024
COMBINE_TILE = 512
INPROJ_TILE = 1024
DISPATCH_TILE = 2048
POST_CHUNK = 256
QUERY_TILE = 512
DMA_UNROLL = 8
EXPERT_ROWS = 512
VMEM_LIMIT = 56 * 1024 * 1024
ATTN_VMEM_LIMIT = 60 * 1024 * 1024


def _cparams(*sem):
    return pltpu.CompilerParams(dimension_semantics=sem, vmem_limit_bytes=VMEM_LIMIT)


def _ada_kernel(c_ref, w_ref, b_ref, o_ref):
    c = c_ref[...]
    a = c * jax.nn.sigmoid(c)
    o_ref[...] = jnp.dot(a, w_ref[...], preferred_element_type=F32, precision=HIGHEST) + b_ref[...]


def _ada(c, w_ada, b_ada):
    B, D = c.shape
    N = w_ada.shape[1]
    tn = 1536
    return pl.pallas_call(
        _ada_kernel,
        grid=(N // tn,),
        in_specs=[pl.BlockSpec((B, D), lambda j: (0, 0)),
                  pl.BlockSpec((D, tn), lambda j: (0, j)),
                  pl.BlockSpec((1, tn), lambda j: (0, j))],
        out_specs=pl.BlockSpec((B, tn), lambda j: (0, j)),
        out_shape=jax.ShapeDtypeStruct((B, N), F32),
        compiler_params=_cparams("arbitrary"),
        name="ada",
    )(c, w_ada, b_ada.reshape(1, N))


def _inproj_kernel(x_ref, g_ref, sc_ref, sh_ref, w_ref, z_ref, *, n_chunk):
    x = x_ref[...]
    ms = jnp.mean(x * x, axis=-1, keepdims=True)
    y = x * lax.rsqrt(ms + EPS) * g_ref[...]
    h = (y * (1.0 + sc_ref[0]) + sh_ref[0]).astype(BF16)
    for n in range(0, z_ref.shape[1], n_chunk):
        z_ref[:, n:n + n_chunk] = jnp.dot(h, w_ref[:, n:n + n_chunk],
                                          preferred_element_type=F32).astype(BF16)


def _inproj(x2, g1, sc1, sh1, w_in_bf, seq_len):
    T, D = x2.shape
    N = w_in_bf.shape[1]
    tm = INPROJ_TILE
    assert seq_len % tm == 0
    bmap = lambda i: (i // (seq_len // tm), 0, 0)
    return pl.pallas_call(
        functools.partial(_inproj_kernel, n_chunk=512),
        grid=(T // tm,),
        in_specs=[pl.BlockSpec((tm, D), lambda i: (i, 0)),
                  pl.BlockSpec((1, D), lambda i: (0, 0)),
                  pl.BlockSpec((1, 1, D), bmap),
                  pl.BlockSpec((1, 1, D), bmap),
                  pl.BlockSpec((D, N), lambda i: (0, 0), pipeline_mode=pl.Buffered(1))],
        out_specs=pl.BlockSpec((tm, N), lambda i: (i, 0)),
        out_shape=jax.ShapeDtypeStruct((T, N), BF16),
        compiler_params=_cparams("arbitrary"),
        name="inproj",
    )(x2, g1, sc1, sh1, w_in_bf)


def _fold_rows(x, op):
    parts = [x[r:r + 8] for r in range(0, x.shape[0], 8)]
    while len(parts) > 1:
        parts = [op(parts[i], parts[i + 1]) for i in range(0, len(parts) - 1, 2)] + (
            [parts[-1]] if len(parts) % 2 else [])
    return parts[0]


def _attn_kernel(q_ref, k_ref, v_ref, cos_ref, sin_ref, cost_ref, sint_ref, gq_ref, gk_ref, o_ref,
                 kaug_ref, vt_ref, kmp_ref, kst_ref, s_ref, mcol_ref, qa_ref):
    S = q_ref.shape[1]
    blk = MOBA_BLOCK
    qt = QUERY_TILE
    sub = qt // blk
    nb = S // blk
    nq = S // qt
    nbp = kst_ref.shape[0] // 6
    hd = HEAD_DIM

    half = ROT_DIM // 2
    lane_r = lax.broadcasted_iota(jnp.int32, (blk, LANES), 1)
    rot_lo = (lane_r & (hd - 1)) < half
    same_head = jnp.where((lax.broadcasted_iota(jnp.int32, (LANES, LANES), 0) < hd)
                          == (lax.broadcasted_iota(jnp.int32, (LANES, LANES), 1) < hd), 1.0, 0.0).astype(BF16)

    def norm_rope_keys(xb, r0):
        x = xb.astype(F32)
        sq = x * x
        sq_hi = sq.astype(BF16)
        sq_lo = (sq - sq_hi.astype(F32)).astype(BF16)
        ssq = (jnp.dot(sq_hi, same_head, preferred_element_type=F32)
               + jnp.dot(sq_lo, same_head, preferred_element_type=F32))
        y = x * lax.rsqrt(ssq * (1.0 / hd) + EPS) * gk_ref[...]
        rot = jnp.where(rot_lo, pltpu.roll(y, LANES - half, 1), pltpu.roll(y, half, 1))
        return y * cos_ref[pl.ds(r0, blk), :] + rot * sin_ref[pl.ds(r0, blk), :]

    def norm_rope_queries_t(xb, r0):
        xT = xb.astype(F32).T
        sq = xT * xT
        cos = cost_ref[:, pl.ds(r0, qt)]
        sin = sint_ref[:, pl.ds(r0, qt)]
        rows = []
        for h in range(2):
            lo, hi = h * hd, (h + 1) * hd
            ssq = jnp.sum(_fold_rows(sq[lo:hi], jnp.add), axis=0, keepdims=True)
            y = xT[lo:hi] * lax.rsqrt(ssq * (1.0 / hd) + EPS) * gq_ref[lo:hi, :]
            y1, y2 = y[0:half], y[half:2 * half]
            rows += [y1 * cos - y2 * sin, y2 * cos + y1 * sin, y[2 * half:]]
        return jnp.concatenate(rows, axis=0)

    kmp_ref[...] = jnp.zeros_like(kmp_ref)
    ones_row = jnp.where(lax.broadcasted_iota(jnp.int32, (16, blk), 0) == 0, 1.0, 0.0).astype(BF16)
    lane_k = lax.broadcasted_iota(jnp.int32, (blk, LANES), 1)
    head0_k = lane_k < HEAD_DIM
    lane_m = lax.broadcasted_iota(jnp.int32, (nbp, LANES), 1)

    def prepare_keys(t):
        for u in range(sub):
            j = t * sub + u
            r0 = pl.multiple_of(jnp.minimum(j, nb - 1) * blk, blk)
            kr = norm_rope_keys(k_ref[0, pl.ds(r0, blk), :], r0)
            kmp_ref[pl.ds(j, 1), :] = jnp.sum(kr, axis=0, keepdims=True) * (1.0 / blk)
            kaug_ref[0, j] = jnp.where(head0_k, kr, jnp.where(lane_k - hd == j, 1.0, 0.0)).astype(BF16)
            kaug_ref[1, j] = jnp.where(head0_k, jnp.where(lane_k == j, 1.0, 0.0), kr).astype(BF16)
            vT = v_ref[0, pl.ds(r0, blk), :].astype(F32).T
            for h in range(2):
                vt_ref[h, j, 0:HEAD_DIM, :] = vT[h * HEAD_DIM:(h + 1) * HEAD_DIM].astype(BF16)
                vt_ref[h, j, HEAD_DIM:HEAD_DIM + 16, :] = ones_row
        kmp = kmp_ref[0:nbp, :]
        k_hi = kmp.astype(BF16)
        k_lo = (kmp - k_hi.astype(F32)).astype(BF16)
        zero = jnp.zeros((nbp, LANES), BF16)
        parts = []
        for h in range(2):
            mine = (lane_m < hd) if h == 0 else (lane_m >= hd)
            parts += [jnp.where(mine, k_hi, zero), jnp.where(mine, k_lo, zero)]
        parts += [parts[0], parts[2]]
        for n, part in enumerate(parts):
            kst_ref[n * nbp:(n + 1) * nbp, :] = part

    prepare_keys(0)

    key_i = lax.broadcasted_iota(jnp.int32, (blk, qt), 0)
    qry_i = lax.broadcasted_iota(jnp.int32, (blk, qt), 1)
    causal = [(qry_i < u * blk) | (qry_i >= (u + 1) * blk) | (key_i <= qry_i - u * blk) for u in range(sub)]
    rowf = lax.broadcasted_iota(jnp.int32, (nbp, qt), 0).astype(F32)
    subf = (lax.broadcasted_iota(jnp.int32, (nbp, qt), 1) // blk).astype(F32)
    q_scale = (hd ** -0.5) * 1.4426950408889634

    def query_operands(t):
        r0 = pl.multiple_of(t * qt, qt)
        qT = norm_rope_queries_t(q_ref[0, pl.ds(r0, qt), :], r0)
        cur = lax.convert_element_type(t * sub, F32) + subf
        q_hi = qT.astype(BF16)
        q_lo = (qT - q_hi.astype(F32)).astype(BF16)
        g1 = jnp.dot(kst_ref[0:4 * nbp, :], q_hi, preferred_element_type=F32)
        g2 = jnp.dot(kst_ref[4 * nbp:6 * nbp, :], q_lo, preferred_element_type=F32)
        qa = []
        for h in range(2):
            gate = g1[2 * h * nbp:(2 * h + 1) * nbp] + g1[(2 * h + 1) * nbp:(2 * h + 2) * nbp] + g2[h * nbp:(h + 1) * nbp]
            g = jnp.where(rowf < cur, gate, -jnp.inf)
            keep = rowf == cur
            for r in range(MOBA_TOPK):
                m = jnp.max(g, axis=0, keepdims=True)
                idx = jnp.min(jnp.where(g == m, rowf, 1e9), axis=0, keepdims=True)
                pick = (rowf == idx) & (cur > r)
                keep = keep | pick
                g = jnp.where(pick, -jnp.inf, g)
            bias = jnp.where(keep, 0.0, NEG)
            qs = qT[h * hd:(h + 1) * hd] * q_scale
            pad = jnp.zeros((LANES - hd - nbp, qt), F32)
            pieces = [qs, bias, pad] if h == 0 else [bias, pad, qs]
            qa.append(jnp.concatenate(pieces, axis=0).astype(BF16))
        return qa

    def pass1_tile(par, h, j, qa_h, mask):
        sT = jnp.dot(kaug_ref[h, j], qa_h, preferred_element_type=F32)
        if mask is not None:
            sT = jnp.where(mask, sT, NEG)
        s_ref[par, h, j] = sT
        return _fold_rows(sT, jnp.maximum)

    def pass2_tile(par, h, j):
        pT = jnp.exp2(s_ref[par, h, j] - mcol_ref[h, 0:1, :]).astype(BF16)
        return jnp.dot(vt_ref[h, j], pT, preferred_element_type=F32)

    def pass1_own(t, par, qa):
        mx = []
        for h in range(2):
            f = [pass1_tile(par, h, t * sub + u, qa[h], causal[u]) for u in range(sub)]
            mx.append(functools.reduce(jnp.maximum, f))
        return mx

    def pass2_own(t, par):
        return [sum(pass2_tile(par, h, t * sub + u) for u in range(sub)) for h in range(2)]

    def pass1_pair(p, par, qa, mx):
        return [functools.reduce(jnp.maximum, [mx[h]] + [pass1_tile(par, h, 2 * p + u, qa[h], None) for u in range(2)])
                for h in range(2)]

    def pass2_pair(p, par, acc):
        return [acc[h] + sum(pass2_tile(par, h, 2 * p + u) for u in range(2)) for h in range(2)]

    def finish_pass1(mx):
        for h in range(2):
            mcol_ref[h] = jnp.broadcast_to(jnp.max(mx[h], axis=0, keepdims=True), mcol_ref.shape[1:])

    def finish_pass2(t, acc):
        outT = jnp.concatenate([acc[h][0:hd] / acc[h][hd:hd + 1] for h in range(2)], axis=0)
        o_ref[0, pl.ds(pl.multiple_of(t * qt, qt), qt), :] = outT.T.astype(BF16)

    def stage(t, par):

        def prepare_next():
            prepare_keys(t + 1)
            nxt = query_operands(jnp.minimum(t + 1, nq - 1))
            for h in range(2):
                qa_ref[1 - par, h] = nxt[h]

        @pl.when(t == 0)
        def _():
            finish_pass1(pass1_own(t, par, query_operands(t)))
            prepare_next()

        @pl.when((t > 0) & (t < nq))
        def _():
            qa = [qa_ref[par, h] for h in range(2)]
            mx = pass1_own(t, par, qa)
            acc = pass2_own(t - 1, 1 - par)
            prepare_next()
            n_prev = (t - 1) * sub // 2

            def both(p, c):
                mx, acc = c
                return tuple(pass1_pair(p, par, qa, mx)), tuple(pass2_pair(p, 1 - par, acc))

            def both_twice(p2, c):
                return both(2 * p2 + 1, both(2 * p2, c))

            c = lax.fori_loop(0, n_prev // 2, both_twice, (tuple(mx), tuple(acc)))
            mx, acc = lax.fori_loop(n_prev // 2 * 2, n_prev, both, c)
            for p in range(sub // 2):
                mx = pass1_pair(n_prev + p, par, qa, mx)
            finish_pass2(t - 1, acc)
            finish_pass1(mx)

        @pl.when(t == nq)
        def _():
            acc = pass2_own(t - 1, 1 - par)
            acc = lax.fori_loop(0, (t - 1) * sub // 2, lambda p, a: tuple(pass2_pair(p, 1 - par, a)), tuple(acc))
            finish_pass2(t - 1, acc)

    def stage_pair(tt, carry):
        stage(2 * tt, 0)
        stage(2 * tt + 1, 1)
        return carry

    lax.fori_loop(0, (nq + 2) // 2, stage_pair, 0)


def _attention(z3, cosf, sinf, cost, sint, gq_cols, gk2):
    B, S, _ = z3.shape
    n_pair = N_HEADS // 2
    kq = ATTN_WIDTH // LANES
    nb = S // MOBA_BLOCK
    nbp = -(-nb // 16) * 16
    sub = QUERY_TILE // MOBA_BLOCK
    assert sub <= 8
    return pl.pallas_call(
        _attn_kernel,
        grid=(B, n_pair),
        in_specs=[pl.BlockSpec((1, S, LANES), lambda b, p: (b, 0, p)),
                  pl.BlockSpec((1, S, LANES), lambda b, p: (b, 0, kq + p)),
                  pl.BlockSpec((1, S, LANES), lambda b, p: (b, 0, 2 * kq + p)),
                  pl.BlockSpec((S, LANES), lambda b, p: (0, 0), pipeline_mode=pl.Buffered(1)),
                  pl.BlockSpec((S, LANES), lambda b, p: (0, 0), pipeline_mode=pl.Buffered(1)),
                  pl.BlockSpec(cost.shape, lambda b, p: (0, 0)),
                  pl.BlockSpec(sint.shape, lambda b, p: (0, 0)),
                  pl.BlockSpec((LANES, QUERY_TILE), lambda b, p: (0, 0)),
                  pl.BlockSpec((1, LANES), lambda b, p: (0, 0))],
        out_specs=pl.BlockSpec((1, S, LANES), lambda b, p: (b, 0, p)),
        out_shape=jax.ShapeDtypeStruct((B, S, ATTN_WIDTH), BF16),
        scratch_shapes=[pltpu.VMEM((2, nb + sub, MOBA_BLOCK, LANES), BF16),
                        pltpu.VMEM((2, nb + sub, HEAD_DIM + 16, MOBA_BLOCK), BF16),
                        pltpu.VMEM((nbp + 8, LANES), F32),
                        pltpu.VMEM((6 * nbp, LANES), BF16),
                        pltpu.VMEM((2, 2, nb, MOBA_BLOCK, QUERY_TILE), F32),
                        pltpu.VMEM((2, 8, QUERY_TILE), F32),
                        pltpu.VMEM((2, 2, LANES, QUERY_TILE), BF16)],
        compiler_params=pltpu.CompilerParams(dimension_semantics=("arbitrary", "arbitrary"),
                                             vmem_limit_bytes=ATTN_VMEM_LIMIT),
        name="attn",
    )(z3, z3, z3, cosf, sinf, cost, sint, gq_cols, gk2)


def _slabs(width):
    return width // LANES


def _load_slabs(ref, row0, rows, n_slab, lead=()):
    return jnp.concatenate([ref[lead + (pl.ds(row0 * n_slab + s, rows, stride=n_slab), slice(None))]
                            for s in range(n_slab)], axis=1)


def _store_slabs(ref, row0, val):
    rows, width = val.shape
    n_slab = _slabs(width)
    for s in range(n_slab):
        ref[pl.ds(row0 * n_slab + s, rows, stride=n_slab), :] = val[:, s * LANES:(s + 1) * LANES]


def _post_kernel(x_ref, ya_ref, xb_ref, bg_ref, cg_ref, gta_ref, gtb_ref, ga1_ref, sc2_ref, sh2_ref,
                 cw_ref, cb_ref, wpa_ref, wpb_ref, wo_ref, g2_ref, wr_ref, br_ref, tri_ref,
                 x1_ref, h2_ref, ri_ref, rf_ref, cnt_ref, ubuf_ref, run_ref, *, tiles_per_seq):
    i = pl.program_id(0)
    tm = x_ref.shape[0]
    rc = POST_CHUNK
    halo = 8

    @pl.when(i == 0)
    def _():
        run_ref[...] = jnp.zeros_like(run_ref)

    @pl.when(i % tiles_per_seq == 0)
    def _():
        ubuf_ref[0:halo, :] = jnp.zeros((halo, CONV_WIDTH), F32)

    ubuf_ref[halo:halo + tm, :] = cg_ref[...].astype(F32) * xb_ref[...].astype(F32)
    cw = cw_ref[...]
    lanef = lax.broadcasted_iota(jnp.int32, (rc, LANES), 1).astype(F32)
    half_d = x_ref.shape[1] // 2
    run = run_ref[0:1, :]

    for c in range(tm // rc):
        rows = pl.ds(c * rc, rc)
        conv = (cw[0:1, :] * ubuf_ref[pl.ds(halo - 2 + c * rc, rc), :]
                + cw[1:2, :] * ubuf_ref[pl.ds(halo - 1 + c * rc, rc), :]
                + cw[2:3, :] * ubuf_ref[pl.ds(halo + c * rc, rc), :])
        y_b = (bg_ref[rows, :].astype(F32) * (conv + cb_ref[...])).astype(BF16)
        pa = jnp.dot(ya_ref[rows, :], wpa_ref[...], preferred_element_type=F32)
        pb = jnp.dot(y_b, wpb_ref[...], preferred_element_type=F32)
        merged = (jax.nn.sigmoid(gta_ref[rows, :].astype(F32)) * pa
                  + jax.nn.sigmoid(gtb_ref[rows, :].astype(F32)) * pb).astype(BF16)
        x1 = x_ref[rows, :] + ga1_ref[0] * jnp.dot(merged, wo_ref[...], preferred_element_type=F32)
        x1_ref[rows, :] = x1

        ms = jnp.mean(x1 * x1, axis=-1, keepdims=True)
        h2 = x1 * lax.rsqrt(ms + EPS) * g2_ref[...]
        h2 = h2 * (1.0 + sc2_ref[0]) + sh2_ref[0]
        h_hi = h2.astype(BF16)
        h_hi32 = h_hi.astype(F32)
        bits = pltpu.bitcast(h_hi32, jnp.uint32)
        _store_slabs(h2_ref, c * rc, (bits[:, :half_d] >> 16) | bits[:, half_d:])

        h_lo = (h2 - h_hi32).astype(BF16)
        r = jnp.dot(h_hi, wr_ref[...], preferred_element_type=F32)
        logit = (r[:, :LANES] + r[:, LANES:]
                 + jnp.dot(h_lo, wr_ref[:, :LANES], preferred_element_type=F32) + br_ref[...])
        gl = jnp.where(lanef < N_GROUPS, logit, -jnp.inf)
        gmax = jnp.max(gl, axis=-1, keepdims=True)
        g_idx = jnp.min(jnp.where(gl == gmax, lanef, 1e9), axis=-1, keepdims=True)
        g_w = 1.0 / jnp.sum(jnp.exp(gl - gmax), axis=-1, keepdims=True)
        e_lo = N_GROUPS + EXPERTS_PER_GROUP * g_idx
        el = jnp.where((lanef >= e_lo) & (lanef < e_lo + EXPERTS_PER_GROUP), logit, -jnp.inf)
        v0 = jnp.max(el, axis=-1, keepdims=True)
        i0 = jnp.min(jnp.where(el == v0, lanef, 1e9), axis=-1, keepdims=True)
        el = jnp.where(lanef == i0, -jnp.inf, el)
        v1 = jnp.max(el, axis=-1, keepdims=True)
        i1 = jnp.min(jnp.where(el == v1, lanef, 1e9), axis=-1, keepdims=True)
        t = jnp.exp(v1 - v0)
        w0 = g_w / (1.0 + t)
        w1 = g_w * t / (1.0 + t)
        e0 = i0 - N_GROUPS
        e1 = i1 - N_GROUPS

        oh0 = lanef == e0
        oh1 = lanef == e1
        oh = jnp.where(oh0 | oh1, 1.0, 0.0)
        before = jnp.dot(tri_ref[...], oh.astype(BF16), preferred_element_type=F32) + run
        r0 = jnp.sum(jnp.where(oh0, before, 0.0), axis=-1, keepdims=True)
        r1 = jnp.sum(jnp.where(oh1, before, 0.0), axis=-1, keepdims=True)
        run = run + jnp.sum(oh, axis=0, keepdims=True)

        ri = jnp.where(lanef == 0, e0, jnp.where(lanef == 1, e1, jnp.where(lanef == 2, r0, jnp.where(lanef == 3, r1, 0.0))))
        ri_ref[:, rows] = ri.astype(jnp.int32).T[0:8]
        rf_ref[rows, :] = jnp.where(lanef == 0, w0, jnp.where(lanef == 1, w1, 0.0))

    ubuf_ref[0:halo, :] = ubuf_ref[tm:tm + halo, :]
    run_ref[...] = jnp.broadcast_to(run, run_ref.shape)
    cnt_ref[...] = jnp.broadcast_to(run, cnt_ref.shape)


def _post(x2, ya2, z2, ga1, sc2, sh2, conv_w, conv_b, wpa, wpb, wo, g2, wr, br, tri, tiles_per_seq):
    T, D = x2.shape
    tm = POST_TILE
    cw = CONV_WIDTH
    xcol = 3 * ATTN_WIDTH // cw
    gcol = (3 * ATTN_WIDTH + 3 * cw) // D
    bmap = lambda i: (i // tiles_per_seq, 0, 0)
    const = lambda i: (0, 0)
    return pl.pallas_call(
        functools.partial(_post_kernel, tiles_per_seq=tiles_per_seq),
        grid=(T // tm,),
        in_specs=[pl.BlockSpec((tm, D), lambda i: (i, 0)),
                  pl.BlockSpec((tm, ATTN_WIDTH), lambda i: (i, 0)),
                  pl.BlockSpec((tm, cw), lambda i: (i, xcol)),
                  pl.BlockSpec((tm, cw), lambda i: (i, xcol + 1)),
                  pl.BlockSpec((tm, cw), lambda i: (i, xcol + 2)),
                  pl.BlockSpec((tm, D), lambda i: (i, gcol)),
                  pl.BlockSpec((tm, D), lambda i: (i, gcol + 1)),
                  pl.BlockSpec((1, 1, D), bmap),
                  pl.BlockSpec((1, 1, D), bmap),
                  pl.BlockSpec((1, 1, D), bmap),
                  pl.BlockSpec((CONV_K, cw), const),
                  pl.BlockSpec((1, cw), const),
                  pl.BlockSpec((ATTN_WIDTH, D), const),
                  pl.BlockSpec((cw, D), const),
                  pl.BlockSpec((D, D), const),
                  pl.BlockSpec((1, D), const),
                  pl.BlockSpec((D, 2 * LANES), const),
                  pl.BlockSpec((1, LANES), const),
                  pl.BlockSpec((POST_CHUNK, POST_CHUNK), const)],
        out_specs=[pl.BlockSpec((tm, D), lambda i: (i, 0)),
                   pl.BlockSpec((tm * _slabs(D // 2), LANES), lambda i: (i, 0)),
                   pl.BlockSpec((8, tm), lambda i: (0, i)),
                   pl.BlockSpec((tm, LANES), lambda i: (i, 0)),
                   pl.BlockSpec((8, LANES), const)],
        out_shape=[jax.ShapeDtypeStruct((T, D), F32),
                   jax.ShapeDtypeStruct((T * _slabs(D // 2), LANES), jnp.uint32),
                   jax.ShapeDtypeStruct((8, T), jnp.int32),
                   jax.ShapeDtypeStruct((T, LANES), F32),
                   jax.ShapeDtypeStruct((8, LANES), F32)],
        scratch_shapes=[pltpu.VMEM((tm + 16, cw), F32),
                        pltpu.VMEM((8, LANES), F32)],
        compiler_params=_cparams("arbitrary"),
        name="post",
    )(x2, ya2, z2, z2, z2, z2, z2, ga1, sc2, sh2, conv_w, conv_b, wpa, wpb, wo, g2, wr, br, tri)


def _slots_kernel(ps_ref, ri_ref, d_ref, *, n_slab):
    e = ri_ref[0:TOPK_IN_GROUP, :]
    start = jnp.zeros(e.shape, jnp.int32)
    for k in range(N_EXPERTS):
        start = jnp.where(e == k, ps_ref[k], start)
    d_ref[...] = (start + ri_ref[TOPK_IN_GROUP:2 * TOPK_IN_GROUP, :]) * n_slab


def _slots(pad_start, riT, n_slab):
    T = riT.shape[1]
    grid_spec = pltpu.PrefetchScalarGridSpec(
        num_scalar_prefetch=1,
        grid=(1,),
        in_specs=[pl.BlockSpec(riT.shape, lambda i, ps: (0, 0))],
        out_specs=pl.BlockSpec((TOPK_IN_GROUP, T), lambda i, ps: (0, 0)),
    )
    return pl.pallas_call(
        functools.partial(_slots_kernel, n_slab=n_slab),
        grid_spec=grid_spec,
        out_shape=jax.ShapeDtypeStruct((TOPK_IN_GROUP, T), jnp.int32),
        compiler_params=_cparams("arbitrary"),
        name="slots",
    )(pad_start, riT)


def _dispatch_kernel(d0_ref, d1_ref, tail_ref, h_ref, xs_ref, zero_ref, sem, zsem, *, n_slab):
    tm = h_ref.shape[0] // n_slab
    base = pl.program_id(0) * tm
    blk_rows = zero_ref.shape[0]

    @pl.when(pl.program_id(0) == 0)
    def _():
        zero_ref[...] = jnp.zeros_like(zero_ref)

        def tail_copy(e):
            start = pl.multiple_of(tail_ref[e] * blk_rows, blk_rows)
            return pltpu.make_async_copy(zero_ref, xs_ref.at[pl.ds(start, blk_rows), :], zsem)

        for e in range(tail_ref.shape[0]):
            @pl.when(tail_ref[e] >= 0)
            def _():
                tail_copy(e).start()
        for e in range(tail_ref.shape[0]):
            @pl.when(tail_ref[e] >= 0)
            def _():
                tail_copy(e).wait()

    def body(g, carry):
        r8 = pl.multiple_of(g * DMA_UNROLL, DMA_UNROLL)
        for u in range(DMA_UNROLL):
            src = h_ref.at[pl.ds((r8 + u) * n_slab, n_slab), :]
            pltpu.make_async_copy(src, xs_ref.at[pl.ds(d0_ref[base + r8 + u], n_slab), :], sem).start(priority=0)
            pltpu.make_async_copy(src, xs_ref.at[pl.ds(d1_ref[base + r8 + u], n_slab), :], sem).start(priority=1)
        return carry

    lax.fori_loop(0, tm // DMA_UNROLL, body, 0)
    for _ in range(TOPK_IN_GROUP):
        pltpu.make_async_copy(h_ref, xs_ref.at[pl.ds(0, tm * n_slab), :], sem).wait()


def _dispatch(dest0, dest1, tail_blk, h2p, n_pad, n_slab):
    tm = DISPATCH_TILE
    T = h2p.shape[0] // n_slab
    grid_spec = pltpu.PrefetchScalarGridSpec(
        num_scalar_prefetch=3,
        grid=(T // tm,),
        in_specs=[pl.BlockSpec((tm * n_slab, LANES), lambda i, d0, d1, tb: (i, 0))],
        out_specs=pl.BlockSpec(memory_space=pl.ANY),
        scratch_shapes=[pltpu.VMEM((EXPERT_ROWS * n_slab, LANES), h2p.dtype),
                        pltpu.SemaphoreType.DMA(()),
                        pltpu.SemaphoreType.DMA(())],
    )
    return pl.pallas_call(
        functools.partial(_dispatch_kernel, n_slab=n_slab),
        grid_spec=grid_spec,
        out_shape=jax.ShapeDtypeStruct((n_pad * n_slab, LANES), h2p.dtype),
        compiler_params=_cparams("arbitrary"),
        name="dispatch",
    )(dest0, dest1, tail_blk, h2p)


def _expert_kernel(be_ref, nv_ref, nxt_ref, slot_ref, xs_ref, w1f_ref, w3f_ref, w2f_ref, ys_ref,
                   w1_ref, w3_ref, w2_ref, f1_ref, f3_ref, f2_ref, sem):
    i = pl.program_id(0)
    wf_hbm = (w1f_ref, w3f_ref, w2f_ref)
    wf_vmem = (f1_ref, f3_ref, f2_ref)
    w_bf16 = (w1_ref, w3_ref, w2_ref)

    def fetch(e, s):
        return [pltpu.make_async_copy(wf_hbm[m].at[e], wf_vmem[m].at[s], sem.at[s, m]) for m in range(3)]

    @pl.when(i == 0)
    def _():
        for cp in fetch(be_ref[0], slot_ref[0]):
            cp.start()

    @pl.when((i == 0) | (be_ref[i] != be_ref[jnp.maximum(i - 1, 0)]))
    def _():
        s = slot_ref[i]
        for m, cp in enumerate(fetch(be_ref[i], s)):
            cp.wait()
            w_bf16[m][...] = wf_vmem[m][s].astype(BF16)

        @pl.when(nxt_ref[i] >= 0)
        def _():
            for cp in fetch(nxt_ref[i], 1 - s):
                cp.start()

    @pl.when(i >= nv_ref[0])
    def _():
        ys_ref[...] = jnp.zeros_like(ys_ref)

    @pl.when(i < nv_ref[0])
    def _():
        half_d = w1_ref.shape[0] // 2
        xp = _load_slabs(xs_ref, 0, EXPERT_ROWS, _slabs(half_d))
        x_lo = pltpu.bitcast(xp << 16, F32).astype(BF16)
        x_hi = pltpu.bitcast(xp & jnp.uint32(0xFFFF0000), F32).astype(BF16)
        a = (jnp.dot(x_lo, w1_ref[:half_d], preferred_element_type=F32)
             + jnp.dot(x_hi, w1_ref[half_d:], preferred_element_type=F32))
        b = (jnp.dot(x_lo, w3_ref[:half_d], preferred_element_type=F32)
             + jnp.dot(x_hi, w3_ref[half_d:], preferred_element_type=F32))
        hid = (a * jax.nn.sigmoid(a) * b).astype(BF16)
        y = jnp.dot(hid, w2_ref[...], preferred_element_type=F32)
        bits = pltpu.bitcast(y.astype(BF16).astype(F32), jnp.uint32)
        _store_slabs(ys_ref, 0, (bits[:, :half_d] >> 16) | bits[:, half_d:])


def _experts(blk_expert, n_valid, xs, w1f, w3f, w2f):
    _, D, F = w1f.shape
    rb = EXPERT_ROWS
    blk_rows = rb * _slabs(D // 2)
    n_steps = xs.shape[0] // blk_rows
    first = jnp.concatenate([jnp.ones((1,), jnp.bool_), blk_expert[1:] != blk_expert[:-1]])
    slot = ((jnp.cumsum(first.astype(jnp.int32)) - 1) % 2).astype(jnp.int32)
    later = jnp.where(blk_expert[None, :] > blk_expert[:, None], blk_expert[None, :], N_EXPERTS)
    nxt = jnp.min(later, axis=1)
    nxt = jnp.where(nxt < N_EXPERTS, nxt, -1).astype(jnp.int32)
    row_blk = lambda i, be, nv, nx, sl: (jnp.minimum(i, nv[0] - 1), 0)
    out_blk = lambda i, be, nv, nx, sl: (i, 0)
    grid_spec = pltpu.PrefetchScalarGridSpec(
        num_scalar_prefetch=4,
        grid=(n_steps,),
        in_specs=[pl.BlockSpec((blk_rows, LANES), row_blk),
                  pl.BlockSpec(memory_space=pl.ANY),
                  pl.BlockSpec(memory_space=pl.ANY),
                  pl.BlockSpec(memory_space=pl.ANY)],
        out_specs=pl.BlockSpec((blk_rows, LANES), out_blk),
        scratch_shapes=[pltpu.VMEM((D, F), BF16), pltpu.VMEM((D, F), BF16), pltpu.VMEM((F, D), BF16),
                        pltpu.VMEM((2, D, F), F32), pltpu.VMEM((2, D, F), F32), pltpu.VMEM((2, F, D), F32),
                        pltpu.SemaphoreType.DMA((2, 3))],
    )
    return pl.pallas_call(
        _expert_kernel,
        grid_spec=grid_spec,
        out_shape=jax.ShapeDtypeStruct(xs.shape, jnp.uint32),
        compiler_params=_cparams("arbitrary"),
        name="experts",
    )(blk_expert, n_valid, nxt, slot, xs, w1f, w3f, w2f)


def _combine_kernel(d0_ref, d1_ref, x1_ref, rf_ref, ga2_ref, ys_ref, o_ref, buf_ref, sem, *, n_slab):
    i = pl.program_id(0)
    n = pl.num_programs(0)
    tm = x1_ref.shape[0]

    def gather(step, slot):
        base = step * tm

        def body(g, carry):
            r8 = pl.multiple_of(g * DMA_UNROLL, DMA_UNROLL)
            for u in range(DMA_UNROLL):
                dst = pl.ds((r8 + u) * n_slab, n_slab)
                pltpu.make_async_copy(ys_ref.at[pl.ds(d0_ref[base + r8 + u], n_slab), :],
                                      buf_ref.at[slot, 0, dst, :], sem.at[slot]).start(priority=0)
                pltpu.make_async_copy(ys_ref.at[pl.ds(d1_ref[base + r8 + u], n_slab), :],
                                      buf_ref.at[slot, 1, dst, :], sem.at[slot]).start(priority=1)
            return carry

        lax.fori_loop(0, tm // DMA_UNROLL, body, 0)

    @pl.when(i == 0)
    def _():
        gather(0, 0)

    @pl.when(i + 1 < n)
    def _():
        gather(i + 1, (i + 1) % 2)

    slot = i % 2
    for k in range(TOPK_IN_GROUP):
        pltpu.make_async_copy(ys_ref.at[pl.ds(0, tm * n_slab), :], buf_ref.at[slot, k], sem.at[slot]).wait()
    rf = rf_ref[...]
    w = [rf[:, k:k + 1] for k in range(TOPK_IN_GROUP)]
    yp = [_load_slabs(buf_ref, 0, tm, n_slab, lead=(slot, k)) for k in range(TOPK_IN_GROUP)]
    half_d = n_slab * LANES
    y_lo = sum(w[k] * pltpu.bitcast(yp[k] << 16, F32) for k in range(TOPK_IN_GROUP))
    y_hi = sum(w[k] * pltpu.bitcast(yp[k] & jnp.uint32(0xFFFF0000), F32) for k in range(TOPK_IN_GROUP))
    ga2 = ga2_ref[0]
    o_ref[:, :half_d] = x1_ref[:, :half_d] + ga2[:, :half_d] * y_lo
    o_ref[:, half_d:] = x1_ref[:, half_d:] + ga2[:, half_d:] * y_hi


def _combine(dest0, dest1, x1, rf, ga2, ys, tiles_per_seq):
    T, D = x1.shape
    n_slab = _slabs(D // 2)
    tm = COMBINE_TILE
    row = lambda i, d0, d1: (i, 0)
    grid_spec = pltpu.PrefetchScalarGridSpec(
        num_scalar_prefetch=2,
        grid=(T // tm,),
        in_specs=[pl.BlockSpec((tm, D), row),
                  pl.BlockSpec((tm, LANES), row),
                  pl.BlockSpec((1, 1, D), lambda i, d0, d1: (i // tiles_per_seq, 0, 0)),
                  pl.BlockSpec(memory_space=pl.ANY)],
        out_specs=pl.BlockSpec((tm, D), row),
        scratch_shapes=[pltpu.VMEM((2, TOPK_IN_GROUP, tm * n_slab, LANES), ys.dtype),
                        pltpu.SemaphoreType.DMA((2,))],
    )
    return pl.pallas_call(
        functools.partial(_combine_kernel, n_slab=n_slab),
        grid_spec=grid_spec,
        out_shape=jax.ShapeDtypeStruct((T, D), F32),
        compiler_params=_cparams("arbitrary"),
        name="combine",
    )(dest0, dest1, x1, rf, ga2, ys)


def _rope_tables(S):
    pos = jnp.arange(S, dtype=F32)
    inv_freq = ROPE_THETA ** (-jnp.arange(0, ROT_DIM, 2, dtype=F32) / ROT_DIM)
    ang = pos[:, None] * inv_freq[None, :]
    cos, sin = jnp.cos(ang), jnp.sin(ang)
    half = ROT_DIM // 2
    ones = jnp.ones((S, HEAD_DIM - ROT_DIM), F32)
    cos_h = jnp.concatenate([cos, cos, ones], axis=1)
    sin_h = jnp.concatenate([-sin, sin, 0.0 * ones], axis=1)
    return jnp.tile(cos_h, (1, LANES // HEAD_DIM)), jnp.tile(sin_h, (1, LANES // HEAD_DIM)), cos.T, sin.T


def kernel(x, c, w_ada, b_ada, g_norm1, g_norm2, w_in, g_q, g_k, conv_w, conv_b,
           w_pa, w_pb, w_o, w_rg, b_rg, w_re, b_re, w1, w3, w2):
    B, S, D = x.shape
    T = B * S
    assert S % POST_TILE == 0 and S % COMBINE_TILE == 0 and T % DISPATCH_TILE == 0 and S % QUERY_TILE == 0 and QUERY_TILE % (2 * MOBA_BLOCK) == 0
    assert S // MOBA_BLOCK <= LANES - HEAD_DIM
    l = 0

    mod = _ada(c, w_ada[l], b_ada[l])
    sh1, sc1, ga1, sh2, sc2, ga2 = [m.reshape(B, 1, D) for m in jnp.split(mod, N_MOD, axis=-1)]

    x2 = x.reshape(T, D)
    z2 = _inproj(x2, g_norm1[l].reshape(1, D), sc1, sh1, w_in[l].astype(BF16), S)

    cosf, sinf, cost, sint = _rope_tables(S)
    rep = LANES // HEAD_DIM
    gq_cols = jnp.broadcast_to(jnp.tile(g_q[l], rep).reshape(LANES, 1), (LANES, QUERY_TILE))
    ya = _attention(z2.reshape(B, S, -1), cosf, sinf, cost, sint, gq_cols, jnp.tile(g_k[l], rep).reshape(1, LANES))

    wr = jnp.zeros((D, LANES), F32).at[:, :N_GROUPS].set(w_rg[l]).at[:, N_GROUPS:N_GROUPS + N_EXPERTS].set(w_re[l])
    br = jnp.zeros((1, LANES), F32).at[0, :N_GROUPS].set(b_rg[l]).at[0, N_GROUPS:N_GROUPS + N_EXPERTS].set(b_re[l])
    wr_hi = wr.astype(BF16)
    wr2 = jnp.concatenate([wr_hi, (wr - wr_hi.astype(F32)).astype(BF16)], axis=1)
    tri = (lax.broadcasted_iota(jnp.int32, (POST_CHUNK, POST_CHUNK), 1)
           < lax.broadcasted_iota(jnp.int32, (POST_CHUNK, POST_CHUNK), 0)).astype(BF16)
    x1, h2, ri, rf, cnt = _post(x2, ya.reshape(T, ATTN_WIDTH), z2, ga1, sc2, sh2,
                                conv_w[l], conv_b[l].reshape(1, CONV_WIDTH),
                                w_pa[l].astype(BF16), w_pb[l].astype(BF16), w_o[l].astype(BF16),
                                g_norm2[l].reshape(1, D), wr2, br, tri, S // POST_TILE)

    rb = EXPERT_ROWS
    counts = cnt[0, :N_EXPERTS].astype(jnp.int32)
    padded = (counts + rb - 1) // rb * rb
    pad_end = jnp.cumsum(padded)
    pad_start = pad_end - padded
    dest = _slots(pad_start.astype(jnp.int32), ri, _slabs(D // 2))
    dest0, dest1 = dest[0], dest[1]
    n_blocks = -(-T * TOPK_IN_GROUP // rb) + N_EXPERTS
    n_pad = n_blocks * rb
    n_valid = (pad_end[-1] // rb).astype(jnp.int32)
    blk_start = jnp.minimum(jnp.arange(n_blocks, dtype=jnp.int32), n_valid - 1) * rb
    blk_expert = jnp.sum(pad_end[None, :] <= blk_start[:, None], axis=-1).astype(jnp.int32)
    unused = n_valid + jnp.arange(N_EXPERTS, dtype=jnp.int32)
    tail_blk = jnp.concatenate([jnp.where(padded > 0, pad_end // rb - 1, -1),
                                jnp.where(unused < n_blocks, unused, -1)]).astype(jnp.int32)

    xs = _dispatch(dest0, dest1, tail_blk, h2, n_pad, _slabs(D // 2))
    ys = _experts(blk_expert, n_valid.reshape(1), xs, w1[l], w3[l], w2[l])
    out = _combine(dest0, dest1, x1, rf, ga2, ys, S // COMBINE_TILE)
    return out.reshape(B, S, D)
```

```python
import functools

import jax
import jax.numpy as jnp
from jax import lax
from jax.experimental import pallas as pl
from jax.experimental.pallas import tpu as pltpu

F32 = jnp.float32
BF16 = jnp.bfloat16
HIGHEST = lax.Precision.HIGHEST

N_HEADS = 8
HEAD_DIM = 64
ATTN_WIDTH = N_HEADS * HEAD_DIM
CONV_WIDTH = 512
CONV_K = 3
MOBA_BLOCK = 256
MOBA_TOPK = 3
ROPE_THETA = 500000.0
ROT_DIM = HEAD_DIM // 4
N_GROUPS = 4
EXPERTS_PER_GROUP = 8
N_EXPERTS = N_GROUPS * EXPERTS_PER_GROUP
TOPK_IN_GROUP = 2
N_MOD = 6
EPS = 1e-6

LANES = 128
NEG = -1e30
POST_TILE = 1024
COMBINE_TILE = 512
INPROJ_TILE = 1024
DISPATCH_TILE = 2048
POST_CHUNK = 256
QUERY_TILE = 512
DMA_UNROLL = 8
EXPERT_ROWS = 512
VMEM_LIMIT = 56 * 1024 * 1024
ATTN_VMEM_LIMIT = 60 * 1024 * 1024


def _cparams(*sem):
    return pltpu.CompilerParams(dimension_semantics=sem, vmem_limit_bytes=VMEM_LIMIT)


def _ada_kernel(c_ref, w_ref, b_ref, o_ref):
    c = c_ref[...]
    a = c * jax.nn.sigmoid(c)
    o_ref[...] = jnp.dot(a, w_ref[...], preferred_element_type=F32, precision=HIGHEST) + b_ref[...]


def _ada(c, w_ada, b_ada):
    B, D = c.shape
    N = w_ada.shape[1]
    tn = 1536
    return pl.pallas_call(
        _ada_kernel,
        grid=(N // tn,),
        in_specs=[pl.BlockSpec((B, D), lambda j: (0, 0)),
                  pl.BlockSpec((D, tn), lambda j: (0, j)),
                  pl.BlockSpec((1, tn), lambda j: (0, j))],
        out_specs=pl.BlockSpec((B, tn), lambda j: (0, j)),
        out_shape=jax.ShapeDtypeStruct((B, N), F32),
        compiler_params=_cparams("arbitrary"),
        name="ada",
    )(c, w_ada, b_ada.reshape(1, N))


def _inproj_kernel(x_ref, g_ref, sc_ref, sh_ref, w_ref, z_ref, *, n_chunk):
    x = x_ref[...]
    ms = jnp.mean(x * x, axis=-1, keepdims=True)
    y = x * lax.rsqrt(ms + EPS) * g_ref[...]
    h = (y * (1.0 + sc_ref[0]) + sh_ref[0]).astype(BF16)
    for n in range(0, z_ref.shape[1], n_chunk):
        z_ref[:, n:n + n_chunk] = jnp.dot(h, w_ref[:, n:n + n_chunk],
                                          preferred_element_type=F32).astype(BF16)


def _inproj(x2, g1, sc1, sh1, w_in_bf, seq_len):
    T, D = x2.shape
    N = w_in_bf.shape[1]
    tm = INPROJ_TILE
    assert seq_len % tm == 0
    bmap = lambda i: (i // (seq_len // tm), 0, 0)
    return pl.pallas_call(
        functools.partial(_inproj_kernel, n_chunk=512),
        grid=(T // tm,),
        in_specs=[pl.BlockSpec((tm, D), lambda i: (i, 0)),
                  pl.BlockSpec((1, D), lambda i: (0, 0)),
                  pl.BlockSpec((1, 1, D), bmap),
                  pl.BlockSpec((1, 1, D), bmap),
                  pl.BlockSpec((D, N), lambda i: (0, 0), pipeline_mode=pl.Buffered(1))],
        out_specs=pl.BlockSpec((tm, N), lambda i: (i, 0)),
        out_shape=jax.ShapeDtypeStruct((T, N), BF16),
        compiler_params=_cparams("arbitrary"),
        name="inproj",
    )(x2, g1, sc1, sh1, w_in_bf)


def _fold_rows(x, op):
    parts = [x[r:r + 8] for r in range(0, x.shape[0], 8)]
    while len(parts) > 1:
        parts = [op(parts[i], parts[i + 1]) for i in range(0, len(parts) - 1, 2)] + (
            [parts[-1]] if len(parts) % 2 else [])
    return parts[0]


def _attn_kernel(q_ref, k_ref, v_ref, cos_ref, sin_ref, cost_ref, sint_ref, gq_ref, gk_ref, o_ref,
                 kaug_ref, vt_ref, kmp_ref, kst_ref, s_ref, mcol_ref, qa_ref):
    S = q_ref.shape[1]
    blk = MOBA_BLOCK
    qt = QUERY_TILE
    sub = qt // blk
    nb = S // blk
    nq = S // qt
    nbp = kst_ref.shape[0] // 6
    hd = HEAD_DIM

    half = ROT_DIM // 2
    lane_r = lax.broadcasted_iota(jnp.int32, (blk, LANES), 1)
    rot_lo = (lane_r & (hd - 1)) < half
    same_head = jnp.where((lax.broadcasted_iota(jnp.int32, (LANES, LANES), 0) < hd)
                          == (lax.broadcasted_iota(jnp.int32, (LANES, LANES), 1) < hd), 1.0, 0.0).astype(BF16)

    def norm_rope_keys(xb, r0):
        x = xb.astype(F32)
        sq = x * x
        sq_hi = sq.astype(BF16)
        sq_lo = (sq - sq_hi.astype(F32)).astype(BF16)
        ssq = (jnp.dot(sq_hi, same_head, preferred_element_type=F32)
               + jnp.dot(sq_lo, same_head, preferred_element_type=F32))
        y = x * lax.rsqrt(ssq * (1.0 / hd) + EPS) * gk_ref[...]
        rot = jnp.where(rot_lo, pltpu.roll(y, LANES - half, 1), pltpu.roll(y, half, 1))
        return y * cos_ref[pl.ds(r0, blk), :] + rot * sin_ref[pl.ds(r0, blk), :]

    def norm_rope_queries_t(xb, r0):
        xT = xb.astype(F32).T
        sq = xT * xT
        cos = cost_ref[:, pl.ds(r0, qt)]
        sin = sint_ref[:, pl.ds(r0, qt)]
        rows = []
        for h in range(2):
            lo, hi = h * hd, (h + 1) * hd
            ssq = jnp.sum(_fold_rows(sq[lo:hi], jnp.add), axis=0, keepdims=True)
            y = xT[lo:hi] * lax.rsqrt(ssq * (1.0 / hd) + EPS) * gq_ref[lo:hi, :]
            y1, y2 = y[0:half], y[half:2 * half]
            rows += [y1 * cos - y2 * sin, y2 * cos + y1 * sin, y[2 * half:]]
        return jnp.concatenate(rows, axis=0)

    kmp_ref[...] = jnp.zeros_like(kmp_ref)
    ones_row = jnp.where(lax.broadcasted_iota(jnp.int32, (16, blk), 0) == 0, 1.0, 0.0).astype(BF16)
    lane_k = lax.broadcasted_iota(jnp.int32, (blk, LANES), 1)
    head0_k = lane_k < HEAD_DIM
    lane_m = lax.broadcasted_iota(jnp.int32, (nbp, LANES), 1)

    def prepare_keys(t):
        for u in range(sub):
            j = t * sub + u
            r0 = pl.multiple_of(jnp.minimum(j, nb - 1) * blk, blk)
            kr = norm_rope_keys(k_ref[0, pl.ds(r0, blk), :], r0)
            kmp_ref[pl.ds(j, 1), :] = jnp.sum(kr, axis=0, keepdims=True) * (1.0 / blk)
            kaug_ref[0, j] = jnp.where(head0_k, kr, jnp.where(lane_k - hd == j, 1.0, 0.0)).astype(BF16)
            kaug_ref[1, j] = jnp.where(head0_k, jnp.where(lane_k == j, 1.0, 0.0), kr).astype(BF16)
            vT = v_ref[0, pl.ds(r0, blk), :].astype(F32).T
            for h in range(2):
                vt_ref[h, j, 0:HEAD_DIM, :] = vT[h * HEAD_DIM:(h + 1) * HEAD_DIM].astype(BF16)
                vt_ref[h, j, HEAD_DIM:HEAD_DIM + 16, :] = ones_row
        kmp = kmp_ref[0:nbp, :]
        k_hi = kmp.astype(BF16)
        k_lo = (kmp - k_hi.astype(F32)).astype(BF16)
        zero = jnp.zeros((nbp, LANES), BF16)
        parts = []
        for h in range(2):
            mine = (lane_m < hd) if h == 0 else (lane_m >= hd)
            parts += [jnp.where(mine, k_hi, zero), jnp.where(mine, k_lo, zero)]
        parts += [parts[0], parts[2]]
        for n, part in enumerate(parts):
            kst_ref[n * nbp:(n + 1) * nbp, :] = part

    prepare_keys(0)

    causal = (lax.broadcasted_iota(jnp.int32, (blk, blk), 0) <= lax.broadcasted_iota(jnp.int32, (blk, blk), 1))
    rowf = lax.broadcasted_iota(jnp.int32, (nbp, qt), 0).astype(F32)
    subf = (lax.broadcasted_iota(jnp.int32, (nbp, qt), 1) // blk).astype(F32)
    q_scale = (hd ** -0.5) * 1.4426950408889634

    def query_operands(t):
        r0 = pl.multiple_of(t * qt, qt)
        qT = norm_rope_queries_t(q_ref[0, pl.ds(r0, qt), :], r0)
        cur = lax.convert_element_type(t * sub, F32) + subf
        q_hi = qT.astype(BF16)
        q_lo = (qT - q_hi.astype(F32)).astype(BF16)
        g1 = jnp.dot(kst_ref[0:4 * nbp, :], q_hi, preferred_element_type=F32)
        g2 = jnp.dot(kst_ref[4 * nbp:6 * nbp, :], q_lo, preferred_element_type=F32)
        qa = []
        for h in range(2):
            gate = g1[2 * h * nbp:(2 * h + 1) * nbp] + g1[(2 * h + 1) * nbp:(2 * h + 2) * nbp] + g2[h * nbp:(h + 1) * nbp]
            g = jnp.where(rowf < cur, gate, -jnp.inf)
            keep = rowf == cur
            for r in range(MOBA_TOPK):
                m = jnp.max(g, axis=0, keepdims=True)
                idx = jnp.min(jnp.where(g == m, rowf, 1e9), axis=0, keepdims=True)
                pick = (rowf == idx) & (cur > r)
                keep = keep | pick
                g = jnp.where(pick, -jnp.inf, g)
            bias = jnp.where(keep, 0.0, NEG)
            qs = qT[h * hd:(h + 1) * hd] * q_scale
            pad = jnp.zeros((LANES - hd - nbp, qt), F32)
            pieces = [qs, bias, pad] if h == 0 else [bias, pad, qs]
            qa.append(jnp.concatenate(pieces, axis=0).astype(BF16))
        return qa

    def pass1_tile(par, h, j, qa_h, own=None):
        sT = jnp.dot(kaug_ref[h, j], qa_h, preferred_element_type=F32)
        if own is not None:
            lo, hi = own * blk, (own + 1) * blk
            pieces = [sT[:, :lo]] * (lo > 0) + [jnp.where(causal, sT[:, lo:hi], NEG)] + [sT[:, hi:]] * (hi < qt)
            sT = jnp.concatenate(pieces, axis=1)
        s_ref[par, h, j] = sT
        return _fold_rows(sT, jnp.maximum)

    def pass2_tile(par, h, j):
        pT = jnp.exp2(s_ref[par, h, j] - mcol_ref[h, 0:1, :]).astype(BF16)
        return jnp.dot(vt_ref[h, j], pT, preferred_element_type=F32)

    def pass1_own(t, par, qa):
        mx = []
        for h in range(2):
            f = [pass1_tile(par, h, t * sub + u, qa[h], own=u) for u in range(sub)]
            mx.append(functools.reduce(jnp.maximum, f))
        return mx

    def pass2_own(t, par):
        return [sum(pass2_tile(par, h, t * sub + u) for u in range(sub)) for h in range(2)]

    def pass1_pair(p, par, qa, mx):
        return [functools.reduce(jnp.maximum, [mx[h]] + [pass1_tile(par, h, 2 * p + u, qa[h]) for u in range(2)])
                for h in range(2)]

    def pass2_pair(p, par, acc):
        return [acc[h] + sum(pass2_tile(par, h, 2 * p + u) for u in range(2)) for h in range(2)]

    def finish_pass1(mx):
        for h in range(2):
            mcol_ref[h] = jnp.broadcast_to(jnp.max(mx[h], axis=0, keepdims=True), mcol_ref.shape[1:])

    def finish_pass2(t, acc):
        outT = jnp.concatenate([acc[h][0:hd] / acc[h][hd:hd + 1] for h in range(2)], axis=0)
        o_ref[0, pl.ds(pl.multiple_of(t * qt, qt), qt), :] = outT.T.astype(BF16)

    def stage(t, par):

        def prepare_next():
            prepare_keys(t + 1)
            nxt = query_operands(jnp.minimum(t + 1, nq - 1))
            for h in range(2):
                qa_ref[1 - par, h] = nxt[h]

        @pl.when(t == 0)
        def _():
            finish_pass1(pass1_own(t, par, query_operands(t)))
            prepare_next()

        @pl.when((t > 0) & (t < nq))
        def _():
            qa = [qa_ref[par, h] for h in range(2)]
            mx = pass1_own(t, par, qa)
            acc = pass2_own(t - 1, 1 - par)
            prepare_next()
            n_prev = (t - 1) * sub // 2

            def both(p, c):
                mx, acc = c
                return tuple(pass1_pair(p, par, qa, mx)), tuple(pass2_pair(p, 1 - par, acc))

            def both_twice(p2, c):
                return both(2 * p2 + 1, both(2 * p2, c))

            c = lax.fori_loop(0, n_prev // 2, both_twice, (tuple(mx), tuple(acc)))
            mx, acc = lax.fori_loop(n_prev // 2 * 2, n_prev, both, c)
            for p in range(sub // 2):
                mx = pass1_pair(n_prev + p, par, qa, mx)
            finish_pass2(t - 1, acc)
            finish_pass1(mx)

        @pl.when(t == nq)
        def _():
            acc = tuple(pass2_own(t - 1, 1 - par))
            n_prev = (t - 1) * sub // 2
            acc = lax.fori_loop(0, n_prev // 2, lambda p2, a: tuple(pass2_pair(2 * p2 + 1, 1 - par, pass2_pair(2 * p2, 1 - par, a))), acc)
            acc = lax.fori_loop(n_prev // 2 * 2, n_prev, lambda p, a: tuple(pass2_pair(p, 1 - par, a)), acc)
            finish_pass2(t - 1, acc)

    def stage_pair(tt, carry):
        stage(2 * tt, 0)
        stage(2 * tt + 1, 1)
        return carry

    lax.fori_loop(0, (nq + 2) // 2, stage_pair, 0)


def _attention(z3, cosf, sinf, cost, sint, gq_cols, gk2):
    B, S, _ = z3.shape
    n_pair = N_HEADS // 2
    kq = ATTN_WIDTH // LANES
    nb = S // MOBA_BLOCK
    nbp = -(-nb // 16) * 16
    sub = QUERY_TILE // MOBA_BLOCK
    assert sub <= 8
    return pl.pallas_call(
        _attn_kernel,
        grid=(B, n_pair),
        in_specs=[pl.BlockSpec((1, S, LANES), lambda b, p: (b, 0, p)),
                  pl.BlockSpec((1, S, LANES), lambda b, p: (b, 0, kq + p)),
                  pl.BlockSpec((1, S, LANES), lambda b, p: (b, 0, 2 * kq + p)),
                  pl.BlockSpec((S, LANES), lambda b, p: (0, 0), pipeline_mode=pl.Buffered(1)),
                  pl.BlockSpec((S, LANES), lambda b, p: (0, 0), pipeline_mode=pl.Buffered(1)),
                  pl.BlockSpec(cost.shape, lambda b, p: (0, 0)),
                  pl.BlockSpec(sint.shape, lambda b, p: (0, 0)),
                  pl.BlockSpec((LANES, QUERY_TILE), lambda b, p: (0, 0)),
                  pl.BlockSpec((1, LANES), lambda b, p: (0, 0))],
        out_specs=pl.BlockSpec((1, S, LANES), lambda b, p: (b, 0, p)),
        out_shape=jax.ShapeDtypeStruct((B, S, ATTN_WIDTH), BF16),
        scratch_shapes=[pltpu.VMEM((2, nb + sub, MOBA_BLOCK, LANES), BF16),
                        pltpu.VMEM((2, nb + sub, HEAD_DIM + 16, MOBA_BLOCK), BF16),
                        pltpu.VMEM((nbp + 8, LANES), F32),
                        pltpu.VMEM((6 * nbp, LANES), BF16),
                        pltpu.VMEM((2, 2, nb, MOBA_BLOCK, QUERY_TILE), F32),
                        pltpu.VMEM((2, 8, QUERY_TILE), F32),
                        pltpu.VMEM((2, 2, LANES, QUERY_TILE), BF16)],
        compiler_params=pltpu.CompilerParams(dimension_semantics=("arbitrary", "arbitrary"),
                                             vmem_limit_bytes=ATTN_VMEM_LIMIT),
        name="attn",
    )(z3, z3, z3, cosf, sinf, cost, sint, gq_cols, gk2)


def _slabs(width):
    return width // LANES


def _load_slabs(ref, row0, rows, n_slab, lead=()):
    return jnp.concatenate([ref[lead + (pl.ds(row0 * n_slab + s, rows, stride=n_slab), slice(None))]
                            for s in range(n_slab)], axis=1)


def _store_slabs(ref, row0, val):
    rows, width = val.shape
    n_slab = _slabs(width)
    for s in range(n_slab):
        ref[pl.ds(row0 * n_slab + s, rows, stride=n_slab), :] = val[:, s * LANES:(s + 1) * LANES]


def _post_kernel(x_ref, ya_ref, xb_ref, bg_ref, cg_ref, gta_ref, gtb_ref, ga1_ref, sc2_ref, sh2_ref,
                 cw_ref, cb_ref, wpa_ref, wpb_ref, wo_ref, g2_ref, wr_ref, br_ref, tri_ref,
                 x1_ref, h2_ref, ri_ref, rf_ref, cnt_ref, ubuf_ref, run_ref, *, tiles_per_seq):
    i = pl.program_id(0)
    tm = x_ref.shape[0]
    rc = POST_CHUNK
    halo = 8

    @pl.when(i == 0)
    def _():
        run_ref[...] = jnp.zeros_like(run_ref)

    @pl.when(i % tiles_per_seq == 0)
    def _():
        ubuf_ref[0:halo, :] = jnp.zeros((halo, CONV_WIDTH), F32)

    ubuf_ref[halo:halo + tm, :] = cg_ref[...].astype(F32) * xb_ref[...].astype(F32)
    cw = cw_ref[...]
    lanef = lax.broadcasted_iota(jnp.int32, (rc, LANES), 1).astype(F32)
    half_d = x_ref.shape[1] // 2
    run = run_ref[0:1, :]

    for c in range(tm // rc):
        rows = pl.ds(c * rc, rc)
        conv = (cw[0:1, :] * ubuf_ref[pl.ds(halo - 2 + c * rc, rc), :]
                + cw[1:2, :] * ubuf_ref[pl.ds(halo - 1 + c * rc, rc), :]
                + cw[2:3, :] * ubuf_ref[pl.ds(halo + c * rc, rc), :])
        y_b = (bg_ref[rows, :].astype(F32) * (conv + cb_ref[...])).astype(BF16)
        pa = jnp.dot(ya_ref[rows, :], wpa_ref[...], preferred_element_type=F32)
        pb = jnp.dot(y_b, wpb_ref[...], preferred_element_type=F32)
        merged = (jax.nn.sigmoid(gta_ref[rows, :].astype(F32)) * pa
                  + jax.nn.sigmoid(gtb_ref[rows, :].astype(F32)) * pb).astype(BF16)
        x1 = x_ref[rows, :] + ga1_ref[0] * jnp.dot(merged, wo_ref[...], preferred_element_type=F32)
        x1_ref[rows, :] = x1

        ms = jnp.mean(x1 * x1, axis=-1, keepdims=True)
        h2 = x1 * lax.rsqrt(ms + EPS) * g2_ref[...]
        h2 = h2 * (1.0 + sc2_ref[0]) + sh2_ref[0]
        h_hi = h2.astype(BF16)
        h_hi32 = h_hi.astype(F32)
        bits = pltpu.bitcast(h_hi32, jnp.uint32)
        _store_slabs(h2_ref, c * rc, (bits[:, :half_d] >> 16) | bits[:, half_d:])

        h_lo = (h2 - h_hi32).astype(BF16)
        r = jnp.dot(h_hi, wr_ref[...], preferred_element_type=F32)
        logit = (r[:, :LANES] + r[:, LANES:]
                 + jnp.dot(h_lo, wr_ref[:, :LANES], preferred_element_type=F32) + br_ref[...])
        gl = jnp.where(lanef < N_GROUPS, logit, -jnp.inf)
        gmax = jnp.max(gl, axis=-1, keepdims=True)
        g_idx = jnp.min(jnp.where(gl == gmax, lanef, 1e9), axis=-1, keepdims=True)
        g_w = 1.0 / jnp.sum(jnp.exp(gl - gmax), axis=-1, keepdims=True)
        e_lo = N_GROUPS + EXPERTS_PER_GROUP * g_idx
        el = jnp.where((lanef >= e_lo) & (lanef < e_lo + EXPERTS_PER_GROUP), logit, -jnp.inf)
        v0 = jnp.max(el, axis=-1, keepdims=True)
        i0 = jnp.min(jnp.where(el == v0, lanef, 1e9), axis=-1, keepdims=True)
        el = jnp.where(lanef == i0, -jnp.inf, el)
        v1 = jnp.max(el, axis=-1, keepdims=True)
        i1 = jnp.min(jnp.where(el == v1, lanef, 1e9), axis=-1, keepdims=True)
        t = jnp.exp(v1 - v0)
        w0 = g_w / (1.0 + t)
        w1 = g_w * t / (1.0 + t)
        e0 = i0 - N_GROUPS
        e1 = i1 - N_GROUPS

        oh0 = lanef == e0
        oh1 = lanef == e1
        oh = jnp.where(oh0 | oh1, 1.0, 0.0)
        before = jnp.dot(tri_ref[...], oh.astype(BF16), preferred_element_type=F32) + run
        r0 = jnp.sum(jnp.where(oh0, before, 0.0), axis=-1, keepdims=True)
        r1 = jnp.sum(jnp.where(oh1, before, 0.0), axis=-1, keepdims=True)
        run = run + jnp.sum(oh, axis=0, keepdims=True)

        ri = jnp.where(lanef == 0, e0, jnp.where(lanef == 1, e1, jnp.where(lanef == 2, r0, jnp.where(lanef == 3, r1, 0.0))))
        ri_ref[:, rows] = ri.astype(jnp.int32).T[0:8]
        rf_ref[rows, :] = jnp.where(lanef == 0, w0, jnp.where(lanef == 1, w1, 0.0))

    ubuf_ref[0:halo, :] = ubuf_ref[tm:tm + halo, :]
    run_ref[...] = jnp.broadcast_to(run, run_ref.shape)
    cnt_ref[...] = jnp.broadcast_to(run, cnt_ref.shape)


def _post(x2, ya2, z2, ga1, sc2, sh2, conv_w, conv_b, wpa, wpb, wo, g2, wr, br, tri, tiles_per_seq):
    T, D = x2.shape
    tm = POST_TILE
    cw = CONV_WIDTH
    xcol = 3 * ATTN_WIDTH // cw
    gcol = (3 * ATTN_WIDTH + 3 * cw) // D
    bmap = lambda i: (i // tiles_per_seq, 0, 0)
    const = lambda i: (0, 0)
    return pl.pallas_call(
        functools.partial(_post_kernel, tiles_per_seq=tiles_per_seq),
        grid=(T // tm,),
        in_specs=[pl.BlockSpec((tm, D), lambda i: (i, 0)),
                  pl.BlockSpec((tm, ATTN_WIDTH), lambda i: (i, 0)),
                  pl.BlockSpec((tm, cw), lambda i: (i, xcol)),
                  pl.BlockSpec((tm, cw), lambda i: (i, xcol + 1)),
                  pl.BlockSpec((tm, cw), lambda i: (i, xcol + 2)),
                  pl.BlockSpec((tm, D), lambda i: (i, gcol)),
                  pl.BlockSpec((tm, D), lambda i: (i, gcol + 1)),
                  pl.BlockSpec((1, 1, D), bmap),
                  pl.BlockSpec((1, 1, D), bmap),
                  pl.BlockSpec((1, 1, D), bmap),
                  pl.BlockSpec((CONV_K, cw), const),
                  pl.BlockSpec((1, cw), const),
                  pl.BlockSpec((ATTN_WIDTH, D), const),
                  pl.BlockSpec((cw, D), const),
                  pl.BlockSpec((D, D), const),
                  pl.BlockSpec((1, D), const),
                  pl.BlockSpec((D, 2 * LANES), const),
                  pl.BlockSpec((1, LANES), const),
                  pl.BlockSpec((POST_CHUNK, POST_CHUNK), const)],
        out_specs=[pl.BlockSpec((tm, D), lambda i: (i, 0)),
                   pl.BlockSpec((tm * _slabs(D // 2), LANES), lambda i: (i, 0)),
                   pl.BlockSpec((8, tm), lambda i: (0, i)),
                   pl.BlockSpec((tm, LANES), lambda i: (i, 0)),
                   pl.BlockSpec((8, LANES), const)],
        out_shape=[jax.ShapeDtypeStruct((T, D), F32),
                   jax.ShapeDtypeStruct((T * _slabs(D // 2), LANES), jnp.uint32),
                   jax.ShapeDtypeStruct((8, T), jnp.int32),
                   jax.ShapeDtypeStruct((T, LANES), F32),
                   jax.ShapeDtypeStruct((8, LANES), F32)],
        scratch_shapes=[pltpu.VMEM((tm + 16, cw), F32),
                        pltpu.VMEM((8, LANES), F32)],
        compiler_params=_cparams("arbitrary"),
        name="post",
    )(x2, ya2, z2, z2, z2, z2, z2, ga1, sc2, sh2, conv_w, conv_b, wpa, wpb, wo, g2, wr, br, tri)


def _slots_kernel(ps_ref, ri_ref, d_ref, *, n_slab):
    e = ri_ref[0:TOPK_IN_GROUP, :]
    start = jnp.zeros(e.shape, jnp.int32)
    for k in range(N_EXPERTS):
        start = jnp.where(e == k, ps_ref[k], start)
    d_ref[...] = (start + ri_ref[TOPK_IN_GROUP:2 * TOPK_IN_GROUP, :]) * n_slab


def _slots(pad_start, riT, n_slab):
    T = riT.shape[1]
    grid_spec = pltpu.PrefetchScalarGridSpec(
        num_scalar_prefetch=1,
        grid=(1,),
        in_specs=[pl.BlockSpec(riT.shape, lambda i, ps: (0, 0))],
        out_specs=pl.BlockSpec((TOPK_IN_GROUP, T), lambda i, ps: (0, 0)),
    )
    return pl.pallas_call(
        functools.partial(_slots_kernel, n_slab=n_slab),
        grid_spec=grid_spec,
        out_shape=jax.ShapeDtypeStruct((TOPK_IN_GROUP, T), jnp.int32),
        compiler_params=_cparams("arbitrary"),
        name="slots",
    )(pad_start, riT)


def _dispatch_kernel(d0_ref, d1_ref, tail_ref, h_ref, xs_ref, zero_ref, sem, zsem, *, n_slab):
    tm = h_ref.shape[0] // n_slab
    base = pl.program_id(0) * tm
    blk_rows = zero_ref.shape[0]

    @pl.when(pl.program_id(0) == 0)
    def _():
        zero_ref[...] = jnp.zeros_like(zero_ref)

        def tail_copy(e):
            start = pl.multiple_of(tail_ref[e] * blk_rows, blk_rows)
            return pltpu.make_async_copy(zero_ref, xs_ref.at[pl.ds(start, blk_rows), :], zsem)

        for e in range(tail_ref.shape[0]):
            @pl.when(tail_ref[e] >= 0)
            def _():
                tail_copy(e).start()
        for e in range(tail_ref.shape[0]):
            @pl.when(tail_ref[e] >= 0)
            def _():
                tail_copy(e).wait()

    def body(g, carry):
        r8 = pl.multiple_of(g * DMA_UNROLL, DMA_UNROLL)
        for u in range(DMA_UNROLL):
            src = h_ref.at[pl.ds((r8 + u) * n_slab, n_slab), :]
            pltpu.make_async_copy(src, xs_ref.at[pl.ds(d0_ref[base + r8 + u], n_slab), :], sem).start(priority=0)
            pltpu.make_async_copy(src, xs_ref.at[pl.ds(d1_ref[base + r8 + u], n_slab), :], sem).start(priority=1)
        return carry

    lax.fori_loop(0, tm // DMA_UNROLL, body, 0)
    for _ in range(TOPK_IN_GROUP):
        pltpu.make_async_copy(h_ref, xs_ref.at[pl.ds(0, tm * n_slab), :], sem).wait()


def _dispatch(dest0, dest1, tail_blk, h2p, n_pad, n_slab):
    tm = DISPATCH_TILE
    T = h2p.shape[0] // n_slab
    grid_spec = pltpu.PrefetchScalarGridSpec(
        num_scalar_prefetch=3,
        grid=(T // tm,),
        in_specs=[pl.BlockSpec((tm * n_slab, LANES), lambda i, d0, d1, tb: (i, 0))],
        out_specs=pl.BlockSpec(memory_space=pl.ANY),
        scratch_shapes=[pltpu.VMEM((EXPERT_ROWS * n_slab, LANES), h2p.dtype),
                        pltpu.SemaphoreType.DMA(()),
                        pltpu.SemaphoreType.DMA(())],
    )
    return pl.pallas_call(
        functools.partial(_dispatch_kernel, n_slab=n_slab),
        grid_spec=grid_spec,
        out_shape=jax.ShapeDtypeStruct((n_pad * n_slab, LANES), h2p.dtype),
        compiler_params=_cparams("arbitrary"),
        name="dispatch",
    )(dest0, dest1, tail_blk, h2p)


def _expert_kernel(be_ref, nv_ref, nxt_ref, slot_ref, xs_ref, w1f_ref, w3f_ref, w2f_ref, ys_ref,
                   w1_ref, w3_ref, w2_ref, f1_ref, f3_ref, f2_ref, sem):
    i = pl.program_id(0)
    wf_hbm = (w1f_ref, w3f_ref, w2f_ref)
    wf_vmem = (f1_ref, f3_ref, f2_ref)
    w_bf16 = (w1_ref, w3_ref, w2_ref)

    def fetch(e, s):
        return [pltpu.make_async_copy(wf_hbm[m].at[e], wf_vmem[m].at[s], sem.at[s, m]) for m in range(3)]

    @pl.when(i == 0)
    def _():
        for cp in fetch(be_ref[0], slot_ref[0]):
            cp.start()

    @pl.when((i == 0) | (be_ref[i] != be_ref[jnp.maximum(i - 1, 0)]))
    def _():
        s = slot_ref[i]
        for m, cp in enumerate(fetch(be_ref[i], s)):
            cp.wait()
            w_bf16[m][...] = wf_vmem[m][s].astype(BF16)

        @pl.when(nxt_ref[i] >= 0)
        def _():
            for cp in fetch(nxt_ref[i], 1 - s):
                cp.start()

    @pl.when(i >= nv_ref[0])
    def _():
        ys_ref[...] = jnp.zeros_like(ys_ref)

    @pl.when(i < nv_ref[0])
    def _():
        half_d = w1_ref.shape[0] // 2
        xp = _load_slabs(xs_ref, 0, EXPERT_ROWS, _slabs(half_d))
        x_lo = pltpu.bitcast(xp << 16, F32).astype(BF16)
        x_hi = pltpu.bitcast(xp & jnp.uint32(0xFFFF0000), F32).astype(BF16)
        a = (jnp.dot(x_lo, w1_ref[:half_d], preferred_element_type=F32)
             + jnp.dot(x_hi, w1_ref[half_d:], preferred_element_type=F32))
        b = (jnp.dot(x_lo, w3_ref[:half_d], preferred_element_type=F32)
             + jnp.dot(x_hi, w3_ref[half_d:], preferred_element_type=F32))
        hid = (a * jax.nn.sigmoid(a) * b).astype(BF16)
        y = jnp.dot(hid, w2_ref[...], preferred_element_type=F32)
        bits = pltpu.bitcast(y.astype(BF16).astype(F32), jnp.uint32)
        _store_slabs(ys_ref, 0, (bits[:, :half_d] >> 16) | bits[:, half_d:])


def _experts(blk_expert, n_valid, xs, w1f, w3f, w2f):
    _, D, F = w1f.shape
    rb = EXPERT_ROWS
    blk_rows = rb * _slabs(D // 2)
    n_steps = xs.shape[0] // blk_rows
    first = jnp.concatenate([jnp.ones((1,), jnp.bool_), blk_expert[1:] != blk_expert[:-1]])
    slot = ((jnp.cumsum(first.astype(jnp.int32)) - 1) % 2).astype(jnp.int32)
    later = jnp.where(blk_expert[None, :] > blk_expert[:, None], blk_expert[None, :], N_EXPERTS)
    nxt = jnp.min(later, axis=1)
    nxt = jnp.where(nxt < N_EXPERTS, nxt, -1).astype(jnp.int32)
    row_blk = lambda i, be, nv, nx, sl: (jnp.minimum(i, nv[0] - 1), 0)
    out_blk = lambda i, be, nv, nx, sl: (i, 0)
    grid_spec = pltpu.PrefetchScalarGridSpec(
        num_scalar_prefetch=4,
        grid=(n_steps,),
        in_specs=[pl.BlockSpec((blk_rows, LANES), row_blk),
                  pl.BlockSpec(memory_space=pl.ANY),
                  pl.BlockSpec(memory_space=pl.ANY),
                  pl.BlockSpec(memory_space=pl.ANY)],
        out_specs=pl.BlockSpec((blk_rows, LANES), out_blk),
        scratch_shapes=[pltpu.VMEM((D, F), BF16), pltpu.VMEM((D, F), BF16), pltpu.VMEM((F, D), BF16),
                        pltpu.VMEM((2, D, F), F32), pltpu.VMEM((2, D, F), F32), pltpu.VMEM((2, F, D), F32),
                        pltpu.SemaphoreType.DMA((2, 3))],
    )
    return pl.pallas_call(
        _expert_kernel,
        grid_spec=grid_spec,
        out_shape=jax.ShapeDtypeStruct(xs.shape, jnp.uint32),
        compiler_params=_cparams("arbitrary"),
        name="experts",
    )(blk_expert, n_valid, nxt, slot, xs, w1f, w3f, w2f)


def _combine_kernel(d0_ref, d1_ref, x1_ref, rf_ref, ga2_ref, ys_ref, o_ref, buf_ref, sem, *, n_slab):
    i = pl.program_id(0)
    n = pl.num_programs(0)
    tm = x1_ref.shape[0]

    def gather(step, slot):
        base = step * tm

        def body(g, carry):
            r8 = pl.multiple_of(g * DMA_UNROLL, DMA_UNROLL)
            for u in range(DMA_UNROLL):
                dst = pl.ds((r8 + u) * n_slab, n_slab)
                pltpu.make_async_copy(ys_ref.at[pl.ds(d0_ref[base + r8 + u], n_slab), :],
                                      buf_ref.at[slot, 0, dst, :], sem.at[slot]).start(priority=0)
                pltpu.make_async_copy(ys_ref.at[pl.ds(d1_ref[base + r8 + u], n_slab), :],
                                      buf_ref.at[slot, 1, dst, :], sem.at[slot]).start(priority=1)
            return carry

        lax.fori_loop(0, tm // DMA_UNROLL, body, 0)

    @pl.when(i == 0)
    def _():
        gather(0, 0)

    @pl.when(i + 1 < n)
    def _():
        gather(i + 1, (i + 1) % 2)

    slot = i % 2
    for k in range(TOPK_IN_GROUP):
        pltpu.make_async_copy(ys_ref.at[pl.ds(0, tm * n_slab), :], buf_ref.at[slot, k], sem.at[slot]).wait()
    rf = rf_ref[...]
    w = [rf[:, k:k + 1] for k in range(TOPK_IN_GROUP)]
    yp = [_load_slabs(buf_ref, 0, tm, n_slab, lead=(slot, k)) for k in range(TOPK_IN_GROUP)]
    half_d = n_slab * LANES
    y_lo = sum(w[k] * pltpu.bitcast(yp[k] << 16, F32) for k in range(TOPK_IN_GROUP))
    y_hi = sum(w[k] * pltpu.bitcast(yp[k] & jnp.uint32(0xFFFF0000), F32) for k in range(TOPK_IN_GROUP))
    ga2 = ga2_ref[0]
    o_ref[:, :half_d] = x1_ref[:, :half_d] + ga2[:, :half_d] * y_lo
    o_ref[:, half_d:] = x1_ref[:, half_d:] + ga2[:, half_d:] * y_hi


def _combine(dest0, dest1, x1, rf, ga2, ys, tiles_per_seq):
    T, D = x1.shape
    n_slab = _slabs(D // 2)
    tm = COMBINE_TILE
    row = lambda i, d0, d1: (i, 0)
    grid_spec = pltpu.PrefetchScalarGridSpec(
        num_scalar_prefetch=2,
        grid=(T // tm,),
        in_specs=[pl.BlockSpec((tm, D), row),
                  pl.BlockSpec((tm, LANES), row),
                  pl.BlockSpec((1, 1, D), lambda i, d0, d1: (i // tiles_per_seq, 0, 0)),
                  pl.BlockSpec(memory_space=pl.ANY)],
        out_specs=pl.BlockSpec((tm, D), row),
        scratch_shapes=[pltpu.VMEM((2, TOPK_IN_GROUP, tm * n_slab, LANES), ys.dtype),
                        pltpu.SemaphoreType.DMA((2,))],
    )
    return pl.pallas_call(
        functools.partial(_combine_kernel, n_slab=n_slab),
        grid_spec=grid_spec,
        out_shape=jax.ShapeDtypeStruct((T, D), F32),
        compiler_params=_cparams("arbitrary"),
        name="combine",
    )(dest0, dest1, x1, rf, ga2, ys)


def _rope_tables(S):
    pos = jnp.arange(S, dtype=F32)
    inv_freq = ROPE_THETA ** (-jnp.arange(0, ROT_DIM, 2, dtype=F32) / ROT_DIM)
    ang = pos[:, None] * inv_freq[None, :]
    cos, sin = jnp.cos(ang), jnp.sin(ang)
    half = ROT_DIM // 2
    ones = jnp.ones((S, HEAD_DIM - ROT_DIM), F32)
    cos_h = jnp.concatenate([cos, cos, ones], axis=1)
    sin_h = jnp.concatenate([-sin, sin, 0.0 * ones], axis=1)
    return jnp.tile(cos_h, (1, LANES // HEAD_DIM)), jnp.tile(sin_h, (1, LANES // HEAD_DIM)), cos.T, sin.T


def kernel(x, c, w_ada, b_ada, g_norm1, g_norm2, w_in, g_q, g_k, conv_w, conv_b,
           w_pa, w_pb, w_o, w_rg, b_rg, w_re, b_re, w1, w3, w2):
    B, S, D = x.shape
    T = B * S
    assert S % POST_TILE == 0 and S % COMBINE_TILE == 0 and T % DISPATCH_TILE == 0 and S % QUERY_TILE == 0 and QUERY_TILE % (2 * MOBA_BLOCK) == 0
    assert S // MOBA_BLOCK <= LANES - HEAD_DIM
    l = 0

    mod = _ada(c, w_ada[l], b_ada[l])
    sh1, sc1, ga1, sh2, sc2, ga2 = [m.reshape(B, 1, D) for m in jnp.split(mod, N_MOD, axis=-1)]

    x2 = x.reshape(T, D)
    z2 = _inproj(x2, g_norm1[l].reshape(1, D), sc1, sh1, w_in[l].astype(BF16), S)

    cosf, sinf, cost, sint = _rope_tables(S)
    rep = LANES // HEAD_DIM
    gq_cols = jnp.broadcast_to(jnp.tile(g_q[l], rep).reshape(LANES, 1), (LANES, QUERY_TILE))
    ya = _attention(z2.reshape(B, S, -1), cosf, sinf, cost, sint, gq_cols, jnp.tile(g_k[l], rep).reshape(1, LANES))

    wr = jnp.zeros((D, LANES), F32).at[:, :N_GROUPS].set(w_rg[l]).at[:, N_GROUPS:N_GROUPS + N_EXPERTS].set(w_re[l])
    br = jnp.zeros((1, LANES), F32).at[0, :N_GROUPS].set(b_rg[l]).at[0, N_GROUPS:N_GROUPS + N_EXPERTS].set(b_re[l])
    wr_hi = wr.astype(BF16)
    wr2 = jnp.concatenate([wr_hi, (wr - wr_hi.astype(F32)).astype(BF16)], axis=1)
    tri = (lax.broadcasted_iota(jnp.int32, (POST_CHUNK, POST_CHUNK), 1)
           < lax.broadcasted_iota(jnp.int32, (POST_CHUNK, POST_CHUNK), 0)).astype(BF16)
    x1, h2, ri, rf, cnt = _post(x2, ya.reshape(T, ATTN_WIDTH), z2, ga1, sc2, sh2,
                                conv_w[l], conv_b[l].reshape(1, CONV_WIDTH),
                                w_pa[l].astype(BF16), w_pb[l].astype(BF16), w_o[l].astype(BF16),
                                g_norm2[l].reshape(1, D), wr2, br, tri, S // POST_TILE)

    rb = EXPERT_ROWS
    counts = cnt[0, :N_EXPERTS].astype(jnp.int32)
    padded = (counts + rb - 1) // rb * rb
    pad_end = jnp.cumsum(padded)
    pad_start = pad_end - padded
    dest = _slots(pad_start.astype(jnp.int32), ri, _slabs(D // 2))
    dest0, dest1 = dest[0], dest[1]
    n_blocks = -(-T * TOPK_IN_GROUP // rb) + N_EXPERTS
    n_pad = n_blocks * rb
    n_valid = (pad_end[-1] // rb).astype(jnp.int32)
    blk_start = jnp.minimum(jnp.arange(n_blocks, dtype=jnp.int32), n_valid - 1) * rb
    blk_expert = jnp.sum(pad_end[None, :] <= blk_start[:, None], axis=-1).astype(jnp.int32)
    unused = n_valid + jnp.arange(N_EXPERTS, dtype=jnp.int32)
    tail_blk = jnp.concatenate([jnp.where(padded > 0, pad_end // rb - 1, -1),
                                jnp.where(unused < n_blocks, unused, -1)]).astype(jnp.int32)

    xs = _dispatch(dest0, dest1, tail_blk, h2, n_pad, _slabs(D // 2))
    ys = _experts(blk_expert, n_valid.reshape(1), xs, w1[l], w3[l], w2[l])
    out = _combine(dest0, dest1, x1, rf, ga2, ys, S // COMBINE_TILE)
    return out.reshape(B, S, D)
```

```python
import functools

import jax
import jax.numpy as jnp
from jax import lax
from jax.experimental import pallas as pl
from jax.experimental.pallas import tpu as pltpu

F32 = jnp.float32
BF16 = jnp.bfloat16
HIGHEST = lax.Precision.HIGHEST

N_HEADS = 8
HEAD_DIM = 64
ATTN_WIDTH = N_HEADS * HEAD_DIM
CONV_WIDTH = 512
CONV_K = 3
MOBA_BLOCK = 256
MOBA_TOPK = 3
ROPE_THETA = 500000.0
ROT_DIM = HEAD_DIM // 4
N_GROUPS = 4
EXPERTS_PER_GROUP = 8
N_EXPERTS = N_GROUPS * EXPERTS_PER_GROUP
TOPK_IN_GROUP = 2
N_MOD = 6
EPS = 1e-6

LANES = 128
NEG = -1e30
POST_TILE = 1024
COMBINE_TILE = 512
INPROJ_TILE = 1024
DISPATCH_TILE = 2048
POST_CHUNK = 512
QUERY_TILE = 512
DMA_UNROLL = 8
EXPERT_ROWS = 512
VMEM_LIMIT = 56 * 1024 * 1024
ATTN_VMEM_LIMIT = 60 * 1024 * 1024


def _cparams(*sem):
    return pltpu.CompilerParams(dimension_semantics=sem, vmem_limit_bytes=VMEM_LIMIT)


def _ada_kernel(c_ref, w_ref, b_ref, o_ref):
    c = c_ref[...]
    a = c * jax.nn.sigmoid(c)
    o_ref[...] = jnp.dot(a, w_ref[...], preferred_element_type=F32, precision=HIGHEST) + b_ref[...]


def _ada(c, w_ada, b_ada):
    B, D = c.shape
    N = w_ada.shape[1]
    tn = 1536
    return pl.pallas_call(
        _ada_kernel,
        grid=(N // tn,),
        in_specs=[pl.BlockSpec((B, D), lambda j: (0, 0)),
                  pl.BlockSpec((D, tn), lambda j: (0, j)),
                  pl.BlockSpec((1, tn), lambda j: (0, j))],
        out_specs=pl.BlockSpec((B, tn), lambda j: (0, j)),
        out_shape=jax.ShapeDtypeStruct((B, N), F32),
        compiler_params=_cparams("arbitrary"),
        name="ada",
    )(c, w_ada, b_ada.reshape(1, N))


def _inproj_kernel(x_ref, g_ref, sc_ref, sh_ref, w_ref, z_ref, *, n_chunk):
    x = x_ref[...]
    ms = jnp.mean(x * x, axis=-1, keepdims=True)
    y = x * lax.rsqrt(ms + EPS) * g_ref[...]
    h = (y * (1.0 + sc_ref[0]) + sh_ref[0]).astype(BF16)
    for n in range(0, z_ref.shape[1], n_chunk):
        z_ref[:, n:n + n_chunk] = jnp.dot(h, w_ref[:, n:n + n_chunk],
                                          preferred_element_type=F32).astype(BF16)


def _inproj(x2, g1, sc1, sh1, w_in_bf, seq_len):
    T, D = x2.shape
    N = w_in_bf.shape[1]
    tm = INPROJ_TILE
    assert seq_len % tm == 0
    bmap = lambda i: (i // (seq_len // tm), 0, 0)
    return pl.pallas_call(
        functools.partial(_inproj_kernel, n_chunk=512),
        grid=(T // tm,),
        in_specs=[pl.BlockSpec((tm, D), lambda i: (i, 0)),
                  pl.BlockSpec((1, D), lambda i: (0, 0)),
                  pl.BlockSpec((1, 1, D), bmap),
                  pl.BlockSpec((1, 1, D), bmap),
                  pl.BlockSpec((D, N), lambda i: (0, 0), pipeline_mode=pl.Buffered(1))],
        out_specs=pl.BlockSpec((tm, N), lambda i: (i, 0)),
        out_shape=jax.ShapeDtypeStruct((T, N), BF16),
        compiler_params=_cparams("arbitrary"),
        name="inproj",
    )(x2, g1, sc1, sh1, w_in_bf)


def _fold_rows(x, op):
    parts = [x[r:r + 8] for r in range(0, x.shape[0], 8)]
    while len(parts) > 1:
        parts = [op(parts[i], parts[i + 1]) for i in range(0, len(parts) - 1, 2)] + (
            [parts[-1]] if len(parts) % 2 else [])
    return parts[0]


def _attn_kernel(q_ref, k_ref, v_ref, cos_ref, sin_ref, cost_ref, sint_ref, gq_ref, gk_ref, o_ref,
                 kaug_ref, vt_ref, kmp_ref, kst_ref, s_ref, mcol_ref, qa_ref):
    S = q_ref.shape[1]
    blk = MOBA_BLOCK
    qt = QUERY_TILE
    sub = qt // blk
    nb = S // blk
    nq = S // qt
    nbp = kst_ref.shape[0] // 6
    hd = HEAD_DIM

    half = ROT_DIM // 2
    lane_r = lax.broadcasted_iota(jnp.int32, (blk, LANES), 1)
    rot_lo = (lane_r & (hd - 1)) < half
    same_head = jnp.where((lax.broadcasted_iota(jnp.int32, (LANES, LANES), 0) < hd)
                          == (lax.broadcasted_iota(jnp.int32, (LANES, LANES), 1) < hd), 1.0, 0.0).astype(BF16)

    def norm_rope_keys(xb, r0):
        x = xb.astype(F32)
        sq = x * x
        sq_hi = sq.astype(BF16)
        sq_lo = (sq - sq_hi.astype(F32)).astype(BF16)
        ssq = (jnp.dot(sq_hi, same_head, preferred_element_type=F32)
               + jnp.dot(sq_lo, same_head, preferred_element_type=F32))
        y = x * lax.rsqrt(ssq * (1.0 / hd) + EPS) * gk_ref[...]
        rot = jnp.where(rot_lo, pltpu.roll(y, LANES - half, 1), pltpu.roll(y, half, 1))
        return y * cos_ref[pl.ds(r0, blk), :] + rot * sin_ref[pl.ds(r0, blk), :]

    def norm_rope_queries_t(xb, r0):
        xT = xb.astype(F32).T
        sq = xT * xT
        cos = cost_ref[:, pl.ds(r0, qt)]
        sin = sint_ref[:, pl.ds(r0, qt)]
        rows = []
        for h in range(2):
            lo, hi = h * hd, (h + 1) * hd
            ssq = jnp.sum(_fold_rows(sq[lo:hi], jnp.add), axis=0, keepdims=True)
            y = xT[lo:hi] * lax.rsqrt(ssq * (1.0 / hd) + EPS) * gq_ref[lo:hi, :]
            y1, y2 = y[0:half], y[half:2 * half]
            rows += [y1 * cos - y2 * sin, y2 * cos + y1 * sin, y[2 * half:]]
        return jnp.concatenate(rows, axis=0)

    kmp_ref[...] = jnp.zeros_like(kmp_ref)
    ones_row = jnp.where(lax.broadcasted_iota(jnp.int32, (16, blk), 0) == 0, 1.0, 0.0).astype(BF16)
    lane_k = lax.broadcasted_iota(jnp.int32, (blk, LANES), 1)
    head0_k = lane_k < HEAD_DIM
    lane_m = lax.broadcasted_iota(jnp.int32, (nbp, LANES), 1)

    def prepare_keys(t):
        for u in range(sub):
            j = t * sub + u
            r0 = pl.multiple_of(jnp.minimum(j, nb - 1) * blk, blk)
            kr = norm_rope_keys(k_ref[0, pl.ds(r0, blk), :], r0)
            kmp_ref[pl.ds(j, 1), :] = jnp.sum(kr, axis=0, keepdims=True) * (1.0 / blk)
            kaug_ref[0, j] = jnp.where(head0_k, kr, jnp.where(lane_k - hd == j, 1.0, 0.0)).astype(BF16)
            kaug_ref[1, j] = jnp.where(head0_k, jnp.where(lane_k == j, 1.0, 0.0), kr).astype(BF16)
            vT = v_ref[0, pl.ds(r0, blk), :].astype(F32).T
            for h in range(2):
                vt_ref[h, j, 0:HEAD_DIM, :] = vT[h * HEAD_DIM:(h + 1) * HEAD_DIM].astype(BF16)
                vt_ref[h, j, HEAD_DIM:HEAD_DIM + 16, :] = ones_row
        kmp = kmp_ref[0:nbp, :]
        k_hi = kmp.astype(BF16)
        k_lo = (kmp - k_hi.astype(F32)).astype(BF16)
        zero = jnp.zeros((nbp, LANES), BF16)
        parts = []
        for h in range(2):
            mine = (lane_m < hd) if h == 0 else (lane_m >= hd)
            parts += [jnp.where(mine, k_hi, zero), jnp.where(mine, k_lo, zero)]
        parts += [parts[0], parts[2]]
        for n, part in enumerate(parts):
            kst_ref[n * nbp:(n + 1) * nbp, :] = part

    prepare_keys(0)

    causal = (lax.broadcasted_iota(jnp.int32, (blk, blk), 0) <= lax.broadcasted_iota(jnp.int32, (blk, blk), 1))
    rowf = lax.broadcasted_iota(jnp.int32, (nbp, qt), 0).astype(F32)
    subf = (lax.broadcasted_iota(jnp.int32, (nbp, qt), 1) // blk).astype(F32)
    q_scale = (hd ** -0.5) * 1.4426950408889634

    def query_operands(t):
        r0 = pl.multiple_of(t * qt, qt)
        qT = norm_rope_queries_t(q_ref[0, pl.ds(r0, qt), :], r0)
        cur = lax.convert_element_type(t * sub, F32) + subf
        q_hi = qT.astype(BF16)
        q_lo = (qT - q_hi.astype(F32)).astype(BF16)
        g1 = jnp.dot(kst_ref[0:4 * nbp, :], q_hi, preferred_element_type=F32)
        g2 = jnp.dot(kst_ref[4 * nbp:6 * nbp, :], q_lo, preferred_element_type=F32)
        qa = []
        for h in range(2):
            gate = g1[2 * h * nbp:(2 * h + 1) * nbp] + g1[(2 * h + 1) * nbp:(2 * h + 2) * nbp] + g2[h * nbp:(h + 1) * nbp]
            g = jnp.where(rowf < cur, gate, -jnp.inf)
            keep = rowf == cur
            for r in range(MOBA_TOPK):
                m = jnp.max(g, axis=0, keepdims=True)
                idx = jnp.min(jnp.where(g == m, rowf, 1e9), axis=0, keepdims=True)
                pick = (rowf == idx) & (cur > r)
                keep = keep | pick
                g = jnp.where(pick, -jnp.inf, g)
            bias = jnp.where(keep, 0.0, NEG)
            qs = qT[h * hd:(h + 1) * hd] * q_scale
            pad = jnp.zeros((LANES - hd - nbp, qt), F32)
            pieces = [qs, bias, pad] if h == 0 else [bias, pad, qs]
            qa.append(jnp.concatenate(pieces, axis=0).astype(BF16))
        return qa

    def pass1_tile(par, h, j, qa_h, own=None):
        sT = jnp.dot(kaug_ref[h, j], qa_h, preferred_element_type=F32)
        if own is not None:
            lo, hi = own * blk, (own + 1) * blk
            pieces = [sT[:, :lo]] * (lo > 0) + [jnp.where(causal, sT[:, lo:hi], NEG)] + [sT[:, hi:]] * (hi < qt)
            sT = jnp.concatenate(pieces, axis=1)
        s_ref[par, h, j] = sT
        return _fold_rows(sT, jnp.maximum)

    def pass2_tile(par, h, j):
        pT = jnp.exp2(s_ref[par, h, j] - mcol_ref[h, 0:1, :]).astype(BF16)
        return jnp.dot(vt_ref[h, j], pT, preferred_element_type=F32)

    def pass1_own(t, par, qa):
        mx = []
        for h in range(2):
            f = [pass1_tile(par, h, t * sub + u, qa[h], own=u) for u in range(sub)]
            mx.append(functools.reduce(jnp.maximum, f))
        return mx

    def pass2_own(t, par):
        return [sum(pass2_tile(par, h, t * sub + u) for u in range(sub)) for h in range(2)]

    def pass1_pair(p, par, qa, mx):
        return [functools.reduce(jnp.maximum, [mx[h]] + [pass1_tile(par, h, 2 * p + u, qa[h]) for u in range(2)])
                for h in range(2)]

    def pass2_pair(p, par, acc):
        return [acc[h] + sum(pass2_tile(par, h, 2 * p + u) for u in range(2)) for h in range(2)]

    def finish_pass1(mx):
        for h in range(2):
            mcol_ref[h] = jnp.broadcast_to(jnp.max(mx[h], axis=0, keepdims=True), mcol_ref.shape[1:])

    def finish_pass2(t, acc):
        outT = jnp.concatenate([acc[h][0:hd] / acc[h][hd:hd + 1] for h in range(2)], axis=0)
        o_ref[0, pl.ds(pl.multiple_of(t * qt, qt), qt), :] = outT.T.astype(BF16)

    def stage(t, par):

        def prepare_next():
            prepare_keys(t + 1)
            nxt = query_operands(jnp.minimum(t + 1, nq - 1))
            for h in range(2):
                qa_ref[1 - par, h] = nxt[h]

        @pl.when(t == 0)
        def _():
            finish_pass1(pass1_own(t, par, query_operands(t)))
            prepare_next()

        @pl.when((t > 0) & (t < nq))
        def _():
            qa = [qa_ref[par, h] for h in range(2)]
            mx = pass1_own(t, par, qa)
            acc = pass2_own(t - 1, 1 - par)
            prepare_next()
            n_prev = (t - 1) * sub // 2

            def both(p, c):
                mx, acc = c
                return tuple(pass1_pair(p, par, qa, mx)), tuple(pass2_pair(p, 1 - par, acc))

            def both_twice(p2, c):
                return both(2 * p2 + 1, both(2 * p2, c))

            c = lax.fori_loop(0, n_prev // 2, both_twice, (tuple(mx), tuple(acc)))
            mx, acc = lax.fori_loop(n_prev // 2 * 2, n_prev, both, c)
            for p in range(sub // 2):
                mx = pass1_pair(n_prev + p, par, qa, mx)
            finish_pass2(t - 1, acc)
            finish_pass1(mx)

        @pl.when(t == nq)
        def _():
            acc = tuple(pass2_own(t - 1, 1 - par))
            n_prev = (t - 1) * sub // 2
            acc = lax.fori_loop(0, n_prev // 2, lambda p2, a: tuple(pass2_pair(2 * p2 + 1, 1 - par, pass2_pair(2 * p2, 1 - par, a))), acc)
            acc = lax.fori_loop(n_prev // 2 * 2, n_prev, lambda p, a: tuple(pass2_pair(p, 1 - par, a)), acc)
            finish_pass2(t - 1, acc)

    def stage_pair(tt, carry):
        stage(2 * tt, 0)
        stage(2 * tt + 1, 1)
        return carry

    lax.fori_loop(0, (nq + 2) // 2, stage_pair, 0)


def _attention(z3, cosf, sinf, cost, sint, gq_cols, gk2):
    B, S, _ = z3.shape
    n_pair = N_HEADS // 2
    kq = ATTN_WIDTH // LANES
    nb = S // MOBA_BLOCK
    nbp = -(-nb // 16) * 16
    sub = QUERY_TILE // MOBA_BLOCK
    assert sub <= 8
    return pl.pallas_call(
        _attn_kernel,
        grid=(B, n_pair),
        in_specs=[pl.BlockSpec((1, S, LANES), lambda b, p: (b, 0, p)),
                  pl.BlockSpec((1, S, LANES), lambda b, p: (b, 0, kq + p)),
                  pl.BlockSpec((1, S, LANES), lambda b, p: (b, 0, 2 * kq + p)),
                  pl.BlockSpec((S, LANES), lambda b, p: (0, 0), pipeline_mode=pl.Buffered(1)),
                  pl.BlockSpec((S, LANES), lambda b, p: (0, 0), pipeline_mode=pl.Buffered(1)),
                  pl.BlockSpec(cost.shape, lambda b, p: (0, 0)),
                  pl.BlockSpec(sint.shape, lambda b, p: (0, 0)),
                  pl.BlockSpec((LANES, QUERY_TILE), lambda b, p: (0, 0)),
                  pl.BlockSpec((1, LANES), lambda b, p: (0, 0))],
        out_specs=pl.BlockSpec((1, S, LANES), lambda b, p: (b, 0, p)),
        out_shape=jax.ShapeDtypeStruct((B, S, ATTN_WIDTH), BF16),
        scratch_shapes=[pltpu.VMEM((2, nb + sub, MOBA_BLOCK, LANES), BF16),
                        pltpu.VMEM((2, nb + sub, HEAD_DIM + 16, MOBA_BLOCK), BF16),
                        pltpu.VMEM((nbp + 8, LANES), F32),
                        pltpu.VMEM((6 * nbp, LANES), BF16),
                        pltpu.VMEM((2, 2, nb, MOBA_BLOCK, QUERY_TILE), F32),
                        pltpu.VMEM((2, 8, QUERY_TILE), F32),
                        pltpu.VMEM((2, 2, LANES, QUERY_TILE), BF16)],
        compiler_params=pltpu.CompilerParams(dimension_semantics=("arbitrary", "arbitrary"),
                                             vmem_limit_bytes=ATTN_VMEM_LIMIT),
        name="attn",
    )(z3, z3, z3, cosf, sinf, cost, sint, gq_cols, gk2)


def _slabs(width):
    return width // LANES


def _load_slabs(ref, row0, rows, n_slab, lead=()):
    return jnp.concatenate([ref[lead + (pl.ds(row0 * n_slab + s, rows, stride=n_slab), slice(None))]
                            for s in range(n_slab)], axis=1)


def _store_slabs(ref, row0, val):
    rows, width = val.shape
    n_slab = _slabs(width)
    for s in range(n_slab):
        ref[pl.ds(row0 * n_slab + s, rows, stride=n_slab), :] = val[:, s * LANES:(s + 1) * LANES]


def _post_kernel(x_ref, ya_ref, xb_ref, bg_ref, cg_ref, gta_ref, gtb_ref, ga1_ref, sc2_ref, sh2_ref,
                 cw_ref, cb_ref, wpa_ref, wpb_ref, wo_ref, g2_ref, wr_ref, br_ref, tri_ref,
                 x1_ref, h2_ref, ri_ref, rf_ref, cnt_ref, ubuf_ref, run_ref, *, tiles_per_seq):
    i = pl.program_id(0)
    tm = x_ref.shape[0]
    rc = POST_CHUNK
    halo = 8

    @pl.when(i == 0)
    def _():
        run_ref[...] = jnp.zeros_like(run_ref)

    @pl.when(i % tiles_per_seq == 0)
    def _():
        ubuf_ref[0:halo, :] = jnp.zeros((halo, CONV_WIDTH), F32)

    ubuf_ref[halo:halo + tm, :] = cg_ref[...].astype(F32) * xb_ref[...].astype(F32)
    cw = cw_ref[...]
    lanef = lax.broadcasted_iota(jnp.int32, (rc, LANES), 1).astype(F32)
    half_d = x_ref.shape[1] // 2
    run = run_ref[0:1, :]

    for c in range(tm // rc):
        rows = pl.ds(c * rc, rc)
        conv = (cw[0:1, :] * ubuf_ref[pl.ds(halo - 2 + c * rc, rc), :]
                + cw[1:2, :] * ubuf_ref[pl.ds(halo - 1 + c * rc, rc), :]
                + cw[2:3, :] * ubuf_ref[pl.ds(halo + c * rc, rc), :])
        y_b = (bg_ref[rows, :].astype(F32) * (conv + cb_ref[...])).astype(BF16)
        pa = jnp.dot(ya_ref[rows, :], wpa_ref[...], preferred_element_type=F32)
        pb = jnp.dot(y_b, wpb_ref[...], preferred_element_type=F32)
        merged = (jax.nn.sigmoid(gta_ref[rows, :].astype(F32)) * pa
                  + jax.nn.sigmoid(gtb_ref[rows, :].astype(F32)) * pb).astype(BF16)
        x1 = x_ref[rows, :] + ga1_ref[0] * jnp.dot(merged, wo_ref[...], preferred_element_type=F32)
        x1_ref[rows, :] = x1

        ms = jnp.mean(x1 * x1, axis=-1, keepdims=True)
        h2 = x1 * lax.rsqrt(ms + EPS) * g2_ref[...]
        h2 = h2 * (1.0 + sc2_ref[0]) + sh2_ref[0]
        h_hi = h2.astype(BF16)
        h_hi32 = h_hi.astype(F32)
        bits = pltpu.bitcast(h_hi32, jnp.uint32)
        _store_slabs(h2_ref, c * rc, (bits[:, :half_d] >> 16) | bits[:, half_d:])

        h_lo = (h2 - h_hi32).astype(BF16)
        r = jnp.dot(h_hi, wr_ref[...], preferred_element_type=F32)
        logit = (r[:, :LANES] + r[:, LANES:]
                 + jnp.dot(h_lo, wr_ref[:, :LANES], preferred_element_type=F32) + br_ref[...])
        gl = jnp.where(lanef < N_GROUPS, logit, -jnp.inf)
        gmax = jnp.max(gl, axis=-1, keepdims=True)
        g_idx = jnp.min(jnp.where(gl == gmax, lanef, 1e9), axis=-1, keepdims=True)
        g_w = 1.0 / jnp.sum(jnp.exp(gl - gmax), axis=-1, keepdims=True)
        e_lo = N_GROUPS + EXPERTS_PER_GROUP * g_idx
        el = jnp.where((lanef >= e_lo) & (lanef < e_lo + EXPERTS_PER_GROUP), logit, -jnp.inf)
        v0 = jnp.max(el, axis=-1, keepdims=True)
        i0 = jnp.min(jnp.where(el == v0, lanef, 1e9), axis=-1, keepdims=True)
        el = jnp.where(lanef == i0, -jnp.inf, el)
        v1 = jnp.max(el, axis=-1, keepdims=True)
        i1 = jnp.min(jnp.where(el == v1, lanef, 1e9), axis=-1, keepdims=True)
        t = jnp.exp(v1 - v0)
        w0 = g_w / (1.0 + t)
        w1 = g_w * t / (1.0 + t)
        e0 = i0 - N_GROUPS
        e1 = i1 - N_GROUPS

        oh0 = lanef == e0
        oh1 = lanef == e1
        oh = jnp.where(oh0 | oh1, 1.0, 0.0)
        before = jnp.dot(tri_ref[...], oh.astype(BF16), preferred_element_type=F32) + run
        r0 = jnp.sum(jnp.where(oh0, before, 0.0), axis=-1, keepdims=True)
        r1 = jnp.sum(jnp.where(oh1, before, 0.0), axis=-1, keepdims=True)
        run = run + jnp.sum(oh, axis=0, keepdims=True)

        ri = jnp.where(lanef == 0, e0, jnp.where(lanef == 1, e1, jnp.where(lanef == 2, r0, jnp.where(lanef == 3, r1, 0.0))))
        ri_ref[:, rows] = ri.astype(jnp.int32).T[0:8]
        rf_ref[rows, :] = jnp.where(lanef == 0, w0, jnp.where(lanef == 1, w1, 0.0))

    ubuf_ref[0:halo, :] = ubuf_ref[tm:tm + halo, :]
    run_ref[...] = jnp.broadcast_to(run, run_ref.shape)
    cnt_ref[...] = jnp.broadcast_to(run, cnt_ref.shape)


def _post(x2, ya2, z2, ga1, sc2, sh2, conv_w, conv_b, wpa, wpb, wo, g2, wr, br, tri, tiles_per_seq):
    T, D = x2.shape
    tm = POST_TILE
    cw = CONV_WIDTH
    xcol = 3 * ATTN_WIDTH // cw
    gcol = (3 * ATTN_WIDTH + 3 * cw) // D
    bmap = lambda i: (i // tiles_per_seq, 0, 0)
    const = lambda i: (0, 0)
    return pl.pallas_call(
        functools.partial(_post_kernel, tiles_per_seq=tiles_per_seq),
        grid=(T // tm,),
        in_specs=[pl.BlockSpec((tm, D), lambda i: (i, 0)),
                  pl.BlockSpec((tm, ATTN_WIDTH), lambda i: (i, 0)),
                  pl.BlockSpec((tm, cw), lambda i: (i, xcol)),
                  pl.BlockSpec((tm, cw), lambda i: (i, xcol + 1)),
                  pl.BlockSpec((tm, cw), lambda i: (i, xcol + 2)),
                  pl.BlockSpec((tm, D), lambda i: (i, gcol)),
                  pl.BlockSpec((tm, D), lambda i: (i, gcol + 1)),
                  pl.BlockSpec((1, 1, D), bmap),
                  pl.BlockSpec((1, 1, D), bmap),
                  pl.BlockSpec((1, 1, D), bmap),
                  pl.BlockSpec((CONV_K, cw), const),
                  pl.BlockSpec((1, cw), const),
                  pl.BlockSpec((ATTN_WIDTH, D), const),
                  pl.BlockSpec((cw, D), const),
                  pl.BlockSpec((D, D), const),
                  pl.BlockSpec((1, D), const),
                  pl.BlockSpec((D, 2 * LANES), const),
                  pl.BlockSpec((1, LANES), const),
                  pl.BlockSpec((POST_CHUNK, POST_CHUNK), const)],
        out_specs=[pl.BlockSpec((tm, D), lambda i: (i, 0)),
                   pl.BlockSpec((tm * _slabs(D // 2), LANES), lambda i: (i, 0)),
                   pl.BlockSpec((8, tm), lambda i: (0, i)),
                   pl.BlockSpec((tm, LANES), lambda i: (i, 0)),
                   pl.BlockSpec((8, LANES), const)],
        out_shape=[jax.ShapeDtypeStruct((T, D), F32),
                   jax.ShapeDtypeStruct((T * _slabs(D // 2), LANES), jnp.uint32),
                   jax.ShapeDtypeStruct((8, T), jnp.int32),
                   jax.ShapeDtypeStruct((T, LANES), F32),
                   jax.ShapeDtypeStruct((8, LANES), F32)],
        scratch_shapes=[pltpu.VMEM((tm + 16, cw), F32),
                        pltpu.VMEM((8, LANES), F32)],
        compiler_params=_cparams("arbitrary"),
        name="post",
    )(x2, ya2, z2, z2, z2, z2, z2, ga1, sc2, sh2, conv_w, conv_b, wpa, wpb, wo, g2, wr, br, tri)


def _slots_kernel(ps_ref, ri_ref, d_ref, *, n_slab):
    e = ri_ref[0:TOPK_IN_GROUP, :]
    start = jnp.zeros(e.shape, jnp.int32)
    for k in range(N_EXPERTS):
        start = jnp.where(e == k, ps_ref[k], start)
    d_ref[...] = (start + ri_ref[TOPK_IN_GROUP:2 * TOPK_IN_GROUP, :]) * n_slab


def _slots(pad_start, riT, n_slab):
    T = riT.shape[1]
    grid_spec = pltpu.PrefetchScalarGridSpec(
        num_scalar_prefetch=1,
        grid=(1,),
        in_specs=[pl.BlockSpec(riT.shape, lambda i, ps: (0, 0))],
        out_specs=pl.BlockSpec((TOPK_IN_GROUP, T), lambda i, ps: (0, 0)),
    )
    return pl.pallas_call(
        functools.partial(_slots_kernel, n_slab=n_slab),
        grid_spec=grid_spec,
        out_shape=jax.ShapeDtypeStruct((TOPK_IN_GROUP, T), jnp.int32),
        compiler_params=_cparams("arbitrary"),
        name="slots",
    )(pad_start, riT)


def _dispatch_kernel(d0_ref, d1_ref, tail_ref, h_ref, xs_ref, zero_ref, sem, zsem, *, n_slab):
    tm = h_ref.shape[0] // n_slab
    base = pl.program_id(0) * tm
    blk_rows = zero_ref.shape[0]

    @pl.when(pl.program_id(0) == 0)
    def _():
        zero_ref[...] = jnp.zeros_like(zero_ref)

        def tail_copy(e):
            start = pl.multiple_of(tail_ref[e] * blk_rows, blk_rows)
            return pltpu.make_async_copy(zero_ref, xs_ref.at[pl.ds(start, blk_rows), :], zsem)

        for e in range(tail_ref.shape[0]):
            @pl.when(tail_ref[e] >= 0)
            def _():
                tail_copy(e).start()
        for e in range(tail_ref.shape[0]):
            @pl.when(tail_ref[e] >= 0)
            def _():
                tail_copy(e).wait()

    def body(g, carry):
        r8 = pl.multiple_of(g * DMA_UNROLL, DMA_UNROLL)
        for u in range(DMA_UNROLL):
            src = h_ref.at[pl.ds((r8 + u) * n_slab, n_slab), :]
            pltpu.make_async_copy(src, xs_ref.at[pl.ds(d0_ref[base + r8 + u], n_slab), :], sem).start(priority=0)
            pltpu.make_async_copy(src, xs_ref.at[pl.ds(d1_ref[base + r8 + u], n_slab), :], sem).start(priority=1)
        return carry

    lax.fori_loop(0, tm // DMA_UNROLL, body, 0)
    for _ in range(TOPK_IN_GROUP):
        pltpu.make_async_copy(h_ref, xs_ref.at[pl.ds(0, tm * n_slab), :], sem).wait()


def _dispatch(dest0, dest1, tail_blk, h2p, n_pad, n_slab):
    tm = DISPATCH_TILE
    T = h2p.shape[0] // n_slab
    grid_spec = pltpu.PrefetchScalarGridSpec(
        num_scalar_prefetch=3,
        grid=(T // tm,),
        in_specs=[pl.BlockSpec((tm * n_slab, LANES), lambda i, d0, d1, tb: (i, 0))],
        out_specs=pl.BlockSpec(memory_space=pl.ANY),
        scratch_shapes=[pltpu.VMEM((EXPERT_ROWS * n_slab, LANES), h2p.dtype),
                        pltpu.SemaphoreType.DMA(()),
                        pltpu.SemaphoreType.DMA(())],
    )
    return pl.pallas_call(
        functools.partial(_dispatch_kernel, n_slab=n_slab),
        grid_spec=grid_spec,
        out_shape=jax.ShapeDtypeStruct((n_pad * n_slab, LANES), h2p.dtype),
        compiler_params=_cparams("arbitrary"),
        name="dispatch",
    )(dest0, dest1, tail_blk, h2p)


def _expert_kernel(be_ref, nv_ref, nxt_ref, slot_ref, xs_ref, w1f_ref, w3f_ref, w2f_ref, ys_ref,
                   w1_ref, w3_ref, w2_ref, f1_ref, f3_ref, f2_ref, sem):
    i = pl.program_id(0)
    wf_hbm = (w1f_ref, w3f_ref, w2f_ref)
    wf_vmem = (f1_ref, f3_ref, f2_ref)
    w_bf16 = (w1_ref, w3_ref, w2_ref)

    def fetch(e, s):
        return [pltpu.make_async_copy(wf_hbm[m].at[e], wf_vmem[m].at[s], sem.at[s, m]) for m in range(3)]

    @pl.when(i == 0)
    def _():
        for cp in fetch(be_ref[0], slot_ref[0]):
            cp.start()

    @pl.when((i == 0) | (be_ref[i] != be_ref[jnp.maximum(i - 1, 0)]))
    def _():
        s = slot_ref[i]
        for m, cp in enumerate(fetch(be_ref[i], s)):
            cp.wait()
            w_bf16[m][...] = wf_vmem[m][s].astype(BF16)

        @pl.when(nxt_ref[i] >= 0)
        def _():
            for cp in fetch(nxt_ref[i], 1 - s):
                cp.start()

    @pl.when(i >= nv_ref[0])
    def _():
        ys_ref[...] = jnp.zeros_like(ys_ref)

    @pl.when(i < nv_ref[0])
    def _():
        half_d = w1_ref.shape[0] // 2
        xp = _load_slabs(xs_ref, 0, EXPERT_ROWS, _slabs(half_d))
        x_lo = pltpu.bitcast(xp << 16, F32).astype(BF16)
        x_hi = pltpu.bitcast(xp & jnp.uint32(0xFFFF0000), F32).astype(BF16)
        a = (jnp.dot(x_lo, w1_ref[:half_d], preferred_element_type=F32)
             + jnp.dot(x_hi, w1_ref[half_d:], preferred_element_type=F32))
        b = (jnp.dot(x_lo, w3_ref[:half_d], preferred_element_type=F32)
             + jnp.dot(x_hi, w3_ref[half_d:], preferred_element_type=F32))
        hid = (a * jax.nn.sigmoid(a) * b).astype(BF16)
        y = jnp.dot(hid, w2_ref[...], preferred_element_type=F32)
        bits = pltpu.bitcast(y.astype(BF16).astype(F32), jnp.uint32)
        _store_slabs(ys_ref, 0, (bits[:, :half_d] >> 16) | bits[:, half_d:])


def _experts(blk_expert, n_valid, xs, w1f, w3f, w2f):
    _, D, F = w1f.shape
    rb = EXPERT_ROWS
    blk_rows = rb * _slabs(D // 2)
    n_steps = xs.shape[0] // blk_rows
    first = jnp.concatenate([jnp.ones((1,), jnp.bool_), blk_expert[1:] != blk_expert[:-1]])
    slot = ((jnp.cumsum(first.astype(jnp.int32)) - 1) % 2).astype(jnp.int32)
    later = jnp.where(blk_expert[None, :] > blk_expert[:, None], blk_expert[None, :], N_EXPERTS)
    nxt = jnp.min(later, axis=1)
    nxt = jnp.where(nxt < N_EXPERTS, nxt, -1).astype(jnp.int32)
    row_blk = lambda i, be, nv, nx, sl: (jnp.minimum(i, nv[0] - 1), 0)
    out_blk = lambda i, be, nv, nx, sl: (i, 0)
    grid_spec = pltpu.PrefetchScalarGridSpec(
        num_scalar_prefetch=4,
        grid=(n_steps,),
        in_specs=[pl.BlockSpec((blk_rows, LANES), row_blk),
                  pl.BlockSpec(memory_space=pl.ANY),
                  pl.BlockSpec(memory_space=pl.ANY),
                  pl.BlockSpec(memory_space=pl.ANY)],
        out_specs=pl.BlockSpec((blk_rows, LANES), out_blk),
        scratch_shapes=[pltpu.VMEM((D, F), BF16), pltpu.VMEM((D, F), BF16), pltpu.VMEM((F, D), BF16),
                        pltpu.VMEM((2, D, F), F32), pltpu.VMEM((2, D, F), F32), pltpu.VMEM((2, F, D), F32),
                        pltpu.SemaphoreType.DMA((2, 3))],
    )
    return pl.pallas_call(
        _expert_kernel,
        grid_spec=grid_spec,
        out_shape=jax.ShapeDtypeStruct(xs.shape, jnp.uint32),
        compiler_params=_cparams("arbitrary"),
        name="experts",
    )(blk_expert, n_valid, nxt, slot, xs, w1f, w3f, w2f)


def _combine_kernel(d0_ref, d1_ref, x1_ref, rf_ref, ga2_ref, ys_ref, o_ref, buf_ref, sem, *, n_slab):
    i = pl.program_id(0)
    n = pl.num_programs(0)
    tm = x1_ref.shape[0]

    def gather(step, slot):
        base = step * tm

        def body(g, carry):
            r8 = pl.multiple_of(g * DMA_UNROLL, DMA_UNROLL)
            for u in range(DMA_UNROLL):
                dst = pl.ds((r8 + u) * n_slab, n_slab)
                pltpu.make_async_copy(ys_ref.at[pl.ds(d0_ref[base + r8 + u], n_slab), :],
                                      buf_ref.at[slot, 0, dst, :], sem.at[slot]).start(priority=0)
                pltpu.make_async_copy(ys_ref.at[pl.ds(d1_ref[base + r8 + u], n_slab), :],
                                      buf_ref.at[slot, 1, dst, :], sem.at[slot]).start(priority=1)
            return carry

        lax.fori_loop(0, tm // DMA_UNROLL, body, 0)

    @pl.when(i == 0)
    def _():
        gather(0, 0)

    @pl.when(i + 1 < n)
    def _():
        gather(i + 1, (i + 1) % 2)

    slot = i % 2
    for k in range(TOPK_IN_GROUP):
        pltpu.make_async_copy(ys_ref.at[pl.ds(0, tm * n_slab), :], buf_ref.at[slot, k], sem.at[slot]).wait()
    rf = rf_ref[...]
    w = [rf[:, k:k + 1] for k in range(TOPK_IN_GROUP)]
    yp = [_load_slabs(buf_ref, 0, tm, n_slab, lead=(slot, k)) for k in range(TOPK_IN_GROUP)]
    half_d = n_slab * LANES
    y_lo = sum(w[k] * pltpu.bitcast(yp[k] << 16, F32) for k in range(TOPK_IN_GROUP))
    y_hi = sum(w[k] * pltpu.bitcast(yp[k] & jnp.uint32(0xFFFF0000), F32) for k in range(TOPK_IN_GROUP))
    ga2 = ga2_ref[0]
    o_ref[:, :half_d] = x1_ref[:, :half_d] + ga2[:, :half_d] * y_lo
    o_ref[:, half_d:] = x1_ref[:, half_d:] + ga2[:, half_d:] * y_hi


def _combine(dest0, dest1, x1, rf, ga2, ys, tiles_per_seq):
    T, D = x1.shape
    n_slab = _slabs(D // 2)
    tm = COMBINE_TILE
    row = lambda i, d0, d1: (i, 0)
    grid_spec = pltpu.PrefetchScalarGridSpec(
        num_scalar_prefetch=2,
        grid=(T // tm,),
        in_specs=[pl.BlockSpec((tm, D), row),
                  pl.BlockSpec((tm, LANES), row),
                  pl.BlockSpec((1, 1, D), lambda i, d0, d1: (i // tiles_per_seq, 0, 0)),
                  pl.BlockSpec(memory_space=pl.ANY)],
        out_specs=pl.BlockSpec((tm, D), row),
        scratch_shapes=[pltpu.VMEM((2, TOPK_IN_GROUP, tm * n_slab, LANES), ys.dtype),
                        pltpu.SemaphoreType.DMA((2,))],
    )
    return pl.pallas_call(
        functools.partial(_combine_kernel, n_slab=n_slab),
        grid_spec=grid_spec,
        out_shape=jax.ShapeDtypeStruct((T, D), F32),
        compiler_params=_cparams("arbitrary"),
        name="combine",
    )(dest0, dest1, x1, rf, ga2, ys)


def _rope_tables(S):
    pos = jnp.arange(S, dtype=F32)
    inv_freq = ROPE_THETA ** (-jnp.arange(0, ROT_DIM, 2, dtype=F32) / ROT_DIM)
    ang = pos[:, None] * inv_freq[None, :]
    cos, sin = jnp.cos(ang), jnp.sin(ang)
    half = ROT_DIM // 2
    ones = jnp.ones((S, HEAD_DIM - ROT_DIM), F32)
    cos_h = jnp.concatenate([cos, cos, ones], axis=1)
    sin_h = jnp.concatenate([-sin, sin, 0.0 * ones], axis=1)
    return jnp.tile(cos_h, (1, LANES // HEAD_DIM)), jnp.tile(sin_h, (1, LANES // HEAD_DIM)), cos.T, sin.T


def kernel(x, c, w_ada, b_ada, g_norm1, g_norm2, w_in, g_q, g_k, conv_w, conv_b,
           w_pa, w_pb, w_o, w_rg, b_rg, w_re, b_re, w1, w3, w2):
    B, S, D = x.shape
    T = B * S
    assert S % POST_TILE == 0 and S % COMBINE_TILE == 0 and T % DISPATCH_TILE == 0 and S % QUERY_TILE == 0 and QUERY_TILE % (2 * MOBA_BLOCK) == 0
    assert S // MOBA_BLOCK <= LANES - HEAD_DIM
    l = 0

    mod = _ada(c, w_ada[l], b_ada[l])
    sh1, sc1, ga1, sh2, sc2, ga2 = [m.reshape(B, 1, D) for m in jnp.split(mod, N_MOD, axis=-1)]

    x2 = x.reshape(T, D)
    z2 = _inproj(x2, g_norm1[l].reshape(1, D), sc1, sh1, w_in[l].astype(BF16), S)

    cosf, sinf, cost, sint = _rope_tables(S)
    rep = LANES // HEAD_DIM
    gq_cols = jnp.broadcast_to(jnp.tile(g_q[l], rep).reshape(LANES, 1), (LANES, QUERY_TILE))
    ya = _attention(z2.reshape(B, S, -1), cosf, sinf, cost, sint, gq_cols, jnp.tile(g_k[l], rep).reshape(1, LANES))

    wr = jnp.zeros((D, LANES), F32).at[:, :N_GROUPS].set(w_rg[l]).at[:, N_GROUPS:N_GROUPS + N_EXPERTS].set(w_re[l])
    br = jnp.zeros((1, LANES), F32).at[0, :N_GROUPS].set(b_rg[l]).at[0, N_GROUPS:N_GROUPS + N_EXPERTS].set(b_re[l])
    wr_hi = wr.astype(BF16)
    wr2 = jnp.concatenate([wr_hi, (wr - wr_hi.astype(F32)).astype(BF16)], axis=1)
    tri = (lax.broadcasted_iota(jnp.int32, (POST_CHUNK, POST_CHUNK), 1)
           < lax.broadcasted_iota(jnp.int32, (POST_CHUNK, POST_CHUNK), 0)).astype(BF16)
    x1, h2, ri, rf, cnt = _post(x2, ya.reshape(T, ATTN_WIDTH), z2, ga1, sc2, sh2,
                                conv_w[l], conv_b[l].reshape(1, CONV_WIDTH),
                                w_pa[l].astype(BF16), w_pb[l].astype(BF16), w_o[l].astype(BF16),
                                g_norm2[l].reshape(1, D), wr2, br, tri, S // POST_TILE)

    rb = EXPERT_ROWS
    counts = cnt[0, :N_EXPERTS].astype(jnp.int32)
    padded = (counts + rb - 1) // rb * rb
    pad_end = jnp.cumsum(padded)
    pad_start = pad_end - padded
    dest = _slots(pad_start.astype(jnp.int32), ri, _slabs(D // 2))
    dest0, dest1 = dest[0], dest[1]
    n_blocks = -(-T * TOPK_IN_GROUP // rb) + N_EXPERTS
    n_pad = n_blocks * rb
    n_valid = (pad_end[-1] // rb).astype(jnp.int32)
    blk_start = jnp.minimum(jnp.arange(n_blocks, dtype=jnp.int32), n_valid - 1) * rb
    blk_expert = jnp.sum(pad_end[None, :] <= blk_start[:, None], axis=-1).astype(jnp.int32)
    unused = n_valid + jnp.arange(N_EXPERTS, dtype=jnp.int32)
    tail_blk = jnp.concatenate([jnp.where(padded > 0, pad_end // rb - 1, -1),
                                jnp.where(unused < n_blocks, unused, -1)]).astype(jnp.int32)

    xs = _dispatch(dest0, dest1, tail_blk, h2, n_pad, _slabs(D // 2))
    ys = _experts(blk_expert, n_valid.reshape(1), xs, w1[l], w3[l], w2[l])
    out = _combine(dest0, dest1, x1, rf, ga2, ys, S // COMBINE_TILE)
    return out.reshape(B, S, D)
```

```python
import functools

import jax
import jax.numpy as jnp
from jax import lax
from jax.experimental import pallas as pl
from jax.experimental.pallas import tpu as pltpu

F32 = jnp.float32
BF16 = jnp.bfloat16
HIGHEST = lax.Precision.HIGHEST

N_HEADS = 8
HEAD_DIM = 64
ATTN_WIDTH = N_HEADS * HEAD_DIM
CONV_WIDTH = 512
CONV_K = 3
MOBA_BLOCK = 256
MOBA_TOPK = 3
ROPE_THETA = 500000.0
ROT_DIM = HEAD_DIM // 4
N_GROUPS = 4
EXPERTS_PER_GROUP = 8
N_EXPERTS = N_GROUPS * EXPERTS_PER_GROUP
TOPK_IN_GROUP = 2
N_MOD = 6
EPS = 1e-6

LANES = 128
NEG = -1e30
POST_TILE = 1024
COMBINE_TILE = 512
INPROJ_TILE = 1024
DISPATCH_TILE = 4096
POST_CHUNK = 512
QUERY_TILE = 512
DMA_UNROLL = 8
EXPERT_ROWS = 512
VMEM_LIMIT = 56 * 1024 * 1024
ATTN_VMEM_LIMIT = 60 * 1024 * 1024


def _cparams(*sem):
    return pltpu.CompilerParams(dimension_semantics=sem, vmem_limit_bytes=VMEM_LIMIT)


def _ada_kernel(c_ref, w_ref, b_ref, o_ref):
    c = c_ref[...]
    a = c * jax.nn.sigmoid(c)
    o_ref[...] = jnp.dot(a, w_ref[...], preferred_element_type=F32, precision=HIGHEST) + b_ref[...]


def _ada(c, w_ada, b_ada):
    B, D = c.shape
    N = w_ada.shape[1]
    tn = 1536
    return pl.pallas_call(
        _ada_kernel,
        grid=(N // tn,),
        in_specs=[pl.BlockSpec((B, D), lambda j: (0, 0)),
                  pl.BlockSpec((D, tn), lambda j: (0, j)),
                  pl.BlockSpec((1, tn), lambda j: (0, j))],
        out_specs=pl.BlockSpec((B, tn), lambda j: (0, j)),
        out_shape=jax.ShapeDtypeStruct((B, N), F32),
        compiler_params=_cparams("arbitrary"),
        name="ada",
    )(c, w_ada, b_ada.reshape(1, N))


def _inproj_kernel(x_ref, g_ref, sc_ref, sh_ref, w_ref, z_ref, *, n_chunk):
    x = x_ref[...]
    ms = jnp.mean(x * x, axis=-1, keepdims=True)
    y = x * lax.rsqrt(ms + EPS) * g_ref[...]
    h = (y * (1.0 + sc_ref[0]) + sh_ref[0]).astype(BF16)
    for n in range(0, z_ref.shape[1], n_chunk):
        z_ref[:, n:n + n_chunk] = jnp.dot(h, w_ref[:, n:n + n_chunk],
                                          preferred_element_type=F32).astype(BF16)


def _inproj(x2, g1, sc1, sh1, w_in_bf, seq_len):
    T, D = x2.shape
    N = w_in_bf.shape[1]
    tm = INPROJ_TILE
    assert seq_len % tm == 0
    bmap = lambda i: (i // (seq_len // tm), 0, 0)
    return pl.pallas_call(
        functools.partial(_inproj_kernel, n_chunk=512),
        grid=(T // tm,),
        in_specs=[pl.BlockSpec((tm, D), lambda i: (i, 0)),
                  pl.BlockSpec((1, D), lambda i: (0, 0)),
                  pl.BlockSpec((1, 1, D), bmap),
                  pl.BlockSpec((1, 1, D), bmap),
                  pl.BlockSpec((D, N), lambda i: (0, 0), pipeline_mode=pl.Buffered(1))],
        out_specs=pl.BlockSpec((tm, N), lambda i: (i, 0)),
        out_shape=jax.ShapeDtypeStruct((T, N), BF16),
        compiler_params=_cparams("arbitrary"),
        name="inproj",
    )(x2, g1, sc1, sh1, w_in_bf)


def _fold_rows(x, op):
    parts = [x[r:r + 8] for r in range(0, x.shape[0], 8)]
    while len(parts) > 1:
        parts = [op(parts[i], parts[i + 1]) for i in range(0, len(parts) - 1, 2)] + (
            [parts[-1]] if len(parts) % 2 else [])
    return parts[0]


def _attn_kernel(q_ref, k_ref, v_ref, cos_ref, sin_ref, cost_ref, sint_ref, gq_ref, gk_ref, o_ref,
                 kaug_ref, vt_ref, kmp_ref, kst_ref, s_ref, mcol_ref, qa_ref):
    S = q_ref.shape[1]
    blk = MOBA_BLOCK
    qt = QUERY_TILE
    sub = qt // blk
    nb = S // blk
    nq = S // qt
    nbp = kst_ref.shape[0] // 6
    hd = HEAD_DIM

    half = ROT_DIM // 2
    lane_r = lax.broadcasted_iota(jnp.int32, (blk, LANES), 1)
    rot_lo = (lane_r & (hd - 1)) < half
    same_head = jnp.where((lax.broadcasted_iota(jnp.int32, (LANES, LANES), 0) < hd)
                          == (lax.broadcasted_iota(jnp.int32, (LANES, LANES), 1) < hd), 1.0, 0.0).astype(BF16)

    def norm_rope_keys(xb, r0):
        x = xb.astype(F32)
        sq = x * x
        sq_hi = sq.astype(BF16)
        sq_lo = (sq - sq_hi.astype(F32)).astype(BF16)
        ssq = (jnp.dot(sq_hi, same_head, preferred_element_type=F32)
               + jnp.dot(sq_lo, same_head, preferred_element_type=F32))
        y = x * lax.rsqrt(ssq * (1.0 / hd) + EPS) * gk_ref[...]
        rot = jnp.where(rot_lo, pltpu.roll(y, LANES - half, 1), pltpu.roll(y, half, 1))
        return y * cos_ref[pl.ds(r0, blk), :] + rot * sin_ref[pl.ds(r0, blk), :]

    def norm_rope_queries_t(xb, r0):
        xT = xb.astype(F32).T
        sq = xT * xT
        cos = cost_ref[:, pl.ds(r0, qt)]
        sin = sint_ref[:, pl.ds(r0, qt)]
        rows = []
        for h in range(2):
            lo, hi = h * hd, (h + 1) * hd
            ssq = jnp.sum(_fold_rows(sq[lo:hi], jnp.add), axis=0, keepdims=True)
            y = xT[lo:hi] * lax.rsqrt(ssq * (1.0 / hd) + EPS) * gq_ref[lo:hi, :]
            y1, y2 = y[0:half], y[half:2 * half]
            rows += [y1 * cos - y2 * sin, y2 * cos + y1 * sin, y[2 * half:]]
        return jnp.concatenate(rows, axis=0)

    kmp_ref[...] = jnp.zeros_like(kmp_ref)
    ones_row = jnp.where(lax.broadcasted_iota(jnp.int32, (16, blk), 0) == 0, 1.0, 0.0).astype(BF16)
    lane_k = lax.broadcasted_iota(jnp.int32, (blk, LANES), 1)
    head0_k = lane_k < HEAD_DIM
    lane_m = lax.broadcasted_iota(jnp.int32, (nbp, LANES), 1)

    def prepare_keys(t):
        for u in range(sub):
            j = t * sub + u
            r0 = pl.multiple_of(jnp.minimum(j, nb - 1) * blk, blk)
            kr = norm_rope_keys(k_ref[0, pl.ds(r0, blk), :], r0)
            kmp_ref[pl.ds(j, 1), :] = jnp.sum(kr, axis=0, keepdims=True) * (1.0 / blk)
            kaug_ref[0, j] = jnp.where(head0_k, kr, jnp.where(lane_k - hd == j, 1.0, 0.0)).astype(BF16)
            kaug_ref[1, j] = jnp.where(head0_k, jnp.where(lane_k == j, 1.0, 0.0), kr).astype(BF16)
            vT = v_ref[0, pl.ds(r0, blk), :].astype(F32).T
            for h in range(2):
                vt_ref[h, j, 0:HEAD_DIM, :] = vT[h * HEAD_DIM:(h + 1) * HEAD_DIM].astype(BF16)
                vt_ref[h, j, HEAD_DIM:HEAD_DIM + 16, :] = ones_row
        kmp = kmp_ref[0:nbp, :]
        k_hi = kmp.astype(BF16)
        k_lo = (kmp - k_hi.astype(F32)).astype(BF16)
        zero = jnp.zeros((nbp, LANES), BF16)
        parts = []
        for h in range(2):
            mine = (lane_m < hd) if h == 0 else (lane_m >= hd)
            parts += [jnp.where(mine, k_hi, zero), jnp.where(mine, k_lo, zero)]
        parts += [parts[0], parts[2]]
        for n, part in enumerate(parts):
            kst_ref[n * nbp:(n + 1) * nbp, :] = part

    prepare_keys(0)

    causal = (lax.broadcasted_iota(jnp.int32, (blk, blk), 0) <= lax.broadcasted_iota(jnp.int32, (blk, blk), 1))
    rowf = lax.broadcasted_iota(jnp.int32, (nbp, qt), 0).astype(F32)
    subf = (lax.broadcasted_iota(jnp.int32, (nbp, qt), 1) // blk).astype(F32)
    q_scale = (hd ** -0.5) * 1.4426950408889634

    def query_operands(t):
        r0 = pl.multiple_of(t * qt, qt)
        qT = norm_rope_queries_t(q_ref[0, pl.ds(r0, qt), :], r0)
        cur = lax.convert_element_type(t * sub, F32) + subf
        q_hi = qT.astype(BF16)
        q_lo = (qT - q_hi.astype(F32)).astype(BF16)
        g1 = jnp.dot(kst_ref[0:4 * nbp, :], q_hi, preferred_element_type=F32)
        g2 = jnp.dot(kst_ref[4 * nbp:6 * nbp, :], q_lo, preferred_element_type=F32)
        qa = []
        for h in range(2):
            gate = g1[2 * h * nbp:(2 * h + 1) * nbp] + g1[(2 * h + 1) * nbp:(2 * h + 2) * nbp] + g2[h * nbp:(h + 1) * nbp]
            g = jnp.where(rowf < cur, gate, -jnp.inf)
            keep = rowf == cur
            for r in range(MOBA_TOPK):
                m = jnp.max(g, axis=0, keepdims=True)
                idx = jnp.min(jnp.where(g == m, rowf, 1e9), axis=0, keepdims=True)
                pick = (rowf == idx) & (cur > r)
                keep = keep | pick
                g = jnp.where(pick, -jnp.inf, g)
            bias = jnp.where(keep, 0.0, NEG)
            qs = qT[h * hd:(h + 1) * hd] * q_scale
            pad = jnp.zeros((LANES - hd - nbp, qt), F32)
            pieces = [qs, bias, pad] if h == 0 else [bias, pad, qs]
            qa.append(jnp.concatenate(pieces, axis=0).astype(BF16))
        return qa

    def logits(h, j0, n, qa_h):
        return jnp.dot(kaug_ref[h, pl.ds(j0, n)].reshape(n * blk, LANES), qa_h, preferred_element_type=F32)

    def pass2_tile(par, h, j):
        pT = jnp.exp2(s_ref[par, h, j] - mcol_ref[h, 0:1, :]).astype(BF16)
        return jnp.dot(vt_ref[h, j], pT, preferred_element_type=F32)

    def pass1_own(t, par, qa):
        mx = []
        for h in range(2):
            f = None
            for u in range(sub):
                lo = u * blk
                su = jnp.dot(kaug_ref[h, t * sub + u], qa[h][:, lo:], preferred_element_type=F32)
                pieces = [jnp.where(causal, su[:, :blk], NEG)] + [su[:, blk:]] * (lo + blk < qt)
                su = jnp.concatenate(pieces, axis=1)
                s_ref[par, h, t * sub + u, :, lo:] = su
                fu = _fold_rows(su, jnp.maximum)
                f = fu if f is None else jnp.concatenate([f[:, :lo], jnp.maximum(f[:, lo:], fu)], axis=1)
            mx.append(f)
        return mx

    def pass2_own(t, par):
        out = []
        for h in range(2):
            acc = None
            for u in range(sub):
                lo = u * blk
                pT = jnp.exp2(s_ref[par, h, t * sub + u, :, lo:] - mcol_ref[h, 0:1, lo:]).astype(BF16)
                d = jnp.dot(vt_ref[h, t * sub + u], pT, preferred_element_type=F32)
                acc = d if acc is None else jnp.concatenate([acc[:, :lo], acc[:, lo:] + d], axis=1)
            out.append(acc)
        return out

    def pass1_pair(p, par, qa, mx):
        out = []
        for h in range(2):
            sT = logits(h, 2 * p, 2, qa[h])
            s_ref[par, h, pl.ds(2 * p, 2)] = sT.reshape(2, blk, qt)
            out.append(jnp.maximum(mx[h], _fold_rows(sT, jnp.maximum)))
        return out

    def pass2_pair(p, par, acc):
        return [acc[h] + sum(pass2_tile(par, h, 2 * p + u) for u in range(2)) for h in range(2)]

    def finish_pass1(mx):
        for h in range(2):
            mcol_ref[h] = jnp.broadcast_to(jnp.max(mx[h], axis=0, keepdims=True), mcol_ref.shape[1:])

    def finish_pass2(t, acc):
        outT = jnp.concatenate([acc[h][0:hd] / acc[h][hd:hd + 1] for h in range(2)], axis=0)
        o_ref[0, pl.ds(pl.multiple_of(t * qt, qt), qt), :] = outT.T.astype(BF16)

    def stage(t, par):

        def prepare_next():
            prepare_keys(t + 1)
            nxt = query_operands(jnp.minimum(t + 1, nq - 1))
            for h in range(2):
                qa_ref[1 - par, h] = nxt[h]

        @pl.when(t == 0)
        def _():
            finish_pass1(pass1_own(t, par, query_operands(t)))
            prepare_next()

        @pl.when((t > 0) & (t < nq))
        def _():
            qa = [qa_ref[par, h] for h in range(2)]
            mx = pass1_own(t, par, qa)
            acc = pass2_own(t - 1, 1 - par)
            prepare_next()
            n_prev = (t - 1) * sub // 2

            def both(p, c):
                mx, acc = c
                return tuple(pass1_pair(p, par, qa, mx)), tuple(pass2_pair(p, 1 - par, acc))

            def both_twice(p2, c):
                return both(2 * p2 + 1, both(2 * p2, c))

            c = lax.fori_loop(0, n_prev // 2, both_twice, (tuple(mx), tuple(acc)))
            mx, acc = lax.fori_loop(n_prev // 2 * 2, n_prev, both, c)
            for p in range(sub // 2):
                mx = pass1_pair(n_prev + p, par, qa, mx)
            finish_pass2(t - 1, acc)
            finish_pass1(mx)

        @pl.when(t == nq)
        def _():
            acc = tuple(pass2_own(t - 1, 1 - par))
            n_prev = (t - 1) * sub // 2
            acc = lax.fori_loop(0, n_prev // 2, lambda p2, a: tuple(pass2_pair(2 * p2 + 1, 1 - par, pass2_pair(2 * p2, 1 - par, a))), acc)
            acc = lax.fori_loop(n_prev // 2 * 2, n_prev, lambda p, a: tuple(pass2_pair(p, 1 - par, a)), acc)
            finish_pass2(t - 1, acc)

    def stage_pair(tt, carry):
        stage(2 * tt, 0)
        stage(2 * tt + 1, 1)
        return carry

    lax.fori_loop(0, (nq + 2) // 2, stage_pair, 0)


def _attention(z3, cosf, sinf, cost, sint, gq_cols, gk2):
    B, S, _ = z3.shape
    n_pair = N_HEADS // 2
    kq = ATTN_WIDTH // LANES
    nb = S // MOBA_BLOCK
    nbp = -(-nb // 16) * 16
    sub = QUERY_TILE // MOBA_BLOCK
    assert sub <= 8
    return pl.pallas_call(
        _attn_kernel,
        grid=(B, n_pair),
        in_specs=[pl.BlockSpec((1, S, LANES), lambda b, p: (b, 0, p)),
                  pl.BlockSpec((1, S, LANES), lambda b, p: (b, 0, kq + p)),
                  pl.BlockSpec((1, S, LANES), lambda b, p: (b, 0, 2 * kq + p)),
                  pl.BlockSpec((S, LANES), lambda b, p: (0, 0), pipeline_mode=pl.Buffered(1)),
                  pl.BlockSpec((S, LANES), lambda b, p: (0, 0), pipeline_mode=pl.Buffered(1)),
                  pl.BlockSpec(cost.shape, lambda b, p: (0, 0)),
                  pl.BlockSpec(sint.shape, lambda b, p: (0, 0)),
                  pl.BlockSpec((LANES, QUERY_TILE), lambda b, p: (0, 0)),
                  pl.BlockSpec((1, LANES), lambda b, p: (0, 0))],
        out_specs=pl.BlockSpec((1, S, LANES), lambda b, p: (b, 0, p)),
        out_shape=jax.ShapeDtypeStruct((B, S, ATTN_WIDTH), BF16),
        scratch_shapes=[pltpu.VMEM((2, nb + sub, MOBA_BLOCK, LANES), BF16),
                        pltpu.VMEM((2, nb + sub, HEAD_DIM + 16, MOBA_BLOCK), BF16),
                        pltpu.VMEM((nbp + 8, LANES), F32),
                        pltpu.VMEM((6 * nbp, LANES), BF16),
                        pltpu.VMEM((2, 2, nb, MOBA_BLOCK, QUERY_TILE), F32),
                        pltpu.VMEM((2, 8, QUERY_TILE), F32),
                        pltpu.VMEM((2, 2, LANES, QUERY_TILE), BF16)],
        compiler_params=pltpu.CompilerParams(dimension_semantics=("arbitrary", "arbitrary"),
                                             vmem_limit_bytes=ATTN_VMEM_LIMIT),
        name="attn",
    )(z3, z3, z3, cosf, sinf, cost, sint, gq_cols, gk2)


def _slabs(width):
    return width // LANES


def _load_slabs(ref, row0, rows, n_slab, lead=()):
    return jnp.concatenate([ref[lead + (pl.ds(row0 * n_slab + s, rows, stride=n_slab), slice(None))]
                            for s in range(n_slab)], axis=1)


def _store_slabs(ref, row0, val):
    rows, width = val.shape
    n_slab = _slabs(width)
    for s in range(n_slab):
        ref[pl.ds(row0 * n_slab + s, rows, stride=n_slab), :] = val[:, s * LANES:(s + 1) * LANES]


def _post_kernel(x_ref, ya_ref, xb_ref, bg_ref, cg_ref, gta_ref, gtb_ref, ga1_ref, sc2_ref, sh2_ref,
                 cw_ref, cb_ref, wpa_ref, wpb_ref, wo_ref, g2_ref, wr_ref, br_ref, tri_ref,
                 x1_ref, h2_ref, ri_ref, rf_ref, cnt_ref, ubuf_ref, run_ref, *, tiles_per_seq):
    i = pl.program_id(0)
    tm = x_ref.shape[0]
    rc = POST_CHUNK
    halo = 8

    @pl.when(i == 0)
    def _():
        run_ref[...] = jnp.zeros_like(run_ref)

    @pl.when(i % tiles_per_seq == 0)
    def _():
        ubuf_ref[0:halo, :] = jnp.zeros((halo, CONV_WIDTH), F32)

    ubuf_ref[halo:halo + tm, :] = cg_ref[...].astype(F32) * xb_ref[...].astype(F32)
    cw = cw_ref[...]
    lanef = lax.broadcasted_iota(jnp.int32, (rc, LANES), 1).astype(F32)
    half_d = x_ref.shape[1] // 2
    run = run_ref[0:1, :]

    for c in range(tm // rc):
        rows = pl.ds(c * rc, rc)
        conv = (cw[0:1, :] * ubuf_ref[pl.ds(halo - 2 + c * rc, rc), :]
                + cw[1:2, :] * ubuf_ref[pl.ds(halo - 1 + c * rc, rc), :]
                + cw[2:3, :] * ubuf_ref[pl.ds(halo + c * rc, rc), :])
        y_b = (bg_ref[rows, :].astype(F32) * (conv + cb_ref[...])).astype(BF16)
        pa = jnp.dot(ya_ref[rows, :], wpa_ref[...], preferred_element_type=F32)
        pb = jnp.dot(y_b, wpb_ref[...], preferred_element_type=F32)
        merged = (jax.nn.sigmoid(gta_ref[rows, :].astype(F32)) * pa
                  + jax.nn.sigmoid(gtb_ref[rows, :].astype(F32)) * pb).astype(BF16)
        x1 = x_ref[rows, :] + ga1_ref[0] * jnp.dot(merged, wo_ref[...], preferred_element_type=F32)
        x1_ref[rows, :] = x1

        ms = jnp.mean(x1 * x1, axis=-1, keepdims=True)
        h2 = x1 * lax.rsqrt(ms + EPS) * g2_ref[...]
        h2 = h2 * (1.0 + sc2_ref[0]) + sh2_ref[0]
        h_hi = h2.astype(BF16)
        h_hi32 = h_hi.astype(F32)
        bits = pltpu.bitcast(h_hi32, jnp.uint32)
        _store_slabs(h2_ref, c * rc, (bits[:, :half_d] >> 16) | bits[:, half_d:])

        h_lo = (h2 - h_hi32).astype(BF16)
        r = jnp.dot(h_hi, wr_ref[...], preferred_element_type=F32)
        logit = (r[:, :LANES] + r[:, LANES:]
                 + jnp.dot(h_lo, wr_ref[:, :LANES], preferred_element_type=F32) + br_ref[...])
        gl = jnp.where(lanef < N_GROUPS, logit, -jnp.inf)
        gmax = jnp.max(gl, axis=-1, keepdims=True)
        g_idx = jnp.min(jnp.where(gl == gmax, lanef, 1e9), axis=-1, keepdims=True)
        g_w = 1.0 / jnp.sum(jnp.exp(gl - gmax), axis=-1, keepdims=True)
        e_lo = N_GROUPS + EXPERTS_PER_GROUP * g_idx
        el = jnp.where((lanef >= e_lo) & (lanef < e_lo + EXPERTS_PER_GROUP), logit, -jnp.inf)
        v0 = jnp.max(el, axis=-1, keepdims=True)
        i0 = jnp.min(jnp.where(el == v0, lanef, 1e9), axis=-1, keepdims=True)
        el = jnp.where(lanef == i0, -jnp.inf, el)
        v1 = jnp.max(el, axis=-1, keepdims=True)
        i1 = jnp.min(jnp.where(el == v1, lanef, 1e9), axis=-1, keepdims=True)
        t = jnp.exp(v1 - v0)
        w0 = g_w / (1.0 + t)
        w1 = g_w * t / (1.0 + t)
        e0 = i0 - N_GROUPS
        e1 = i1 - N_GROUPS

        oh0 = lanef == e0
        oh1 = lanef == e1
        oh = jnp.where(oh0 | oh1, 1.0, 0.0)
        before = jnp.dot(tri_ref[...], oh.astype(BF16), preferred_element_type=F32) + run
        r0 = jnp.sum(jnp.where(oh0, before, 0.0), axis=-1, keepdims=True)
        r1 = jnp.sum(jnp.where(oh1, before, 0.0), axis=-1, keepdims=True)
        run = run + jnp.sum(oh, axis=0, keepdims=True)

        ri = jnp.where(lanef == 0, e0, jnp.where(lanef == 1, e1, jnp.where(lanef == 2, r0, jnp.where(lanef == 3, r1, 0.0))))
        ri_ref[:, rows] = ri.astype(jnp.int32).T[0:8]
        rf_ref[rows, :] = jnp.where(lanef == 0, w0, jnp.where(lanef == 1, w1, 0.0))

    ubuf_ref[0:halo, :] = ubuf_ref[tm:tm + halo, :]
    run_ref[...] = jnp.broadcast_to(run, run_ref.shape)
    cnt_ref[...] = jnp.broadcast_to(run, cnt_ref.shape)


def _post(x2, ya2, z2, ga1, sc2, sh2, conv_w, conv_b, wpa, wpb, wo, g2, wr, br, tri, tiles_per_seq):
    T, D = x2.shape
    tm = POST_TILE
    cw = CONV_WIDTH
    xcol = 3 * ATTN_WIDTH // cw
    gcol = (3 * ATTN_WIDTH + 3 * cw) // D
    bmap = lambda i: (i // tiles_per_seq, 0, 0)
    const = lambda i: (0, 0)
    return pl.pallas_call(
        functools.partial(_post_kernel, tiles_per_seq=tiles_per_seq),
        grid=(T // tm,),
        in_specs=[pl.BlockSpec((tm, D), lambda i: (i, 0)),
                  pl.BlockSpec((tm, ATTN_WIDTH), lambda i: (i, 0)),
                  pl.BlockSpec((tm, cw), lambda i: (i, xcol)),
                  pl.BlockSpec((tm, cw), lambda i: (i, xcol + 1)),
                  pl.BlockSpec((tm, cw), lambda i: (i, xcol + 2)),
                  pl.BlockSpec((tm, D), lambda i: (i, gcol)),
                  pl.BlockSpec((tm, D), lambda i: (i, gcol + 1)),
                  pl.BlockSpec((1, 1, D), bmap),
                  pl.BlockSpec((1, 1, D), bmap),
                  pl.BlockSpec((1, 1, D), bmap),
                  pl.BlockSpec((CONV_K, cw), const),
                  pl.BlockSpec((1, cw), const),
                  pl.BlockSpec((ATTN_WIDTH, D), const),
                  pl.BlockSpec((cw, D), const),
                  pl.BlockSpec((D, D), const),
                  pl.BlockSpec((1, D), const),
                  pl.BlockSpec((D, 2 * LANES), const),
                  pl.BlockSpec((1, LANES), const),
                  pl.BlockSpec((POST_CHUNK, POST_CHUNK), const)],
        out_specs=[pl.BlockSpec((tm, D), lambda i: (i, 0)),
                   pl.BlockSpec((tm * _slabs(D // 2), LANES), lambda i: (i, 0)),
                   pl.BlockSpec((8, tm), lambda i: (0, i)),
                   pl.BlockSpec((tm, LANES), lambda i: (i, 0)),
                   pl.BlockSpec((8, LANES), const)],
        out_shape=[jax.ShapeDtypeStruct((T, D), F32),
                   jax.ShapeDtypeStruct((T * _slabs(D // 2), LANES), jnp.uint32),
                   jax.ShapeDtypeStruct((8, T), jnp.int32),
                   jax.ShapeDtypeStruct((T, LANES), F32),
                   jax.ShapeDtypeStruct((8, LANES), F32)],
        scratch_shapes=[pltpu.VMEM((tm + 16, cw), F32),
                        pltpu.VMEM((8, LANES), F32)],
        compiler_params=_cparams("arbitrary"),
        name="post",
    )(x2, ya2, z2, z2, z2, z2, z2, ga1, sc2, sh2, conv_w, conv_b, wpa, wpb, wo, g2, wr, br, tri)


def _slots_kernel(ps_ref, ri_ref, d_ref, *, n_slab):
    e = ri_ref[0:TOPK_IN_GROUP, :]
    start = jnp.zeros(e.shape, jnp.int32)
    for k in range(N_EXPERTS):
        start = jnp.where(e == k, ps_ref[k], start)
    d_ref[...] = (start + ri_ref[TOPK_IN_GROUP:2 * TOPK_IN_GROUP, :]) * n_slab


def _slots(pad_start, riT, n_slab):
    T = riT.shape[1]
    grid_spec = pltpu.PrefetchScalarGridSpec(
        num_scalar_prefetch=1,
        grid=(1,),
        in_specs=[pl.BlockSpec(riT.shape, lambda i, ps: (0, 0))],
        out_specs=pl.BlockSpec((TOPK_IN_GROUP, T), lambda i, ps: (0, 0)),
    )
    return pl.pallas_call(
        functools.partial(_slots_kernel, n_slab=n_slab),
        grid_spec=grid_spec,
        out_shape=jax.ShapeDtypeStruct((TOPK_IN_GROUP, T), jnp.int32),
        compiler_params=_cparams("arbitrary"),
        name="slots",
    )(pad_start, riT)


def _dispatch_kernel(d0_ref, d1_ref, tail_ref, h_ref, xs_ref, zero_ref, sem, zsem, *, n_slab):
    tm = h_ref.shape[0] // n_slab
    base = pl.program_id(0) * tm
    blk_rows = zero_ref.shape[0]

    @pl.when(pl.program_id(0) == 0)
    def _():
        zero_ref[...] = jnp.zeros_like(zero_ref)

        def tail_copy(e):
            start = pl.multiple_of(tail_ref[e] * blk_rows, blk_rows)
            return pltpu.make_async_copy(zero_ref, xs_ref.at[pl.ds(start, blk_rows), :], zsem)

        for e in range(tail_ref.shape[0]):
            @pl.when(tail_ref[e] >= 0)
            def _():
                tail_copy(e).start()
        for e in range(tail_ref.shape[0]):
            @pl.when(tail_ref[e] >= 0)
            def _():
                tail_copy(e).wait()

    def body(g, carry):
        r8 = pl.multiple_of(g * DMA_UNROLL, DMA_UNROLL)
        for u in range(DMA_UNROLL):
            src = h_ref.at[pl.ds((r8 + u) * n_slab, n_slab), :]
            pltpu.make_async_copy(src, xs_ref.at[pl.ds(d0_ref[base + r8 + u], n_slab), :], sem).start(priority=0)
            pltpu.make_async_copy(src, xs_ref.at[pl.ds(d1_ref[base + r8 + u], n_slab), :], sem).start(priority=1)
        return carry

    lax.fori_loop(0, tm // DMA_UNROLL, body, 0)
    for _ in range(TOPK_IN_GROUP):
        pltpu.make_async_copy(h_ref, xs_ref.at[pl.ds(0, tm * n_slab), :], sem).wait()


def _dispatch(dest0, dest1, tail_blk, h2p, n_pad, n_slab):
    tm = DISPATCH_TILE
    T = h2p.shape[0] // n_slab
    grid_spec = pltpu.PrefetchScalarGridSpec(
        num_scalar_prefetch=3,
        grid=(T // tm,),
        in_specs=[pl.BlockSpec((tm * n_slab, LANES), lambda i, d0, d1, tb: (i, 0))],
        out_specs=pl.BlockSpec(memory_space=pl.ANY),
        scratch_shapes=[pltpu.VMEM((EXPERT_ROWS * n_slab, LANES), h2p.dtype),
                        pltpu.SemaphoreType.DMA(()),
                        pltpu.SemaphoreType.DMA(())],
    )
    return pl.pallas_call(
        functools.partial(_dispatch_kernel, n_slab=n_slab),
        grid_spec=grid_spec,
        out_shape=jax.ShapeDtypeStruct((n_pad * n_slab, LANES), h2p.dtype),
        compiler_params=_cparams("arbitrary"),
        name="dispatch",
    )(dest0, dest1, tail_blk, h2p)


def _expert_kernel(be_ref, nv_ref, nxt_ref, slot_ref, xs_ref, w1f_ref, w3f_ref, w2f_ref, ys_ref,
                   w1_ref, w3_ref, w2_ref, f1_ref, f3_ref, f2_ref, sem):
    i = pl.program_id(0)
    wf_hbm = (w1f_ref, w3f_ref, w2f_ref)
    wf_vmem = (f1_ref, f3_ref, f2_ref)
    w_bf16 = (w1_ref, w3_ref, w2_ref)

    def fetch(e, s):
        return [pltpu.make_async_copy(wf_hbm[m].at[e], wf_vmem[m].at[s], sem.at[s, m]) for m in range(3)]

    @pl.when(i == 0)
    def _():
        for cp in fetch(be_ref[0], slot_ref[0]):
            cp.start()

    @pl.when((i == 0) | (be_ref[i] != be_ref[jnp.maximum(i - 1, 0)]))
    def _():
        s = slot_ref[i]
        for m, cp in enumerate(fetch(be_ref[i], s)):
            cp.wait()
            w_bf16[m][...] = wf_vmem[m][s].astype(BF16)

        @pl.when(nxt_ref[i] >= 0)
        def _():
            for cp in fetch(nxt_ref[i], 1 - s):
                cp.start()

    @pl.when(i >= nv_ref[0])
    def _():
        ys_ref[...] = jnp.zeros_like(ys_ref)

    @pl.when(i < nv_ref[0])
    def _():
        half_d = w1_ref.shape[0] // 2
        xp = _load_slabs(xs_ref, 0, EXPERT_ROWS, _slabs(half_d))
        x_lo = pltpu.bitcast(xp << 16, F32).astype(BF16)
        x_hi = pltpu.bitcast(xp & jnp.uint32(0xFFFF0000), F32).astype(BF16)
        x = jnp.concatenate([x_lo, x_hi], axis=1)
        a = jnp.dot(x, w1_ref[...], preferred_element_type=F32)
        b = jnp.dot(x, w3_ref[...], preferred_element_type=F32)
        hid = (a * jax.nn.sigmoid(a) * b).astype(BF16)
        y = jnp.dot(hid, w2_ref[...], preferred_element_type=F32)
        bits = pltpu.bitcast(y.astype(BF16).astype(F32), jnp.uint32)
        _store_slabs(ys_ref, 0, (bits[:, :half_d] >> 16) | bits[:, half_d:])


def _experts(blk_expert, n_valid, xs, w1f, w3f, w2f):
    _, D, F = w1f.shape
    rb = EXPERT_ROWS
    blk_rows = rb * _slabs(D // 2)
    n_steps = xs.shape[0] // blk_rows
    first = jnp.concatenate([jnp.ones((1,), jnp.bool_), blk_expert[1:] != blk_expert[:-1]])
    slot = ((jnp.cumsum(first.astype(jnp.int32)) - 1) % 2).astype(jnp.int32)
    later = jnp.where(blk_expert[None, :] > blk_expert[:, None], blk_expert[None, :], N_EXPERTS)
    nxt = jnp.min(later, axis=1)
    nxt = jnp.where(nxt < N_EXPERTS, nxt, -1).astype(jnp.int32)
    row_blk = lambda i, be, nv, nx, sl: (jnp.minimum(i, nv[0] - 1), 0)
    out_blk = lambda i, be, nv, nx, sl: (i, 0)
    grid_spec = pltpu.PrefetchScalarGridSpec(
        num_scalar_prefetch=4,
        grid=(n_steps,),
        in_specs=[pl.BlockSpec((blk_rows, LANES), row_blk),
                  pl.BlockSpec(memory_space=pl.ANY),
                  pl.BlockSpec(memory_space=pl.ANY),
                  pl.BlockSpec(memory_space=pl.ANY)],
        out_specs=pl.BlockSpec((blk_rows, LANES), out_blk),
        scratch_shapes=[pltpu.VMEM((D, F), BF16), pltpu.VMEM((D, F), BF16), pltpu.VMEM((F, D), BF16),
                        pltpu.VMEM((2, D, F), F32), pltpu.VMEM((2, D, F), F32), pltpu.VMEM((2, F, D), F32),
                        pltpu.SemaphoreType.DMA((2, 3))],
    )
    return pl.pallas_call(
        _expert_kernel,
        grid_spec=grid_spec,
        out_shape=jax.ShapeDtypeStruct(xs.shape, jnp.uint32),
        compiler_params=_cparams("arbitrary"),
        name="experts",
    )(blk_expert, n_valid, nxt, slot, xs, w1f, w3f, w2f)


def _combine_kernel(d0_ref, d1_ref, x1_ref, rf_ref, ga2_ref, ys_ref, o_ref, buf_ref, sem, *, n_slab):
    i = pl.program_id(0)
    n = pl.num_programs(0)
    tm = x1_ref.shape[0]

    def gather(step, slot):
        base = step * tm

        def body(g, carry):
            r8 = pl.multiple_of(g * DMA_UNROLL, DMA_UNROLL)
            for u in range(DMA_UNROLL):
                dst = pl.ds((r8 + u) * n_slab, n_slab)
                pltpu.make_async_copy(ys_ref.at[pl.ds(d0_ref[base + r8 + u], n_slab), :],
                                      buf_ref.at[slot, 0, dst, :], sem.at[slot]).start(priority=0)
                pltpu.make_async_copy(ys_ref.at[pl.ds(d1_ref[base + r8 + u], n_slab), :],
                                      buf_ref.at[slot, 1, dst, :], sem.at[slot]).start(priority=1)
            return carry

        lax.fori_loop(0, tm // DMA_UNROLL, body, 0)

    @pl.when(i == 0)
    def _():
        gather(0, 0)

    @pl.when(i + 1 < n)
    def _():
        gather(i + 1, (i + 1) % 2)

    slot = i % 2
    for k in range(TOPK_IN_GROUP):
        pltpu.make_async_copy(ys_ref.at[pl.ds(0, tm * n_slab), :], buf_ref.at[slot, k], sem.at[slot]).wait()
    rf = rf_ref[...]
    w = [rf[:, k:k + 1] for k in range(TOPK_IN_GROUP)]
    yp = [_load_slabs(buf_ref, 0, tm, n_slab, lead=(slot, k)) for k in range(TOPK_IN_GROUP)]
    half_d = n_slab * LANES
    y_lo = sum(w[k] * pltpu.bitcast(yp[k] << 16, F32) for k in range(TOPK_IN_GROUP))
    y_hi = sum(w[k] * pltpu.bitcast(yp[k] & jnp.uint32(0xFFFF0000), F32) for k in range(TOPK_IN_GROUP))
    ga2 = ga2_ref[0]
    o_ref[:, :half_d] = x1_ref[:, :half_d] + ga2[:, :half_d] * y_lo
    o_ref[:, half_d:] = x1_ref[:, half_d:] + ga2[:, half_d:] * y_hi


def _combine(dest0, dest1, x1, rf, ga2, ys, tiles_per_seq):
    T, D = x1.shape
    n_slab = _slabs(D // 2)
    tm = COMBINE_TILE
    row = lambda i, d0, d1: (i, 0)
    grid_spec = pltpu.PrefetchScalarGridSpec(
        num_scalar_prefetch=2,
        grid=(T // tm,),
        in_specs=[pl.BlockSpec((tm, D), row),
                  pl.BlockSpec((tm, LANES), row),
                  pl.BlockSpec((1, 1, D), lambda i, d0, d1: (i // tiles_per_seq, 0, 0)),
                  pl.BlockSpec(memory_space=pl.ANY)],
        out_specs=pl.BlockSpec((tm, D), row),
        scratch_shapes=[pltpu.VMEM((2, TOPK_IN_GROUP, tm * n_slab, LANES), ys.dtype),
                        pltpu.SemaphoreType.DMA((2,))],
    )
    return pl.pallas_call(
        functools.partial(_combine_kernel, n_slab=n_slab),
        grid_spec=grid_spec,
        out_shape=jax.ShapeDtypeStruct((T, D), F32),
        compiler_params=_cparams("arbitrary"),
        name="combine",
    )(dest0, dest1, x1, rf, ga2, ys)


def _rope_tables(S):
    pos = jnp.arange(S, dtype=F32)
    inv_freq = ROPE_THETA ** (-jnp.arange(0, ROT_DIM, 2, dtype=F32) / ROT_DIM)
    ang = pos[:, None] * inv_freq[None, :]
    cos, sin = jnp.cos(ang), jnp.sin(ang)
    half = ROT_DIM // 2
    ones = jnp.ones((S, HEAD_DIM - ROT_DIM), F32)
    cos_h = jnp.concatenate([cos, cos, ones], axis=1)
    sin_h = jnp.concatenate([-sin, sin, 0.0 * ones], axis=1)
    return jnp.tile(cos_h, (1, LANES // HEAD_DIM)), jnp.tile(sin_h, (1, LANES // HEAD_DIM)), cos.T, sin.T


def kernel(x, c, w_ada, b_ada, g_norm1, g_norm2, w_in, g_q, g_k, conv_w, conv_b,
           w_pa, w_pb, w_o, w_rg, b_rg, w_re, b_re, w1, w3, w2):
    B, S, D = x.shape
    T = B * S
    assert S % POST_TILE == 0 and S % COMBINE_TILE == 0 and T % DISPATCH_TILE == 0 and S % QUERY_TILE == 0 and QUERY_TILE % (2 * MOBA_BLOCK) == 0
    assert S // MOBA_BLOCK <= LANES - HEAD_DIM
    l = 0

    mod = _ada(c, w_ada[l], b_ada[l])
    sh1, sc1, ga1, sh2, sc2, ga2 = [m.reshape(B, 1, D) for m in jnp.split(mod, N_MOD, axis=-1)]

    x2 = x.reshape(T, D)
    z2 = _inproj(x2, g_norm1[l].reshape(1, D), sc1, sh1, w_in[l].astype(BF16), S)

    cosf, sinf, cost, sint = _rope_tables(S)
    rep = LANES // HEAD_DIM
    gq_cols = jnp.broadcast_to(jnp.tile(g_q[l], rep).reshape(LANES, 1), (LANES, QUERY_TILE))
    ya = _attention(z2.reshape(B, S, -1), cosf, sinf, cost, sint, gq_cols, jnp.tile(g_k[l], rep).reshape(1, LANES))

    wr = jnp.zeros((D, LANES), F32).at[:, :N_GROUPS].set(w_rg[l]).at[:, N_GROUPS:N_GROUPS + N_EXPERTS].set(w_re[l])
    br = jnp.zeros((1, LANES), F32).at[0, :N_GROUPS].set(b_rg[l]).at[0, N_GROUPS:N_GROUPS + N_EXPERTS].set(b_re[l])
    wr_hi = wr.astype(BF16)
    wr2 = jnp.concatenate([wr_hi, (wr - wr_hi.astype(F32)).astype(BF16)], axis=1)
    tri = (lax.broadcasted_iota(jnp.int32, (POST_CHUNK, POST_CHUNK), 1)
           < lax.broadcasted_iota(jnp.int32, (POST_CHUNK, POST_CHUNK), 0)).astype(BF16)
    x1, h2, ri, rf, cnt = _post(x2, ya.reshape(T, ATTN_WIDTH), z2, ga1, sc2, sh2,
                                conv_w[l], conv_b[l].reshape(1, CONV_WIDTH),
                                w_pa[l].astype(BF16), w_pb[l].astype(BF16), w_o[l].astype(BF16),
                                g_norm2[l].reshape(1, D), wr2, br, tri, S // POST_TILE)

    rb = EXPERT_ROWS
    counts = cnt[0, :N_EXPERTS].astype(jnp.int32)
    padded = (counts + rb - 1) // rb * rb
    pad_end = jnp.cumsum(padded)
    pad_start = pad_end - padded
    dest = _slots(pad_start.astype(jnp.int32), ri, _slabs(D // 2))
    dest0, dest1 = dest[0], dest[1]
    n_blocks = -(-T * TOPK_IN_GROUP // rb) + N_EXPERTS
    n_pad = n_blocks * rb
    n_valid = (pad_end[-1] // rb).astype(jnp.int32)
    blk_start = jnp.minimum(jnp.arange(n_blocks, dtype=jnp.int32), n_valid - 1) * rb
    blk_expert = jnp.sum(pad_end[None, :] <= blk_start[:, None], axis=-1).astype(jnp.int32)
    unused = n_valid + jnp.arange(N_EXPERTS, dtype=jnp.int32)
    tail_blk = jnp.concatenate([jnp.where(padded > 0, pad_end // rb - 1, -1),
                                jnp.where(unused < n_blocks, unused, -1)]).astype(jnp.int32)

    xs = _dispatch(dest0, dest1, tail_blk, h2, n_pad, _slabs(D // 2))
    ys = _experts(blk_expert, n_valid.reshape(1), xs, w1[l], w3[l], w2[l])
    out = _combine(dest0, dest1, x1, rf, ga2, ys, S // COMBINE_TILE)
    return out.reshape(B, S, D)
```

```python
import functools

import jax
import jax.numpy as jnp
from jax import lax
from jax.experimental import pallas as pl
from jax.experimental.pallas import tpu as pltpu

F32 = jnp.float32
BF16 = jnp.bfloat16
HIGHEST = lax.Precision.HIGHEST

N_HEADS = 8
HEAD_DIM = 64
ATTN_WIDTH = N_HEADS * HEAD_DIM
CONV_WIDTH = 512
CONV_K = 3
MOBA_BLOCK = 256
MOBA_TOPK = 3
ROPE_THETA = 500000.0
ROT_DIM = HEAD_DIM // 4
N_GROUPS = 4
EXPERTS_PER_GROUP = 8
N_EXPERTS = N_GROUPS * EXPERTS_PER_GROUP
TOPK_IN_GROUP = 2
N_MOD = 6
EPS = 1e-6

LANES = 128
NEG = -1e30
POST_TILE = 1024
COMBINE_TILE = 256
INPROJ_TILE = 1024
DISPATCH_TILE = 4096
POST_CHUNK = 512
QUERY_TILE = 512
DMA_UNROLL = 8
EXPERT_ROWS = 512
VMEM_LIMIT = 56 * 1024 * 1024
ATTN_VMEM_LIMIT = 60 * 1024 * 1024


def _cparams(*sem):
    return pltpu.CompilerParams(dimension_semantics=sem, vmem_limit_bytes=VMEM_LIMIT)


def _ada_kernel(c_ref, w_ref, b_ref, o_ref):
    c = c_ref[...]
    a = c * jax.nn.sigmoid(c)
    o_ref[...] = jnp.dot(a, w_ref[...], preferred_element_type=F32, precision=HIGHEST) + b_ref[...]


def _ada(c, w_ada, b_ada):
    B, D = c.shape
    N = w_ada.shape[1]
    tn = 1536
    return pl.pallas_call(
        _ada_kernel,
        grid=(N // tn,),
        in_specs=[pl.BlockSpec((B, D), lambda j: (0, 0)),
                  pl.BlockSpec((D, tn), lambda j: (0, j)),
                  pl.BlockSpec((1, tn), lambda j: (0, j))],
        out_specs=pl.BlockSpec((B, tn), lambda j: (0, j)),
        out_shape=jax.ShapeDtypeStruct((B, N), F32),
        compiler_params=_cparams("arbitrary"),
        name="ada",
    )(c, w_ada, b_ada.reshape(1, N))


def _inproj_kernel(x_ref, g_ref, sc_ref, sh_ref, w_ref, z_ref, *, n_chunk):
    x = x_ref[...]
    ms = jnp.mean(x * x, axis=-1, keepdims=True)
    y = x * lax.rsqrt(ms + EPS) * g_ref[...]
    h = (y * (1.0 + sc_ref[0]) + sh_ref[0]).astype(BF16)
    for n in range(0, z_ref.shape[1], n_chunk):
        z_ref[:, n:n + n_chunk] = jnp.dot(h, w_ref[:, n:n + n_chunk],
                                          preferred_element_type=F32).astype(BF16)


def _inproj(x2, g1, sc1, sh1, w_in_bf, seq_len):
    T, D = x2.shape
    N = w_in_bf.shape[1]
    tm = INPROJ_TILE
    assert seq_len % tm == 0
    bmap = lambda i: (i // (seq_len // tm), 0, 0)
    return pl.pallas_call(
        functools.partial(_inproj_kernel, n_chunk=512),
        grid=(T // tm,),
        in_specs=[pl.BlockSpec((tm, D), lambda i: (i, 0)),
                  pl.BlockSpec((1, D), lambda i: (0, 0)),
                  pl.BlockSpec((1, 1, D), bmap),
                  pl.BlockSpec((1, 1, D), bmap),
                  pl.BlockSpec((D, N), lambda i: (0, 0), pipeline_mode=pl.Buffered(1))],
        out_specs=pl.BlockSpec((tm, N), lambda i: (i, 0)),
        out_shape=jax.ShapeDtypeStruct((T, N), BF16),
        compiler_params=_cparams("arbitrary"),
        name="inproj",
    )(x2, g1, sc1, sh1, w_in_bf)


def _fold_rows(x, op):
    parts = [x[r:r + 8] for r in range(0, x.shape[0], 8)]
    while len(parts) > 1:
        parts = [op(parts[i], parts[i + 1]) for i in range(0, len(parts) - 1, 2)] + (
            [parts[-1]] if len(parts) % 2 else [])
    return parts[0]


def _attn_kernel(q_ref, k_ref, v_ref, cos_ref, sin_ref, cost_ref, sint_ref, gq_ref, gk_ref, o_ref,
                 kaug_ref, vt_ref, kmp_ref, kst_ref, s_ref, mcol_ref, qa_ref):
    S = q_ref.shape[1]
    blk = MOBA_BLOCK
    qt = QUERY_TILE
    sub = qt // blk
    nb = S // blk
    nq = S // qt
    nbp = kst_ref.shape[0] // 6
    hd = HEAD_DIM

    half = ROT_DIM // 2
    lane_r = lax.broadcasted_iota(jnp.int32, (blk, LANES), 1)
    rot_lo = (lane_r & (hd - 1)) < half
    same_head = jnp.where((lax.broadcasted_iota(jnp.int32, (LANES, LANES), 0) < hd)
                          == (lax.broadcasted_iota(jnp.int32, (LANES, LANES), 1) < hd), 1.0, 0.0).astype(BF16)

    def norm_rope_keys(xb, r0):
        x = xb.astype(F32)
        sq = x * x
        sq_hi = sq.astype(BF16)
        sq_lo = (sq - sq_hi.astype(F32)).astype(BF16)
        ssq = (jnp.dot(sq_hi, same_head, preferred_element_type=F32)
               + jnp.dot(sq_lo, same_head, preferred_element_type=F32))
        y = x * lax.rsqrt(ssq * (1.0 / hd) + EPS) * gk_ref[...]
        rot = jnp.where(rot_lo, pltpu.roll(y, LANES - half, 1), pltpu.roll(y, half, 1))
        return y * cos_ref[pl.ds(r0, blk), :] + rot * sin_ref[pl.ds(r0, blk), :]

    def norm_rope_queries_t(xb, r0):
        xT = xb.astype(F32).T
        sq = xT * xT
        cos = cost_ref[:, pl.ds(r0, qt)]
        sin = sint_ref[:, pl.ds(r0, qt)]
        rows = []
        for h in range(2):
            lo, hi = h * hd, (h + 1) * hd
            ssq = jnp.sum(_fold_rows(sq[lo:hi], jnp.add), axis=0, keepdims=True)
            y = xT[lo:hi] * lax.rsqrt(ssq * (1.0 / hd) + EPS) * gq_ref[lo:hi, :]
            y1, y2 = y[0:half], y[half:2 * half]
            rows += [y1 * cos - y2 * sin, y2 * cos + y1 * sin, y[2 * half:]]
        return jnp.concatenate(rows, axis=0)

    kmp_ref[...] = jnp.zeros_like(kmp_ref)
    ones_row = jnp.where(lax.broadcasted_iota(jnp.int32, (16, blk), 0) == 0, 1.0, 0.0).astype(BF16)
    lane_k = lax.broadcasted_iota(jnp.int32, (blk, LANES), 1)
    head0_k = lane_k < HEAD_DIM
    lane_m = lax.broadcasted_iota(jnp.int32, (nbp, LANES), 1)

    def prepare_keys(t):
        for u in range(sub):
            j = t * sub + u
            r0 = pl.multiple_of(jnp.minimum(j, nb - 1) * blk, blk)
            kr = norm_rope_keys(k_ref[0, pl.ds(r0, blk), :], r0)
            kmp_ref[pl.ds(j, 1), :] = jnp.sum(kr, axis=0, keepdims=True) * (1.0 / blk)
            kaug_ref[0, j] = jnp.where(head0_k, kr, jnp.where(lane_k - hd == j, 1.0, 0.0)).astype(BF16)
            kaug_ref[1, j] = jnp.where(head0_k, jnp.where(lane_k == j, 1.0, 0.0), kr).astype(BF16)
            vT = v_ref[0, pl.ds(r0, blk), :].astype(F32).T
            for h in range(2):
                vt_ref[h, j, 0:HEAD_DIM, :] = vT[h * HEAD_DIM:(h + 1) * HEAD_DIM].astype(BF16)
                vt_ref[h, j, HEAD_DIM:HEAD_DIM + 16, :] = ones_row
        kmp = kmp_ref[0:nbp, :]
        k_hi = kmp.astype(BF16)
        k_lo = (kmp - k_hi.astype(F32)).astype(BF16)
        zero = jnp.zeros((nbp, LANES), BF16)
        parts = []
        for h in range(2):
            mine = (lane_m < hd) if h == 0 else (lane_m >= hd)
            parts += [jnp.where(mine, k_hi, zero), jnp.where(mine, k_lo, zero)]
        parts += [parts[0], parts[2]]
        for n, part in enumerate(parts):
            kst_ref[n * nbp:(n + 1) * nbp, :] = part

    prepare_keys(0)

    causal = (lax.broadcasted_iota(jnp.int32, (blk, blk), 0) <= lax.broadcasted_iota(jnp.int32, (blk, blk), 1))
    rowf = lax.broadcasted_iota(jnp.int32, (nbp, qt), 0).astype(F32)
    subf = (lax.broadcasted_iota(jnp.int32, (nbp, qt), 1) // blk).astype(F32)
    q_scale = (hd ** -0.5) * 1.4426950408889634

    def query_operands(t):
        r0 = pl.multiple_of(t * qt, qt)
        qT = norm_rope_queries_t(q_ref[0, pl.ds(r0, qt), :], r0)
        cur = lax.convert_element_type(t * sub, F32) + subf
        q_hi = qT.astype(BF16)
        q_lo = (qT - q_hi.astype(F32)).astype(BF16)
        g1 = jnp.dot(kst_ref[0:4 * nbp, :], q_hi, preferred_element_type=F32)
        g2 = jnp.dot(kst_ref[4 * nbp:6 * nbp, :], q_lo, preferred_element_type=F32)
        qa = []
        for h in range(2):
            gate = g1[2 * h * nbp:(2 * h + 1) * nbp] + g1[(2 * h + 1) * nbp:(2 * h + 2) * nbp] + g2[h * nbp:(h + 1) * nbp]
            g = jnp.where(rowf < cur, gate, -jnp.inf)
            keep = rowf == cur
            for r in range(MOBA_TOPK):
                m = jnp.max(g, axis=0, keepdims=True)
                idx = jnp.min(jnp.where(g == m, rowf, 1e9), axis=0, keepdims=True)
                pick = (rowf == idx) & (cur > r)
                keep = keep | pick
                g = jnp.where(pick, -jnp.inf, g)
            bias = jnp.where(keep, 0.0, NEG)
            qs = qT[h * hd:(h + 1) * hd] * q_scale
            pad = jnp.zeros((LANES - hd - nbp, qt), F32)
            pieces = [qs, bias, pad] if h == 0 else [bias, pad, qs]
            qa.append(jnp.concatenate(pieces, axis=0).astype(BF16))
        return qa

    def logits(h, j0, n, qa_h):
        return jnp.dot(kaug_ref[h, pl.ds(j0, n)].reshape(n * blk, LANES), qa_h, preferred_element_type=F32)

    def pass2_tile(par, h, j):
        pT = jnp.exp2(s_ref[par, h, j] - mcol_ref[h, 0:1, :]).astype(BF16)
        return jnp.dot(vt_ref[h, j], pT, preferred_element_type=F32)

    def pass1_own(t, par, qa):
        mx = []
        for h in range(2):
            f = None
            for u in range(sub):
                lo = u * blk
                su = jnp.dot(kaug_ref[h, t * sub + u], qa[h][:, lo:], preferred_element_type=F32)
                pieces = [jnp.where(causal, su[:, :blk], NEG)] + [su[:, blk:]] * (lo + blk < qt)
                su = jnp.concatenate(pieces, axis=1)
                s_ref[par, h, t * sub + u, :, lo:] = su
                fu = _fold_rows(su, jnp.maximum)
                f = fu if f is None else jnp.concatenate([f[:, :lo], jnp.maximum(f[:, lo:], fu)], axis=1)
            mx.append(f)
        return mx

    def pass2_own(t, par):
        out = []
        for h in range(2):
            acc = None
            for u in range(sub):
                lo = u * blk
                pT = jnp.exp2(s_ref[par, h, t * sub + u, :, lo:] - mcol_ref[h, 0:1, lo:]).astype(BF16)
                d = jnp.dot(vt_ref[h, t * sub + u], pT, preferred_element_type=F32)
                acc = d if acc is None else jnp.concatenate([acc[:, :lo], acc[:, lo:] + d], axis=1)
            out.append(acc)
        return out

    def pass1_pair(p, par, qa, mx):
        out = []
        for h in range(2):
            sT = logits(h, 2 * p, 2, qa[h])
            s_ref[par, h, pl.ds(2 * p, 2)] = sT.reshape(2, blk, qt)
            out.append(jnp.maximum(mx[h], _fold_rows(sT, jnp.maximum)))
        return out

    def pass2_pair(p, par, acc):
        return [acc[h] + sum(pass2_tile(par, h, 2 * p + u) for u in range(2)) for h in range(2)]

    def finish_pass1(mx):
        for h in range(2):
            mcol_ref[h] = jnp.broadcast_to(jnp.max(mx[h], axis=0, keepdims=True), mcol_ref.shape[1:])

    def finish_pass2(t, acc):
        outT = jnp.concatenate([acc[h][0:hd] / acc[h][hd:hd + 1] for h in range(2)], axis=0)
        o_ref[0, pl.ds(pl.multiple_of(t * qt, qt), qt), :] = outT.T.astype(BF16)

    def stage(t, par):

        def prepare_next():
            prepare_keys(t + 1)
            nxt = query_operands(jnp.minimum(t + 1, nq - 1))
            for h in range(2):
                qa_ref[1 - par, h] = nxt[h]

        @pl.when(t == 0)
        def _():
            finish_pass1(pass1_own(t, par, query_operands(t)))
            prepare_next()

        @pl.when((t > 0) & (t < nq))
        def _():
            qa = [qa_ref[par, h] for h in range(2)]
            mx = pass1_own(t, par, qa)
            acc = pass2_own(t - 1, 1 - par)
            prepare_next()
            n_prev = (t - 1) * sub // 2

            def both(p, c):
                mx, acc = c
                return tuple(pass1_pair(p, par, qa, mx)), tuple(pass2_pair(p, 1 - par, acc))

            def both_twice(p2, c):
                return both(2 * p2 + 1, both(2 * p2, c))

            c = lax.fori_loop(0, n_prev // 2, both_twice, (tuple(mx), tuple(acc)))
            mx, acc = lax.fori_loop(n_prev // 2 * 2, n_prev, both, c)
            for p in range(sub // 2):
                mx = pass1_pair(n_prev + p, par, qa, mx)
            finish_pass2(t - 1, acc)
            finish_pass1(mx)

        @pl.when(t == nq)
        def _():
            acc = tuple(pass2_own(t - 1, 1 - par))
            n_prev = (t - 1) * sub // 2
            acc = lax.fori_loop(0, n_prev // 2, lambda p2, a: tuple(pass2_pair(2 * p2 + 1, 1 - par, pass2_pair(2 * p2, 1 - par, a))), acc)
            acc = lax.fori_loop(n_prev // 2 * 2, n_prev, lambda p, a: tuple(pass2_pair(p, 1 - par, a)), acc)
            finish_pass2(t - 1, acc)

    def stage_pair(tt, carry):
        stage(2 * tt, 0)
        stage(2 * tt + 1, 1)
        return carry

    lax.fori_loop(0, (nq + 2) // 2, stage_pair, 0)


def _attention(z3, cosf, sinf, cost, sint, gq_cols, gk2):
    B, S, _ = z3.shape
    n_pair = N_HEADS // 2
    kq = ATTN_WIDTH // LANES
    nb = S // MOBA_BLOCK
    nbp = -(-nb // 16) * 16
    sub = QUERY_TILE // MOBA_BLOCK
    assert sub <= 8
    return pl.pallas_call(
        _attn_kernel,
        grid=(B, n_pair),
        in_specs=[pl.BlockSpec((1, S, LANES), lambda b, p: (b, 0, p)),
                  pl.BlockSpec((1, S, LANES), lambda b, p: (b, 0, kq + p)),
                  pl.BlockSpec((1, S, LANES), lambda b, p: (b, 0, 2 * kq + p)),
                  pl.BlockSpec((S, LANES), lambda b, p: (0, 0), pipeline_mode=pl.Buffered(1)),
                  pl.BlockSpec((S, LANES), lambda b, p: (0, 0), pipeline_mode=pl.Buffered(1)),
                  pl.BlockSpec(cost.shape, lambda b, p: (0, 0)),
                  pl.BlockSpec(sint.shape, lambda b, p: (0, 0)),
                  pl.BlockSpec((LANES, QUERY_TILE), lambda b, p: (0, 0)),
                  pl.BlockSpec((1, LANES), lambda b, p: (0, 0))],
        out_specs=pl.BlockSpec((1, S, LANES), lambda b, p: (b, 0, p)),
        out_shape=jax.ShapeDtypeStruct((B, S, ATTN_WIDTH), BF16),
        scratch_shapes=[pltpu.VMEM((2, nb + sub, MOBA_BLOCK, LANES), BF16),
                        pltpu.VMEM((2, nb + sub, HEAD_DIM + 16, MOBA_BLOCK), BF16),
                        pltpu.VMEM((nbp + 8, LANES), F32),
                        pltpu.VMEM((6 * nbp, LANES), BF16),
                        pltpu.VMEM((2, 2, nb, MOBA_BLOCK, QUERY_TILE), F32),
                        pltpu.VMEM((2, 8, QUERY_TILE), F32),
                        pltpu.VMEM((2, 2, LANES, QUERY_TILE), BF16)],
        compiler_params=pltpu.CompilerParams(dimension_semantics=("arbitrary", "arbitrary"),
                                             vmem_limit_bytes=ATTN_VMEM_LIMIT),
        name="attn",
    )(z3, z3, z3, cosf, sinf, cost, sint, gq_cols, gk2)


def _slabs(width):
    return width // LANES


def _load_slabs(ref, row0, rows, n_slab, lead=()):
    return jnp.concatenate([ref[lead + (pl.ds(row0 * n_slab + s, rows, stride=n_slab), slice(None))]
                            for s in range(n_slab)], axis=1)


def _store_slabs(ref, row0, val):
    rows, width = val.shape
    n_slab = _slabs(width)
    for s in range(n_slab):
        ref[pl.ds(row0 * n_slab + s, rows, stride=n_slab), :] = val[:, s * LANES:(s + 1) * LANES]


def _post_kernel(x_ref, ya_ref, xb_ref, bg_ref, cg_ref, gta_ref, gtb_ref, ga1_ref, sc2_ref, sh2_ref,
                 cw_ref, cb_ref, wpa_ref, wpb_ref, wo_ref, g2_ref, wr_ref, br_ref, tri_ref,
                 x1_ref, h2_ref, ri_ref, rf_ref, cnt_ref, ubuf_ref, run_ref, *, tiles_per_seq):
    i = pl.program_id(0)
    tm = x_ref.shape[0]
    rc = POST_CHUNK
    halo = 8

    @pl.when(i == 0)
    def _():
        run_ref[...] = jnp.zeros_like(run_ref)

    @pl.when(i % tiles_per_seq == 0)
    def _():
        ubuf_ref[0:halo, :] = jnp.zeros((halo, CONV_WIDTH), F32)

    ubuf_ref[halo:halo + tm, :] = cg_ref[...].astype(F32) * xb_ref[...].astype(F32)
    cw = cw_ref[...]
    lanef = lax.broadcasted_iota(jnp.int32, (rc, LANES), 1).astype(F32)
    half_d = x_ref.shape[1] // 2
    run = run_ref[0:1, :]

    for c in range(tm // rc):
        rows = pl.ds(c * rc, rc)
        conv = (cw[0:1, :] * ubuf_ref[pl.ds(halo - 2 + c * rc, rc), :]
                + cw[1:2, :] * ubuf_ref[pl.ds(halo - 1 + c * rc, rc), :]
                + cw[2:3, :] * ubuf_ref[pl.ds(halo + c * rc, rc), :])
        y_b = (bg_ref[rows, :].astype(F32) * (conv + cb_ref[...])).astype(BF16)
        pa = jnp.dot(ya_ref[rows, :], wpa_ref[...], preferred_element_type=F32)
        pb = jnp.dot(y_b, wpb_ref[...], preferred_element_type=F32)
        merged = (jax.nn.sigmoid(gta_ref[rows, :].astype(F32)) * pa
                  + jax.nn.sigmoid(gtb_ref[rows, :].astype(F32)) * pb).astype(BF16)
        x1 = x_ref[rows, :] + ga1_ref[0] * jnp.dot(merged, wo_ref[...], preferred_element_type=F32)
        x1_ref[rows, :] = x1

        ms = jnp.mean(x1 * x1, axis=-1, keepdims=True)
        h2 = x1 * lax.rsqrt(ms + EPS) * g2_ref[...]
        h2 = h2 * (1.0 + sc2_ref[0]) + sh2_ref[0]
        h_hi = h2.astype(BF16)
        h_hi32 = h_hi.astype(F32)
        bits = pltpu.bitcast(h_hi32, jnp.uint32)
        _store_slabs(h2_ref, c * rc, (bits[:, :half_d] >> 16) | bits[:, half_d:])

        h_lo = (h2 - h_hi32).astype(BF16)
        r = jnp.dot(h_hi, wr_ref[...], preferred_element_type=F32)
        logit = (r[:, :LANES] + r[:, LANES:]
                 + jnp.dot(h_lo, wr_ref[:, :LANES], preferred_element_type=F32) + br_ref[...])
        gl = jnp.where(lanef < N_GROUPS, logit, -jnp.inf)
        gmax = jnp.max(gl, axis=-1, keepdims=True)
        g_idx = jnp.min(jnp.where(gl == gmax, lanef, 1e9), axis=-1, keepdims=True)
        g_w = 1.0 / jnp.sum(jnp.exp(gl - gmax), axis=-1, keepdims=True)
        e_lo = N_GROUPS + EXPERTS_PER_GROUP * g_idx
        el = jnp.where((lanef >= e_lo) & (lanef < e_lo + EXPERTS_PER_GROUP), logit, -jnp.inf)
        v0 = jnp.max(el, axis=-1, keepdims=True)
        i0 = jnp.min(jnp.where(el == v0, lanef, 1e9), axis=-1, keepdims=True)
        el = jnp.where(lanef == i0, -jnp.inf, el)
        v1 = jnp.max(el, axis=-1, keepdims=True)
        i1 = jnp.min(jnp.where(el == v1, lanef, 1e9), axis=-1, keepdims=True)
        t = jnp.exp(v1 - v0)
        w0 = g_w / (1.0 + t)
        w1 = g_w * t / (1.0 + t)
        e0 = i0 - N_GROUPS
        e1 = i1 - N_GROUPS

        oh0 = lanef == e0
        oh1 = lanef == e1
        oh = jnp.where(oh0 | oh1, 1.0, 0.0)
        before = jnp.dot(tri_ref[...], oh.astype(BF16), preferred_element_type=F32) + run
        r0 = jnp.sum(jnp.where(oh0, before, 0.0), axis=-1, keepdims=True)
        r1 = jnp.sum(jnp.where(oh1, before, 0.0), axis=-1, keepdims=True)
        run = run + jnp.sum(oh, axis=0, keepdims=True)

        ri = jnp.where(lanef == 0, e0, jnp.where(lanef == 1, e1, jnp.where(lanef == 2, r0, jnp.where(lanef == 3, r1, 0.0))))
        ri_ref[:, rows] = ri.astype(jnp.int32).T[0:8]
        rf_ref[rows, :] = jnp.where(lanef == 0, w0, jnp.where(lanef == 1, w1, 0.0))

    ubuf_ref[0:halo, :] = ubuf_ref[tm:tm + halo, :]
    run_ref[...] = jnp.broadcast_to(run, run_ref.shape)
    cnt_ref[...] = jnp.broadcast_to(run, cnt_ref.shape)


def _post(x2, ya2, z2, ga1, sc2, sh2, conv_w, conv_b, wpa, wpb, wo, g2, wr, br, tri, tiles_per_seq):
    T, D = x2.shape
    tm = POST_TILE
    cw = CONV_WIDTH
    xcol = 3 * ATTN_WIDTH // cw
    gcol = (3 * ATTN_WIDTH + 3 * cw) // D
    bmap = lambda i: (i // tiles_per_seq, 0, 0)
    const = lambda i: (0, 0)
    return pl.pallas_call(
        functools.partial(_post_kernel, tiles_per_seq=tiles_per_seq),
        grid=(T // tm,),
        in_specs=[pl.BlockSpec((tm, D), lambda i: (i, 0)),
                  pl.BlockSpec((tm, ATTN_WIDTH), lambda i: (i, 0)),
                  pl.BlockSpec((tm, cw), lambda i: (i, xcol)),
                  pl.BlockSpec((tm, cw), lambda i: (i, xcol + 1)),
                  pl.BlockSpec((tm, cw), lambda i: (i, xcol + 2)),
                  pl.BlockSpec((tm, D), lambda i: (i, gcol)),
                  pl.BlockSpec((tm, D), lambda i: (i, gcol + 1)),
                  pl.BlockSpec((1, 1, D), bmap),
                  pl.BlockSpec((1, 1, D), bmap),
                  pl.BlockSpec((1, 1, D), bmap),
                  pl.BlockSpec((CONV_K, cw), const),
                  pl.BlockSpec((1, cw), const),
                  pl.BlockSpec((ATTN_WIDTH, D), const),
                  pl.BlockSpec((cw, D), const),
                  pl.BlockSpec((D, D), const),
                  pl.BlockSpec((1, D), const),
                  pl.BlockSpec((D, 2 * LANES), const),
                  pl.BlockSpec((1, LANES), const),
                  pl.BlockSpec((POST_CHUNK, POST_CHUNK), const)],
        out_specs=[pl.BlockSpec((tm, D), lambda i: (i, 0)),
                   pl.BlockSpec((tm * _slabs(D // 2), LANES), lambda i: (i, 0)),
                   pl.BlockSpec((8, tm), lambda i: (0, i)),
                   pl.BlockSpec((tm, LANES), lambda i: (i, 0)),
                   pl.BlockSpec((8, LANES), const)],
        out_shape=[jax.ShapeDtypeStruct((T, D), F32),
                   jax.ShapeDtypeStruct((T * _slabs(D // 2), LANES), jnp.uint32),
                   jax.ShapeDtypeStruct((8, T), jnp.int32),
                   jax.ShapeDtypeStruct((T, LANES), F32),
                   jax.ShapeDtypeStruct((8, LANES), F32)],
        scratch_shapes=[pltpu.VMEM((tm + 16, cw), F32),
                        pltpu.VMEM((8, LANES), F32)],
        compiler_params=_cparams("arbitrary"),
        name="post",
    )(x2, ya2, z2, z2, z2, z2, z2, ga1, sc2, sh2, conv_w, conv_b, wpa, wpb, wo, g2, wr, br, tri)


def _slots_kernel(ps_ref, ri_ref, d_ref, *, n_slab):
    e = ri_ref[0:TOPK_IN_GROUP, :]
    start = jnp.zeros(e.shape, jnp.int32)
    for k in range(N_EXPERTS):
        start = jnp.where(e == k, ps_ref[k], start)
    d_ref[...] = (start + ri_ref[TOPK_IN_GROUP:2 * TOPK_IN_GROUP, :]) * n_slab


def _slots(pad_start, riT, n_slab):
    T = riT.shape[1]
    grid_spec = pltpu.PrefetchScalarGridSpec(
        num_scalar_prefetch=1,
        grid=(1,),
        in_specs=[pl.BlockSpec(riT.shape, lambda i, ps: (0, 0))],
        out_specs=pl.BlockSpec((TOPK_IN_GROUP, T), lambda i, ps: (0, 0)),
    )
    return pl.pallas_call(
        functools.partial(_slots_kernel, n_slab=n_slab),
        grid_spec=grid_spec,
        out_shape=jax.ShapeDtypeStruct((TOPK_IN_GROUP, T), jnp.int32),
        compiler_params=_cparams("arbitrary"),
        name="slots",
    )(pad_start, riT)


def _dispatch_kernel(d0_ref, d1_ref, tail_ref, h_ref, xs_ref, zero_ref, sem, zsem, *, n_slab):
    tm = h_ref.shape[0] // n_slab
    base = pl.program_id(0) * tm
    blk_rows = zero_ref.shape[0]

    @pl.when(pl.program_id(0) == 0)
    def _():
        zero_ref[...] = jnp.zeros_like(zero_ref)

        def tail_copy(e):
            start = pl.multiple_of(tail_ref[e] * blk_rows, blk_rows)
            return pltpu.make_async_copy(zero_ref, xs_ref.at[pl.ds(start, blk_rows), :], zsem)

        for e in range(tail_ref.shape[0]):
            @pl.when(tail_ref[e] >= 0)
            def _():
                tail_copy(e).start()
        for e in range(tail_ref.shape[0]):
            @pl.when(tail_ref[e] >= 0)
            def _():
                tail_copy(e).wait()

    def body(g, carry):
        r8 = pl.multiple_of(g * DMA_UNROLL, DMA_UNROLL)
        for u in range(DMA_UNROLL):
            src = h_ref.at[pl.ds((r8 + u) * n_slab, n_slab), :]
            pltpu.make_async_copy(src, xs_ref.at[pl.ds(d0_ref[base + r8 + u], n_slab), :], sem).start(priority=0)
            pltpu.make_async_copy(src, xs_ref.at[pl.ds(d1_ref[base + r8 + u], n_slab), :], sem).start(priority=1)
        return carry

    lax.fori_loop(0, tm // DMA_UNROLL, body, 0)
    for _ in range(TOPK_IN_GROUP):
        pltpu.make_async_copy(h_ref, xs_ref.at[pl.ds(0, tm * n_slab), :], sem).wait()


def _dispatch(dest0, dest1, tail_blk, h2p, n_pad, n_slab):
    tm = DISPATCH_TILE
    T = h2p.shape[0] // n_slab
    grid_spec = pltpu.PrefetchScalarGridSpec(
        num_scalar_prefetch=3,
        grid=(T // tm,),
        in_specs=[pl.BlockSpec((tm * n_slab, LANES), lambda i, d0, d1, tb: (i, 0))],
        out_specs=pl.BlockSpec(memory_space=pl.ANY),
        scratch_shapes=[pltpu.VMEM((EXPERT_ROWS * n_slab, LANES), h2p.dtype),
                        pltpu.SemaphoreType.DMA(()),
                        pltpu.SemaphoreType.DMA(())],
    )
    return pl.pallas_call(
        functools.partial(_dispatch_kernel, n_slab=n_slab),
        grid_spec=grid_spec,
        out_shape=jax.ShapeDtypeStruct((n_pad * n_slab, LANES), h2p.dtype),
        compiler_params=_cparams("arbitrary"),
        name="dispatch",
    )(dest0, dest1, tail_blk, h2p)


def _expert_kernel(be_ref, nv_ref, nxt_ref, slot_ref, xs_ref, w1f_ref, w3f_ref, w2f_ref, ys_ref,
                   w1_ref, w3_ref, w2_ref, f1_ref, f3_ref, f2_ref, sem):
    i = pl.program_id(0)
    wf_hbm = (w1f_ref, w3f_ref, w2f_ref)
    wf_vmem = (f1_ref, f3_ref, f2_ref)
    w_bf16 = (w1_ref, w3_ref, w2_ref)

    def fetch(e, s):
        return [pltpu.make_async_copy(wf_hbm[m].at[e], wf_vmem[m].at[s], sem.at[s, m]) for m in range(3)]

    @pl.when(i == 0)
    def _():
        for cp in fetch(be_ref[0], slot_ref[0]):
            cp.start()

    @pl.when((i == 0) | (be_ref[i] != be_ref[jnp.maximum(i - 1, 0)]))
    def _():
        s = slot_ref[i]
        for m, cp in enumerate(fetch(be_ref[i], s)):
            cp.wait()
            w_bf16[m][...] = wf_vmem[m][s].astype(BF16)

        @pl.when(nxt_ref[i] >= 0)
        def _():
            for cp in fetch(nxt_ref[i], 1 - s):
                cp.start()

    @pl.when(i >= nv_ref[0])
    def _():
        ys_ref[...] = jnp.zeros_like(ys_ref)

    @pl.when(i < nv_ref[0])
    def _():
        half_d = w1_ref.shape[0] // 2
        xp = _load_slabs(xs_ref, 0, EXPERT_ROWS, _slabs(half_d))
        x_lo = pltpu.bitcast(xp << 16, F32).astype(BF16)
        x_hi = pltpu.bitcast(xp & jnp.uint32(0xFFFF0000), F32).astype(BF16)
        x = jnp.concatenate([x_lo, x_hi], axis=1)
        a = jnp.dot(x, w1_ref[...], preferred_element_type=F32)
        b = jnp.dot(x, w3_ref[...], preferred_element_type=F32)
        hid = (a * jax.nn.sigmoid(a) * b).astype(BF16)
        y = jnp.dot(hid, w2_ref[...], preferred_element_type=F32)
        bits = pltpu.bitcast(y.astype(BF16).astype(F32), jnp.uint32)
        _store_slabs(ys_ref, 0, (bits[:, :half_d] >> 16) | bits[:, half_d:])


def _experts(blk_expert, n_valid, xs, w1f, w3f, w2f):
    _, D, F = w1f.shape
    rb = EXPERT_ROWS
    blk_rows = rb * _slabs(D // 2)
    n_steps = xs.shape[0] // blk_rows
    first = jnp.concatenate([jnp.ones((1,), jnp.bool_), blk_expert[1:] != blk_expert[:-1]])
    slot = ((jnp.cumsum(first.astype(jnp.int32)) - 1) % 2).astype(jnp.int32)
    later = jnp.where(blk_expert[None, :] > blk_expert[:, None], blk_expert[None, :], N_EXPERTS)
    nxt = jnp.min(later, axis=1)
    nxt = jnp.where(nxt < N_EXPERTS, nxt, -1).astype(jnp.int32)
    row_blk = lambda i, be, nv, nx, sl: (jnp.minimum(i, nv[0] - 1), 0)
    out_blk = lambda i, be, nv, nx, sl: (i, 0)
    grid_spec = pltpu.PrefetchScalarGridSpec(
        num_scalar_prefetch=4,
        grid=(n_steps,),
        in_specs=[pl.BlockSpec((blk_rows, LANES), row_blk),
                  pl.BlockSpec(memory_space=pl.ANY),
                  pl.BlockSpec(memory_space=pl.ANY),
                  pl.BlockSpec(memory_space=pl.ANY)],
        out_specs=pl.BlockSpec((blk_rows, LANES), out_blk),
        scratch_shapes=[pltpu.VMEM((D, F), BF16), pltpu.VMEM((D, F), BF16), pltpu.VMEM((F, D), BF16),
                        pltpu.VMEM((2, D, F), F32), pltpu.VMEM((2, D, F), F32), pltpu.VMEM((2, F, D), F32),
                        pltpu.SemaphoreType.DMA((2, 3))],
    )
    return pl.pallas_call(
        _expert_kernel,
        grid_spec=grid_spec,
        out_shape=jax.ShapeDtypeStruct(xs.shape, jnp.uint32),
        compiler_params=_cparams("arbitrary"),
        name="experts",
    )(blk_expert, n_valid, nxt, slot, xs, w1f, w3f, w2f)


def _combine_kernel(d0_ref, d1_ref, x1_ref, rf_ref, ga2_ref, ys_ref, o_ref, buf_ref, sem, *, n_slab):
    i = pl.program_id(0)
    n = pl.num_programs(0)
    tm = x1_ref.shape[0]

    def gather(step, slot):
        base = step * tm

        def body(g, carry):
            r8 = pl.multiple_of(g * DMA_UNROLL, DMA_UNROLL)
            for u in range(DMA_UNROLL):
                dst = pl.ds((r8 + u) * n_slab, n_slab)
                pltpu.make_async_copy(ys_ref.at[pl.ds(d0_ref[base + r8 + u], n_slab), :],
                                      buf_ref.at[slot, 0, dst, :], sem.at[slot]).start(priority=0)
                pltpu.make_async_copy(ys_ref.at[pl.ds(d1_ref[base + r8 + u], n_slab), :],
                                      buf_ref.at[slot, 1, dst, :], sem.at[slot]).start(priority=1)
            return carry

        lax.fori_loop(0, tm // DMA_UNROLL, body, 0)

    @pl.when(i == 0)
    def _():
        gather(0, 0)

    @pl.when(i + 1 < n)
    def _():
        gather(i + 1, (i + 1) % 2)

    slot = i % 2
    for k in range(TOPK_IN_GROUP):
        pltpu.make_async_copy(ys_ref.at[pl.ds(0, tm * n_slab), :], buf_ref.at[slot, k], sem.at[slot]).wait()
    rf = rf_ref[...]
    w = [rf[:, k:k + 1] for k in range(TOPK_IN_GROUP)]
    yp = [_load_slabs(buf_ref, 0, tm, n_slab, lead=(slot, k)) for k in range(TOPK_IN_GROUP)]
    half_d = n_slab * LANES
    y_lo = sum(w[k] * pltpu.bitcast(yp[k] << 16, F32) for k in range(TOPK_IN_GROUP))
    y_hi = sum(w[k] * pltpu.bitcast(yp[k] & jnp.uint32(0xFFFF0000), F32) for k in range(TOPK_IN_GROUP))
    ga2 = ga2_ref[0]
    o_ref[:, :half_d] = x1_ref[:, :half_d] + ga2[:, :half_d] * y_lo
    o_ref[:, half_d:] = x1_ref[:, half_d:] + ga2[:, half_d:] * y_hi


def _combine(dest0, dest1, x1, rf, ga2, ys, tiles_per_seq):
    T, D = x1.shape
    n_slab = _slabs(D // 2)
    tm = COMBINE_TILE
    row = lambda i, d0, d1: (i, 0)
    grid_spec = pltpu.PrefetchScalarGridSpec(
        num_scalar_prefetch=2,
        grid=(T // tm,),
        in_specs=[pl.BlockSpec((tm, D), row),
                  pl.BlockSpec((tm, LANES), row),
                  pl.BlockSpec((1, 1, D), lambda i, d0, d1: (i // tiles_per_seq, 0, 0)),
                  pl.BlockSpec(memory_space=pl.ANY)],
        out_specs=pl.BlockSpec((tm, D), row),
        scratch_shapes=[pltpu.VMEM((2, TOPK_IN_GROUP, tm * n_slab, LANES), ys.dtype),
                        pltpu.SemaphoreType.DMA((2,))],
    )
    return pl.pallas_call(
        functools.partial(_combine_kernel, n_slab=n_slab),
        grid_spec=grid_spec,
        out_shape=jax.ShapeDtypeStruct((T, D), F32),
        compiler_params=_cparams("arbitrary"),
        name="combine",
    )(dest0, dest1, x1, rf, ga2, ys)


def _rope_tables(S):
    pos = jnp.arange(S, dtype=F32)
    inv_freq = ROPE_THETA ** (-jnp.arange(0, ROT_DIM, 2, dtype=F32) / ROT_DIM)
    ang = pos[:, None] * inv_freq[None, :]
    cos, sin = jnp.cos(ang), jnp.sin(ang)
    half = ROT_DIM // 2
    ones = jnp.ones((S, HEAD_DIM - ROT_DIM), F32)
    cos_h = jnp.concatenate([cos, cos, ones], axis=1)
    sin_h = jnp.concatenate([-sin, sin, 0.0 * ones], axis=1)
    return jnp.tile(cos_h, (1, LANES // HEAD_DIM)), jnp.tile(sin_h, (1, LANES // HEAD_DIM)), cos.T, sin.T


def kernel(x, c, w_ada, b_ada, g_norm1, g_norm2, w_in, g_q, g_k, conv_w, conv_b,
           w_pa, w_pb, w_o, w_rg, b_rg, w_re, b_re, w1, w3, w2):
    B, S, D = x.shape
    T = B * S
    assert S % POST_TILE == 0 and S % COMBINE_TILE == 0 and T % DISPATCH_TILE == 0 and S % QUERY_TILE == 0 and QUERY_TILE % (2 * MOBA_BLOCK) == 0
    assert S // MOBA_BLOCK <= LANES - HEAD_DIM
    l = 0

    mod = _ada(c, w_ada[l], b_ada[l])
    sh1, sc1, ga1, sh2, sc2, ga2 = [m.reshape(B, 1, D) for m in jnp.split(mod, N_MOD, axis=-1)]

    x2 = x.reshape(T, D)
    z2 = _inproj(x2, g_norm1[l].reshape(1, D), sc1, sh1, w_in[l].astype(BF16), S)

    cosf, sinf, cost, sint = _rope_tables(S)
    rep = LANES // HEAD_DIM
    gq_cols = jnp.broadcast_to(jnp.tile(g_q[l], rep).reshape(LANES, 1), (LANES, QUERY_TILE))
    ya = _attention(z2.reshape(B, S, -1), cosf, sinf, cost, sint, gq_cols, jnp.tile(g_k[l], rep).reshape(1, LANES))

    wr = jnp.zeros((D, LANES), F32).at[:, :N_GROUPS].set(w_rg[l]).at[:, N_GROUPS:N_GROUPS + N_EXPERTS].set(w_re[l])
    br = jnp.zeros((1, LANES), F32).at[0, :N_GROUPS].set(b_rg[l]).at[0, N_GROUPS:N_GROUPS + N_EXPERTS].set(b_re[l])
    wr_hi = wr.astype(BF16)
    wr2 = jnp.concatenate([wr_hi, (wr - wr_hi.astype(F32)).astype(BF16)], axis=1)
    tri = (lax.broadcasted_iota(jnp.int32, (POST_CHUNK, POST_CHUNK), 1)
           < lax.broadcasted_iota(jnp.int32, (POST_CHUNK, POST_CHUNK), 0)).astype(BF16)
    x1, h2, ri, rf, cnt = _post(x2, ya.reshape(T, ATTN_WIDTH), z2, ga1, sc2, sh2,
                                conv_w[l], conv_b[l].reshape(1, CONV_WIDTH),
                                w_pa[l].astype(BF16), w_pb[l].astype(BF16), w_o[l].astype(BF16),
                                g_norm2[l].reshape(1, D), wr2, br, tri, S // POST_TILE)

    rb = EXPERT_ROWS
    counts = cnt[0, :N_EXPERTS].astype(jnp.int32)
    padded = (counts + rb - 1) // rb * rb
    pad_end = jnp.cumsum(padded)
    pad_start = pad_end - padded
    dest = _slots(pad_start.astype(jnp.int32), ri, _slabs(D // 2))
    dest0, dest1 = dest[0], dest[1]
    n_blocks = -(-T * TOPK_IN_GROUP // rb) + N_EXPERTS
    n_pad = n_blocks * rb
    n_valid = (pad_end[-1] // rb).astype(jnp.int32)
    blk_start = jnp.minimum(jnp.arange(n_blocks, dtype=jnp.int32), n_valid - 1) * rb
    blk_expert = jnp.sum(pad_end[None, :] <= blk_start[:, None], axis=-1).astype(jnp.int32)
    unused = n_valid + jnp.arange(N_EXPERTS, dtype=jnp.int32)
    tail_blk = jnp.concatenate([jnp.where(padded > 0, pad_end // rb - 1, -1),
                                jnp.where(unused < n_blocks, unused, -1)]).astype(jnp.int32)

    xs = _dispatch(dest0, dest1, tail_blk, h2, n_pad, _slabs(D // 2))
    ys = _experts(blk_expert, n_valid.reshape(1), xs, w1[l], w3[l], w2[l])
    out = _combine(dest0, dest1, x1, rf, ga2, ys, S // COMBINE_TILE)
    return out.reshape(B, S, D)
```

```python
import functools

import jax
import jax.numpy as jnp
from jax import lax
from jax.experimental import pallas as pl
from jax.experimental.pallas import tpu as pltpu

F32 = jnp.float32
BF16 = jnp.bfloat16
HIGHEST = lax.Precision.HIGHEST

N_HEADS = 8
HEAD_DIM = 64
ATTN_WIDTH = N_HEADS * HEAD_DIM
CONV_WIDTH = 512
CONV_K = 3
MOBA_BLOCK = 256
MOBA_TOPK = 3
ROPE_THETA = 500000.0
ROT_DIM = HEAD_DIM // 4
N_GROUPS = 4
EXPERTS_PER_GROUP = 8
N_EXPERTS = N_GROUPS * EXPERTS_PER_GROUP
TOPK_IN_GROUP = 2
N_MOD = 6
EPS = 1e-6

LANES = 128
NEG = -1e30
POST_TILE = 1024
COMBINE_TILE = 512
INPROJ_TILE = 1024
DISPATCH_TILE = 4096
POST_CHUNK = 512
QUERY_TILE = 512
DMA_UNROLL = 8
EXPERT_ROWS = 512
VMEM_LIMIT = 56 * 1024 * 1024
ATTN_VMEM_LIMIT = 60 * 1024 * 1024


def _cparams(*sem):
    return pltpu.CompilerParams(dimension_semantics=sem, vmem_limit_bytes=VMEM_LIMIT)


def _ada_kernel(c_ref, w_ref, b_ref, o_ref):
    c = c_ref[...]
    a = c * jax.nn.sigmoid(c)
    o_ref[...] = jnp.dot(a, w_ref[...], preferred_element_type=F32, precision=HIGHEST) + b_ref[...]


def _ada(c, w_ada, b_ada):
    B, D = c.shape
    N = w_ada.shape[1]
    tn = 1536
    return pl.pallas_call(
        _ada_kernel,
        grid=(N // tn,),
        in_specs=[pl.BlockSpec((B, D), lambda j: (0, 0)),
                  pl.BlockSpec((D, tn), lambda j: (0, j)),
                  pl.BlockSpec((1, tn), lambda j: (0, j))],
        out_specs=pl.BlockSpec((B, tn), lambda j: (0, j)),
        out_shape=jax.ShapeDtypeStruct((B, N), F32),
        compiler_params=_cparams("arbitrary"),
        name="ada",
    )(c, w_ada, b_ada.reshape(1, N))


def _inproj_kernel(x_ref, g_ref, sc_ref, sh_ref, w_ref, z_ref, *, n_chunk):
    x = x_ref[...]
    ms = jnp.mean(x * x, axis=-1, keepdims=True)
    y = x * lax.rsqrt(ms + EPS) * g_ref[...]
    h = (y * (1.0 + sc_ref[0]) + sh_ref[0]).astype(BF16)
    for n in range(0, z_ref.shape[1], n_chunk):
        z_ref[:, n:n + n_chunk] = jnp.dot(h, w_ref[:, n:n + n_chunk],
                                          preferred_element_type=F32).astype(BF16)


def _inproj(x2, g1, sc1, sh1, w_in_bf, seq_len):
    T, D = x2.shape
    N = w_in_bf.shape[1]
    tm = INPROJ_TILE
    assert seq_len % tm == 0
    bmap = lambda i: (i // (seq_len // tm), 0, 0)
    return pl.pallas_call(
        functools.partial(_inproj_kernel, n_chunk=512),
        grid=(T // tm,),
        in_specs=[pl.BlockSpec((tm, D), lambda i: (i, 0)),
                  pl.BlockSpec((1, D), lambda i: (0, 0)),
                  pl.BlockSpec((1, 1, D), bmap),
                  pl.BlockSpec((1, 1, D), bmap),
                  pl.BlockSpec((D, N), lambda i: (0, 0), pipeline_mode=pl.Buffered(1))],
        out_specs=pl.BlockSpec((tm, N), lambda i: (i, 0)),
        out_shape=jax.ShapeDtypeStruct((T, N), BF16),
        compiler_params=_cparams("arbitrary"),
        name="inproj",
    )(x2, g1, sc1, sh1, w_in_bf)


def _fold_rows(x, op):
    parts = [x[r:r + 8] for r in range(0, x.shape[0], 8)]
    while len(parts) > 1:
        parts = [op(parts[i], parts[i + 1]) for i in range(0, len(parts) - 1, 2)] + (
            [parts[-1]] if len(parts) % 2 else [])
    return parts[0]


def _attn_kernel(q_ref, k_ref, v_ref, cos_ref, sin_ref, cost_ref, sint_ref, gq_ref, gk_ref, o_ref,
                 kaug_ref, vt_ref, kmp_ref, kst_ref, s_ref, mcol_ref, qa_ref):
    S = q_ref.shape[1]
    blk = MOBA_BLOCK
    qt = QUERY_TILE
    sub = qt // blk
    nb = S // blk
    nq = S // qt
    nbp = kst_ref.shape[0] // 6
    hd = HEAD_DIM

    half = ROT_DIM // 2
    lane_r = lax.broadcasted_iota(jnp.int32, (blk, LANES), 1)
    rot_lo = (lane_r & (hd - 1)) < half
    same_head = jnp.where((lax.broadcasted_iota(jnp.int32, (LANES, LANES), 0) < hd)
                          == (lax.broadcasted_iota(jnp.int32, (LANES, LANES), 1) < hd), 1.0, 0.0).astype(BF16)

    def norm_rope_keys(xb, r0):
        x = xb.astype(F32)
        sq = x * x
        sq_hi = sq.astype(BF16)
        sq_lo = (sq - sq_hi.astype(F32)).astype(BF16)
        ssq = (jnp.dot(sq_hi, same_head, preferred_element_type=F32)
               + jnp.dot(sq_lo, same_head, preferred_element_type=F32))
        y = x * lax.rsqrt(ssq * (1.0 / hd) + EPS) * gk_ref[...]
        rot = jnp.where(rot_lo, pltpu.roll(y, LANES - half, 1), pltpu.roll(y, half, 1))
        return y * cos_ref[pl.ds(r0, blk), :] + rot * sin_ref[pl.ds(r0, blk), :]

    def norm_rope_queries_t(xb, r0):
        xT = xb.astype(F32).T
        sq = xT * xT
        cos = cost_ref[:, pl.ds(r0, qt)]
        sin = sint_ref[:, pl.ds(r0, qt)]
        rows = []
        for h in range(2):
            lo, hi = h * hd, (h + 1) * hd
            ssq = jnp.sum(_fold_rows(sq[lo:hi], jnp.add), axis=0, keepdims=True)
            y = xT[lo:hi] * lax.rsqrt(ssq * (1.0 / hd) + EPS) * gq_ref[lo:hi, :]
            y1, y2 = y[0:half], y[half:2 * half]
            rows += [y1 * cos - y2 * sin, y2 * cos + y1 * sin, y[2 * half:]]
        return jnp.concatenate(rows, axis=0)

    kmp_ref[...] = jnp.zeros_like(kmp_ref)
    ones_row = jnp.where(lax.broadcasted_iota(jnp.int32, (16, blk), 0) == 0, 1.0, 0.0).astype(BF16)
    lane_k = lax.broadcasted_iota(jnp.int32, (blk, LANES), 1)
    head0_k = lane_k < HEAD_DIM
    lane_m = lax.broadcasted_iota(jnp.int32, (nbp, LANES), 1)

    def prepare_keys(t):
        for u in range(sub):
            j = t * sub + u
            r0 = pl.multiple_of(jnp.minimum(j, nb - 1) * blk, blk)
            kr = norm_rope_keys(k_ref[0, pl.ds(r0, blk), :], r0)
            kmp_ref[pl.ds(j, 1), :] = jnp.sum(kr, axis=0, keepdims=True) * (1.0 / blk)
            kaug_ref[0, j] = jnp.where(head0_k, kr, jnp.where(lane_k - hd == j, 1.0, 0.0)).astype(BF16)
            kaug_ref[1, j] = jnp.where(head0_k, jnp.where(lane_k == j, 1.0, 0.0), kr).astype(BF16)
            vT = v_ref[0, pl.ds(r0, blk), :].astype(F32).T
            for h in range(2):
                vt_ref[h, j, 0:HEAD_DIM, :] = vT[h * HEAD_DIM:(h + 1) * HEAD_DIM].astype(BF16)
                vt_ref[h, j, HEAD_DIM:HEAD_DIM + 16, :] = ones_row
        kmp = kmp_ref[0:nbp, :]
        k_hi = kmp.astype(BF16)
        k_lo = (kmp - k_hi.astype(F32)).astype(BF16)
        zero = jnp.zeros((nbp, LANES), BF16)
        parts = []
        for h in range(2):
            mine = (lane_m < hd) if h == 0 else (lane_m >= hd)
            parts += [jnp.where(mine, k_hi, zero), jnp.where(mine, k_lo, zero)]
        parts += [parts[0], parts[2]]
        for n, part in enumerate(parts):
            kst_ref[n * nbp:(n + 1) * nbp, :] = part

    prepare_keys(0)

    causal = (lax.broadcasted_iota(jnp.int32, (blk, blk), 0) <= lax.broadcasted_iota(jnp.int32, (blk, blk), 1))
    rowf = lax.broadcasted_iota(jnp.int32, (nbp, qt), 0).astype(F32)
    subf = (lax.broadcasted_iota(jnp.int32, (nbp, qt), 1) // blk).astype(F32)
    q_scale = (hd ** -0.5) * 1.4426950408889634

    def query_operands(t):
        r0 = pl.multiple_of(t * qt, qt)
        qT = norm_rope_queries_t(q_ref[0, pl.ds(r0, qt), :], r0)
        cur = lax.convert_element_type(t * sub, F32) + subf
        q_hi = qT.astype(BF16)
        q_lo = (qT - q_hi.astype(F32)).astype(BF16)
        g1 = jnp.dot(kst_ref[0:4 * nbp, :], q_hi, preferred_element_type=F32)
        g2 = jnp.dot(kst_ref[4 * nbp:6 * nbp, :], q_lo, preferred_element_type=F32)
        qa = []
        for h in range(2):
            gate = g1[2 * h * nbp:(2 * h + 1) * nbp] + g1[(2 * h + 1) * nbp:(2 * h + 2) * nbp] + g2[h * nbp:(h + 1) * nbp]
            g = jnp.where(rowf < cur, gate, -jnp.inf)
            keep = rowf == cur
            for r in range(MOBA_TOPK):
                m = jnp.max(g, axis=0, keepdims=True)
                idx = jnp.min(jnp.where(g == m, rowf, 1e9), axis=0, keepdims=True)
                pick = (rowf == idx) & (cur > r)
                keep = keep | pick
                g = jnp.where(pick, -jnp.inf, g)
            bias = jnp.where(keep, 0.0, NEG)
            qs = qT[h * hd:(h + 1) * hd] * q_scale
            pad = jnp.zeros((LANES - hd - nbp, qt), F32)
            pieces = [qs, bias, pad] if h == 0 else [bias, pad, qs]
            qa.append(jnp.concatenate(pieces, axis=0).astype(BF16))
        return qa

    def logits(h, j0, n, qa_h):
        return jnp.dot(kaug_ref[h, pl.ds(j0, n)].reshape(n * blk, LANES), qa_h, preferred_element_type=F32)

    def pass2_tile(par, h, j):
        pT = jnp.exp2((s_ref[par, h, j] - mcol_ref[h, 0:1, :]).astype(BF16))
        return jnp.dot(vt_ref[h, j], pT, preferred_element_type=F32)

    def pass1_own(t, par, qa):
        mx = []
        for h in range(2):
            f = None
            for u in range(sub):
                lo = u * blk
                su = jnp.dot(kaug_ref[h, t * sub + u], qa[h][:, lo:], preferred_element_type=F32)
                pieces = [jnp.where(causal, su[:, :blk], NEG)] + [su[:, blk:]] * (lo + blk < qt)
                su = jnp.concatenate(pieces, axis=1)
                s_ref[par, h, t * sub + u, :, lo:] = su
                fu = _fold_rows(su, jnp.maximum)
                f = fu if f is None else jnp.concatenate([f[:, :lo], jnp.maximum(f[:, lo:], fu)], axis=1)
            mx.append(f)
        return mx

    def pass2_own(t, par):
        out = []
        for h in range(2):
            acc = None
            for u in range(sub):
                lo = u * blk
                pT = jnp.exp2((s_ref[par, h, t * sub + u, :, lo:] - mcol_ref[h, 0:1, lo:]).astype(BF16))
                d = jnp.dot(vt_ref[h, t * sub + u], pT, preferred_element_type=F32)
                acc = d if acc is None else jnp.concatenate([acc[:, :lo], acc[:, lo:] + d], axis=1)
            out.append(acc)
        return out

    def pass1_pair(p, par, qa, mx):
        out = []
        for h in range(2):
            sT = logits(h, 2 * p, 2, qa[h])
            s_ref[par, h, pl.ds(2 * p, 2)] = sT.reshape(2, blk, qt)
            out.append(jnp.maximum(mx[h], _fold_rows(sT, jnp.maximum)))
        return out

    def pass2_pair(p, par, acc):
        return [acc[h] + sum(pass2_tile(par, h, 2 * p + u) for u in range(2)) for h in range(2)]

    def finish_pass1(mx):
        for h in range(2):
            mcol_ref[h] = jnp.broadcast_to(jnp.max(mx[h], axis=0, keepdims=True), mcol_ref.shape[1:])

    def finish_pass2(t, acc):
        outT = jnp.concatenate([acc[h][0:hd] / acc[h][hd:hd + 1] for h in range(2)], axis=0)
        o_ref[0, pl.ds(pl.multiple_of(t * qt, qt), qt), :] = outT.T.astype(BF16)

    def stage(t, par):

        def prepare_next():
            prepare_keys(t + 1)
            nxt = query_operands(jnp.minimum(t + 1, nq - 1))
            for h in range(2):
                qa_ref[1 - par, h] = nxt[h]

        @pl.when(t == 0)
        def _():
            finish_pass1(pass1_own(t, par, query_operands(t)))
            prepare_next()

        @pl.when((t > 0) & (t < nq))
        def _():
            qa = [qa_ref[par, h] for h in range(2)]
            mx = pass1_own(t, par, qa)
            acc = pass2_own(t - 1, 1 - par)
            prepare_next()
            n_prev = (t - 1) * sub // 2

            def both(p, c):
                mx, acc = c
                return tuple(pass1_pair(p, par, qa, mx)), tuple(pass2_pair(p, 1 - par, acc))

            def both_twice(p2, c):
                return both(2 * p2 + 1, both(2 * p2, c))

            c = lax.fori_loop(0, n_prev // 2, both_twice, (tuple(mx), tuple(acc)))
            mx, acc = lax.fori_loop(n_prev // 2 * 2, n_prev, both, c)
            for p in range(sub // 2):
                mx = pass1_pair(n_prev + p, par, qa, mx)
            finish_pass2(t - 1, acc)
            finish_pass1(mx)

        @pl.when(t == nq)
        def _():
            acc = tuple(pass2_own(t - 1, 1 - par))
            n_prev = (t - 1) * sub // 2
            acc = lax.fori_loop(0, n_prev // 2, lambda p2, a: tuple(pass2_pair(2 * p2 + 1, 1 - par, pass2_pair(2 * p2, 1 - par, a))), acc)
            acc = lax.fori_loop(n_prev // 2 * 2, n_prev, lambda p, a: tuple(pass2_pair(p, 1 - par, a)), acc)
            finish_pass2(t - 1, acc)

    def stage_pair(tt, carry):
        stage(2 * tt, 0)
        stage(2 * tt + 1, 1)
        return carry

    lax.fori_loop(0, (nq + 2) // 2, stage_pair, 0)


def _attention(z3, cosf, sinf, cost, sint, gq_cols, gk2):
    B, S, _ = z3.shape
    n_pair = N_HEADS // 2
    kq = ATTN_WIDTH // LANES
    nb = S // MOBA_BLOCK
    nbp = -(-nb // 16) * 16
    sub = QUERY_TILE // MOBA_BLOCK
    assert sub <= 8
    return pl.pallas_call(
        _attn_kernel,
        grid=(B, n_pair),
        in_specs=[pl.BlockSpec((1, S, LANES), lambda b, p: (b, 0, p)),
                  pl.BlockSpec((1, S, LANES), lambda b, p: (b, 0, kq + p)),
                  pl.BlockSpec((1, S, LANES), lambda b, p: (b, 0, 2 * kq + p)),
                  pl.BlockSpec((S, LANES), lambda b, p: (0, 0), pipeline_mode=pl.Buffered(1)),
                  pl.BlockSpec((S, LANES), lambda b, p: (0, 0), pipeline_mode=pl.Buffered(1)),
                  pl.BlockSpec(cost.shape, lambda b, p: (0, 0)),
                  pl.BlockSpec(sint.shape, lambda b, p: (0, 0)),
                  pl.BlockSpec((LANES, QUERY_TILE), lambda b, p: (0, 0)),
                  pl.BlockSpec((1, LANES), lambda b, p: (0, 0))],
        out_specs=pl.BlockSpec((1, S, LANES), lambda b, p: (b, 0, p)),
        out_shape=jax.ShapeDtypeStruct((B, S, ATTN_WIDTH), BF16),
        scratch_shapes=[pltpu.VMEM((2, nb + sub, MOBA_BLOCK, LANES), BF16),
                        pltpu.VMEM((2, nb + sub, HEAD_DIM + 16, MOBA_BLOCK), BF16),
                        pltpu.VMEM((nbp + 8, LANES), F32),
                        pltpu.VMEM((6 * nbp, LANES), BF16),
                        pltpu.VMEM((2, 2, nb, MOBA_BLOCK, QUERY_TILE), F32),
                        pltpu.VMEM((2, 8, QUERY_TILE), F32),
                        pltpu.VMEM((2, 2, LANES, QUERY_TILE), BF16)],
        compiler_params=pltpu.CompilerParams(dimension_semantics=("arbitrary", "arbitrary"),
                                             vmem_limit_bytes=ATTN_VMEM_LIMIT),
        name="attn",
    )(z3, z3, z3, cosf, sinf, cost, sint, gq_cols, gk2)


def _slabs(width):
    return width // LANES


def _load_slabs(ref, row0, rows, n_slab, lead=()):
    return jnp.concatenate([ref[lead + (pl.ds(row0 * n_slab + s, rows, stride=n_slab), slice(None))]
                            for s in range(n_slab)], axis=1)


def _store_slabs(ref, row0, val):
    rows, width = val.shape
    n_slab = _slabs(width)
    for s in range(n_slab):
        ref[pl.ds(row0 * n_slab + s, rows, stride=n_slab), :] = val[:, s * LANES:(s + 1) * LANES]


def _post_kernel(x_ref, ya_ref, xb_ref, bg_ref, cg_ref, gta_ref, gtb_ref, ga1_ref, sc2_ref, sh2_ref,
                 cw_ref, cb_ref, wpa_ref, wpb_ref, wo_ref, g2_ref, wr_ref, br_ref, tri_ref,
                 x1_ref, h2_ref, ri_ref, rf_ref, cnt_ref, ubuf_ref, run_ref, *, tiles_per_seq):
    i = pl.program_id(0)
    tm = x_ref.shape[0]
    rc = POST_CHUNK
    halo = 8

    @pl.when(i == 0)
    def _():
        run_ref[...] = jnp.zeros_like(run_ref)

    @pl.when(i % tiles_per_seq == 0)
    def _():
        ubuf_ref[0:halo, :] = jnp.zeros((halo, CONV_WIDTH), F32)

    ubuf_ref[halo:halo + tm, :] = cg_ref[...].astype(F32) * xb_ref[...].astype(F32)
    cw = cw_ref[...]
    lanef = lax.broadcasted_iota(jnp.int32, (rc, LANES), 1).astype(F32)
    half_d = x_ref.shape[1] // 2
    run = run_ref[0:1, :]

    for c in range(tm // rc):
        rows = pl.ds(c * rc, rc)
        conv = (cw[0:1, :] * ubuf_ref[pl.ds(halo - 2 + c * rc, rc), :]
                + cw[1:2, :] * ubuf_ref[pl.ds(halo - 1 + c * rc, rc), :]
                + cw[2:3, :] * ubuf_ref[pl.ds(halo + c * rc, rc), :])
        y_b = (bg_ref[rows, :].astype(F32) * (conv + cb_ref[...])).astype(BF16)
        pa = jnp.dot(ya_ref[rows, :], wpa_ref[...], preferred_element_type=F32)
        pb = jnp.dot(y_b, wpb_ref[...], preferred_element_type=F32)
        merged = (jax.nn.sigmoid(gta_ref[rows, :].astype(F32)) * pa
                  + jax.nn.sigmoid(gtb_ref[rows, :].astype(F32)) * pb).astype(BF16)
        x1 = x_ref[rows, :] + ga1_ref[0] * jnp.dot(merged, wo_ref[...], preferred_element_type=F32)
        x1_ref[rows, :] = x1

        ms = jnp.mean(x1 * x1, axis=-1, keepdims=True)
        h2 = x1 * lax.rsqrt(ms + EPS) * g2_ref[...]
        h2 = h2 * (1.0 + sc2_ref[0]) + sh2_ref[0]
        h_hi = h2.astype(BF16)
        h_hi32 = h_hi.astype(F32)
        bits = pltpu.bitcast(h_hi32, jnp.uint32)
        _store_slabs(h2_ref, c * rc, (bits[:, :half_d] >> 16) | bits[:, half_d:])

        h_lo = (h2 - h_hi32).astype(BF16)
        r = jnp.dot(h_hi, wr_ref[...], preferred_element_type=F32)
        logit = (r[:, :LANES] + r[:, LANES:]
                 + jnp.dot(h_lo, wr_ref[:, :LANES], preferred_element_type=F32) + br_ref[...])
        gl = jnp.where(lanef < N_GROUPS, logit, -jnp.inf)
        gmax = jnp.max(gl, axis=-1, keepdims=True)
        g_idx = jnp.min(jnp.where(gl == gmax, lanef, 1e9), axis=-1, keepdims=True)
        g_w = 1.0 / jnp.sum(jnp.exp(gl - gmax), axis=-1, keepdims=True)
        e_lo = N_GROUPS + EXPERTS_PER_GROUP * g_idx
        el = jnp.where((lanef >= e_lo) & (lanef < e_lo + EXPERTS_PER_GROUP), logit, -jnp.inf)
        v0 = jnp.max(el, axis=-1, keepdims=True)
        i0 = jnp.min(jnp.where(el == v0, lanef, 1e9), axis=-1, keepdims=True)
        el = jnp.where(lanef == i0, -jnp.inf, el)
        v1 = jnp.max(el, axis=-1, keepdims=True)
        i1 = jnp.min(jnp.where(el == v1, lanef, 1e9), axis=-1, keepdims=True)
        t = jnp.exp(v1 - v0)
        w0 = g_w / (1.0 + t)
        w1 = g_w * t / (1.0 + t)
        e0 = i0 - N_GROUPS
        e1 = i1 - N_GROUPS

        oh0 = lanef == e0
        oh1 = lanef == e1
        oh = jnp.where(oh0 | oh1, 1.0, 0.0)
        before = jnp.dot(tri_ref[...], oh.astype(BF16), preferred_element_type=F32) + run
        r0 = jnp.sum(jnp.where(oh0, before, 0.0), axis=-1, keepdims=True)
        r1 = jnp.sum(jnp.where(oh1, before, 0.0), axis=-1, keepdims=True)
        run = run + jnp.sum(oh, axis=0, keepdims=True)

        ri = jnp.where(lanef == 0, e0, jnp.where(lanef == 1, e1, jnp.where(lanef == 2, r0, jnp.where(lanef == 3, r1, 0.0))))
        ri_ref[:, rows] = ri.astype(jnp.int32).T[0:8]
        rf_ref[rows, :] = jnp.where(lanef == 0, w0, jnp.where(lanef == 1, w1, 0.0))

    ubuf_ref[0:halo, :] = ubuf_ref[tm:tm + halo, :]
    run_ref[...] = jnp.broadcast_to(run, run_ref.shape)
    cnt_ref[...] = jnp.broadcast_to(run, cnt_ref.shape)


def _post(x2, ya2, z2, ga1, sc2, sh2, conv_w, conv_b, wpa, wpb, wo, g2, wr, br, tri, tiles_per_seq):
    T, D = x2.shape
    tm = POST_TILE
    cw = CONV_WIDTH
    xcol = 3 * ATTN_WIDTH // cw
    gcol = (3 * ATTN_WIDTH + 3 * cw) // D
    bmap = lambda i: (i // tiles_per_seq, 0, 0)
    const = lambda i: (0, 0)
    return pl.pallas_call(
        functools.partial(_post_kernel, tiles_per_seq=tiles_per_seq),
        grid=(T // tm,),
        in_specs=[pl.BlockSpec((tm, D), lambda i: (i, 0)),
                  pl.BlockSpec((tm, ATTN_WIDTH), lambda i: (i, 0)),
                  pl.BlockSpec((tm, cw), lambda i: (i, xcol)),
                  pl.BlockSpec((tm, cw), lambda i: (i, xcol + 1)),
                  pl.BlockSpec((tm, cw), lambda i: (i, xcol + 2)),
                  pl.BlockSpec((tm, D), lambda i: (i, gcol)),
                  pl.BlockSpec((tm, D), lambda i: (i, gcol + 1)),
                  pl.BlockSpec((1, 1, D), bmap),
                  pl.BlockSpec((1, 1, D), bmap),
                  pl.BlockSpec((1, 1, D), bmap),
                  pl.BlockSpec((CONV_K, cw), const),
                  pl.BlockSpec((1, cw), const),
                  pl.BlockSpec((ATTN_WIDTH, D), const),
                  pl.BlockSpec((cw, D), const),
                  pl.BlockSpec((D, D), const),
                  pl.BlockSpec((1, D), const),
                  pl.BlockSpec((D, 2 * LANES), const),
                  pl.BlockSpec((1, LANES), const),
                  pl.BlockSpec((POST_CHUNK, POST_CHUNK), const)],
        out_specs=[pl.BlockSpec((tm, D), lambda i: (i, 0)),
                   pl.BlockSpec((tm * _slabs(D // 2), LANES), lambda i: (i, 0)),
                   pl.BlockSpec((8, tm), lambda i: (0, i)),
                   pl.BlockSpec((tm, LANES), lambda i: (i, 0)),
                   pl.BlockSpec((8, LANES), const)],
        out_shape=[jax.ShapeDtypeStruct((T, D), F32),
                   jax.ShapeDtypeStruct((T * _slabs(D // 2), LANES), jnp.uint32),
                   jax.ShapeDtypeStruct((8, T), jnp.int32),
                   jax.ShapeDtypeStruct((T, LANES), F32),
                   jax.ShapeDtypeStruct((8, LANES), F32)],
        scratch_shapes=[pltpu.VMEM((tm + 16, cw), F32),
                        pltpu.VMEM((8, LANES), F32)],
        compiler_params=_cparams("arbitrary"),
        name="post",
    )(x2, ya2, z2, z2, z2, z2, z2, ga1, sc2, sh2, conv_w, conv_b, wpa, wpb, wo, g2, wr, br, tri)


def _slots_kernel(ps_ref, ri_ref, d_ref, *, n_slab):
    e = ri_ref[0:TOPK_IN_GROUP, :]
    start = jnp.zeros(e.shape, jnp.int32)
    for k in range(N_EXPERTS):
        start = jnp.where(e == k, ps_ref[k], start)
    d_ref[...] = (start + ri_ref[TOPK_IN_GROUP:2 * TOPK_IN_GROUP, :]) * n_slab


def _slots(pad_start, riT, n_slab):
    T = riT.shape[1]
    grid_spec = pltpu.PrefetchScalarGridSpec(
        num_scalar_prefetch=1,
        grid=(1,),
        in_specs=[pl.BlockSpec(riT.shape, lambda i, ps: (0, 0))],
        out_specs=pl.BlockSpec((TOPK_IN_GROUP, T), lambda i, ps: (0, 0)),
    )
    return pl.pallas_call(
        functools.partial(_slots_kernel, n_slab=n_slab),
        grid_spec=grid_spec,
        out_shape=jax.ShapeDtypeStruct((TOPK_IN_GROUP, T), jnp.int32),
        compiler_params=_cparams("arbitrary"),
        name="slots",
    )(pad_start, riT)


def _dispatch_kernel(d0_ref, d1_ref, tail_ref, h_ref, xs_ref, zero_ref, sem, zsem, *, n_slab):
    tm = h_ref.shape[0] // n_slab
    base = pl.program_id(0) * tm
    blk_rows = zero_ref.shape[0]

    @pl.when(pl.program_id(0) == 0)
    def _():
        zero_ref[...] = jnp.zeros_like(zero_ref)

        def tail_copy(e):
            start = pl.multiple_of(tail_ref[e] * blk_rows, blk_rows)
            return pltpu.make_async_copy(zero_ref, xs_ref.at[pl.ds(start, blk_rows), :], zsem)

        for e in range(tail_ref.shape[0]):
            @pl.when(tail_ref[e] >= 0)
            def _():
                tail_copy(e).start()
        for e in range(tail_ref.shape[0]):
            @pl.when(tail_ref[e] >= 0)
            def _():
                tail_copy(e).wait()

    def body(g, carry):
        r8 = pl.multiple_of(g * DMA_UNROLL, DMA_UNROLL)
        for u in range(DMA_UNROLL):
            src = h_ref.at[pl.ds((r8 + u) * n_slab, n_slab), :]
            pltpu.make_async_copy(src, xs_ref.at[pl.ds(d0_ref[base + r8 + u], n_slab), :], sem).start(priority=0)
            pltpu.make_async_copy(src, xs_ref.at[pl.ds(d1_ref[base + r8 + u], n_slab), :], sem).start(priority=1)
        return carry

    lax.fori_loop(0, tm // DMA_UNROLL, body, 0)
    for _ in range(TOPK_IN_GROUP):
        pltpu.make_async_copy(h_ref, xs_ref.at[pl.ds(0, tm * n_slab), :], sem).wait()


def _dispatch(dest0, dest1, tail_blk, h2p, n_pad, n_slab):
    tm = DISPATCH_TILE
    T = h2p.shape[0] // n_slab
    grid_spec = pltpu.PrefetchScalarGridSpec(
        num_scalar_prefetch=3,
        grid=(T // tm,),
        in_specs=[pl.BlockSpec((tm * n_slab, LANES), lambda i, d0, d1, tb: (i, 0))],
        out_specs=pl.BlockSpec(memory_space=pl.ANY),
        scratch_shapes=[pltpu.VMEM((EXPERT_ROWS * n_slab, LANES), h2p.dtype),
                        pltpu.SemaphoreType.DMA(()),
                        pltpu.SemaphoreType.DMA(())],
    )
    return pl.pallas_call(
        functools.partial(_dispatch_kernel, n_slab=n_slab),
        grid_spec=grid_spec,
        out_shape=jax.ShapeDtypeStruct((n_pad * n_slab, LANES), h2p.dtype),
        compiler_params=_cparams("arbitrary"),
        name="dispatch",
    )(dest0, dest1, tail_blk, h2p)


def _expert_kernel(be_ref, nv_ref, nxt_ref, slot_ref, xs_ref, w1f_ref, w3f_ref, w2f_ref, ys_ref,
                   w1_ref, w3_ref, w2_ref, f1_ref, f3_ref, f2_ref, sem):
    i = pl.program_id(0)
    wf_hbm = (w1f_ref, w3f_ref, w2f_ref)
    wf_vmem = (f1_ref, f3_ref, f2_ref)
    w_bf16 = (w1_ref, w3_ref, w2_ref)

    def fetch(e, s):
        return [pltpu.make_async_copy(wf_hbm[m].at[e], wf_vmem[m].at[s], sem.at[s, m]) for m in range(3)]

    @pl.when(i == 0)
    def _():
        for cp in fetch(be_ref[0], slot_ref[0]):
            cp.start()

    @pl.when((i == 0) | (be_ref[i] != be_ref[jnp.maximum(i - 1, 0)]))
    def _():
        s = slot_ref[i]
        for m, cp in enumerate(fetch(be_ref[i], s)):
            cp.wait()
            w_bf16[m][...] = wf_vmem[m][s].astype(BF16)

        @pl.when(nxt_ref[i] >= 0)
        def _():
            for cp in fetch(nxt_ref[i], 1 - s):
                cp.start()

    @pl.when(i >= nv_ref[0])
    def _():
        ys_ref[...] = jnp.zeros_like(ys_ref)

    @pl.when(i < nv_ref[0])
    def _():
        half_d = w1_ref.shape[0] // 2
        xp = _load_slabs(xs_ref, 0, EXPERT_ROWS, _slabs(half_d))
        x_lo = pltpu.bitcast(xp << 16, F32).astype(BF16)
        x_hi = pltpu.bitcast(xp & jnp.uint32(0xFFFF0000), F32).astype(BF16)
        x = jnp.concatenate([x_lo, x_hi], axis=1)
        a = jnp.dot(x, w1_ref[...], preferred_element_type=F32)
        b = jnp.dot(x, w3_ref[...], preferred_element_type=F32)
        hid = (a * jax.nn.sigmoid(a) * b).astype(BF16)
        y = jnp.dot(hid, w2_ref[...], preferred_element_type=F32)
        bits = pltpu.bitcast(y.astype(BF16).astype(F32), jnp.uint32)
        _store_slabs(ys_ref, 0, (bits[:, :half_d] >> 16) | bits[:, half_d:])


def _experts(blk_expert, n_valid, xs, w1f, w3f, w2f):
    _, D, F = w1f.shape
    rb = EXPERT_ROWS
    blk_rows = rb * _slabs(D // 2)
    n_steps = xs.shape[0] // blk_rows
    first = jnp.concatenate([jnp.ones((1,), jnp.bool_), blk_expert[1:] != blk_expert[:-1]])
    slot = ((jnp.cumsum(first.astype(jnp.int32)) - 1) % 2).astype(jnp.int32)
    later = jnp.where(blk_expert[None, :] > blk_expert[:, None], blk_expert[None, :], N_EXPERTS)
    nxt = jnp.min(later, axis=1)
    nxt = jnp.where(nxt < N_EXPERTS, nxt, -1).astype(jnp.int32)
    row_blk = lambda i, be, nv, nx, sl: (jnp.minimum(i, nv[0] - 1), 0)
    out_blk = lambda i, be, nv, nx, sl: (i, 0)
    grid_spec = pltpu.PrefetchScalarGridSpec(
        num_scalar_prefetch=4,
        grid=(n_steps,),
        in_specs=[pl.BlockSpec((blk_rows, LANES), row_blk),
                  pl.BlockSpec(memory_space=pl.ANY),
                  pl.BlockSpec(memory_space=pl.ANY),
                  pl.BlockSpec(memory_space=pl.ANY)],
        out_specs=pl.BlockSpec((blk_rows, LANES), out_blk),
        scratch_shapes=[pltpu.VMEM((D, F), BF16), pltpu.VMEM((D, F), BF16), pltpu.VMEM((F, D), BF16),
                        pltpu.VMEM((2, D, F), F32), pltpu.VMEM((2, D, F), F32), pltpu.VMEM((2, F, D), F32),
                        pltpu.SemaphoreType.DMA((2, 3))],
    )
    return pl.pallas_call(
        _expert_kernel,
        grid_spec=grid_spec,
        out_shape=jax.ShapeDtypeStruct(xs.shape, jnp.uint32),
        compiler_params=_cparams("arbitrary"),
        name="experts",
    )(blk_expert, n_valid, nxt, slot, xs, w1f, w3f, w2f)


def _combine_kernel(d0_ref, d1_ref, x1_ref, rf_ref, ga2_ref, ys_ref, o_ref, buf_ref, sem, *, n_slab):
    i = pl.program_id(0)
    n = pl.num_programs(0)
    tm = x1_ref.shape[0]

    def gather(step, slot):
        base = step * tm

        def body(g, carry):
            r8 = pl.multiple_of(g * DMA_UNROLL, DMA_UNROLL)
            for u in range(DMA_UNROLL):
                dst = pl.ds((r8 + u) * n_slab, n_slab)
                pltpu.make_async_copy(ys_ref.at[pl.ds(d0_ref[base + r8 + u], n_slab), :],
                                      buf_ref.at[slot, 0, dst, :], sem.at[slot]).start(priority=0)
                pltpu.make_async_copy(ys_ref.at[pl.ds(d1_ref[base + r8 + u], n_slab), :],
                                      buf_ref.at[slot, 1, dst, :], sem.at[slot]).start(priority=1)
            return carry

        lax.fori_loop(0, tm // DMA_UNROLL, body, 0)

    @pl.when(i == 0)
    def _():
        gather(0, 0)

    @pl.when(i + 1 < n)
    def _():
        gather(i + 1, (i + 1) % 2)

    slot = i % 2
    for k in range(TOPK_IN_GROUP):
        pltpu.make_async_copy(ys_ref.at[pl.ds(0, tm * n_slab), :], buf_ref.at[slot, k], sem.at[slot]).wait()
    rf = rf_ref[...]
    w = [rf[:, k:k + 1] for k in range(TOPK_IN_GROUP)]
    yp = [_load_slabs(buf_ref, 0, tm, n_slab, lead=(slot, k)) for k in range(TOPK_IN_GROUP)]
    half_d = n_slab * LANES
    y_lo = sum(w[k] * pltpu.bitcast(yp[k] << 16, F32) for k in range(TOPK_IN_GROUP))
    y_hi = sum(w[k] * pltpu.bitcast(yp[k] & jnp.uint32(0xFFFF0000), F32) for k in range(TOPK_IN_GROUP))
    ga2 = ga2_ref[0]
    o_ref[:, :half_d] = x1_ref[:, :half_d] + ga2[:, :half_d] * y_lo
    o_ref[:, half_d:] = x1_ref[:, half_d:] + ga2[:, half_d:] * y_hi


def _combine(dest0, dest1, x1, rf, ga2, ys, tiles_per_seq):
    T, D = x1.shape
    n_slab = _slabs(D // 2)
    tm = COMBINE_TILE
    row = lambda i, d0, d1: (i, 0)
    grid_spec = pltpu.PrefetchScalarGridSpec(
        num_scalar_prefetch=2,
        grid=(T // tm,),
        in_specs=[pl.BlockSpec((tm, D), row),
                  pl.BlockSpec((tm, LANES), row),
                  pl.BlockSpec((1, 1, D), lambda i, d0, d1: (i // tiles_per_seq, 0, 0)),
                  pl.BlockSpec(memory_space=pl.ANY)],
        out_specs=pl.BlockSpec((tm, D), row),
        scratch_shapes=[pltpu.VMEM((2, TOPK_IN_GROUP, tm * n_slab, LANES), ys.dtype),
                        pltpu.SemaphoreType.DMA((2,))],
    )
    return pl.pallas_call(
        functools.partial(_combine_kernel, n_slab=n_slab),
        grid_spec=grid_spec,
        out_shape=jax.ShapeDtypeStruct((T, D), F32),
        compiler_params=_cparams("arbitrary"),
        name="combine",
    )(dest0, dest1, x1, rf, ga2, ys)


def _rope_tables(S):
    pos = jnp.arange(S, dtype=F32)
    inv_freq = ROPE_THETA ** (-jnp.arange(0, ROT_DIM, 2, dtype=F32) / ROT_DIM)
    ang = pos[:, None] * inv_freq[None, :]
    cos, sin = jnp.cos(ang), jnp.sin(ang)
    half = ROT_DIM // 2
    ones = jnp.ones((S, HEAD_DIM - ROT_DIM), F32)
    cos_h = jnp.concatenate([cos, cos, ones], axis=1)
    sin_h = jnp.concatenate([-sin, sin, 0.0 * ones], axis=1)
    return jnp.tile(cos_h, (1, LANES // HEAD_DIM)), jnp.tile(sin_h, (1, LANES // HEAD_DIM)), cos.T, sin.T


def kernel(x, c, w_ada, b_ada, g_norm1, g_norm2, w_in, g_q, g_k, conv_w, conv_b,
           w_pa, w_pb, w_o, w_rg, b_rg, w_re, b_re, w1, w3, w2):
    B, S, D = x.shape
    T = B * S
    assert S % POST_TILE == 0 and S % COMBINE_TILE == 0 and T % DISPATCH_TILE == 0 and S % QUERY_TILE == 0 and QUERY_TILE % (2 * MOBA_BLOCK) == 0
    assert S // MOBA_BLOCK <= LANES - HEAD_DIM
    l = 0

    mod = _ada(c, w_ada[l], b_ada[l])
    sh1, sc1, ga1, sh2, sc2, ga2 = [m.reshape(B, 1, D) for m in jnp.split(mod, N_MOD, axis=-1)]

    x2 = x.reshape(T, D)
    z2 = _inproj(x2, g_norm1[l].reshape(1, D), sc1, sh1, w_in[l].astype(BF16), S)

    cosf, sinf, cost, sint = _rope_tables(S)
    rep = LANES // HEAD_DIM
    gq_cols = jnp.broadcast_to(jnp.tile(g_q[l], rep).reshape(LANES, 1), (LANES, QUERY_TILE))
    ya = _attention(z2.reshape(B, S, -1), cosf, sinf, cost, sint, gq_cols, jnp.tile(g_k[l], rep).reshape(1, LANES))

    wr = jnp.zeros((D, LANES), F32).at[:, :N_GROUPS].set(w_rg[l]).at[:, N_GROUPS:N_GROUPS + N_EXPERTS].set(w_re[l])
    br = jnp.zeros((1, LANES), F32).at[0, :N_GROUPS].set(b_rg[l]).at[0, N_GROUPS:N_GROUPS + N_EXPERTS].set(b_re[l])
    wr_hi = wr.astype(BF16)
    wr2 = jnp.concatenate([wr_hi, (wr - wr_hi.astype(F32)).astype(BF16)], axis=1)
    tri = (lax.broadcasted_iota(jnp.int32, (POST_CHUNK, POST_CHUNK), 1)
           < lax.broadcasted_iota(jnp.int32, (POST_CHUNK, POST_CHUNK), 0)).astype(BF16)
    x1, h2, ri, rf, cnt = _post(x2, ya.reshape(T, ATTN_WIDTH), z2, ga1, sc2, sh2,
                                conv_w[l], conv_b[l].reshape(1, CONV_WIDTH),
                                w_pa[l].astype(BF16), w_pb[l].astype(BF16), w_o[l].astype(BF16),
                                g_norm2[l].reshape(1, D), wr2, br, tri, S // POST_TILE)

    rb = EXPERT_ROWS
    counts = cnt[0, :N_EXPERTS].astype(jnp.int32)
    padded = (counts + rb - 1) // rb * rb
    pad_end = jnp.cumsum(padded)
    pad_start = pad_end - padded
    dest = _slots(pad_start.astype(jnp.int32), ri, _slabs(D // 2))
    dest0, dest1 = dest[0], dest[1]
    n_blocks = -(-T * TOPK_IN_GROUP // rb) + N_EXPERTS
    n_pad = n_blocks * rb
    n_valid = (pad_end[-1] // rb).astype(jnp.int32)
    blk_start = jnp.minimum(jnp.arange(n_blocks, dtype=jnp.int32), n_valid - 1) * rb
    blk_expert = jnp.sum(pad_end[None, :] <= blk_start[:, None], axis=-1).astype(jnp.int32)
    unused = n_valid + jnp.arange(N_EXPERTS, dtype=jnp.int32)
    tail_blk = jnp.concatenate([jnp.where(padded > 0, pad_end // rb - 1, -1),
                                jnp.where(unused < n_blocks, unused, -1)]).astype(jnp.int32)

    xs = _dispatch(dest0, dest1, tail_blk, h2, n_pad, _slabs(D // 2))
    ys = _experts(blk_expert, n_valid.reshape(1), xs, w1[l], w3[l], w2[l])
    out = _combine(dest0, dest1, x1, rf, ga2, ys, S // COMBINE_TILE)
    return out.reshape(B, S, D)
```
